```python
import jax, jax.numpy as jnp
from jax import lax
import numpy as np

D_MODEL = 2048
BATCH = 8
SEQ = 8192
DEPTH = 2

CHUNK = 64
N_MIXERS = 2
N_RET = (DEPTH + 1) // N_MIXERS
N_GDN = DEPTH // N_MIXERS

RET_HEADS = 8
RET_DK = D_MODEL // RET_HEADS
RET_DV = 2 * RET_DK
RET_QK = RET_HEADS * RET_DK
RET_VW = RET_HEADS * RET_DV
ROPE_BASE = 10000.0
GN_EPS = 1e-6

GDN_K_HEADS = 16
GDN_V_HEADS = 32
GDN_DK = D_MODEL // GDN_K_HEADS
GDN_DV = 2 * D_MODEL // GDN_V_HEADS
GDN_QK = GDN_K_HEADS * GDN_DK
GDN_VW = GDN_V_HEADS * GDN_DV
GDN_QKV = 2 * GDN_QK + GDN_VW
GDN_CONV = 4
RMS_EPS = 1e-6
L2_EPS = 1e-6

D_FF = 4 * D_MODEL

DN_ALPHA = (2.0 * DEPTH) ** 0.25
DN_BETA = (8.0 * DEPTH) ** -0.25
LN_EPS = 1e-5

kernel_name = 'chunk_causal_retention_gdn_hybrid'

F32 = jnp.float32


def layer_norm(x, g, b):
    xf = x.astype(F32)
    mu = jnp.mean(xf, -1, keepdims=True)
    var = jnp.mean(jnp.square(xf - mu), -1, keepdims=True)
    return ((xf - mu) * lax.rsqrt(var + LN_EPS) * g + b).astype(x.dtype)


def rotary(t, pos):
    half = t.shape[-1] // 2
    inv = ROPE_BASE ** (-jnp.arange(half, dtype=F32) / half)
    ang = pos.astype(F32)[:, None] * inv[None, :]
    cos = jnp.cos(ang)[None, :, None, :]
    sin = jnp.sin(ang)[None, :, None, :]
    t1, t2 = t[..., :half], t[..., half:]
    return jnp.concatenate([t1 * cos - t2 * sin, t1 * sin + t2 * cos], axis=-1)


def l2norm(t):
    return t * lax.rsqrt(jnp.sum(jnp.square(t), -1, keepdims=True) + L2_EPS)


def to_chunks(t):
    b, s, h, d = t.shape
    return t.reshape(b, s // CHUNK, CHUNK, h, d).transpose(0, 3, 1, 2, 4)


def to_chunks_scalar(t):
    b, s, h = t.shape
    return t.reshape(b, s // CHUNK, CHUNK, h).transpose(0, 3, 1, 2)


def from_chunks(t):
    b, h, n, c, d = t.shape
    return t.transpose(0, 2, 3, 1, 4).reshape(b, n * c, h, d)


def retention_mixer(x, w_in, gn_g, w_out):
    B, S, _ = x.shape
    q, k, v, gate = jnp.split(x @ w_in, [RET_QK, 2 * RET_QK, 2 * RET_QK + RET_VW], axis=-1)
    pos = jnp.arange(S)
    q = rotary(q.reshape(B, S, RET_HEADS, RET_DK).astype(F32), pos)
    k = rotary(k.reshape(B, S, RET_HEADS, RET_DK).astype(F32), pos) * (RET_DK ** -0.5)
    v = v.reshape(B, S, RET_HEADS, RET_DV).astype(F32)
    q, k, v = to_chunks(q), to_chunks(k), to_chunks(v)

    log_gamma = jnp.log1p(-jnp.exp2(-5.0 - jnp.arange(RET_HEADS, dtype=F32)))
    idx = jnp.arange(CHUNK, dtype=F32)
    lg = log_gamma[:, None]
    intra = jnp.exp(lg[..., None] * jnp.abs(idx[:, None] - idx[None, :]))
    scores = jnp.einsum('bhncd,bhnmd->bhncm', q, k) * intra[None, :, None]
    y_intra = jnp.einsum('bhncm,bhnme->bhnce', scores, v)

    q_dec = q * jnp.exp(lg * (idx + 1.0))[None, :, None, :, None]
    k_dec = k * jnp.exp(lg * (CHUNK - 1.0 - idx))[None, :, None, :, None]
    chunk_decay = jnp.exp(log_gamma * CHUNK)[None, :, None, None]

    def step(state, inp):
        qc, kc, vc = inp
        y = jnp.einsum('bhcd,bhde->bhce', qc, state)
        state = state * chunk_decay + jnp.einsum('bhcd,bhce->bhde', kc, vc)
        return state, y

    state0 = jnp.zeros((B, RET_HEADS, RET_DK, RET_DV), F32)
    _, y_inter = lax.scan(step, state0, (jnp.moveaxis(q_dec, 2, 0), jnp.moveaxis(k_dec, 2, 0), jnp.moveaxis(v, 2, 0)))
    y = from_chunks(y_intra + jnp.moveaxis(y_inter, 0, 2))

    mu = jnp.mean(y, -1, keepdims=True)
    var = jnp.mean(jnp.square(y - mu), -1, keepdims=True)
    y = ((y - mu) * lax.rsqrt(var + GN_EPS)).reshape(B, S, RET_VW) * gn_g
    return (jax.nn.silu(gate) * y.astype(x.dtype)) @ w_out


def gated_deltanet_mixer(x, w_in, conv_w, a_log, dt_bias, norm_g, w_out):
    B, S, _ = x.shape
    qkv, z, b, a = jnp.split(x @ w_in, [GDN_QKV, GDN_QKV + GDN_VW, GDN_QKV + GDN_VW + GDN_V_HEADS], axis=-1)
    qkv = jax.nn.silu(lax.conv_general_dilated(
        qkv, conv_w[:, None, :], window_strides=(1,), padding=[(GDN_CONV - 1, 0)],
        dimension_numbers=('NWC', 'WIO', 'NWC'), feature_group_count=GDN_QKV))
    q, k, v = jnp.split(qkv.astype(F32), [GDN_QK, 2 * GDN_QK], axis=-1)
    rep = GDN_V_HEADS // GDN_K_HEADS
    q = jnp.repeat(l2norm(q.reshape(B, S, GDN_K_HEADS, GDN_DK)) * (GDN_DK ** -0.5), rep, axis=2)
    k = jnp.repeat(l2norm(k.reshape(B, S, GDN_K_HEADS, GDN_DK)), rep, axis=2)
    v = v.reshape(B, S, GDN_V_HEADS, GDN_DV)
    beta = jax.nn.sigmoid(b.astype(F32))
    g = -jnp.exp(a_log.astype(F32)) * jax.nn.softplus(a.astype(F32) + dt_bias.astype(F32))

    q, k, v = to_chunks(q), to_chunks(k), to_chunks(v)
    beta, g = to_chunks_scalar(beta), to_chunks_scalar(g)
    g_cum = jnp.cumsum(g, axis=-1)
    idx = jnp.arange(CHUNK)
    causal = idx[:, None] >= idx[None, :]
    strict = idx[:, None] > idx[None, :]
    decay = jnp.exp(jnp.where(causal, g_cum[..., :, None] - g_cum[..., None, :], -jnp.inf))

    k_beta = k * beta[..., None]
    a_mat = jnp.where(strict, jnp.einsum('bhncd,bhnmd->bhncm', k_beta, k) * decay, 0.0)
    eye = jnp.eye(CHUNK, dtype=F32)
    t_mat = lax.linalg.triangular_solve(a_mat + eye, jnp.broadcast_to(eye, a_mat.shape),
                                        left_side=True, lower=True, unit_diagonal=True)
    u = jnp.einsum('bhncm,bhnme->bhnce', t_mat, v * beta[..., None])
    w = jnp.einsum('bhncm,bhnmd->bhncd', t_mat, k_beta * jnp.exp(g_cum)[..., None])
    attn = jnp.einsum('bhncd,bhnmd->bhncm', q, k) * decay
    q_dec = q * jnp.exp(g_cum)[..., None]
    k_dec = k * jnp.exp(g_cum[..., -1:] - g_cum)[..., None]
    chunk_decay = jnp.exp(g_cum[..., -1])

    def step(state, inp):
        qd, kd, wc, uc, ac, cd = inp
        v_new = uc - jnp.einsum('bhcd,bhde->bhce', wc, state)
        y = jnp.einsum('bhcd,bhde->bhce', qd, state) + jnp.einsum('bhcm,bhme->bhce', ac, v_new)
        state = state * cd[..., None, None] + jnp.einsum('bhcd,bhce->bhde', kd, v_new)
        return state, y

    xs = (jnp.moveaxis(q_dec, 2, 0), jnp.moveaxis(k_dec, 2, 0), jnp.moveaxis(w, 2, 0),
          jnp.moveaxis(u, 2, 0), jnp.moveaxis(attn, 2, 0), jnp.moveaxis(chunk_decay, 2, 0))
    state0 = jnp.zeros((B, GDN_V_HEADS, GDN_DK, GDN_DV), F32)
    _, y = lax.scan(step, state0, xs)
    y = from_chunks(jnp.moveaxis(y, 0, 2))

    y = y * lax.rsqrt(jnp.mean(jnp.square(y), -1, keepdims=True) + RMS_EPS) * norm_g
    y = y * jax.nn.silu(z.astype(F32).reshape(B, S, GDN_V_HEADS, GDN_DV))
    return y.reshape(B, S, GDN_VW).astype(x.dtype) @ w_out


def sq_relu_mlp(x, w1, w2):
    return jnp.square(jax.nn.relu(x @ w1)) @ w2


def _fwd_setup_inputs(seed: int = 0) -> dict:
    key = jax.random.key(seed)
    ks = jax.random.split(key, 20)
    nrm = jax.random.normal
    x = nrm(ks[0], (BATCH, SEQ, D_MODEL), F32)

    ret_w_in = nrm(ks[1], (N_RET, D_MODEL, 2 * RET_QK + 2 * RET_VW), F32) * D_MODEL ** -0.5
    ret_gn_g = 1.0 + 0.02 * nrm(ks[2], (N_RET, RET_VW), F32)
    ret_w_out = nrm(ks[3], (N_RET, RET_VW, D_MODEL), F32) * (RET_VW ** -0.5 * DN_BETA)

    gdn_w_in = nrm(ks[4], (N_GDN, D_MODEL, GDN_QKV + GDN_VW + 2 * GDN_V_HEADS), F32) * D_MODEL ** -0.5
    gdn_conv_w = nrm(ks[5], (N_GDN, GDN_CONV, GDN_QKV), F32) * GDN_CONV ** -0.5
    gdn_a_log = jnp.log(jax.random.uniform(ks[6], (N_GDN, GDN_V_HEADS), F32, 1.0, 16.0))
    dt = jnp.exp(jax.random.uniform(ks[7], (N_GDN, GDN_V_HEADS), F32, np.log(1e-3), np.log(1e-1)))
    gdn_dt_bias = dt + jnp.log(-jnp.expm1(-dt))
    gdn_norm_g = 1.0 + 0.02 * nrm(ks[8], (N_GDN, GDN_DV), F32)
    gdn_w_out = nrm(ks[9], (N_GDN, GDN_VW, D_MODEL), F32) * (GDN_VW ** -0.5 * DN_BETA)

    ln_mix_g = 1.0 + 0.02 * nrm(ks[10], (DEPTH, D_MODEL), F32)
    ln_mix_b = 0.02 * nrm(ks[11], (DEPTH, D_MODEL), F32)
    mlp_w1 = nrm(ks[12], (DEPTH, D_MODEL, D_FF), F32) * D_MODEL ** -0.5
    mlp_w2 = nrm(ks[13], (DEPTH, D_FF, D_MODEL), F32) * (D_FF ** -0.5 * DN_BETA)
    ln_ffn_g = 1.0 + 0.02 * nrm(ks[14], (DEPTH, D_MODEL), F32)
    ln_ffn_b = 0.02 * nrm(ks[15], (DEPTH, D_MODEL), F32)
    return {'x': x,
            'ret_w_in': ret_w_in, 'ret_gn_g': ret_gn_g, 'ret_w_out': ret_w_out,
            'gdn_w_in': gdn_w_in, 'gdn_conv_w': gdn_conv_w, 'gdn_a_log': gdn_a_log,
            'gdn_dt_bias': gdn_dt_bias, 'gdn_norm_g': gdn_norm_g, 'gdn_w_out': gdn_w_out,
            'ln_mix_g': ln_mix_g, 'ln_mix_b': ln_mix_b, 'mlp_w1': mlp_w1, 'mlp_w2': mlp_w2,
            'ln_ffn_g': ln_ffn_g, 'ln_ffn_b': ln_ffn_b}


def _fwd_reference(x, ret_w_in, ret_gn_g, ret_w_out, gdn_w_in, gdn_conv_w, gdn_a_log,
              gdn_dt_bias, gdn_norm_g, gdn_w_out, ln_mix_g, ln_mix_b, mlp_w1, mlp_w2,
              ln_ffn_g, ln_ffn_b):
    for i in range(DEPTH):
        j = i // N_MIXERS
        if i % N_MIXERS == 0:
            mix = retention_mixer(x, ret_w_in[j], ret_gn_g[j], ret_w_out[j])
        else:
            mix = gated_deltanet_mixer(x, gdn_w_in[j], gdn_conv_w[j], gdn_a_log[j],
                                       gdn_dt_bias[j], gdn_norm_g[j], gdn_w_out[j])
        x = layer_norm(DN_ALPHA * x + mix, ln_mix_g[i], ln_mix_b[i])
        x = layer_norm(DN_ALPHA * x + sq_relu_mlp(x, mlp_w1[i], mlp_w2[i]), ln_ffn_g[i], ln_ffn_b[i])
    return x


import jax as _jax
import jax.numpy as _jnp

TWIN_FORMAT = 'train_step'
FWD_PARAMS = ['x', 'ret_w_in', 'ret_gn_g', 'ret_w_out', 'gdn_w_in', 'gdn_conv_w', 'gdn_a_log', 'gdn_dt_bias', 'gdn_norm_g', 'gdn_w_out', 'ln_mix_g', 'ln_mix_b', 'mlp_w1', 'mlp_w2', 'ln_ffn_g', 'ln_ffn_b']
TWIN_WEIGHTS = ['ret_w_in', 'ret_gn_g', 'ret_w_out', 'gdn_w_in', 'gdn_conv_w', 'gdn_a_log', 'gdn_dt_bias', 'gdn_norm_g', 'gdn_w_out', 'ln_mix_g', 'ln_mix_b', 'mlp_w1', 'mlp_w2', 'ln_ffn_g', 'ln_ffn_b']
TWIN_DIFF_INPUT = 'x'
TWIN_INPUTS = ['x', 'ret_w_in', 'ret_gn_g', 'ret_w_out', 'gdn_w_in', 'gdn_conv_w', 'gdn_a_log', 'gdn_dt_bias', 'gdn_norm_g', 'gdn_w_out', 'ln_mix_g', 'ln_mix_b', 'mlp_w1', 'mlp_w2', 'ln_ffn_g', 'ln_ffn_b', 'loss_target', 'm_ret_w_in', 'm_ret_gn_g', 'm_ret_w_out', 'm_gdn_w_in', 'm_gdn_conv_w', 'm_gdn_a_log', 'm_gdn_dt_bias', 'm_gdn_norm_g', 'm_gdn_w_out', 'm_ln_mix_g', 'm_ln_mix_b', 'm_mlp_w1', 'm_mlp_w2', 'm_ln_ffn_g', 'm_ln_ffn_b', 'v_ret_w_in', 'v_ret_gn_g', 'v_ret_w_out', 'v_gdn_w_in', 'v_gdn_conv_w', 'v_gdn_a_log', 'v_gdn_dt_bias', 'v_gdn_norm_g', 'v_gdn_w_out', 'v_ln_mix_g', 'v_ln_mix_b', 'v_mlp_w1', 'v_mlp_w2', 'v_ln_ffn_g', 'v_ln_ffn_b']
TWIN_OUTPUTS = ['loss', 'grad_x', 'grad_ret_w_in', 'grad_ret_gn_g', 'grad_ret_w_out', 'grad_gdn_w_in', 'grad_gdn_conv_w', 'grad_gdn_a_log', 'grad_gdn_dt_bias', 'grad_gdn_norm_g', 'grad_gdn_w_out', 'grad_ln_mix_g', 'grad_ln_mix_b', 'grad_mlp_w1', 'grad_mlp_w2', 'grad_ln_ffn_g', 'grad_ln_ffn_b', 'delta_ret_w_in', 'delta_ret_gn_g', 'delta_ret_w_out', 'delta_gdn_w_in', 'delta_gdn_conv_w', 'delta_gdn_a_log', 'delta_gdn_dt_bias', 'delta_gdn_norm_g', 'delta_gdn_w_out', 'delta_ln_mix_g', 'delta_ln_mix_b', 'delta_mlp_w1', 'delta_mlp_w2', 'delta_ln_ffn_g', 'delta_ln_ffn_b', 'new_m_ret_w_in', 'new_m_ret_gn_g', 'new_m_ret_w_out', 'new_m_gdn_w_in', 'new_m_gdn_conv_w', 'new_m_gdn_a_log', 'new_m_gdn_dt_bias', 'new_m_gdn_norm_g', 'new_m_gdn_w_out', 'new_m_ln_mix_g', 'new_m_ln_mix_b', 'new_m_mlp_w1', 'new_m_mlp_w2', 'new_m_ln_ffn_g', 'new_m_ln_ffn_b', 'new_v_ret_w_in', 'new_v_ret_gn_g', 'new_v_ret_w_out', 'new_v_gdn_w_in', 'new_v_gdn_conv_w', 'new_v_gdn_a_log', 'new_v_gdn_dt_bias', 'new_v_gdn_norm_g', 'new_v_gdn_w_out', 'new_v_ln_mix_g', 'new_v_ln_mix_b', 'new_v_mlp_w1', 'new_v_mlp_w2', 'new_v_ln_ffn_g', 'new_v_ln_ffn_b']
TWIN_LEAF_KINDS = {'loss': 'loss', 'grad_x': 'grad_x', 'grad_ret_w_in': 'grad_w', 'grad_ret_gn_g': 'grad_w', 'grad_ret_w_out': 'grad_w', 'grad_gdn_w_in': 'grad_w', 'grad_gdn_conv_w': 'grad_w', 'grad_gdn_a_log': 'grad_w', 'grad_gdn_dt_bias': 'grad_w', 'grad_gdn_norm_g': 'grad_w', 'grad_gdn_w_out': 'grad_w', 'grad_ln_mix_g': 'grad_w', 'grad_ln_mix_b': 'grad_w', 'grad_mlp_w1': 'grad_w', 'grad_mlp_w2': 'grad_w', 'grad_ln_ffn_g': 'grad_w', 'grad_ln_ffn_b': 'grad_w', 'delta_ret_w_in': 'delta_w', 'delta_ret_gn_g': 'delta_w', 'delta_ret_w_out': 'delta_w', 'delta_gdn_w_in': 'delta_w', 'delta_gdn_conv_w': 'delta_w', 'delta_gdn_a_log': 'delta_w', 'delta_gdn_dt_bias': 'delta_w', 'delta_gdn_norm_g': 'delta_w', 'delta_gdn_w_out': 'delta_w', 'delta_ln_mix_g': 'delta_w', 'delta_ln_mix_b': 'delta_w', 'delta_mlp_w1': 'delta_w', 'delta_mlp_w2': 'delta_w', 'delta_ln_ffn_g': 'delta_w', 'delta_ln_ffn_b': 'delta_w', 'new_m_ret_w_in': 'new_m', 'new_m_ret_gn_g': 'new_m', 'new_m_ret_w_out': 'new_m', 'new_m_gdn_w_in': 'new_m', 'new_m_gdn_conv_w': 'new_m', 'new_m_gdn_a_log': 'new_m', 'new_m_gdn_dt_bias': 'new_m', 'new_m_gdn_norm_g': 'new_m', 'new_m_gdn_w_out': 'new_m', 'new_m_ln_mix_g': 'new_m', 'new_m_ln_mix_b': 'new_m', 'new_m_mlp_w1': 'new_m', 'new_m_mlp_w2': 'new_m', 'new_m_ln_ffn_g': 'new_m', 'new_m_ln_ffn_b': 'new_m', 'new_v_ret_w_in': 'new_v', 'new_v_ret_gn_g': 'new_v', 'new_v_ret_w_out': 'new_v', 'new_v_gdn_w_in': 'new_v', 'new_v_gdn_conv_w': 'new_v', 'new_v_gdn_a_log': 'new_v', 'new_v_gdn_dt_bias': 'new_v', 'new_v_gdn_norm_g': 'new_v', 'new_v_gdn_w_out': 'new_v', 'new_v_ln_mix_g': 'new_v', 'new_v_ln_mix_b': 'new_v', 'new_v_mlp_w1': 'new_v', 'new_v_mlp_w2': 'new_v', 'new_v_ln_ffn_g': 'new_v', 'new_v_ln_ffn_b': 'new_v'}


def _forward(args):
    return _fwd_reference(*[args[k] for k in FWD_PARAMS])


def _output_shape():
    def fwd():
        inp = _fwd_setup_inputs(0)
        return _fwd_reference(*[inp[k] for k in FWD_PARAMS])
    out = _jax.eval_shape(fwd)
    return out.shape, out.dtype

N_MICROBATCH = 1
ADAM_LR = 0.001
ADAM_B1 = 0.9
ADAM_B2 = 0.999
ADAM_EPS = 1e-08
ADAM_WD = 0.01
ADAM_STEP = 10
PER_EXAMPLE_BATCH_AXIS = {'x': 0, 'loss_target': 0}
SHARED_INPUTS = []
_WEIGHT_DTYPES = {'ret_w_in': _jnp.float32, 'ret_gn_g': _jnp.float32, 'ret_w_out': _jnp.float32, 'gdn_w_in': _jnp.float32, 'gdn_conv_w': _jnp.float32, 'gdn_a_log': _jnp.float32, 'gdn_dt_bias': _jnp.float32, 'gdn_norm_g': _jnp.float32, 'gdn_w_out': _jnp.float32, 'ln_mix_g': _jnp.float32, 'ln_mix_b': _jnp.float32, 'mlp_w1': _jnp.float32, 'mlp_w2': _jnp.float32, 'ln_ffn_g': _jnp.float32, 'ln_ffn_b': _jnp.float32}
MOMENT_SCALE = {'ret_w_in': 2.335187e-02, 'ret_gn_g': 2.023448e-02, 'ret_w_out': 5.655552e-02, 'gdn_w_in': 1.854090e-02, 'gdn_conv_w': 1.923252e-02, 'gdn_a_log': 5.990353e-02, 'gdn_dt_bias': 5.961737e-02, 'gdn_norm_g': 1.381129e-01, 'gdn_w_out': 6.202583e-02, 'ln_mix_g': 8.273889e-01, 'ln_mix_b': 5.035010e-01, 'mlp_w1': 3.036011e-02, 'mlp_w2': 1.678959e-01, 'ln_ffn_g': 2.271724e+01, 'ln_ffn_b': 5.108644e+00}


def _to_microbatches(a, axis):
    t = _jnp.moveaxis(a, axis, 0)
    t = t.reshape((N_MICROBATCH, t.shape[0] // N_MICROBATCH) + t.shape[1:])
    return _jnp.moveaxis(t, 1, axis + 1)


def setup_inputs(seed: int = 0) -> dict:
    inp = _fwd_setup_inputs(seed)
    key = _jax.random.fold_in(_jax.random.key(seed), 7919)
    shape, _ = _output_shape()
    out = dict(inp)
    out["loss_target"] = _jax.random.normal(_jax.random.fold_in(key, 0), shape, _jnp.float32)
    for i, name in enumerate(TWIN_WEIGHTS):
        w = inp[name].astype(_jnp.float32)
        if MOMENT_SCALE is None:
            s = _jnp.sqrt(_jnp.mean(_jnp.square(w)) + 1e-30)
        else:
            s = MOMENT_SCALE[name]
        km, kv = _jax.random.split(_jax.random.fold_in(key, i + 1))
        out[name] = w
        out["m_" + name] = s * _jax.random.normal(km, w.shape, _jnp.float32)
        out["v_" + name] = (s * s) * _jax.random.uniform(kv, w.shape, _jnp.float32, 0.5, 1.5)
    if N_MICROBATCH > 1:
        for name, axis in PER_EXAMPLE_BATCH_AXIS.items():
            out[name] = _to_microbatches(out[name], axis)
    return {'x': out['x'], 'ret_w_in': out['ret_w_in'], 'ret_gn_g': out['ret_gn_g'], 'ret_w_out': out['ret_w_out'], 'gdn_w_in': out['gdn_w_in'], 'gdn_conv_w': out['gdn_conv_w'], 'gdn_a_log': out['gdn_a_log'], 'gdn_dt_bias': out['gdn_dt_bias'], 'gdn_norm_g': out['gdn_norm_g'], 'gdn_w_out': out['gdn_w_out'], 'ln_mix_g': out['ln_mix_g'], 'ln_mix_b': out['ln_mix_b'], 'mlp_w1': out['mlp_w1'], 'mlp_w2': out['mlp_w2'], 'ln_ffn_g': out['ln_ffn_g'], 'ln_ffn_b': out['ln_ffn_b'], 'loss_target': out['loss_target'], 'm_ret_w_in': out['m_ret_w_in'], 'm_ret_gn_g': out['m_ret_gn_g'], 'm_ret_w_out': out['m_ret_w_out'], 'm_gdn_w_in': out['m_gdn_w_in'], 'm_gdn_conv_w': out['m_gdn_conv_w'], 'm_gdn_a_log': out['m_gdn_a_log'], 'm_gdn_dt_bias': out['m_gdn_dt_bias'], 'm_gdn_norm_g': out['m_gdn_norm_g'], 'm_gdn_w_out': out['m_gdn_w_out'], 'm_ln_mix_g': out['m_ln_mix_g'], 'm_ln_mix_b': out['m_ln_mix_b'], 'm_mlp_w1': out['m_mlp_w1'], 'm_mlp_w2': out['m_mlp_w2'], 'm_ln_ffn_g': out['m_ln_ffn_g'], 'm_ln_ffn_b': out['m_ln_ffn_b'], 'v_ret_w_in': out['v_ret_w_in'], 'v_ret_gn_g': out['v_ret_gn_g'], 'v_ret_w_out': out['v_ret_w_out'], 'v_gdn_w_in': out['v_gdn_w_in'], 'v_gdn_conv_w': out['v_gdn_conv_w'], 'v_gdn_a_log': out['v_gdn_a_log'], 'v_gdn_dt_bias': out['v_gdn_dt_bias'], 'v_gdn_norm_g': out['v_gdn_norm_g'], 'v_gdn_w_out': out['v_gdn_w_out'], 'v_ln_mix_g': out['v_ln_mix_g'], 'v_ln_mix_b': out['v_ln_mix_b'], 'v_mlp_w1': out['v_mlp_w1'], 'v_mlp_w2': out['v_mlp_w2'], 'v_ln_ffn_g': out['v_ln_ffn_g'], 'v_ln_ffn_b': out['v_ln_ffn_b']}


def _loss(weights, diff, rest, loss_target):
    with _jax.named_scope("forward"):
        args = {**rest, TWIN_DIFF_INPUT: diff, **{k: w.astype(_WEIGHT_DTYPES[k]) for k, w in weights.items()}}
        y = _forward(args)
    with _jax.named_scope("loss_head"):
        err = _jnp.square(y.astype(_jnp.float32) - loss_target)
        return 0.5 * _jnp.sum(_jnp.mean(err, axis=-1)) if err.ndim else 0.5 * err


def _adamw(w, g, m, v):
    m = ADAM_B1 * m + (1.0 - ADAM_B1) * g
    v = ADAM_B2 * v + (1.0 - ADAM_B2) * _jnp.square(g)
    m_hat = m / (1.0 - ADAM_B1 ** ADAM_STEP)
    v_hat = v / (1.0 - ADAM_B2 ** ADAM_STEP)
    delta = -ADAM_LR * (m_hat / (_jnp.sqrt(v_hat) + ADAM_EPS) + ADAM_WD * w)
    return delta, m, v


def reference(x, ret_w_in, ret_gn_g, ret_w_out, gdn_w_in, gdn_conv_w, gdn_a_log, gdn_dt_bias, gdn_norm_g, gdn_w_out, ln_mix_g, ln_mix_b, mlp_w1, mlp_w2, ln_ffn_g, ln_ffn_b, loss_target, m_ret_w_in, m_ret_gn_g, m_ret_w_out, m_gdn_w_in, m_gdn_conv_w, m_gdn_a_log, m_gdn_dt_bias, m_gdn_norm_g, m_gdn_w_out, m_ln_mix_g, m_ln_mix_b, m_mlp_w1, m_mlp_w2, m_ln_ffn_g, m_ln_ffn_b, v_ret_w_in, v_ret_gn_g, v_ret_w_out, v_gdn_w_in, v_gdn_conv_w, v_gdn_a_log, v_gdn_dt_bias, v_gdn_norm_g, v_gdn_w_out, v_ln_mix_g, v_ln_mix_b, v_mlp_w1, v_mlp_w2, v_ln_ffn_g, v_ln_ffn_b):
    given = dict(x=x, ret_w_in=ret_w_in, ret_gn_g=ret_gn_g, ret_w_out=ret_w_out, gdn_w_in=gdn_w_in, gdn_conv_w=gdn_conv_w, gdn_a_log=gdn_a_log, gdn_dt_bias=gdn_dt_bias, gdn_norm_g=gdn_norm_g, gdn_w_out=gdn_w_out, ln_mix_g=ln_mix_g, ln_mix_b=ln_mix_b, mlp_w1=mlp_w1, mlp_w2=mlp_w2, ln_ffn_g=ln_ffn_g, ln_ffn_b=ln_ffn_b, loss_target=loss_target, m_ret_w_in=m_ret_w_in, m_ret_gn_g=m_ret_gn_g, m_ret_w_out=m_ret_w_out, m_gdn_w_in=m_gdn_w_in, m_gdn_conv_w=m_gdn_conv_w, m_gdn_a_log=m_gdn_a_log, m_gdn_dt_bias=m_gdn_dt_bias, m_gdn_norm_g=m_gdn_norm_g, m_gdn_w_out=m_gdn_w_out, m_ln_mix_g=m_ln_mix_g, m_ln_mix_b=m_ln_mix_b, m_mlp_w1=m_mlp_w1, m_mlp_w2=m_mlp_w2, m_ln_ffn_g=m_ln_ffn_g, m_ln_ffn_b=m_ln_ffn_b, v_ret_w_in=v_ret_w_in, v_ret_gn_g=v_ret_gn_g, v_ret_w_out=v_ret_w_out, v_gdn_w_in=v_gdn_w_in, v_gdn_conv_w=v_gdn_conv_w, v_gdn_a_log=v_gdn_a_log, v_gdn_dt_bias=v_gdn_dt_bias, v_gdn_norm_g=v_gdn_norm_g, v_gdn_w_out=v_gdn_w_out, v_ln_mix_g=v_ln_mix_g, v_ln_mix_b=v_ln_mix_b, v_mlp_w1=v_mlp_w1, v_mlp_w2=v_mlp_w2, v_ln_ffn_g=v_ln_ffn_g, v_ln_ffn_b=v_ln_ffn_b)
    weights = {n: given[n] for n in TWIN_WEIGHTS}
    shared = {n: given[n] for n in SHARED_INPUTS}
    per_example = {n: given[n] for n in ['x']}
    grad_fn = _jax.value_and_grad(_loss, argnums=(0, 1))

    def one_microbatch(ex, loss_target):
        ex = dict(ex)
        diff = ex.pop(TWIN_DIFF_INPUT)
        return grad_fn(weights, diff, {**shared, **ex}, loss_target)

    if N_MICROBATCH == 1:
        loss, (grad_w, grad_x) = one_microbatch(per_example, given["loss_target"])
    else:
        def body(carry, xs):
            loss_sum, grad_sum = carry
            l_k, (gw_k, gx_k) = one_microbatch(xs[0], xs[1])
            with _jax.named_scope("update"):
                return (loss_sum + l_k, _jax.tree.map(_jnp.add, grad_sum, gw_k)), gx_k

        init = (_jnp.zeros((), _jnp.float32), _jax.tree.map(_jnp.zeros_like, weights))
        (loss, grad_w), grad_x = _jax.lax.scan(body, init, (per_example, given["loss_target"]))
    with _jax.named_scope("update"):
        delta_w, new_m, new_v = {}, {}, {}
        for n in TWIN_WEIGHTS:
            delta_w[n], new_m[n], new_v[n] = _adamw(weights[n], grad_w[n], given["m_" + n], given["v_" + n])
    return (loss, grad_x, *[grad_w[n] for n in TWIN_WEIGHTS], *[delta_w[n] for n in TWIN_WEIGHTS],
            *[new_m[n] for n in TWIN_WEIGHTS], *[new_v[n] for n in TWIN_WEIGHTS])
```

```python
import functools

import numpy as np
import jax
import jax.numpy as jnp
from jax import lax
from jax.experimental import pallas as pl
from jax.experimental.pallas import tpu as pltpu

F32 = jnp.float32
BF16 = jnp.bfloat16
MESH = pl.DeviceIdType.MESH
SDS = jax.ShapeDtypeStruct

D_MODEL = 2048
CHUNK = 64
RET_HEADS, RET_DK, RET_DV = 8, 256, 512
GDN_HV, GDN_D = 32, 128
GDN_QKV = 8192
ALPHA = 4.0 ** 0.25
LN_EPS, GN_EPS, RMS_EPS, L2_EPS = 1e-5, 1e-6, 1e-6, 1e-6
ADAM_LR, ADAM_B1, ADAM_B2, ADAM_EPS, ADAM_WD, ADAM_STEP = 0.001, 0.9, 0.999, 1e-8, 0.01, 10

VMEM_LIMIT_BYTES = 56 * 1024 * 1024
RT = 256
CT = 256
ROWS = 256
LANES = 128
N_CHIPS = 4
N_DEV = 8


def _cp(*sem):
    return pltpu.CompilerParams(dimension_semantics=sem, vmem_limit_bytes=VMEM_LIMIT_BYTES)


def _dot(a, b):
    return jnp.dot(a, b, preferred_element_type=F32)


def _dot_nt(a, b):
    return lax.dot_general(a, b, (((1,), (1,)), ((), ())), preferred_element_type=F32)


def _dot_tn(a, b):
    return lax.dot_general(a, b, (((0,), (0,)), ((), ())), preferred_element_type=F32)


def _split2(x):
    hi = x.astype(BF16)
    lo = (x - hi.astype(F32)).astype(BF16)
    return hi, lo


def _dotx3(a, b):
    ah, al = _split2(a)
    bh, bl = _split2(b)
    return _dot(ah, bh) + (_dot(ah, bl) + _dot(al, bh))


def _dot_exact_l(l_bf16, x):
    hi = x.astype(BF16)
    r = x - hi.astype(F32)
    mid = r.astype(BF16)
    lo = (r - mid.astype(F32)).astype(BF16)
    return _dot(l_bf16, hi) + (_dot(l_bf16, mid) + _dot(l_bf16, lo))


def _sigmoid(x):
    return 1.0 / (1.0 + jnp.exp(-x))


def _iota(shape, dim):
    return lax.broadcasted_iota(jnp.int32, shape, dim)


def _mm(a, b, mode, name, *, out_dtype=F32, tm=1024, tn=1024, tk=2048, epi=None, extra=None, scale=1.0,
        shard_major=False):
    if mode == "nn":
        (M, K), (K2, N) = a.shape, b.shape
    elif mode == "nt":
        (M, K), (N, K2) = a.shape, b.shape
    else:
        (K, M), (K2, N) = a.shape, b.shape
    assert K == K2, (a.shape, b.shape, mode)
    tm, tn, tk = min(tm, M), min(tn, N), min(tk, K)
    assert M % tm == 0 and N % tn == 0 and K % tk == 0, (M, N, K, tm, tn, tk)
    nk = K // tk
    dims = {"nn": (((1,), (0,)), ((), ())), "nt": (((1,), (1,)), ((), ())), "tn": (((0,), (0,)), ((), ()))}[mode]
    if mode == "tn":
        a_spec = pl.BlockSpec((tk, tm), lambda i, j, k: (k, i))
    else:
        a_spec = pl.BlockSpec((tm, tk), lambda i, j, k: (i, k))
    if mode == "nt":
        b_spec = pl.BlockSpec((tn, tk), lambda i, j, k: (j, k))
    else:
        b_spec = pl.BlockSpec((tk, tn), lambda i, j, k: (k, j))
    tile = pl.BlockSpec((tm, tn), lambda i, j, k: (i, j))
    in_specs, ins = [a_spec, b_spec], [a, b]
    if extra is not None:
        in_specs.append(tile)
        ins.append(extra)
    if epi == "relu2":
        out_shape = (SDS((M, N), F32), SDS((M, N), BF16))
        out_specs = (tile, tile)
    elif shard_major:
        per = (N // N_CHIPS) // tn
        assert per * tn * N_CHIPS == N
        out_shape = (SDS((N_CHIPS, M, N // N_CHIPS), out_dtype),)
        out_specs = (pl.BlockSpec((None, tm, tn), lambda i, j, k: (j // per, i, j % per)),)
    else:
        out_shape = (SDS((M, N), out_dtype),)
        out_specs = (tile,)
    n_out = len(out_shape)

    def body(*refs):
        a_ref, b_ref = refs[0], refs[1]
        pos = 2
        x_ref = None
        if extra is not None:
            x_ref = refs[pos]
            pos += 1
        o_refs = refs[pos:pos + n_out]
        acc_ref = refs[pos + n_out] if nk > 1 else None

        def prod():
            av, bv = a_ref[...], b_ref[...]
            if av.dtype != BF16:
                av = av.astype(BF16)
            if bv.dtype != BF16:
                bv = bv.astype(BF16)
            return lax.dot_general(av, bv, dims, preferred_element_type=F32)

        def finish(acc):
            if epi == "relu2":
                o_refs[0][...] = acc
                r = jnp.maximum(acc, 0.0)
                o_refs[1][...] = (r * r).astype(BF16)
            elif epi == "drelu2":
                o_refs[0][...] = (acc * (2.0 * jnp.maximum(x_ref[...], 0.0))).astype(out_dtype)
            elif epi == "add":
                o_refs[0][...] = (acc + scale * x_ref[...]).astype(out_dtype)
            else:
                o_refs[0][...] = acc.astype(out_dtype)

        if nk == 1:
            finish(prod())
        else:
            k = pl.program_id(2)

            @pl.when(k == 0)
            def _():
                acc_ref[...] = prod()

            @pl.when(k > 0)
            def _():
                acc_ref[...] += prod()

            @pl.when(k == nk - 1)
            def _():
                finish(acc_ref[...])

    out = pl.pallas_call(
        body, name=name, grid=(M // tm, N // tn, nk), in_specs=in_specs, out_specs=out_specs, out_shape=out_shape,
        scratch_shapes=[pltpu.VMEM((tm, tn), F32)] if nk > 1 else [],
        compiler_params=_cp("parallel", "parallel", "arbitrary"),
    )(*ins)
    return out if n_out > 1 else out[0]


def _ln_stats(z):
    mu = jnp.mean(z, -1, keepdims=True)
    zc = z - mu
    var = jnp.mean(zc * zc, -1, keepdims=True)
    rstd = lax.rsqrt(var + LN_EPS)
    return zc * rstd, rstd


def _ln_fwd(xin, sub, g, b, name):
    S, Dm = xin.shape
    row = pl.BlockSpec((ROWS, Dm), lambda t: (t, 0))
    vec = pl.BlockSpec((1, Dm), lambda t: (0, 0))

    def body(x_ref, s_ref, g_ref, b_ref, o_ref, ob_ref, z_ref):
        z = ALPHA * x_ref[...] + s_ref[...]
        xh, _ = _ln_stats(z)
        o = xh * g_ref[...] + b_ref[...]
        o_ref[...] = o
        ob_ref[...] = o.astype(BF16)
        z_ref[...] = z

    return pl.pallas_call(
        body, name=name, grid=(S // ROWS,), in_specs=[row, row, vec, vec], out_specs=(row, row, row),
        out_shape=(SDS((S, Dm), F32), SDS((S, Dm), BF16), SDS((S, Dm), F32)), compiler_params=_cp("parallel"),
    )(xin, sub, g, b)


def _ln_bwd(dout, z, g, name):
    S, Dm = z.shape
    row = pl.BlockSpec((ROWS, Dm), lambda t: (t, 0))
    vec = pl.BlockSpec((1, Dm), lambda t: (0, 0))

    def body(d_ref, z_ref, g_ref, dz_ref, dzb_ref, dg_ref, db_ref):
        t = pl.program_id(0)
        xh, rstd = _ln_stats(z_ref[...])
        d = d_ref[...]
        dxh = d * g_ref[...]
        m1 = jnp.mean(dxh, -1, keepdims=True)
        m2 = jnp.mean(dxh * xh, -1, keepdims=True)
        dz = rstd * (dxh - m1 - xh * m2)
        dz_ref[...] = dz
        dzb_ref[...] = dz.astype(BF16)
        pg = jnp.sum(d * xh, axis=0, keepdims=True)
        pb = jnp.sum(d, axis=0, keepdims=True)

        @pl.when(t == 0)
        def _():
            dg_ref[...] = pg
            db_ref[...] = pb

        @pl.when(t > 0)
        def _():
            dg_ref[...] += pg
            db_ref[...] += pb

    return pl.pallas_call(
        body, name=name, grid=(S // ROWS,), in_specs=[row, row, vec], out_specs=(row, row, vec, vec),
        out_shape=(SDS((S, Dm), F32), SDS((S, Dm), BF16), SDS((1, Dm), F32), SDS((1, Dm), F32)),
        compiler_params=_cp("arbitrary"),
    )(dout, z, g)


def _loss_fwd_bwd(y, tgt):
    S, Dm = y.shape
    row = pl.BlockSpec((ROWS, Dm), lambda t: (t, 0))
    one = pl.BlockSpec((1, 1), lambda t: (0, 0))

    def body(y_ref, t_ref, dy_ref, l_ref):
        t = pl.program_id(0)
        diff = y_ref[...] - t_ref[...]
        dy_ref[...] = diff * (1.0 / Dm)
        part = jnp.sum(jnp.sum(diff * diff, axis=1, keepdims=True), axis=0, keepdims=True) * (0.5 / Dm)

        @pl.when(t == 0)
        def _():
            l_ref[...] = part

        @pl.when(t > 0)
        def _():
            l_ref[...] += part

    return pl.pallas_call(
        body, name="loss", grid=(S // ROWS,), in_specs=[row, row], out_specs=(row, one),
        out_shape=(SDS((S, Dm), F32), SDS((1, 1), F32)), compiler_params=_cp("arbitrary"),
    )(y, tgt)


def _ret_consts():
    h = np.arange(RET_HEADS, dtype=np.float64)
    lg = np.log1p(-np.exp2(-5.0 - h))
    return jnp.asarray(np.concatenate([lg, np.exp(lg * RT)]).astype(np.float32))


def _rope_tables(S):
    half = RET_DK // 2
    inv = 10000.0 ** (-jnp.arange(half, dtype=F32) / half)
    ang = jnp.arange(S).astype(F32)[:, None] * inv[None, :]
    return jnp.cos(ang), jnp.sin(ang)


def _ret_masks(lgh):
    ri, ci = _iota((RT, RT), 0), _iota((RT, RT), 1)
    visible = (ci >> 6) <= (ri >> 6)
    m = jnp.where(visible, jnp.exp(lgh * jnp.abs(ri - ci).astype(F32)), 0.0)
    pos = _iota((RT, 1), 0).astype(F32)
    return m, jnp.exp(lgh * (pos + 1.0)), jnp.exp(lgh * (RT - 1.0 - pos))


def _ret_rot(h0, cos, sin):
    S = h0.shape[0]
    half = RET_DK // 2

    def body(q_ref, k_ref, v_ref, c_ref, s_ref, qo_ref, ko_ref, vo_ref):
        c, s = c_ref[...], s_ref[...]

        def rot(t):
            t1, t2 = t[:, :half], t[:, half:]
            return jnp.concatenate([t1 * c - t2 * s, t1 * s + t2 * c], axis=-1)

        qo_ref[...] = rot(q_ref[...]).astype(BF16)
        ko_ref[...] = (rot(k_ref[...]) * (RET_DK ** -0.5)).astype(BF16)
        vo_ref[...] = v_ref[...].astype(BF16)

    qk = lambda off: pl.BlockSpec((RT, RET_DK), lambda t, h: (t, off + h))
    vv = lambda off: pl.BlockSpec((RT, RET_DV), lambda t, h: (t, off + h))
    tab = pl.BlockSpec((RT, half), lambda t, h: (t, 0))
    return pl.pallas_call(
        body, name="ret_rot", grid=(S // RT, RET_HEADS), in_specs=[qk(0), qk(RET_HEADS), vv(RET_HEADS), tab, tab],
        out_specs=(qk(0), qk(0), vv(0)),
        out_shape=(SDS((S, 2048), BF16), SDS((S, 2048), BF16), SDS((S, 4096), BF16)),
        compiler_params=_cp("parallel", "parallel"),
    )(h0, h0, h0, cos, sin)


def _ret_fwd(q, k, v, consts):
    S = q.shape[0]
    nt = S // RT

    def body(c_ref, q_ref, k_ref, v_ref, y_ref, st_ref, s_scr):
        h, t = pl.program_id(0), pl.program_id(1)

        @pl.when(t == 0)
        def _():
            s_scr[...] = jnp.zeros_like(s_scr)

        lgh, cdec = c_ref[h], c_ref[RET_HEADS + h]
        m, dq, dk = _ret_masks(lgh)
        qv, kv, vv = q_ref[...], k_ref[...], v_ref[...]
        p = (_dot_nt(qv, kv) * m).astype(BF16)
        sp = s_scr[...]
        spb = sp.astype(BF16)
        st_ref[...] = spb
        qd = (qv.astype(F32) * dq).astype(BF16)
        kd = (kv.astype(F32) * dk).astype(BF16)
        y_ref[...] = _dot(p, vv) + _dot(qd, spb)
        s_scr[...] = sp * cdec + _dot_tn(kd, vv)

    qk = pl.BlockSpec((RT, RET_DK), lambda h, t: (t, h))
    vs = pl.BlockSpec((RT, RET_DV), lambda h, t: (t, h))
    return pl.pallas_call(
        body, name="ret_fwd", grid=(RET_HEADS, nt),
        in_specs=[pl.BlockSpec(memory_space=pltpu.SMEM), qk, qk, vs],
        out_specs=(vs, pl.BlockSpec((None, None, RET_DK, RET_DV), lambda h, t: (h, t, 0, 0))),
        out_shape=(SDS((S, 4096), F32), SDS((RET_HEADS, nt, RET_DK, RET_DV), BF16)),
        scratch_shapes=[pltpu.VMEM((RET_DK, RET_DV), F32)], compiler_params=_cp("parallel", "arbitrary"),
    )(consts, q, k, v)


def _ret_bwd(q, k, v, dy, states, consts, cos, sin):
    S = q.shape[0]
    nt = S // RT
    half = RET_DK // 2

    def body(c_ref, q_ref, k_ref, v_ref, dy_ref, st_ref, cos_ref, sin_ref, dq_ref, dk_ref, dv_ref, ds_scr):
        h, t = pl.program_id(0), pl.program_id(1)

        @pl.when(t == 0)
        def _():
            ds_scr[...] = jnp.zeros_like(ds_scr)

        lgh, cdec = c_ref[h], c_ref[RET_HEADS + h]
        m, dqc, dkc = _ret_masks(lgh)
        qv, kv, vv, dyv, spb = q_ref[...], k_ref[...], v_ref[...], dy_ref[...], st_ref[...]
        p = (_dot_nt(qv, kv) * m).astype(BF16)
        qd = (qv.astype(F32) * dqc).astype(BF16)
        kd = (kv.astype(F32) * dkc).astype(BF16)
        dsn = ds_scr[...]
        dsb = dsn.astype(BF16)
        dsc = (_dot_nt(dyv, vv) * m).astype(BF16)
        dq = _dot(dsc, kv) + _dot_nt(dyv, spb) * dqc
        dk = _dot_tn(dsc, qv) + _dot_nt(vv, dsb) * dkc
        dv_ref[...] = (_dot_tn(p, dyv) + _dot(kd, dsb)).astype(BF16)
        ds_scr[...] = dsn * cdec + _dot_tn(qd, dyv)
        c, s = cos_ref[...], sin_ref[...]

        def unrot(d):
            d1, d2 = d[:, :half], d[:, half:]
            return jnp.concatenate([d1 * c + d2 * s, d2 * c - d1 * s], axis=-1)

        dq_ref[...] = unrot(dq).astype(BF16)
        dk_ref[...] = (unrot(dk) * (RET_DK ** -0.5)).astype(BF16)

    rev = lambda t: nt - 1 - t
    qk = pl.BlockSpec((RT, RET_DK), lambda h, t: (rev(t), h))
    vs = pl.BlockSpec((RT, RET_DV), lambda h, t: (rev(t), h))
    tab = pl.BlockSpec((RT, half), lambda h, t: (rev(t), 0))
    return pl.pallas_call(
        body, name="ret_bwd", grid=(RET_HEADS, nt),
        in_specs=[pl.BlockSpec(memory_space=pltpu.SMEM), qk, qk, vs, vs,
                  pl.BlockSpec((None, None, RET_DK, RET_DV), lambda h, t: (h, rev(t), 0, 0)), tab, tab],
        out_specs=(qk, qk, vs),
        out_shape=(SDS((S, 2048), BF16), SDS((S, 2048), BF16), SDS((S, 4096), BF16)),
        scratch_shapes=[pltpu.VMEM((RET_DK, RET_DV), F32)], compiler_params=_cp("parallel", "arbitrary"),
    )(consts, q, k, v, dy, states, cos, sin)


def _gn_stats(y):
    mu = jnp.mean(y, -1, keepdims=True)
    yc = y - mu
    var = jnp.mean(yc * yc, -1, keepdims=True)
    rstd = lax.rsqrt(var + GN_EPS)
    return yc * rstd, rstd


def _ret_post_fwd(y, h0, gn_g):
    S = y.shape[0]

    def body(y_ref, gate_ref, g_ref, o_ref):
        yn, _ = _gn_stats(y_ref[...])
        gate = gate_ref[...]
        o_ref[...] = (gate * _sigmoid(gate) * (yn * g_ref[...])).astype(BF16)

    vs = lambda off: pl.BlockSpec((RT, RET_DV), lambda h, t: (t, off + h))
    return pl.pallas_call(
        body, name="ret_post_fwd", grid=(RET_HEADS, S // RT),
        in_specs=[vs(0), vs(2 * RET_HEADS), pl.BlockSpec((1, RET_DV), lambda h, t: (0, h))], out_specs=vs(0),
        out_shape=SDS((S, 4096), BF16), compiler_params=_cp("parallel", "parallel"),
    )(y, h0, gn_g)


def _ret_post_bwd(do, y, h0, gn_g):
    S = y.shape[0]

    def body(do_ref, y_ref, gate_ref, g_ref, dy_ref, dgate_ref, dg_ref):
        t = pl.program_id(1)
        yn, rstd = _gn_stats(y_ref[...])
        gate, g, dov = gate_ref[...], g_ref[...], do_ref[...]
        sg = _sigmoid(gate)
        dgate_ref[...] = (dov * (yn * g) * (sg * (1.0 + gate * (1.0 - sg)))).astype(BF16)
        dyg = dov * (gate * sg)
        dyn = dyg * g
        m1 = jnp.mean(dyn, -1, keepdims=True)
        m2 = jnp.mean(dyn * yn, -1, keepdims=True)
        dy_ref[...] = (rstd * (dyn - m1 - yn * m2)).astype(BF16)
        pg = jnp.sum(dyg * yn, axis=0, keepdims=True)

        @pl.when(t == 0)
        def _():
            dg_ref[...] = pg

        @pl.when(t > 0)
        def _():
            dg_ref[...] += pg

    vs = lambda off: pl.BlockSpec((RT, RET_DV), lambda h, t: (t, off + h))
    vec = pl.BlockSpec((1, RET_DV), lambda h, t: (0, h))
    return pl.pallas_call(
        body, name="ret_post_bwd", grid=(RET_HEADS, S // RT), in_specs=[vs(0), vs(0), vs(2 * RET_HEADS), vec],
        out_specs=(vs(0), vs(0), vec),
        out_shape=(SDS((S, 4096), BF16), SDS((S, 4096), BF16), SDS((1, 4096), F32)),
        compiler_params=_cp("parallel", "arbitrary"),
    )(do, y, h0, gn_g)


def _conv_taps(x_ref, halo_ref, w_ref, ext_scr, t):
    ext_scr[0:8, :] = jnp.where(t == 0, 0.0, halo_ref[...])
    ext_scr[8:, :] = x_ref[...]
    w = w_ref[...]
    n = x_ref.shape[0]
    acc = w[3:4, :] * ext_scr[8:, :]
    for j in range(3):
        acc = acc + w[j:j + 1, :] * ext_scr[pl.ds(5 + j, n), :]
    return acc


def _gdn_conv_fwd(h1, conv_w, kind):
    S = h1.shape[0]
    base = {"q": 0, "k": 1, "v": 2}[kind]
    ncb = 2 if kind == "v" else 1
    C = 2048

    def body(x_ref, halo_ref, w_ref, o_ref, ext_scr):
        acc = _conv_taps(x_ref, halo_ref, w_ref, ext_scr, pl.program_id(0))
        c = acc * _sigmoid(acc)
        if kind == "v":
            o_ref[...] = c.astype(BF16)
        else:
            scale = GDN_D ** -0.5 if kind == "q" else 1.0
            for hh in range(C // GDN_D):
                ch = c[:, hh * GDN_D:(hh + 1) * GDN_D]
                r = lax.rsqrt(jnp.sum(ch * ch, -1, keepdims=True) + L2_EPS)
                o_ref[:, hh * GDN_D:(hh + 1) * GDN_D] = (ch * (r * scale)).astype(BF16)

    hb = ROWS // 8
    return pl.pallas_call(
        body, name="gdn_conv_fwd_" + kind, grid=(S // ROWS, ncb),
        in_specs=[pl.BlockSpec((ROWS, C), lambda t, j: (t, base + j)),
                  pl.BlockSpec((8, C), lambda t, j: (jnp.maximum(t * hb - 1, 0), base + j)),
                  pl.BlockSpec((4, C), lambda t, j: (0, base + j))],
        out_specs=pl.BlockSpec((ROWS, C), lambda t, j: (t, j)), out_shape=SDS((S, C * ncb), BF16),
        scratch_shapes=[pltpu.VMEM((ROWS + 8, C), F32)], compiler_params=_cp("parallel", "parallel"),
    )(h1, h1, conv_w)


def _gdn_conv_bwd_act(h1, conv_w, dn, kind):
    S = h1.shape[0]
    base = {"q": 0, "k": 1, "v": 2}[kind]
    ncb = 2 if kind == "v" else 1
    C = 2048

    def body(x_ref, halo_ref, w_ref, dn_ref, o_ref, ext_scr):
        acc = _conv_taps(x_ref, halo_ref, w_ref, ext_scr, pl.program_id(0))
        sg = _sigmoid(acc)
        dsilu = sg * (1.0 + acc * (1.0 - sg))
        if kind == "v":
            o_ref[...] = dn_ref[...] * dsilu
        else:
            c = acc * sg
            scale = GDN_D ** -0.5 if kind == "q" else 1.0
            for hh in range(C // GDN_D):
                sl = slice(hh * GDN_D, (hh + 1) * GDN_D)
                ch, dnh = c[:, sl], dn_ref[:, sl]
                r = lax.rsqrt(jnp.sum(ch * ch, -1, keepdims=True) + L2_EPS)
                proj = jnp.sum(dnh * ch, -1, keepdims=True)
                o_ref[:, sl] = (scale * r) * (dnh - ch * (proj * r * r)) * dsilu[:, sl]

    hb = ROWS // 8
    return pl.pallas_call(
        body, name="gdn_conv_bwd_act_" + kind, grid=(S // ROWS, ncb),
        in_specs=[pl.BlockSpec((ROWS, C), lambda t, j: (t, base + j)),
                  pl.BlockSpec((8, C), lambda t, j: (jnp.maximum(t * hb - 1, 0), base + j)),
                  pl.BlockSpec((4, C), lambda t, j: (0, base + j)),
                  pl.BlockSpec((ROWS, C), lambda t, j: (t, j))],
        out_specs=pl.BlockSpec((ROWS, C), lambda t, j: (t, j)), out_shape=SDS((S, C * ncb), F32),
        scratch_shapes=[pltpu.VMEM((ROWS + 8, C), F32)], compiler_params=_cp("parallel", "parallel"),
    )(h1, h1, conv_w, dn)


def _gdn_conv_bwd_in(h1, conv_w, dacc):
    S = h1.shape[0]
    C = 2048
    nt = S // ROWS
    hb = ROWS // 8

    def body(x_ref, halo_ref, w_ref, d_ref, dhalo_ref, di_ref, dw_ref, ext_scr, dext_scr):
        t = pl.program_id(1)
        ext_scr[0:8, :] = jnp.where(t == 0, 0.0, halo_ref[...])
        ext_scr[8:, :] = x_ref[...]
        d = d_ref[...]
        dext_scr[0:ROWS, :] = d
        dext_scr[ROWS:, :] = jnp.where(t == nt - 1, 0.0, dhalo_ref[...])
        w = w_ref[...]
        di = w[3:4, :] * d
        for j in range(3):
            di = di + w[j:j + 1, :] * dext_scr[pl.ds(3 - j, ROWS), :]
        di_ref[...] = di.astype(BF16)
        rows = [jnp.sum(d * ext_scr[pl.ds(5 + j, ROWS), :], axis=0, keepdims=True) for j in range(4)]
        pw = jnp.concatenate(rows, axis=0)

        @pl.when(t == 0)
        def _():
            dw_ref[...] = pw

        @pl.when(t > 0)
        def _():
            dw_ref[...] += pw

    return pl.pallas_call(
        body, name="gdn_conv_bwd_in", grid=(GDN_QKV // C, nt),
        in_specs=[pl.BlockSpec((ROWS, C), lambda j, t: (t, j)),
                  pl.BlockSpec((8, C), lambda j, t: (jnp.maximum(t * hb - 1, 0), j)),
                  pl.BlockSpec((4, C), lambda j, t: (0, j)),
                  pl.BlockSpec((ROWS, C), lambda j, t: (t, j)),
                  pl.BlockSpec((8, C), lambda j, t: (jnp.minimum((t + 1) * hb, nt * hb - 1), j))],
        out_specs=(pl.BlockSpec((ROWS, C), lambda j, t: (t, j)), pl.BlockSpec((4, C), lambda j, t: (0, j))),
        out_shape=(SDS((S, GDN_QKV), BF16), SDS((4, GDN_QKV), F32)),
        scratch_shapes=[pltpu.VMEM((ROWS + 8, C), F32), pltpu.VMEM((ROWS + 8, C), F32)],
        compiler_params=_cp("parallel", "arbitrary"),
    )(h1, h1, conv_w, dacc, dacc)


def _chunk_masks():
    ri, ci = _iota((CT, CT), 0), _iota((CT, CT), 1)
    same = (ri >> 6) == (ci >> 6)
    return same & (ri >= ci), same & (ri > ci), ri == ci


def _softplus(x):
    return jnp.maximum(x, 0.0) + jnp.log(1.0 + jnp.exp(-jnp.abs(x)))


def _gdn_scal_fwd(ba, a_log, dt_bias):
    S = ba.shape[0]

    def body(ba_ref, al_ref, dt_ref, beta_ref, g_ref, gc_ref):
        bav = ba_ref[...]
        beta_ref[...] = _sigmoid(bav)
        a = pltpu.roll(bav, LANES - GDN_HV, axis=1)
        g = -jnp.exp(al_ref[...]) * _softplus(a + dt_ref[...])
        g_ref[...] = g
        causal, _, _ = _chunk_masks()
        gc_ref[...] = _dot_exact_l(causal.astype(BF16), g)

    row = pl.BlockSpec((CT, LANES), lambda t: (t, 0))
    vec = pl.BlockSpec((1, LANES), lambda t: (0, 0))
    return pl.pallas_call(
        body, name="gdn_scal_fwd", grid=(S // CT,), in_specs=[row, vec, vec], out_specs=(row, row, row),
        out_shape=(SDS((S, LANES), F32),) * 3, compiler_params=_cp("parallel"),
    )(ba, a_log, dt_bias)


def _gdn_scal_bwd(ba, a_log, dt_bias, g, dbeta, dg):
    S = ba.shape[0]

    def body(ba_ref, al_ref, dt_ref, g_ref, dbeta_ref, dg_ref, dba_ref, dal_ref, ddt_ref):
        t = pl.program_id(0)
        bav = ba_ref[...]
        beta = _sigmoid(bav)
        db = dbeta_ref[...] * beta * (1.0 - beta)
        a = pltpu.roll(bav, LANES - GDN_HV, axis=1)
        dgv = dg_ref[...]
        da = dgv * (-jnp.exp(al_ref[...])) * _sigmoid(a + dt_ref[...])
        lane = _iota(bav.shape, 1)
        da_sh = pltpu.roll(da, GDN_HV, axis=1)
        dba = jnp.where(lane < GDN_HV, db, jnp.where(lane < 2 * GDN_HV, da_sh, 0.0))
        dba_ref[...] = dba.astype(BF16)
        keep = lane < GDN_HV
        pal = jnp.sum(jnp.where(keep, dgv * g_ref[...], 0.0), axis=0, keepdims=True)
        pdt = jnp.sum(jnp.where(keep, da, 0.0), axis=0, keepdims=True)

        @pl.when(t == 0)
        def _():
            dal_ref[...] = pal
            ddt_ref[...] = pdt

        @pl.when(t > 0)
        def _():
            dal_ref[...] += pal
            ddt_ref[...] += pdt

    row = pl.BlockSpec((CT, LANES), lambda t: (t, 0))
    vec = pl.BlockSpec((1, LANES), lambda t: (0, 0))
    return pl.pallas_call(
        body, name="gdn_scal_bwd", grid=(S // CT,), in_specs=[row, vec, vec, row, row, row],
        out_specs=(row, vec, vec), out_shape=(SDS((S, LANES), BF16), SDS((1, LANES), F32), SDS((1, LANES), F32)),
        compiler_params=_cp("arbitrary"),
    )(ba, a_log, dt_bias, g, dbeta, dg)


def _sel_col(x, h):
    return jnp.sum(jnp.where(_iota(x.shape, 1) == h, x, 0.0), axis=1, keepdims=True)


def _decay(gcol, causal):
    gm = jnp.broadcast_to(gcol, (CT, CT))
    diff = gm - gm.T
    return jnp.where(causal, jnp.exp(jnp.where(causal, diff, 0.0)), 0.0)


def _gdn_chunk_fwd(k, v, beta, gc):
    S = k.shape[0]
    nt = S // CT

    def body(k_ref, v_ref, beta_ref, gc_ref, t_ref, u_ref, w_ref):
        h = pl.program_id(1)
        bcol, gcol = _sel_col(beta_ref[...], h), _sel_col(gc_ref[...], h)
        causal, strict, eye = _chunk_masks()
        dm = _decay(gcol, causal)
        kv = k_ref[...]
        kb = kv.astype(F32) * bcol
        a = jnp.where(strict, _dot_nt(kb.astype(BF16), kv) * dm, 0.0)
        xp = -a
        tm = jnp.where(eye, 1.0, 0.0) + xp
        for _ in range(5):
            xp = _dotx3(xp, xp)
            tm = tm + _dotx3(tm, xp)
        t_ref[...] = tm
        tb = tm.astype(BF16)
        u_ref[...] = _dot(tb, (v_ref[...].astype(F32) * bcol).astype(BF16))
        w_ref[...] = _dot(tb, (kb * jnp.exp(gcol)).astype(BF16)).astype(BF16)

    col = pl.BlockSpec((CT, LANES), lambda t, h: (t, 0))
    hv = pl.BlockSpec((CT, GDN_D), lambda t, h: (t, h))
    return pl.pallas_call(
        body, name="gdn_chunk_fwd", grid=(nt, GDN_HV),
        in_specs=[pl.BlockSpec((CT, GDN_D), lambda t, h: (t, h // 2)), hv, col, col],
        out_specs=(pl.BlockSpec((None, None, CT, CT), lambda t, h: (h, t, 0, 0)), hv, hv),
        out_shape=(SDS((GDN_HV, nt, CT, CT), F32), SDS((S, 4096), F32), SDS((S, 4096), BF16)),
        compiler_params=_cp("parallel", "parallel"),
    )(k, v, beta, gc)


def _last_of_chunk(gcol, rows, c):
    return jnp.sum(jnp.where(rows == c * CHUNK + CHUNK - 1, gcol, 0.0), axis=0, keepdims=True)


def _gdn_scan_fwd(q, k, u, w, gc):
    S = q.shape[0]
    nt = S // CT
    ncs = CT // CHUNK

    def body(q_ref, k_ref, u_ref, w_ref, gc_ref, y_ref, vn_ref, st_ref, s_scr, vn_scr):
        h, t = pl.program_id(0), pl.program_id(1)

        @pl.when(t == 0)
        def _():
            s_scr[...] = jnp.zeros_like(s_scr)

        gcol = _sel_col(gc_ref[...], h)
        causal, _, _ = _chunk_masks()
        qv, kv = q_ref[...], k_ref[...]
        attn = (_dot_nt(qv, kv) * _decay(gcol, causal)).astype(BF16)
        qd = (qv.astype(F32) * jnp.exp(gcol)).astype(BF16)
        kf = kv.astype(F32)
        rows = _iota((CT, 1), 0)
        vn_scr[...] = jnp.zeros_like(vn_scr)
        for c in range(ncs):
            r = slice(c * CHUNK, (c + 1) * CHUNK)
            s = s_scr[...]
            st_ref[c] = s
            sb = s.astype(BF16)
            gl = _last_of_chunk(gcol, rows, c)
            kd = (kf[r] * jnp.exp(gl - gcol[r])).astype(BF16)
            vn = (u_ref[r, :] - _dot(w_ref[r, :], sb)).astype(BF16)
            vn_scr[r, :] = vn
            y_ref[r, :] = _dot(qd[r], sb) + _dot(attn[r], vn_scr[...])
            s_scr[...] = s * jnp.exp(gl) + _dot_tn(kd, vn)
        vn_ref[...] = vn_scr[...]

    hk = pl.BlockSpec((CT, GDN_D), lambda h, t: (t, h // 2))
    hv = pl.BlockSpec((CT, GDN_D), lambda h, t: (t, h))
    col = pl.BlockSpec((CT, LANES), lambda h, t: (t, 0))
    return pl.pallas_call(
        body, name="gdn_scan_fwd", grid=(GDN_HV, nt), in_specs=[hk, hk, hv, hv, col],
        out_specs=(hv, hv, pl.BlockSpec((None, ncs, GDN_D, GDN_D), lambda h, t: (h, t, 0, 0))),
        out_shape=(SDS((S, 4096), F32), SDS((S, 4096), BF16), SDS((GDN_HV, S // CHUNK, GDN_D, GDN_D), F32)),
        scratch_shapes=[pltpu.VMEM((GDN_D, GDN_D), F32), pltpu.VMEM((CT, GDN_D), BF16)],
        compiler_params=_cp("parallel", "arbitrary"),
    )(q, k, u, w, gc)


def _gdn_scan_bwd(q, k, w, vn, gc, states, dy):
    S = q.shape[0]
    nt = S // CT
    ncs = CT // CHUNK

    def body(q_ref, k_ref, w_ref, vn_ref, gc_ref, st_ref, dy_ref, du_ref, dw_ref, dq_ref, dk_ref, dgc_ref, ds_scr):
        h, t = pl.program_id(0), pl.program_id(1)

        @pl.when(t == 0)
        def _():
            ds_scr[...] = jnp.zeros_like(ds_scr)

        gcol = _sel_col(gc_ref[...], h)
        causal, _, eye = _chunk_masks()
        dm = _decay(gcol, causal)
        qv, kv, wv, vnv, dyv = q_ref[...], k_ref[...], w_ref[...], vn_ref[...], dy_ref[...]
        qf, kf = qv.astype(F32), kv.astype(F32)
        attn_f = _dot_nt(qv, kv) * dm
        attn = attn_f.astype(BF16)
        egc = jnp.exp(gcol)
        qd_f = qf * egc
        qd = qd_f.astype(BF16)
        dattn = _dot_nt(dyv, vnv)
        at_dy = _dot_tn(attn, dyv)
        rows = _iota((CT, 1), 0)
        dgc_parts = [None] * ncs
        for c in reversed(range(ncs)):
            r = slice(c * CHUNK, (c + 1) * CHUNK)
            s = st_ref[c]
            sb = s.astype(BF16)
            dsn = ds_scr[...]
            dsb = dsn.astype(BF16)
            gl = _last_of_chunk(gcol, rows, c)
            cd = jnp.exp(gl)
            ekd = jnp.exp(gl - gcol[r])
            kd_f = kf[r] * ekd
            dvn = (at_dy[r] + _dot(kd_f.astype(BF16), dsb)).astype(BF16)
            dqd = _dot_nt(dyv[r], sb)
            dkd = _dot_nt(vnv[r], dsb)
            dcd = jnp.sum(jnp.sum(s * dsn, axis=1, keepdims=True), axis=0, keepdims=True)
            ds_scr[...] = dsn * cd + _dot_tn(qd[r], dyv[r]) - _dot_tn(wv[r], dvn)
            du_ref[r, :] = dvn
            dw_ref[r, :] = (-_dot_nt(dvn, sb)).astype(BF16)
            dq_ref[r, :] = dqd * egc[r]
            dk_ref[r, :] = dkd * ekd
            rs_q = jnp.sum(dqd * qd_f[r], axis=1, keepdims=True)
            rs_k = jnp.sum(dkd * kd_f, axis=1, keepdims=True)
            tot = jnp.sum(rs_k, axis=0, keepdims=True) + dcd * cd
            dgc_parts[c] = rs_q - rs_k + jnp.where(rows[r] == c * CHUNK + CHUNK - 1, tot, 0.0)
        dgc = jnp.concatenate(dgc_parts, axis=0)
        dab = (dattn * dm).astype(BF16)
        dq_ref[...] += _dot(dab, kv)
        dk_ref[...] += _dot_tn(dab, qv)
        e1 = dattn * attn_f
        dgc = dgc + jnp.sum(e1, axis=1, keepdims=True) - jnp.sum(e1.T, axis=1, keepdims=True)
        dgc_ref[...] = jnp.sum(jnp.where(eye, jnp.broadcast_to(dgc, (CT, CT)), 0.0), axis=0, keepdims=True)

    rev = lambda t: nt - 1 - t
    hk = pl.BlockSpec((CT, GDN_D), lambda h, t: (rev(t), h // 2))
    hv = pl.BlockSpec((CT, GDN_D), lambda h, t: (rev(t), h))
    col = pl.BlockSpec((CT, LANES), lambda h, t: (rev(t), 0))
    return pl.pallas_call(
        body, name="gdn_scan_bwd", grid=(GDN_HV, nt),
        in_specs=[hk, hk, hv, hv, col, pl.BlockSpec((None, ncs, GDN_D, GDN_D), lambda h, t: (h, rev(t), 0, 0)), hv],
        out_specs=(hv, hv, hv, hv, pl.BlockSpec((None, 1, CT), lambda h, t: (h, 0, rev(t)))),
        out_shape=(SDS((S, 4096), BF16), SDS((S, 4096), BF16), SDS((S, 4096), F32), SDS((S, 4096), F32),
                   SDS((GDN_HV, 1, S), F32)),
        scratch_shapes=[pltpu.VMEM((GDN_D, GDN_D), F32)], compiler_params=_cp("parallel", "arbitrary"),
    )(q, k, w, vn, gc, states, dy)


def _gdn_chunk_bwd(k, v, beta, gc, tmat, du, dw, dq_p, dk_p, dgc_p):
    S = k.shape[0]
    nt = S // CT

    def body(k_ref, v_ref, beta_ref, gc_ref, t_ref, du_ref, dw_ref, dqp_ref, dkp_ref, dgcp_ref,
             dq_ref, dk_ref, dv_ref, dbeta_ref, dg_ref):
        h = pl.program_id(1)
        bcol, gcol = _sel_col(beta_ref[...], h), _sel_col(gc_ref[...], h)
        causal, strict, eye = _chunk_masks()
        dm = _decay(gcol, causal)
        kv, vv = k_ref[...], v_ref[...]
        kf, vf = kv.astype(F32), vv.astype(F32)
        kb = kf * bcol
        kbb = kb.astype(BF16)
        a = jnp.where(strict, _dot_nt(kbb, kv) * dm, 0.0)
        egc = jnp.exp(gcol)
        kg_f = kb * egc
        tb = t_ref[...].astype(BF16)
        duv, dwv = du_ref[...], dw_ref[...]
        dt = _dot_nt(duv, (vf * bcol).astype(BF16)) + _dot_nt(dwv, kg_f.astype(BF16))
        dvb = _dot_tn(tb, duv)
        dkg = _dot_tn(tb, dwv)
        da = -_dot_nt(_dot_tn(tb, dt.astype(BF16)).astype(BF16), tb)
        rm = jnp.where(strict, da, 0.0)
        rdb = (rm * dm).astype(BF16)
        dkb = _dot(rdb, kv) + dkg * egc
        dk = _dot_tn(rdb, kbb) + dkp_ref[...] + dkb * bcol
        e2 = rm * a
        dgc_row = dgcp_ref[...]
        dgc_in = jnp.sum(jnp.where(eye, jnp.broadcast_to(dgc_row, (CT, CT)), 0.0), axis=1, keepdims=True)
        dgc = (jnp.sum(e2, axis=1, keepdims=True) - jnp.sum(e2.T, axis=1, keepdims=True)
               + jnp.sum(dkg * kg_f, axis=1, keepdims=True) + dgc_in)
        dbeta = jnp.sum(dkb * kf, axis=1, keepdims=True) + jnp.sum(dvb * vf, axis=1, keepdims=True)
        dv_ref[...] = dvb * bcol
        upper = causal.T
        dgw = _dot_exact_l(upper.astype(BF16), jnp.broadcast_to(dgc, (CT, LANES)))
        lane = _iota((CT, LANES), 1)
        dq = dqp_ref[...]

        @pl.when(h % 2 == 0)
        def _():
            dq_ref[...] = dq
            dk_ref[...] = dk

        @pl.when(h % 2 == 1)
        def _():
            dq_ref[...] += dq
            dk_ref[...] += dk

        @pl.when(h == 0)
        def _():
            dbeta_ref[...] = jnp.zeros_like(dbeta_ref)
            dg_ref[...] = jnp.zeros_like(dg_ref)

        dbeta_ref[...] += jnp.where(lane == h, dbeta, 0.0)
        dg_ref[...] += jnp.where(lane == h, dgw, 0.0)

    hk = pl.BlockSpec((CT, GDN_D), lambda t, h: (t, h // 2))
    hv = pl.BlockSpec((CT, GDN_D), lambda t, h: (t, h))
    col = pl.BlockSpec((CT, LANES), lambda t, h: (t, 0))
    return pl.pallas_call(
        body, name="gdn_chunk_bwd", grid=(nt, GDN_HV),
        in_specs=[hk, hv, col, col, pl.BlockSpec((None, None, CT, CT), lambda t, h: (h, t, 0, 0)), hv, hv, hv, hv,
                  pl.BlockSpec((None, 1, CT), lambda t, h: (h, 0, t))],
        out_specs=(hk, hk, hv, col, col),
        out_shape=(SDS((S, 2048), F32), SDS((S, 2048), F32), SDS((S, 4096), F32), SDS((S, LANES), F32),
                   SDS((S, LANES), F32)),
        compiler_params=_cp("parallel", "arbitrary"),
    )(k, v, beta, gc, tmat, du, dw, dq_p, dk_p, dgc_p)


def _gdn_post_fwd(y, h1, norm_g):
    S = y.shape[0]

    def body(y_ref, z_ref, g_ref, o_ref):
        g = g_ref[...]
        for hh in range(GDN_HV):
            sl = slice(hh * GDN_D, (hh + 1) * GDN_D)
            yh, zh = y_ref[:, sl], z_ref[:, sl]
            yn = yh * lax.rsqrt(jnp.mean(yh * yh, -1, keepdims=True) + RMS_EPS)
            o_ref[:, sl] = (yn * g * (zh * _sigmoid(zh))).astype(BF16)

    row = lambda off: pl.BlockSpec((ROWS, 4096), lambda t: (t, off))
    return pl.pallas_call(
        body, name="gdn_post_fwd", grid=(S // ROWS,),
        in_specs=[row(0), row(2), pl.BlockSpec((1, GDN_D), lambda t: (0, 0))], out_specs=row(0),
        out_shape=SDS((S, 4096), BF16), compiler_params=_cp("parallel"),
    )(y, h1, norm_g)


def _gdn_post_bwd(do, y, h1, norm_g):
    S = y.shape[0]

    def body(do_ref, y_ref, z_ref, g_ref, dy_ref, dz_ref, dg_ref):
        t = pl.program_id(0)
        g = g_ref[...]
        pg = jnp.zeros((1, GDN_D), F32)
        for hh in range(GDN_HV):
            sl = slice(hh * GDN_D, (hh + 1) * GDN_D)
            yh, zh, doh = y_ref[:, sl], z_ref[:, sl], do_ref[:, sl]
            rstd = lax.rsqrt(jnp.mean(yh * yh, -1, keepdims=True) + RMS_EPS)
            yn = yh * rstd
            sg = _sigmoid(zh)
            dz_ref[:, sl] = (doh * (yn * g) * (sg * (1.0 + zh * (1.0 - sg)))).astype(BF16)
            dyg = doh * (zh * sg)
            dyn = dyg * g
            dy_ref[:, sl] = (rstd * (dyn - yn * jnp.mean(dyn * yn, -1, keepdims=True))).astype(BF16)
            pg = pg + jnp.sum(dyg * yn, axis=0, keepdims=True)

        @pl.when(t == 0)
        def _():
            dg_ref[...] = pg

        @pl.when(t > 0)
        def _():
            dg_ref[...] += pg

    row = lambda off: pl.BlockSpec((ROWS, 4096), lambda t: (t, off))
    vec = pl.BlockSpec((1, GDN_D), lambda t: (0, 0))
    return pl.pallas_call(
        body, name="gdn_post_bwd", grid=(S // ROWS,), in_specs=[row(0), row(0), row(2), vec],
        out_specs=(row(0), row(0), vec),
        out_shape=(SDS((S, 4096), BF16), SDS((S, 4096), BF16), SDS((1, GDN_D), F32)),
        compiler_params=_cp("arbitrary"),
    )(do, y, h1, norm_g)


def _local_step(x, tgt, wri, wro, wgi_main, wba, wgo, w1, w2, conv_w, small):
    S = x.shape[0]
    xb = x.astype(BF16)
    rc = _ret_consts()
    cos, sin = _rope_tables(S)

    h0 = _mm(xb, wri, "nn", "mm_ret_in")
    qr, kr, vr = _ret_rot(h0, cos, sin)
    yr, ret_st = _ret_fwd(qr, kr, vr, rc)
    o0 = _ret_post_fwd(yr, h0, small["ret_gn_g"])
    mix0 = _mm(o0, wro, "nn", "mm_ret_out")
    x1, x1b, z1 = _ln_fwd(x, mix0, small["ln_mix_g"][0:1], small["ln_mix_b"][0:1], "ln_mix0_fwd")
    hh0, a0 = _mm(x1b, w1[0], "nn", "mm_mlp0_up", epi="relu2")
    m0 = _mm(a0, w2[0], "nn", "mm_mlp0_down")
    x2, x2b, z2 = _ln_fwd(x1, m0, small["ln_ffn_g"][0:1], small["ln_ffn_b"][0:1], "ln_ffn0_fwd")

    h1 = _mm(x2b, wgi_main, "nn", "mm_gdn_in")
    ba = _mm(x2b, wba, "nn", "mm_gdn_ba")
    qn = _gdn_conv_fwd(h1, conv_w, "q")
    kn = _gdn_conv_fwd(h1, conv_w, "k")
    vg = _gdn_conv_fwd(h1, conv_w, "v")
    beta, g, gc = _gdn_scal_fwd(ba, small["a_log"], small["dt_bias"])
    tmat, u, w = _gdn_chunk_fwd(kn, vg, beta, gc)
    yg, vn, gdn_st = _gdn_scan_fwd(qn, kn, u, w, gc)
    o1 = _gdn_post_fwd(yg, h1, small["norm_g"])
    mix1 = _mm(o1, wgo, "nn", "mm_gdn_out")
    x3, x3b, z3 = _ln_fwd(x2, mix1, small["ln_mix_g"][1:2], small["ln_mix_b"][1:2], "ln_mix1_fwd")
    hh1, a1 = _mm(x3b, w1[1], "nn", "mm_mlp1_up", epi="relu2")
    m1 = _mm(a1, w2[1], "nn", "mm_mlp1_down")
    x4, _, z4 = _ln_fwd(x3, m1, small["ln_ffn_g"][1:2], small["ln_ffn_b"][1:2], "ln_ffn1_fwd")

    dx4, loss = _loss_fwd_bwd(x4, tgt)

    dz4, dz4b, d_lnf_g1, d_lnf_b1 = _ln_bwd(dx4, z4, small["ln_ffn_g"][1:2], "ln_ffn1_bwd")
    dhh1 = _mm(dz4b, w2[1], "nt", "mm_mlp1_down_dx", epi="drelu2", extra=hh1, out_dtype=BF16)
    dw2_1 = _mm(a1, dz4b, "tn", "mm_mlp1_down_dw")
    dx3 = _mm(dhh1, w1[1], "nt", "mm_mlp1_up_dx", epi="add", extra=dz4, scale=ALPHA)
    dw1_1 = _mm(x3b, dhh1, "tn", "mm_mlp1_up_dw", shard_major=True)
    dz3, dz3b, d_lnm_g1, d_lnm_b1 = _ln_bwd(dx3, z3, small["ln_mix_g"][1:2], "ln_mix1_bwd")
    do1 = _mm(dz3b, wgo, "nt", "mm_gdn_out_dx")
    dwgo = _mm(o1, dz3b, "tn", "mm_gdn_out_dw")
    dyg, dzg, d_norm_g = _gdn_post_bwd(do1, yg, h1, small["norm_g"])
    du, dw, dq_p, dk_p, dgc_p = _gdn_scan_bwd(qn, kn, w, vn, gc, gdn_st, dyg)
    dqn, dkn, dvg, dbeta, dg = _gdn_chunk_bwd(kn, vg, beta, gc, tmat, du, dw, dq_p, dk_p, dgc_p)
    dba, d_a_log, d_dt_bias = _gdn_scal_bwd(ba, small["a_log"], small["dt_bias"], g, dbeta, dg)
    dacc = jnp.concatenate([_gdn_conv_bwd_act(h1, conv_w, dqn, "q"), _gdn_conv_bwd_act(h1, conv_w, dkn, "k"),
                            _gdn_conv_bwd_act(h1, conv_w, dvg, "v")], axis=1)
    dqkv, d_conv_w = _gdn_conv_bwd_in(h1, conv_w, dacc)
    dh1 = jnp.concatenate([dqkv, dzg], axis=1)
    dx2_ba = _mm(dba, wba, "nt", "mm_gdn_ba_dx", epi="add", extra=dz3, scale=ALPHA)
    dx2 = _mm(dh1, wgi_main, "nt", "mm_gdn_in_dx", epi="add", extra=dx2_ba)
    dwgi_main = _mm(x2b, dh1, "tn", "mm_gdn_in_dw")
    dwba = _mm(x2b, dba, "tn", "mm_gdn_ba_dw")

    dz2, dz2b, d_lnf_g0, d_lnf_b0 = _ln_bwd(dx2, z2, small["ln_ffn_g"][0:1], "ln_ffn0_bwd")
    dhh0 = _mm(dz2b, w2[0], "nt", "mm_mlp0_down_dx", epi="drelu2", extra=hh0, out_dtype=BF16)
    dw2_0 = _mm(a0, dz2b, "tn", "mm_mlp0_down_dw")
    dx1 = _mm(dhh0, w1[0], "nt", "mm_mlp0_up_dx", epi="add", extra=dz2, scale=ALPHA)
    dw1_0 = _mm(x1b, dhh0, "tn", "mm_mlp0_up_dw", shard_major=True)
    dz1, dz1b, d_lnm_g0, d_lnm_b0 = _ln_bwd(dx1, z1, small["ln_mix_g"][0:1], "ln_mix0_bwd")
    do0 = _mm(dz1b, wro, "nt", "mm_ret_out_dx")
    dwro = _mm(o0, dz1b, "tn", "mm_ret_out_dw")
    dyr, dgate, d_gn_g = _ret_post_bwd(do0, yr, h0, small["ret_gn_g"])
    dq0, dk0, dv0 = _ret_bwd(qr, kr, vr, dyr, ret_st, rc, cos, sin)
    dh0 = jnp.concatenate([dq0, dk0, dv0, dgate], axis=1)
    grad_x = _mm(dh0, wri, "nt", "mm_ret_in_dx", epi="add", extra=dz1, scale=ALPHA)
    dwri = _mm(xb, dh0, "tn", "mm_ret_in_dw", shard_major=True)

    big = dict(ret_w_in=dwri, ret_w_out=dwro, gdn_w_in_main=dwgi_main, gdn_w_in_ba=dwba, gdn_w_out=dwgo,
               mlp_w1=(dw1_0, dw1_1), mlp_w2=(dw2_0, dw2_1))
    sm = dict(ret_gn_g=d_gn_g, a_log=d_a_log, dt_bias=d_dt_bias, norm_g=d_norm_g,
              ln_mix_g=jnp.concatenate([d_lnm_g0, d_lnm_g1], 0), ln_mix_b=jnp.concatenate([d_lnm_b0, d_lnm_b1], 0),
              ln_ffn_g=jnp.concatenate([d_lnf_g0, d_lnf_g1], 0), ln_ffn_b=jnp.concatenate([d_lnf_b0, d_lnf_b1], 0),
              conv_w=d_conv_w)
    return loss, grad_x, big, sm


def _coords():
    return lax.axis_index("x"), lax.axis_index("y"), lax.axis_index("c")


HBM_SPEC = pl.BlockSpec(memory_space=pl.ANY)


def _all_gather_chips(shards, split):
    n = len(shards)
    n_sem = sum(6 if s else 3 for s in split)

    def body(*refs):
        ins, outs = refs[:n], refs[n:2 * n]
        send_sems, recv_sems, local_sems = refs[2 * n:]
        x, y, c = _coords()
        chips = [(1 - x, y), (x, 1 - y), (1 - x, 1 - y)]
        sibling = (x, y, 1 - c)
        me = 2 * x + y

        def piece(ref, p, core):
            if not split[p]:
                return ref
            half = shards[p].shape[0] // 2
            return ref.at[pl.ds(core * half, half)]

        def rcopy(k, src, dst, to):
            return pltpu.make_async_remote_copy(src_ref=src, dst_ref=dst, send_sem=send_sems.at[k],
                                                recv_sem=recv_sems.at[k], device_id=to, device_id_type=MESH)

        local = [pltpu.make_async_copy(ins[p], outs[p].at[me], local_sems.at[p]) for p in range(n)]
        for cp in local:
            cp.start()
        base, k = [], 0
        for p in range(n):
            base.append(k)
            k += 6 if split[p] else 3
        first = []
        for p in range(n):
            for j, chip in enumerate(chips):
                first.append(rcopy(base[p] + j, piece(ins[p], p, c), piece(outs[p].at[me], p, c), (*chip, c)))
        for cp in first:
            cp.start()
        passed = []
        for p in range(n):
            for j, chip in enumerate(chips):
                zone = piece(outs[p].at[2 * chip[0] + chip[1]], p, c)
                rcopy(base[p] + j, zone, zone, sibling).wait_recv()
                if split[p]:
                    fwd = rcopy(base[p] + 3 + j, zone, zone, sibling)
                    fwd.start()
                    passed.append(fwd)
        for p in range(n):
            if split[p]:
                for j, chip in enumerate(chips):
                    zone = piece(outs[p].at[2 * chip[0] + chip[1]], p, 1 - c)
                    rcopy(base[p] + 3 + j, zone, zone, sibling).wait_recv()
        for cp in first + passed:
            cp.wait_send()
        for cp in local:
            cp.wait()

    return pl.pallas_call(
        body, name="all_gather_weights", in_specs=[HBM_SPEC] * n, out_specs=[HBM_SPEC] * n,
        out_shape=[SDS((N_CHIPS,) + s.shape, s.dtype) for s in shards],
        scratch_shapes=[pltpu.SemaphoreType.DMA((n_sem,)), pltpu.SemaphoreType.DMA((n_sem,)),
                        pltpu.SemaphoreType.DMA((n,))],
    )(*shards)


def _exchange(name, arrays, out_shapes, plan):
    n = len(arrays)
    x0 = jnp.zeros((), jnp.int32)
    n_remote, n_local = (len(l) for l in plan((x0, x0, x0), [None] * n, [None] * n, dry=True))

    def body(*refs):
        ins, outs = refs[:n], refs[n:2 * n]
        send_sems, recv_sems, local_sems = refs[2 * n:]
        remote, local = plan(_coords(), ins, outs, dry=False)
        lcs = [pltpu.make_async_copy(src, dst, local_sems.at[i]) for i, (src, dst) in enumerate(local)]
        for cp in lcs:
            cp.start()
        rcs = [pltpu.make_async_remote_copy(src_ref=src, dst_ref=dst, send_sem=send_sems.at[i],
                                            recv_sem=recv_sems.at[i], device_id=to, device_id_type=MESH)
               for i, (src, dst, to) in enumerate(remote)]
        for cp in rcs:
            cp.start()
        for cp in rcs:
            cp.wait_recv()
        for cp in rcs:
            cp.wait_send()
        for cp in lcs:
            cp.wait()

    return pl.pallas_call(
        body, name=name, in_specs=[HBM_SPEC] * n, out_specs=[HBM_SPEC] * n, out_shape=out_shapes,
        scratch_shapes=[pltpu.SemaphoreType.DMA((n_remote,)), pltpu.SemaphoreType.DMA((n_remote,)),
                        pltpu.SemaphoreType.DMA((max(n_local, 1),))],
    )(*arrays)


def _reduce_scatter(grads):
    n = len(grads)
    halves = [g.shape[1] // 2 for g in grads]

    def plan_a(xyc, ins, outs, dry):
        x, y, c = xyc
        return [(None if dry else ins[p].at[:, pl.ds((1 - c) * halves[p], halves[p])], None if dry else outs[p],
                 (x, y, 1 - c)) for p in range(n)], []

    theirs = _exchange("rs_swap_halves", grads, [SDS((N_CHIPS, halves[p]) + g.shape[2:], F32)
                                                 for p, g in enumerate(grads)], plan_a)
    c = lax.axis_index("c")
    chip_sums = [_add_half(g, t, c, "rs_add_%d" % p) for p, (g, t) in enumerate(zip(grads, theirs))]

    def plan_c(xyc, ins, outs, dry):
        x, y, c = xyc
        me = 2 * x + y
        remote, local = [], []
        for p in range(n):
            for chip in [(1 - x, y), (x, 1 - y), (1 - x, 1 - y)]:
                remote.append((None if dry else ins[p].at[2 * chip[0] + chip[1]], None if dry else outs[p].at[me],
                               (*chip, c)))
            local.append((None if dry else ins[p].at[me], None if dry else outs[p].at[me]))
        return remote, local

    parts = _exchange("rs_to_owner", chip_sums, [SDS(s.shape, F32) for s in chip_sums], plan_c)
    reduced = [_sum_chips(pt, "rs_sum_%d" % p) for p, pt in enumerate(parts)]

    def plan_e(xyc, ins, outs, dry):
        x, y, c = xyc
        remote = [(None if dry else ins[p], None if dry else outs[p].at[pl.ds(c * halves[p], halves[p])], (x, y, 1 - c))
                  for p in range(n)]
        local = [(None if dry else ins[p], None if dry else outs[p].at[pl.ds(c * halves[p], halves[p])])
                 for p in range(n)]
        return remote, local

    return _exchange("rs_swap_reduced", reduced, [SDS(g.shape[1:], F32) for g in grads], plan_e)


def _add_half(g, theirs, c, name):
    _, R, C = g.shape
    half = R // 2
    tr = min(256, half)
    nb = half // tr

    def body(c_ref, g_ref, t_ref, o_ref):
        o_ref[...] = g_ref[...] + t_ref[...]

    blk = pl.BlockSpec((None, tr, C), lambda s, i, c_ref: (s, i, 0))
    return pl.pallas_call(
        body, name=name,
        grid_spec=pltpu.PrefetchScalarGridSpec(
            num_scalar_prefetch=1, grid=(N_CHIPS, nb),
            in_specs=[pl.BlockSpec((None, tr, C), lambda s, i, c_ref: (s, c_ref[0] * nb + i, 0)), blk],
            out_specs=blk),
        out_shape=SDS((N_CHIPS, half, C), F32), compiler_params=_cp("parallel", "parallel"),
    )(jnp.reshape(c, (1,)).astype(jnp.int32), g, theirs)


def _sum_chips(parts, name):
    _, r, C = parts.shape
    tr = min(256, r)

    def body(p_ref, o_ref):
        o_ref[...] = ((p_ref[0] + p_ref[1]) + p_ref[2]) + p_ref[3]

    return pl.pallas_call(
        body, name=name, grid=(r // tr,), in_specs=[pl.BlockSpec((N_CHIPS, tr, C), lambda i: (0, i, 0))],
        out_specs=pl.BlockSpec((tr, C), lambda i: (i, 0)), out_shape=SDS((r, C), F32), compiler_params=_cp("parallel"),
    )(parts)


def _all_reduce_small(buf):
    rows = buf.shape[0]

    def body(x_ref, o_ref, all_ref, send_sems, recv_sems):
        x, y, c = _coords()
        me = 4 * x + 2 * y + c
        all_ref[me] = x_ref[...]
        flips = [(fx, fy, fc) for fx in (0, 1) for fy in (0, 1) for fc in (0, 1)][1:]
        copies = []
        for k, (fx, fy, fc) in enumerate(flips):
            to = (x ^ fx, y ^ fy, c ^ fc)
            copies.append(pltpu.make_async_remote_copy(src_ref=x_ref, dst_ref=all_ref.at[me], send_sem=send_sems.at[k],
                                                       recv_sem=recv_sems.at[k], device_id=to, device_id_type=MESH))
        for cp in copies:
            cp.start()
        for cp in copies:
            cp.wait_recv()
        for cp in copies:
            cp.wait_send()
        acc = all_ref[0]
        for d in range(1, N_DEV):
            acc = acc + all_ref[d]
        o_ref[...] = acc

    vm = pl.BlockSpec(memory_space=pltpu.VMEM)
    return pl.pallas_call(
        body, name="all_reduce_small", in_specs=[vm], out_specs=vm, out_shape=SDS((rows, LANES), F32),
        scratch_shapes=[pltpu.VMEM((N_DEV, rows, LANES), F32), pltpu.SemaphoreType.DMA((N_DEV - 1,)),
                        pltpu.SemaphoreType.DMA((N_DEV - 1,))],
    )(buf)


def _adamw(w, g, m, v, name):
    R, C = w.shape
    tr = min(256, R)
    assert R % tr == 0

    def body(w_ref, g_ref, m_ref, v_ref, d_ref, mo_ref, vo_ref):
        gv = g_ref[...]
        mn = ADAM_B1 * m_ref[...] + (1.0 - ADAM_B1) * gv
        vn = ADAM_B2 * v_ref[...] + (1.0 - ADAM_B2) * (gv * gv)
        m_hat = mn / (1.0 - ADAM_B1 ** ADAM_STEP)
        v_hat = vn / (1.0 - ADAM_B2 ** ADAM_STEP)
        d_ref[...] = -ADAM_LR * (m_hat / (jnp.sqrt(v_hat) + ADAM_EPS) + ADAM_WD * w_ref[...])
        mo_ref[...] = mn
        vo_ref[...] = vn

    blk = pl.BlockSpec((tr, C), lambda i: (i, 0))
    return pl.pallas_call(
        body, name=name, grid=(R // tr,), in_specs=[blk] * 4, out_specs=(blk,) * 3,
        out_shape=(SDS((R, C), F32),) * 3, compiler_params=_cp("parallel"),
    )(w, g, m, v)


def _pack(arrs):
    rows = []
    for a in arrs:
        flat = a.reshape(-1).astype(F32)
        pad = (-flat.shape[0]) % LANES
        rows.append(jnp.pad(flat, (0, pad)).reshape(-1, LANES))
    buf = jnp.concatenate(rows, axis=0)
    pad_rows = (-buf.shape[0]) % 8
    return jnp.pad(buf, ((0, pad_rows), (0, 0)))


def _unpack(buf, shapes):
    out, r = [], 0
    for shp in shapes:
        size = int(np.prod(shp))
        nr = -(-size // LANES)
        out.append(buf[r:r + nr].reshape(-1)[:size].reshape(shp))
        r += nr
    return out


def _pad_lanes(a):
    return jnp.pad(a, ((0, 0), (0, LANES - a.shape[1])))


def kernel(x, ret_w_in, ret_gn_g, ret_w_out, gdn_w_in, gdn_conv_w, gdn_a_log, gdn_dt_bias, gdn_norm_g, gdn_w_out, ln_mix_g, ln_mix_b, mlp_w1, mlp_w2, ln_ffn_g, ln_ffn_b, loss_target, m_ret_w_in, m_ret_gn_g, m_ret_w_out, m_gdn_w_in, m_gdn_conv_w, m_gdn_a_log, m_gdn_dt_bias, m_gdn_norm_g, m_gdn_w_out, m_ln_mix_g, m_ln_mix_b, m_mlp_w1, m_mlp_w2, m_ln_ffn_g, m_ln_ffn_b, v_ret_w_in, v_ret_gn_g, v_ret_w_out, v_gdn_w_in, v_gdn_conv_w, v_gdn_a_log, v_gdn_dt_bias, v_gdn_norm_g, v_gdn_w_out, v_ln_mix_g, v_ln_mix_b, v_mlp_w1, v_mlp_w2, v_ln_ffn_g, v_ln_ffn_b):
    cx, cy = lax.axis_index("x"), lax.axis_index("y")
    chip = 2 * cx + cy

    shards = [ret_w_in[0].astype(BF16), ret_w_out[0].astype(BF16), gdn_w_in[0].astype(BF16),
              gdn_w_out[0].astype(BF16), mlp_w1.astype(BF16), mlp_w2.astype(BF16), gdn_conv_w[0]]
    g_ri, g_ro, g_gi, g_go, g_w1, g_w2, g_cv = _all_gather_chips(shards, [True] * 6 + [False])
    cols = lambda g: jnp.transpose(g, (1, 0, 2)).reshape(g.shape[1], -1)
    wri = cols(g_ri)
    wro = g_ro.reshape(4096, D_MODEL)
    wgi = cols(g_gi)
    wgi_main = wgi[:, :GDN_QKV + 4096]
    wba = jnp.pad(wgi[:, GDN_QKV + 4096:], ((0, 0), (0, LANES - 2 * GDN_HV)))
    wgo = g_go.reshape(4096, D_MODEL)
    w1 = [cols(g_w1[:, i]) for i in range(2)]
    w2 = [g_w2[:, i].reshape(4 * D_MODEL, D_MODEL) for i in range(2)]
    conv_w = cols(g_cv)
    small = dict(ret_gn_g=ret_gn_g, a_log=_pad_lanes(gdn_a_log), dt_bias=_pad_lanes(gdn_dt_bias), norm_g=gdn_norm_g,
                 ln_mix_g=ln_mix_g, ln_mix_b=ln_mix_b, ln_ffn_g=ln_ffn_g, ln_ffn_b=ln_ffn_b)

    loss, grad_x, big, sm = _local_step(x[0], loss_target[0], wri, wro, wgi_main, wba, wgo, w1, w2, conv_w, small)

    dwgi = jnp.concatenate([big["gdn_w_in_main"], big["gdn_w_in_ba"][:, :2 * GDN_HV]], axis=1)
    shard_major = lambda g: jnp.transpose(g.reshape(g.shape[0], N_CHIPS, -1), (1, 0, 2))
    to_reduce = [big["ret_w_in"], big["ret_w_out"].reshape(N_CHIPS, 1024, D_MODEL), shard_major(dwgi),
                 big["gdn_w_out"].reshape(N_CHIPS, 1024, D_MODEL),
                 jnp.concatenate(big["mlp_w1"], axis=1),
                 jnp.concatenate([g.reshape(N_CHIPS, D_MODEL, D_MODEL) for g in big["mlp_w2"]], axis=1)]
    g_ret_w_in, g_ret_w_out, g_gdn_w_in, g_gdn_w_out, g_mlp_w1, g_mlp_w2 = _reduce_scatter(to_reduce)

    small_names = ["ret_gn_g", "a_log", "dt_bias", "norm_g", "ln_mix_g", "ln_mix_b", "ln_ffn_g", "ln_ffn_b", "conv_w"]
    small_shapes = [(1, 4096), (1, LANES), (1, LANES), (1, GDN_D), (2, D_MODEL), (2, D_MODEL), (2, D_MODEL),
                    (2, D_MODEL), (4, GDN_QKV)]
    red = _all_reduce_small(_pack([loss] + [sm[k] for k in small_names]))
    red_loss, *red_small = _unpack(red, [(1, 1)] + small_shapes)
    gs = dict(zip(small_names, red_small))
    g_conv = lax.dynamic_slice_in_dim(gs["conv_w"], chip * 2048, 2048, axis=1)
    g_a_log, g_dt_bias = gs["a_log"][:, :GDN_HV], gs["dt_bias"][:, :GDN_HV]

    big_w = [(ret_w_in, m_ret_w_in, v_ret_w_in, g_ret_w_in), (ret_w_out, m_ret_w_out, v_ret_w_out, g_ret_w_out),
             (gdn_w_in, m_gdn_w_in, v_gdn_w_in, g_gdn_w_in), (gdn_w_out, m_gdn_w_out, v_gdn_w_out, g_gdn_w_out),
             (mlp_w1, m_mlp_w1, v_mlp_w1, g_mlp_w1), (mlp_w2, m_mlp_w2, v_mlp_w2, g_mlp_w2)]
    big_out = []
    for i, (w_, m_, v_, g_) in enumerate(big_w):
        two_d = lambda a: a.reshape(-1, a.shape[-1])
        d_, nm_, nv_ = _adamw(two_d(w_), g_, two_d(m_), two_d(v_), "adamw_%d" % i)
        big_out.append(tuple(a.reshape(w_.shape) for a in (g_, d_, nm_, nv_)))
    sm_w = [(ret_gn_g, m_ret_gn_g, v_ret_gn_g, gs["ret_gn_g"]), (gdn_conv_w, m_gdn_conv_w, v_gdn_conv_w, g_conv),
            (gdn_a_log, m_gdn_a_log, v_gdn_a_log, g_a_log), (gdn_dt_bias, m_gdn_dt_bias, v_gdn_dt_bias, g_dt_bias),
            (gdn_norm_g, m_gdn_norm_g, v_gdn_norm_g, gs["norm_g"]), (ln_mix_g, m_ln_mix_g, v_ln_mix_g, gs["ln_mix_g"]),
            (ln_mix_b, m_ln_mix_b, v_ln_mix_b, gs["ln_mix_b"]), (ln_ffn_g, m_ln_ffn_g, v_ln_ffn_g, gs["ln_ffn_g"]),
            (ln_ffn_b, m_ln_ffn_b, v_ln_ffn_b, gs["ln_ffn_b"])]
    sm_shapes = [w_.shape for w_, _, _, _ in sm_w]
    d_s, nm_s, nv_s = _adamw(_pack([w_ for w_, _, _, _ in sm_w]), _pack([g_ for _, _, _, g_ in sm_w]),
                             _pack([m_ for _, m_, _, _ in sm_w]), _pack([v_ for _, _, v_, _ in sm_w]), "adamw_small")
    d_s, nm_s, nv_s = (_unpack(a, sm_shapes) for a in (d_s, nm_s, nv_s))
    sm_out = [(g_.reshape(w_.shape), d_s[i], nm_s[i], nv_s[i]) for i, (w_, _, _, g_) in enumerate(sm_w)]

    per_w = [big_out[0], sm_out[0], big_out[1], big_out[2], sm_out[1], sm_out[2], sm_out[3], sm_out[4], big_out[3],
             sm_out[5], sm_out[6], big_out[4], big_out[5], sm_out[7], sm_out[8]]
    outs = [red_loss.reshape(()), grad_x[None]]
    for kind in range(4):
        outs.extend(t[kind] for t in per_w)
    return tuple(outs)
```

```python
import functools

import numpy as np
import jax
import jax.numpy as jnp
from jax import lax
from jax.experimental import pallas as pl
from jax.experimental.pallas import tpu as pltpu

F32 = jnp.float32
BF16 = jnp.bfloat16
MESH = pl.DeviceIdType.MESH
SDS = jax.ShapeDtypeStruct

D_MODEL = 2048
CHUNK = 64
RET_HEADS, RET_DK, RET_DV = 8, 256, 512
GDN_HV, GDN_D = 32, 128
HP = 2
GDN_QKV = 8192
ALPHA = 4.0 ** 0.25
LN_EPS, GN_EPS, RMS_EPS, L2_EPS = 1e-5, 1e-6, 1e-6, 1e-6
ADAM_LR, ADAM_B1, ADAM_B2, ADAM_EPS, ADAM_WD, ADAM_STEP = 0.001, 0.9, 0.999, 1e-8, 0.01, 10

VMEM_LIMIT_BYTES = 56 * 1024 * 1024
RT = 256
CT = 256
ROWS = 256
LANES = 128
N_CHIPS = 4
N_DEV = 8


def _cp(*sem):
    return pltpu.CompilerParams(dimension_semantics=sem, vmem_limit_bytes=VMEM_LIMIT_BYTES)


def _dot(a, b):
    return jnp.dot(a, b, preferred_element_type=F32)


def _dot_nt(a, b):
    return lax.dot_general(a, b, (((1,), (1,)), ((), ())), preferred_element_type=F32)


def _dot_tn(a, b):
    return lax.dot_general(a, b, (((0,), (0,)), ((), ())), preferred_element_type=F32)


def _split2(x):
    hi = x.astype(BF16)
    lo = (x - hi.astype(F32)).astype(BF16)
    return hi, lo


def _dotx3(a, b):
    ah, al = _split2(a)
    bh, bl = _split2(b)
    return _dot(ah, bh) + (_dot(ah, bl) + _dot(al, bh))


def _dot_exact_l(l_bf16, x):
    hi = x.astype(BF16)
    r = x - hi.astype(F32)
    mid = r.astype(BF16)
    lo = (r - mid.astype(F32)).astype(BF16)
    return _dot(l_bf16, hi) + (_dot(l_bf16, mid) + _dot(l_bf16, lo))


def _sigmoid(x):
    return 1.0 / (1.0 + jnp.exp(-x))


def _iota(shape, dim):
    return lax.broadcasted_iota(jnp.int32, shape, dim)


def _mm(a, b, mode, name, *, out_dtype=F32, tm=1024, tn=1024, tk=2048, epi=None, extra=None, scale=1.0,
        shard_major=False):
    if mode == "nn":
        (M, K), (K2, N) = a.shape, b.shape
    elif mode == "nt":
        (M, K), (N, K2) = a.shape, b.shape
    else:
        (K, M), (K2, N) = a.shape, b.shape
    assert K == K2, (a.shape, b.shape, mode)
    tm, tn, tk = min(tm, M), min(tn, N), min(tk, K)
    assert M % tm == 0 and N % tn == 0 and K % tk == 0, (M, N, K, tm, tn, tk)
    nk = K // tk
    dims = {"nn": (((1,), (0,)), ((), ())), "nt": (((1,), (1,)), ((), ())), "tn": (((0,), (0,)), ((), ()))}[mode]
    if mode == "tn":
        a_spec = pl.BlockSpec((tk, tm), lambda i, j, k: (k, i))
    else:
        a_spec = pl.BlockSpec((tm, tk), lambda i, j, k: (i, k))
    if mode == "nt":
        b_spec = pl.BlockSpec((tn, tk), lambda i, j, k: (j, k))
    else:
        b_spec = pl.BlockSpec((tk, tn), lambda i, j, k: (k, j))
    tile = pl.BlockSpec((tm, tn), lambda i, j, k: (i, j))
    in_specs, ins = [a_spec, b_spec], [a, b]
    if extra is not None:
        in_specs.append(tile)
        ins.append(extra)
    if epi == "relu2":
        out_shape = (SDS((M, N), F32), SDS((M, N), BF16))
        out_specs = (tile, tile)
    elif shard_major:
        per = (N // N_CHIPS) // tn
        assert per * tn * N_CHIPS == N
        out_shape = (SDS((N_CHIPS, M, N // N_CHIPS), out_dtype),)
        out_specs = (pl.BlockSpec((None, tm, tn), lambda i, j, k: (j // per, i, j % per)),)
    else:
        out_shape = (SDS((M, N), out_dtype),)
        out_specs = (tile,)
    n_out = len(out_shape)

    def body(*refs):
        a_ref, b_ref = refs[0], refs[1]
        pos = 2
        x_ref = None
        if extra is not None:
            x_ref = refs[pos]
            pos += 1
        o_refs = refs[pos:pos + n_out]
        acc_ref = refs[pos + n_out] if nk > 1 else None

        def prod():
            av, bv = a_ref[...], b_ref[...]
            if av.dtype != BF16:
                av = av.astype(BF16)
            if bv.dtype != BF16:
                bv = bv.astype(BF16)
            return lax.dot_general(av, bv, dims, preferred_element_type=F32)

        def finish(acc):
            if epi == "relu2":
                o_refs[0][...] = acc
                r = jnp.maximum(acc, 0.0)
                o_refs[1][...] = (r * r).astype(BF16)
            elif epi == "drelu2":
                o_refs[0][...] = (acc * (2.0 * jnp.maximum(x_ref[...], 0.0))).astype(out_dtype)
            elif epi == "add":
                o_refs[0][...] = (acc + scale * x_ref[...]).astype(out_dtype)
            else:
                o_refs[0][...] = acc.astype(out_dtype)

        if nk == 1:
            finish(prod())
        else:
            k = pl.program_id(2)

            @pl.when(k == 0)
            def _():
                acc_ref[...] = prod()

            @pl.when(k > 0)
            def _():
                acc_ref[...] += prod()

            @pl.when(k == nk - 1)
            def _():
                finish(acc_ref[...])

    out = pl.pallas_call(
        body, name=name, grid=(M // tm, N // tn, nk), in_specs=in_specs, out_specs=out_specs, out_shape=out_shape,
        scratch_shapes=[pltpu.VMEM((tm, tn), F32)] if nk > 1 else [],
        compiler_params=_cp("parallel", "parallel", "arbitrary"),
    )(*ins)
    return out if n_out > 1 else out[0]


def _ln_stats(z):
    mu = jnp.mean(z, -1, keepdims=True)
    zc = z - mu
    var = jnp.mean(zc * zc, -1, keepdims=True)
    rstd = lax.rsqrt(var + LN_EPS)
    return zc * rstd, rstd


def _ln_fwd(xin, sub, g, b, name):
    S, Dm = xin.shape
    row = pl.BlockSpec((ROWS, Dm), lambda t: (t, 0))
    vec = pl.BlockSpec((1, Dm), lambda t: (0, 0))

    def body(x_ref, s_ref, g_ref, b_ref, o_ref, ob_ref, z_ref):
        z = ALPHA * x_ref[...] + s_ref[...]
        xh, _ = _ln_stats(z)
        o = xh * g_ref[...] + b_ref[...]
        o_ref[...] = o
        ob_ref[...] = o.astype(BF16)
        z_ref[...] = z

    return pl.pallas_call(
        body, name=name, grid=(S // ROWS,), in_specs=[row, row, vec, vec], out_specs=(row, row, row),
        out_shape=(SDS((S, Dm), F32), SDS((S, Dm), BF16), SDS((S, Dm), F32)), compiler_params=_cp("parallel"),
    )(xin, sub, g, b)


def _ln_bwd(dout, z, g, name):
    S, Dm = z.shape
    row = pl.BlockSpec((ROWS, Dm), lambda t: (t, 0))
    vec = pl.BlockSpec((1, Dm), lambda t: (0, 0))

    def body(d_ref, z_ref, g_ref, dz_ref, dzb_ref, dg_ref, db_ref):
        t = pl.program_id(0)
        xh, rstd = _ln_stats(z_ref[...])
        d = d_ref[...]
        dxh = d * g_ref[...]
        m1 = jnp.mean(dxh, -1, keepdims=True)
        m2 = jnp.mean(dxh * xh, -1, keepdims=True)
        dz = rstd * (dxh - m1 - xh * m2)
        dz_ref[...] = dz
        dzb_ref[...] = dz.astype(BF16)
        pg = jnp.sum(d * xh, axis=0, keepdims=True)
        pb = jnp.sum(d, axis=0, keepdims=True)

        @pl.when(t == 0)
        def _():
            dg_ref[...] = pg
            db_ref[...] = pb

        @pl.when(t > 0)
        def _():
            dg_ref[...] += pg
            db_ref[...] += pb

    return pl.pallas_call(
        body, name=name, grid=(S // ROWS,), in_specs=[row, row, vec], out_specs=(row, row, vec, vec),
        out_shape=(SDS((S, Dm), F32), SDS((S, Dm), BF16), SDS((1, Dm), F32), SDS((1, Dm), F32)),
        compiler_params=_cp("arbitrary"),
    )(dout, z, g)


def _loss_fwd_bwd(y, tgt):
    S, Dm = y.shape
    row = pl.BlockSpec((ROWS, Dm), lambda t: (t, 0))
    one = pl.BlockSpec((1, 1), lambda t: (0, 0))

    def body(y_ref, t_ref, dy_ref, l_ref):
        t = pl.program_id(0)
        diff = y_ref[...] - t_ref[...]
        dy_ref[...] = diff * (1.0 / Dm)
        part = jnp.sum(jnp.sum(diff * diff, axis=1, keepdims=True), axis=0, keepdims=True) * (0.5 / Dm)

        @pl.when(t == 0)
        def _():
            l_ref[...] = part

        @pl.when(t > 0)
        def _():
            l_ref[...] += part

    return pl.pallas_call(
        body, name="loss", grid=(S // ROWS,), in_specs=[row, row], out_specs=(row, one),
        out_shape=(SDS((S, Dm), F32), SDS((1, 1), F32)), compiler_params=_cp("arbitrary"),
    )(y, tgt)


def _ret_consts():
    h = np.arange(RET_HEADS, dtype=np.float64)
    lg = np.log1p(-np.exp2(-5.0 - h))
    return jnp.asarray(np.concatenate([lg, np.exp(lg * RT)]).astype(np.float32))


def _rope_tables(S):
    half = RET_DK // 2
    inv = 10000.0 ** (-jnp.arange(half, dtype=F32) / half)
    ang = jnp.arange(S).astype(F32)[:, None] * inv[None, :]
    return jnp.cos(ang), jnp.sin(ang)


def _ret_masks(lgh):
    ri, ci = _iota((RT, RT), 0), _iota((RT, RT), 1)
    visible = (ci >> 6) <= (ri >> 6)
    m = jnp.where(visible, jnp.exp(lgh * jnp.abs(ri - ci).astype(F32)), 0.0)
    pos = _iota((RT, 1), 0).astype(F32)
    return m, jnp.exp(lgh * (pos + 1.0)), jnp.exp(lgh * (RT - 1.0 - pos))


def _ret_rot(h0, cos, sin):
    S = h0.shape[0]
    half = RET_DK // 2

    def body(q_ref, k_ref, v_ref, c_ref, s_ref, qo_ref, ko_ref, vo_ref):
        c, s = c_ref[...], s_ref[...]

        def rot(t):
            t1, t2 = t[:, :half], t[:, half:]
            return jnp.concatenate([t1 * c - t2 * s, t1 * s + t2 * c], axis=-1)

        qo_ref[...] = rot(q_ref[...]).astype(BF16)
        ko_ref[...] = (rot(k_ref[...]) * (RET_DK ** -0.5)).astype(BF16)
        vo_ref[...] = v_ref[...].astype(BF16)

    qk = lambda off: pl.BlockSpec((RT, RET_DK), lambda t, h: (t, off + h))
    vv = lambda off: pl.BlockSpec((RT, RET_DV), lambda t, h: (t, off + h))
    tab = pl.BlockSpec((RT, half), lambda t, h: (t, 0))
    return pl.pallas_call(
        body, name="ret_rot", grid=(S // RT, RET_HEADS), in_specs=[qk(0), qk(RET_HEADS), vv(RET_HEADS), tab, tab],
        out_specs=(qk(0), qk(0), vv(0)),
        out_shape=(SDS((S, 2048), BF16), SDS((S, 2048), BF16), SDS((S, 4096), BF16)),
        compiler_params=_cp("parallel", "parallel"),
    )(h0, h0, h0, cos, sin)


def _ret_fwd(q, k, v, consts):
    S = q.shape[0]
    nt = S // RT

    def body(c_ref, q_ref, k_ref, v_ref, y_ref, st_ref, s_scr):
        h, t = pl.program_id(0), pl.program_id(1)

        @pl.when(t == 0)
        def _():
            s_scr[...] = jnp.zeros_like(s_scr)

        lgh, cdec = c_ref[h], c_ref[RET_HEADS + h]
        m, dq, dk = _ret_masks(lgh)
        qv, kv, vv = q_ref[...], k_ref[...], v_ref[...]
        p = (_dot_nt(qv, kv) * m).astype(BF16)
        sp = s_scr[...]
        spb = sp.astype(BF16)
        st_ref[...] = spb
        qd = (qv.astype(F32) * dq).astype(BF16)
        kd = (kv.astype(F32) * dk).astype(BF16)
        y_ref[...] = _dot(p, vv) + _dot(qd, spb)
        s_scr[...] = sp * cdec + _dot_tn(kd, vv)

    qk = pl.BlockSpec((RT, RET_DK), lambda h, t: (t, h))
    vs = pl.BlockSpec((RT, RET_DV), lambda h, t: (t, h))
    return pl.pallas_call(
        body, name="ret_fwd", grid=(RET_HEADS, nt),
        in_specs=[pl.BlockSpec(memory_space=pltpu.SMEM), qk, qk, vs],
        out_specs=(vs, pl.BlockSpec((None, None, RET_DK, RET_DV), lambda h, t: (h, t, 0, 0))),
        out_shape=(SDS((S, 4096), F32), SDS((RET_HEADS, nt, RET_DK, RET_DV), BF16)),
        scratch_shapes=[pltpu.VMEM((RET_DK, RET_DV), F32)], compiler_params=_cp("parallel", "arbitrary"),
    )(consts, q, k, v)


def _ret_bwd(q, k, v, dy, states, consts, cos, sin):
    S = q.shape[0]
    nt = S // RT
    half = RET_DK // 2

    def body(c_ref, q_ref, k_ref, v_ref, dy_ref, st_ref, cos_ref, sin_ref, dq_ref, dk_ref, dv_ref, ds_scr):
        h, t = pl.program_id(0), pl.program_id(1)

        @pl.when(t == 0)
        def _():
            ds_scr[...] = jnp.zeros_like(ds_scr)

        lgh, cdec = c_ref[h], c_ref[RET_HEADS + h]
        m, dqc, dkc = _ret_masks(lgh)
        qv, kv, vv, dyv, spb = q_ref[...], k_ref[...], v_ref[...], dy_ref[...], st_ref[...]
        p = (_dot_nt(qv, kv) * m).astype(BF16)
        qd = (qv.astype(F32) * dqc).astype(BF16)
        kd = (kv.astype(F32) * dkc).astype(BF16)
        dsn = ds_scr[...]
        dsb = dsn.astype(BF16)
        dsc = (_dot_nt(dyv, vv) * m).astype(BF16)
        dq = _dot(dsc, kv) + _dot_nt(dyv, spb) * dqc
        dk = _dot_tn(dsc, qv) + _dot_nt(vv, dsb) * dkc
        dv_ref[...] = (_dot_tn(p, dyv) + _dot(kd, dsb)).astype(BF16)
        ds_scr[...] = dsn * cdec + _dot_tn(qd, dyv)
        c, s = cos_ref[...], sin_ref[...]

        def unrot(d):
            d1, d2 = d[:, :half], d[:, half:]
            return jnp.concatenate([d1 * c + d2 * s, d2 * c - d1 * s], axis=-1)

        dq_ref[...] = unrot(dq).astype(BF16)
        dk_ref[...] = (unrot(dk) * (RET_DK ** -0.5)).astype(BF16)

    rev = lambda t: nt - 1 - t
    qk = pl.BlockSpec((RT, RET_DK), lambda h, t: (rev(t), h))
    vs = pl.BlockSpec((RT, RET_DV), lambda h, t: (rev(t), h))
    tab = pl.BlockSpec((RT, half), lambda h, t: (rev(t), 0))
    return pl.pallas_call(
        body, name="ret_bwd", grid=(RET_HEADS, nt),
        in_specs=[pl.BlockSpec(memory_space=pltpu.SMEM), qk, qk, vs, vs,
                  pl.BlockSpec((None, None, RET_DK, RET_DV), lambda h, t: (h, rev(t), 0, 0)), tab, tab],
        out_specs=(qk, qk, vs),
        out_shape=(SDS((S, 2048), BF16), SDS((S, 2048), BF16), SDS((S, 4096), BF16)),
        scratch_shapes=[pltpu.VMEM((RET_DK, RET_DV), F32)], compiler_params=_cp("parallel", "arbitrary"),
    )(consts, q, k, v, dy, states, cos, sin)


def _gn_stats(y):
    mu = jnp.mean(y, -1, keepdims=True)
    yc = y - mu
    var = jnp.mean(yc * yc, -1, keepdims=True)
    rstd = lax.rsqrt(var + GN_EPS)
    return yc * rstd, rstd


def _ret_post_fwd(y, h0, gn_g):
    S = y.shape[0]

    def body(y_ref, gate_ref, g_ref, o_ref):
        yn, _ = _gn_stats(y_ref[...])
        gate = gate_ref[...]
        o_ref[...] = (gate * _sigmoid(gate) * (yn * g_ref[...])).astype(BF16)

    vs = lambda off: pl.BlockSpec((RT, RET_DV), lambda h, t: (t, off + h))
    return pl.pallas_call(
        body, name="ret_post_fwd", grid=(RET_HEADS, S // RT),
        in_specs=[vs(0), vs(2 * RET_HEADS), pl.BlockSpec((1, RET_DV), lambda h, t: (0, h))], out_specs=vs(0),
        out_shape=SDS((S, 4096), BF16), compiler_params=_cp("parallel", "parallel"),
    )(y, h0, gn_g)


def _ret_post_bwd(do, y, h0, gn_g):
    S = y.shape[0]

    def body(do_ref, y_ref, gate_ref, g_ref, dy_ref, dgate_ref, dg_ref):
        t = pl.program_id(1)
        yn, rstd = _gn_stats(y_ref[...])
        gate, g, dov = gate_ref[...], g_ref[...], do_ref[...]
        sg = _sigmoid(gate)
        dgate_ref[...] = (dov * (yn * g) * (sg * (1.0 + gate * (1.0 - sg)))).astype(BF16)
        dyg = dov * (gate * sg)
        dyn = dyg * g
        m1 = jnp.mean(dyn, -1, keepdims=True)
        m2 = jnp.mean(dyn * yn, -1, keepdims=True)
        dy_ref[...] = (rstd * (dyn - m1 - yn * m2)).astype(BF16)
        pg = jnp.sum(dyg * yn, axis=0, keepdims=True)

        @pl.when(t == 0)
        def _():
            dg_ref[...] = pg

        @pl.when(t > 0)
        def _():
            dg_ref[...] += pg

    vs = lambda off: pl.BlockSpec((RT, RET_DV), lambda h, t: (t, off + h))
    vec = pl.BlockSpec((1, RET_DV), lambda h, t: (0, h))
    return pl.pallas_call(
        body, name="ret_post_bwd", grid=(RET_HEADS, S // RT), in_specs=[vs(0), vs(0), vs(2 * RET_HEADS), vec],
        out_specs=(vs(0), vs(0), vec),
        out_shape=(SDS((S, 4096), BF16), SDS((S, 4096), BF16), SDS((1, 4096), F32)),
        compiler_params=_cp("parallel", "arbitrary"),
    )(do, y, h0, gn_g)


def _conv_taps(x_ref, halo_ref, w_ref, ext_scr, t):
    ext_scr[0:8, :] = jnp.where(t == 0, 0.0, halo_ref[...])
    ext_scr[8:, :] = x_ref[...]
    w = w_ref[...]
    n = x_ref.shape[0]
    acc = w[3:4, :] * ext_scr[8:, :]
    for j in range(3):
        acc = acc + w[j:j + 1, :] * ext_scr[pl.ds(5 + j, n), :]
    return acc


def _gdn_conv_fwd(h1, conv_w, kind):
    S = h1.shape[0]
    base = {"q": 0, "k": 1, "v": 2}[kind]
    ncb = 2 if kind == "v" else 1
    C = 2048

    def body(x_ref, halo_ref, w_ref, o_ref, ext_scr):
        acc = _conv_taps(x_ref, halo_ref, w_ref, ext_scr, pl.program_id(0))
        c = acc * _sigmoid(acc)
        if kind == "v":
            o_ref[...] = c.astype(BF16)
        else:
            scale = GDN_D ** -0.5 if kind == "q" else 1.0
            for hh in range(C // GDN_D):
                ch = c[:, hh * GDN_D:(hh + 1) * GDN_D]
                r = lax.rsqrt(jnp.sum(ch * ch, -1, keepdims=True) + L2_EPS)
                o_ref[:, hh * GDN_D:(hh + 1) * GDN_D] = (ch * (r * scale)).astype(BF16)

    hb = ROWS // 8
    return pl.pallas_call(
        body, name="gdn_conv_fwd_" + kind, grid=(S // ROWS, ncb),
        in_specs=[pl.BlockSpec((ROWS, C), lambda t, j: (t, base + j)),
                  pl.BlockSpec((8, C), lambda t, j: (jnp.maximum(t * hb - 1, 0), base + j)),
                  pl.BlockSpec((4, C), lambda t, j: (0, base + j))],
        out_specs=pl.BlockSpec((ROWS, C), lambda t, j: (t, j)), out_shape=SDS((S, C * ncb), BF16),
        scratch_shapes=[pltpu.VMEM((ROWS + 8, C), F32)], compiler_params=_cp("parallel", "parallel"),
    )(h1, h1, conv_w)


def _gdn_conv_bwd_act(h1, conv_w, dn, kind, buf=None):
    S = h1.shape[0]
    base = {"q": 0, "k": 1, "v": 2}[kind]
    ncb = 2 if kind == "v" else 1
    C = 2048

    def body(x_ref, halo_ref, w_ref, dn_ref, *rest):
        o_ref, ext_scr = rest[-2], rest[-1]
        acc = _conv_taps(x_ref, halo_ref, w_ref, ext_scr, pl.program_id(0))
        sg = _sigmoid(acc)
        dsilu = sg * (1.0 + acc * (1.0 - sg))
        if kind == "v":
            o_ref[...] = dn_ref[...] * dsilu
        else:
            c = acc * sg
            scale = GDN_D ** -0.5 if kind == "q" else 1.0
            for hh in range(C // GDN_D):
                sl = slice(hh * GDN_D, (hh + 1) * GDN_D)
                ch, dnh = c[:, sl], dn_ref[:, sl]
                r = lax.rsqrt(jnp.sum(ch * ch, -1, keepdims=True) + L2_EPS)
                proj = jnp.sum(dnh * ch, -1, keepdims=True)
                o_ref[:, sl] = (scale * r) * (dnh - ch * (proj * r * r)) * dsilu[:, sl]

    hb = ROWS // 8
    return pl.pallas_call(
        body, name="gdn_conv_bwd_act_" + kind, grid=(S // ROWS, ncb),
        in_specs=[pl.BlockSpec((ROWS, C), lambda t, j: (t, base + j)),
                  pl.BlockSpec((8, C), lambda t, j: (jnp.maximum(t * hb - 1, 0), base + j)),
                  pl.BlockSpec((4, C), lambda t, j: (0, base + j)),
                  pl.BlockSpec((ROWS, C), lambda t, j: (t, j))] + ([] if buf is None else [HBM_SPEC]),
        out_specs=pl.BlockSpec((ROWS, C), lambda t, j: (t, base + j)), out_shape=SDS((S, GDN_QKV), F32),
        input_output_aliases={} if buf is None else {4: 0},
        scratch_shapes=[pltpu.VMEM((ROWS + 8, C), F32)], compiler_params=_cp("parallel", "parallel"),
    )(*((h1, h1, conv_w, dn) + (() if buf is None else (buf,))))


def _gdn_conv_bwd_in(h1, conv_w, dacc, dh1_buf):
    S = h1.shape[0]
    C = 2048
    nt = S // ROWS
    hb = ROWS // 8

    def body(x_ref, halo_ref, w_ref, d_ref, dhalo_ref, buf_ref, di_ref, dw_ref, ext_scr, dext_scr):
        t = pl.program_id(1)
        ext_scr[0:8, :] = jnp.where(t == 0, 0.0, halo_ref[...])
        ext_scr[8:, :] = x_ref[...]
        d = d_ref[...]
        dext_scr[0:ROWS, :] = d
        dext_scr[ROWS:, :] = jnp.where(t == nt - 1, 0.0, dhalo_ref[...])
        w = w_ref[...]
        di = w[3:4, :] * d
        for j in range(3):
            di = di + w[j:j + 1, :] * dext_scr[pl.ds(3 - j, ROWS), :]
        di_ref[...] = di.astype(BF16)
        rows = [jnp.sum(d * ext_scr[pl.ds(5 + j, ROWS), :], axis=0, keepdims=True) for j in range(4)]
        pw = jnp.concatenate(rows, axis=0)

        @pl.when(t == 0)
        def _():
            dw_ref[...] = pw

        @pl.when(t > 0)
        def _():
            dw_ref[...] += pw

    return pl.pallas_call(
        body, name="gdn_conv_bwd_in", grid=(GDN_QKV // C, nt),
        in_specs=[pl.BlockSpec((ROWS, C), lambda j, t: (t, j)),
                  pl.BlockSpec((8, C), lambda j, t: (jnp.maximum(t * hb - 1, 0), j)),
                  pl.BlockSpec((4, C), lambda j, t: (0, j)),
                  pl.BlockSpec((ROWS, C), lambda j, t: (t, j)),
                  pl.BlockSpec((8, C), lambda j, t: (jnp.minimum((t + 1) * hb, nt * hb - 1), j)), HBM_SPEC],
        out_specs=(pl.BlockSpec((ROWS, C), lambda j, t: (t, j)), pl.BlockSpec((4, C), lambda j, t: (0, j))),
        out_shape=(SDS(dh1_buf.shape, BF16), SDS((4, GDN_QKV), F32)), input_output_aliases={5: 0},
        scratch_shapes=[pltpu.VMEM((ROWS + 8, C), F32), pltpu.VMEM((ROWS + 8, C), F32)],
        compiler_params=_cp("parallel", "arbitrary"),
    )(h1, h1, conv_w, dacc, dacc, dh1_buf)


def _chunk_masks():
    ri, ci = _iota((CT, CT), 0), _iota((CT, CT), 1)
    same = (ri >> 6) == (ci >> 6)
    return same, same & (ri >= ci), same & (ri > ci), same & (ri <= ci), ri == ci


def _fold_dup(m):
    h = m[:, :LANES] + m[:, LANES:]
    return h + pltpu.roll(h, CHUNK, axis=1)


def _unfold_bd(d, same):
    return jnp.where(same, jnp.concatenate([d, d], axis=1), 0.0)


def _softplus(x):
    return jnp.maximum(x, 0.0) + jnp.log(1.0 + jnp.exp(-jnp.abs(x)))


def _gdn_scal_fwd(ba, a_log, dt_bias):
    S = ba.shape[0]

    def body(ba_ref, al_ref, dt_ref, beta_ref, g_ref, gc_ref):
        bav = ba_ref[...]
        beta_ref[...] = _sigmoid(bav)
        a = pltpu.roll(bav, LANES - GDN_HV, axis=1)
        g = -jnp.exp(al_ref[...]) * _softplus(a + dt_ref[...])
        g_ref[...] = g
        causal = _chunk_masks()[1]
        gc_ref[...] = _dot_exact_l(causal.astype(BF16), g)

    row = pl.BlockSpec((CT, LANES), lambda t: (t, 0))
    vec = pl.BlockSpec((1, LANES), lambda t: (0, 0))
    return pl.pallas_call(
        body, name="gdn_scal_fwd", grid=(S // CT,), in_specs=[row, vec, vec], out_specs=(row, row, row),
        out_shape=(SDS((S, LANES), F32),) * 3, compiler_params=_cp("parallel"),
    )(ba, a_log, dt_bias)


def _gdn_scal_bwd(ba, a_log, dt_bias, g, dbeta, dgc):
    S = ba.shape[0]

    def body(ba_ref, al_ref, dt_ref, g_ref, dbeta_ref, dgc_ref, dba_ref, dal_ref, ddt_ref):
        t = pl.program_id(0)
        bav = ba_ref[...]
        beta = _sigmoid(bav)
        db = dbeta_ref[...] * beta * (1.0 - beta)
        a = pltpu.roll(bav, LANES - GDN_HV, axis=1)
        dgv = _dot_exact_l(_chunk_masks()[3].astype(BF16), dgc_ref[...])
        da = dgv * (-jnp.exp(al_ref[...])) * _sigmoid(a + dt_ref[...])
        lane = _iota(bav.shape, 1)
        da_sh = pltpu.roll(da, GDN_HV, axis=1)
        dba = jnp.where(lane < GDN_HV, db, jnp.where(lane < 2 * GDN_HV, da_sh, 0.0))
        dba_ref[...] = dba.astype(BF16)
        keep = lane < GDN_HV
        pal = jnp.sum(jnp.where(keep, dgv * g_ref[...], 0.0), axis=0, keepdims=True)
        pdt = jnp.sum(jnp.where(keep, da, 0.0), axis=0, keepdims=True)

        @pl.when(t == 0)
        def _():
            dal_ref[...] = pal
            ddt_ref[...] = pdt

        @pl.when(t > 0)
        def _():
            dal_ref[...] += pal
            ddt_ref[...] += pdt

    row = pl.BlockSpec((CT, LANES), lambda t: (t, 0))
    vec = pl.BlockSpec((1, LANES), lambda t: (0, 0))
    return pl.pallas_call(
        body, name="gdn_scal_bwd", grid=(S // CT,), in_specs=[row, vec, vec, row, row, row],
        out_specs=(row, vec, vec), out_shape=(SDS((S, LANES), BF16), SDS((1, LANES), F32), SDS((1, LANES), F32)),
        compiler_params=_cp("arbitrary"),
    )(ba, a_log, dt_bias, g, dbeta, dgc)


def _sel_col(x, h):
    return jnp.sum(jnp.where(_iota(x.shape, 1) == h, x, 0.0), axis=1, keepdims=True)


def _decay(gcol, causal):
    gm = jnp.broadcast_to(gcol, (CT, CT))
    diff = gm - gm.T
    return jnp.where(causal, jnp.exp(jnp.where(causal, diff, 0.0)), 0.0)


def _gdn_chunk_fwd(k, v, beta, gc):
    S = k.shape[0]
    nt = S // CT

    def body(k_ref, v_ref, beta_ref, gc_ref, t_ref, u_ref, w_ref):
        kh = pl.program_id(1)
        same, causal, strict, _, _ = _chunk_masks()
        kv = k_ref[...]
        kf = kv.astype(F32)
        kk = _dot_nt(kv, kv)
        eye_dup = jnp.where((_iota((CT, LANES), 0) & (CHUNK - 1)) == (_iota((CT, LANES), 1) & (CHUNK - 1)), 1.0, 0.0)
        xs, xds, pds, cols = [], [], [], []
        for hp in range(HP):
            bcol, gcol = _sel_col(beta_ref[...], HP * kh + hp), _sel_col(gc_ref[...], HP * kh + hp)
            x = jnp.where(strict, -(kk * bcol) * _decay(gcol, causal), 0.0)
            xs.append(x)
            xds.append(_fold_dup(x))
            pds.append(eye_dup)
            cols.append((bcol, gcol))
        for m in range(6):
            for hp in range(HP):
                out = _dotx3(xs[hp], jnp.concatenate([xds[hp], pds[hp]], axis=1))
                pds[hp] = pds[hp] + out[:, LANES:]
                if m < 5:
                    xds[hp] = out[:, :LANES]
                    xs[hp] = _unfold_bd(xds[hp], same)
        for hp in range(HP):
            bcol, gcol = cols[hp]
            cs = slice(hp * GDN_D, (hp + 1) * GDN_D)
            t_ref[hp] = pds[hp]
            tb = _unfold_bd(pds[hp], same).astype(BF16)
            vb = (v_ref[:, cs].astype(F32) * bcol).astype(BF16)
            kg = (kf * (bcol * jnp.exp(gcol))).astype(BF16)
            uw = _dot(tb, jnp.concatenate([vb, kg], axis=1))
            u_ref[:, cs] = uw[:, :GDN_D]
            w_ref[:, cs] = uw[:, GDN_D:].astype(BF16)

    col = pl.BlockSpec((CT, LANES), lambda t, kh: (t, 0))
    hv = pl.BlockSpec((CT, HP * GDN_D), lambda t, kh: (t, kh))
    return pl.pallas_call(
        body, name="gdn_chunk_fwd", grid=(nt, GDN_HV // HP),
        in_specs=[pl.BlockSpec((CT, GDN_D), lambda t, kh: (t, kh)), hv, col, col],
        out_specs=(pl.BlockSpec((HP, None, CT, LANES), lambda t, kh: (kh, t, 0, 0)), hv, hv),
        out_shape=(SDS((GDN_HV, nt, CT, LANES), F32), SDS((S, 4096), F32), SDS((S, 4096), BF16)),
        compiler_params=_cp("parallel", "parallel"),
    )(k, v, beta, gc)


def _last_of_chunk(gcol, rows, c):
    return jnp.sum(jnp.where(rows == c * CHUNK + CHUNK - 1, gcol, 0.0), axis=0, keepdims=True)


def _gdn_scan_fwd(q, k, u, w, gc):
    S = q.shape[0]
    nt = S // CT
    ncs = CT // CHUNK

    def body(q_ref, k_ref, u_ref, w_ref, gc_ref, y_ref, vn_ref, st_ref, s_scr, vn_scr):
        kh, t = pl.program_id(0), pl.program_id(1)

        @pl.when(t == 0)
        def _():
            s_scr[...] = jnp.zeros_like(s_scr)

        causal = _chunk_masks()[1]
        qv, kv = q_ref[...], k_ref[...]
        qf, kf = qv.astype(F32), kv.astype(F32)
        qk = _dot_nt(qv, kv)
        rows = _iota((CT, 1), 0)
        heads = []
        for hp in range(HP):
            gcol = _sel_col(gc_ref[...], HP * kh + hp)
            heads.append((gcol, (qk * _decay(gcol, causal)).astype(BF16), (qf * jnp.exp(gcol)).astype(BF16)))
        vn_scr[...] = jnp.zeros_like(vn_scr)
        for c in range(ncs):
            r = slice(c * CHUNK, (c + 1) * CHUNK)
            for hp in range(HP):
                gcol, attn, qd = heads[hp]
                cs = slice(hp * GDN_D, (hp + 1) * GDN_D)
                s = s_scr[hp]
                st_ref[hp, c] = s
                sb = s.astype(BF16)
                gl = _last_of_chunk(gcol, rows, c)
                kd = (kf[r] * jnp.exp(gl - gcol[r])).astype(BF16)
                vn = (u_ref[r, cs] - _dot(w_ref[r, cs], sb)).astype(BF16)
                vn_scr[r, cs] = vn
                y_ref[r, cs] = _dot(qd[r], sb) + _dot(attn[r], vn_scr[:, cs])
                s_scr[hp] = s * jnp.exp(gl) + _dot_tn(kd, vn)
        vn_ref[...] = vn_scr[...]

    hk = pl.BlockSpec((CT, GDN_D), lambda kh, t: (t, kh))
    hv = pl.BlockSpec((CT, HP * GDN_D), lambda kh, t: (t, kh))
    col = pl.BlockSpec((CT, LANES), lambda kh, t: (t, 0))
    return pl.pallas_call(
        body, name="gdn_scan_fwd", grid=(GDN_HV // HP, nt), in_specs=[hk, hk, hv, hv, col],
        out_specs=(hv, hv, pl.BlockSpec((HP, ncs, GDN_D, GDN_D), lambda kh, t: (kh, t, 0, 0))),
        out_shape=(SDS((S, 4096), F32), SDS((S, 4096), BF16), SDS((GDN_HV, S // CHUNK, GDN_D, GDN_D), F32)),
        scratch_shapes=[pltpu.VMEM((HP, GDN_D, GDN_D), F32), pltpu.VMEM((CT, HP * GDN_D), BF16)],
        compiler_params=_cp("parallel", "arbitrary"),
    )(q, k, u, w, gc)


def _gdn_scan_bwd(q, k, w, vn, gc, states, dy):
    S = q.shape[0]
    nt = S // CT
    ncs = CT // CHUNK

    def body(q_ref, k_ref, w_ref, vn_ref, gc_ref, st_ref, dy_ref, du_ref, dw_ref, dq_ref, dk_ref, dgc_ref, ds_scr):
        kh, t = pl.program_id(0), pl.program_id(1)

        @pl.when(t == 0)
        def _():
            ds_scr[...] = jnp.zeros_like(ds_scr)

        _, causal, _, _, eye = _chunk_masks()
        qv, kv = q_ref[...], k_ref[...]
        qf, kf = qv.astype(F32), kv.astype(F32)
        qk = _dot_nt(qv, kv)
        rows = _iota((CT, 1), 0)
        heads = []
        for hp in range(HP):
            cs = slice(hp * GDN_D, (hp + 1) * GDN_D)
            gcol = _sel_col(gc_ref[...], HP * kh + hp)
            dm = _decay(gcol, causal)
            attn_f = qk * dm
            egc = jnp.exp(gcol)
            qd_f = qf * egc
            dyv, vnv = dy_ref[:, cs], vn_ref[:, cs]
            heads.append(dict(cs=cs, gcol=gcol, dm=dm, attn_f=attn_f, egc=egc, qd_f=qd_f, qd=qd_f.astype(BF16),
                              dy=dyv, vn=vnv, dattn=_dot_nt(dyv, vnv), at_dy=_dot_tn(attn_f.astype(BF16), dyv),
                              dgc=[None] * ncs))
        dq_ref[...] = jnp.zeros_like(dq_ref)
        dk_ref[...] = jnp.zeros_like(dk_ref)
        for c in reversed(range(ncs)):
            r = slice(c * CHUNK, (c + 1) * CHUNK)
            for hp in range(HP):
                hd = heads[hp]
                cs, gcol = hd["cs"], hd["gcol"]
                s = st_ref[hp, c]
                sb = s.astype(BF16)
                dsn = ds_scr[hp]
                dsb = dsn.astype(BF16)
                gl = _last_of_chunk(gcol, rows, c)
                cd = jnp.exp(gl)
                ekd = jnp.exp(gl - gcol[r])
                kd_f = kf[r] * ekd
                dvn = (hd["at_dy"][r] + _dot(kd_f.astype(BF16), dsb)).astype(BF16)
                dqd = _dot_nt(hd["dy"][r], sb)
                dkd = _dot_nt(hd["vn"][r], dsb)
                dcd = jnp.sum(jnp.sum(s * dsn, axis=1, keepdims=True), axis=0, keepdims=True)
                ds_scr[hp] = dsn * cd + _dot_tn(hd["qd"][r], hd["dy"][r]) - _dot_tn(w_ref[r, cs], dvn)
                du_ref[r, cs] = dvn
                dw_ref[r, cs] = (-_dot_nt(dvn, sb)).astype(BF16)
                dq_ref[r, :] += dqd * hd["egc"][r]
                dk_ref[r, :] += dkd * ekd
                rs_q = jnp.sum(dqd * hd["qd_f"][r], axis=1, keepdims=True)
                rs_k = jnp.sum(dkd * kd_f, axis=1, keepdims=True)
                tot = jnp.sum(rs_k, axis=0, keepdims=True) + dcd * cd
                hd["dgc"][c] = rs_q - rs_k + jnp.where(rows[r] == c * CHUNK + CHUNK - 1, tot, 0.0)
        for hp in range(HP):
            hd = heads[hp]
            dab = (hd["dattn"] * hd["dm"]).astype(BF16)
            dq_ref[...] += _dot(dab, kv)
            dk_ref[...] += _dot_tn(dab, qv)
            e1 = hd["dattn"] * hd["attn_f"]
            dgc = (jnp.concatenate(hd["dgc"], axis=0) + jnp.sum(e1, axis=1, keepdims=True)
                   - jnp.sum(e1.T, axis=1, keepdims=True))
            dgc_ref[hp] = jnp.sum(jnp.where(eye, jnp.broadcast_to(dgc, (CT, CT)), 0.0), axis=0, keepdims=True)

    rev = lambda t: nt - 1 - t
    hk = pl.BlockSpec((CT, GDN_D), lambda kh, t: (rev(t), kh))
    hv = pl.BlockSpec((CT, HP * GDN_D), lambda kh, t: (rev(t), kh))
    col = pl.BlockSpec((CT, LANES), lambda kh, t: (rev(t), 0))
    return pl.pallas_call(
        body, name="gdn_scan_bwd", grid=(GDN_HV // HP, nt),
        in_specs=[hk, hk, hv, hv, col, pl.BlockSpec((HP, ncs, GDN_D, GDN_D), lambda kh, t: (kh, rev(t), 0, 0)), hv],
        out_specs=(hv, hv, hk, hk, pl.BlockSpec((HP, 1, CT), lambda kh, t: (kh, 0, rev(t)))),
        out_shape=(SDS((S, 4096), BF16), SDS((S, 4096), BF16), SDS((S, 2048), F32), SDS((S, 2048), F32),
                   SDS((GDN_HV, 1, S), F32)),
        scratch_shapes=[pltpu.VMEM((HP, GDN_D, GDN_D), F32)], compiler_params=_cp("parallel", "arbitrary"),
    )(q, k, w, vn, gc, states, dy)


def _gdn_chunk_bwd(k, v, beta, gc, tmat, du, dw, dk_p, dgc_p):
    S = k.shape[0]
    nt = S // CT

    def body(k_ref, v_ref, beta_ref, gc_ref, t_ref, du_ref, dw_ref, dkp_ref, dgcp_ref,
             dk_ref, dv_ref, dbeta_ref, dgc_ref):
        kh = pl.program_id(1)
        same, causal, strict, _, eye = _chunk_masks()
        kv = k_ref[...]
        kf = kv.astype(F32)
        kk = _dot_nt(kv, kv)
        lane = _iota((CT, LANES), 1)

        @pl.when(kh == 0)
        def _():
            dbeta_ref[...] = jnp.zeros_like(dbeta_ref)
            dgc_ref[...] = jnp.zeros_like(dgc_ref)

        dk = dkp_ref[...]
        for hp in range(HP):
            h = HP * kh + hp
            cs = slice(hp * GDN_D, (hp + 1) * GDN_D)
            bcol, gcol = _sel_col(beta_ref[...], h), _sel_col(gc_ref[...], h)
            dm = _decay(gcol, causal)
            vf = v_ref[:, cs].astype(F32)
            kb = kf * bcol
            a = jnp.where(strict, (kk * bcol) * dm, 0.0)
            egc = jnp.exp(gcol)
            kg_f = kb * egc
            tb = _unfold_bd(t_ref[hp], same).astype(BF16)
            duw = jnp.concatenate([du_ref[:, cs], dw_ref[:, cs]], axis=1)
            dt = _dot_nt(duw, jnp.concatenate([(vf * bcol).astype(BF16), kg_f.astype(BF16)], axis=1))
            dvb_dkg = _dot_tn(tb, duw)
            dvb, dkg = dvb_dkg[:, :GDN_D], dvb_dkg[:, GDN_D:]
            da = -_dot_nt(_dot_tn(tb, dt.astype(BF16)).astype(BF16), tb)
            rm = jnp.where(strict, da, 0.0)
            rdb = (rm * dm).astype(BF16)
            dkb = _dot(rdb, kv) + dkg * egc
            dk = dk + _dot_tn(rdb, kb.astype(BF16)) + dkb * bcol
            e2 = rm * a
            dgc_in = jnp.sum(jnp.where(eye, jnp.broadcast_to(dgcp_ref[hp], (CT, CT)), 0.0), axis=1, keepdims=True)
            dgc = (jnp.sum(e2, axis=1, keepdims=True) - jnp.sum(e2.T, axis=1, keepdims=True)
                   + jnp.sum(dkg * kg_f, axis=1, keepdims=True) + dgc_in)
            dbeta = jnp.sum(dkb * kf, axis=1, keepdims=True) + jnp.sum(dvb * vf, axis=1, keepdims=True)
            dv_ref[:, cs] = dvb * bcol
            dbeta_ref[...] += jnp.where(lane == h, dbeta, 0.0)
            dgc_ref[...] += jnp.where(lane == h, dgc, 0.0)
        dk_ref[...] = dk

    hk = pl.BlockSpec((CT, GDN_D), lambda t, kh: (t, kh))
    hv = pl.BlockSpec((CT, HP * GDN_D), lambda t, kh: (t, kh))
    col = pl.BlockSpec((CT, LANES), lambda t, kh: (t, 0))
    return pl.pallas_call(
        body, name="gdn_chunk_bwd", grid=(nt, GDN_HV // HP),
        in_specs=[hk, hv, col, col, pl.BlockSpec((HP, None, CT, LANES), lambda t, kh: (kh, t, 0, 0)), hv, hv, hk,
                  pl.BlockSpec((HP, 1, CT), lambda t, kh: (kh, 0, t))],
        out_specs=(hk, hv, col, col),
        out_shape=(SDS((S, 2048), F32), SDS((S, 4096), F32), SDS((S, LANES), F32), SDS((S, LANES), F32)),
        compiler_params=_cp("parallel", "arbitrary"),
    )(k, v, beta, gc, tmat, du, dw, dk_p, dgc_p)


def _gdn_post_fwd(y, h1, norm_g):
    S = y.shape[0]

    def body(y_ref, z_ref, g_ref, o_ref):
        g = g_ref[...]
        for hh in range(GDN_HV):
            sl = slice(hh * GDN_D, (hh + 1) * GDN_D)
            yh, zh = y_ref[:, sl], z_ref[:, sl]
            yn = yh * lax.rsqrt(jnp.mean(yh * yh, -1, keepdims=True) + RMS_EPS)
            o_ref[:, sl] = (yn * g * (zh * _sigmoid(zh))).astype(BF16)

    row = lambda off: pl.BlockSpec((ROWS, 4096), lambda t: (t, off))
    return pl.pallas_call(
        body, name="gdn_post_fwd", grid=(S // ROWS,),
        in_specs=[row(0), row(2), pl.BlockSpec((1, GDN_D), lambda t: (0, 0))], out_specs=row(0),
        out_shape=SDS((S, 4096), BF16), compiler_params=_cp("parallel"),
    )(y, h1, norm_g)


def _gdn_post_bwd(do, y, h1, norm_g):
    S = y.shape[0]

    def body(do_ref, y_ref, z_ref, g_ref, dy_ref, dz_ref, dg_ref):
        t = pl.program_id(0)
        g = g_ref[...]
        pg = jnp.zeros((1, GDN_D), F32)
        for hh in range(GDN_HV):
            sl = slice(hh * GDN_D, (hh + 1) * GDN_D)
            yh, zh, doh = y_ref[:, sl], z_ref[:, sl], do_ref[:, sl]
            rstd = lax.rsqrt(jnp.mean(yh * yh, -1, keepdims=True) + RMS_EPS)
            yn = yh * rstd
            sg = _sigmoid(zh)
            dz_ref[:, sl] = (doh * (yn * g) * (sg * (1.0 + zh * (1.0 - sg)))).astype(BF16)
            dyg = doh * (zh * sg)
            dyn = dyg * g
            dy_ref[:, sl] = (rstd * (dyn - yn * jnp.mean(dyn * yn, -1, keepdims=True))).astype(BF16)
            pg = pg + jnp.sum(dyg * yn, axis=0, keepdims=True)

        @pl.when(t == 0)
        def _():
            dg_ref[...] = pg

        @pl.when(t > 0)
        def _():
            dg_ref[...] += pg

    row = lambda off: pl.BlockSpec((ROWS, 4096), lambda t: (t, off))
    vec = pl.BlockSpec((1, GDN_D), lambda t: (0, 0))
    return pl.pallas_call(
        body, name="gdn_post_bwd", grid=(S // ROWS,), in_specs=[row(0), row(0), row(2), vec],
        out_specs=(row(0), row(2), vec),
        out_shape=(SDS((S, 4096), BF16), SDS((S, 3 * 4096), BF16), SDS((1, GDN_D), F32)),
        compiler_params=_cp("arbitrary"),
    )(do, y, h1, norm_g)


def _local_step(x, tgt, wri, wro, wgi_main, wba, wgo, w1, w2, conv_w, small):
    S = x.shape[0]
    xb = x.astype(BF16)
    rc = _ret_consts()
    cos, sin = _rope_tables(S)

    h0 = _mm(xb, wri, "nn", "mm_ret_in")
    qr, kr, vr = _ret_rot(h0, cos, sin)
    yr, ret_st = _ret_fwd(qr, kr, vr, rc)
    o0 = _ret_post_fwd(yr, h0, small["ret_gn_g"])
    mix0 = _mm(o0, wro, "nn", "mm_ret_out")
    x1, x1b, z1 = _ln_fwd(x, mix0, small["ln_mix_g"][0:1], small["ln_mix_b"][0:1], "ln_mix0_fwd")
    hh0, a0 = _mm(x1b, w1[0], "nn", "mm_mlp0_up", epi="relu2")
    m0 = _mm(a0, w2[0], "nn", "mm_mlp0_down")
    x2, x2b, z2 = _ln_fwd(x1, m0, small["ln_ffn_g"][0:1], small["ln_ffn_b"][0:1], "ln_ffn0_fwd")

    h1 = _mm(x2b, wgi_main, "nn", "mm_gdn_in")
    ba = _mm(x2b, wba, "nn", "mm_gdn_ba")
    qn = _gdn_conv_fwd(h1, conv_w, "q")
    kn = _gdn_conv_fwd(h1, conv_w, "k")
    vg = _gdn_conv_fwd(h1, conv_w, "v")
    beta, g, gc = _gdn_scal_fwd(ba, small["a_log"], small["dt_bias"])
    tmat, u, w = _gdn_chunk_fwd(kn, vg, beta, gc)
    yg, vn, gdn_st = _gdn_scan_fwd(qn, kn, u, w, gc)
    o1 = _gdn_post_fwd(yg, h1, small["norm_g"])
    mix1 = _mm(o1, wgo, "nn", "mm_gdn_out")
    x3, x3b, z3 = _ln_fwd(x2, mix1, small["ln_mix_g"][1:2], small["ln_mix_b"][1:2], "ln_mix1_fwd")
    hh1, a1 = _mm(x3b, w1[1], "nn", "mm_mlp1_up", epi="relu2")
    m1 = _mm(a1, w2[1], "nn", "mm_mlp1_down")
    x4, _, z4 = _ln_fwd(x3, m1, small["ln_ffn_g"][1:2], small["ln_ffn_b"][1:2], "ln_ffn1_fwd")

    dx4, loss = _loss_fwd_bwd(x4, tgt)

    dz4, dz4b, d_lnf_g1, d_lnf_b1 = _ln_bwd(dx4, z4, small["ln_ffn_g"][1:2], "ln_ffn1_bwd")
    dhh1 = _mm(dz4b, w2[1], "nt", "mm_mlp1_down_dx", epi="drelu2", extra=hh1, out_dtype=BF16)
    dw2_1 = _mm(a1, dz4b, "tn", "mm_mlp1_down_dw")
    dx3 = _mm(dhh1, w1[1], "nt", "mm_mlp1_up_dx", epi="add", extra=dz4, scale=ALPHA)
    dw1_1 = _mm(x3b, dhh1, "tn", "mm_mlp1_up_dw", shard_major=True)
    dz3, dz3b, d_lnm_g1, d_lnm_b1 = _ln_bwd(dx3, z3, small["ln_mix_g"][1:2], "ln_mix1_bwd")
    do1 = _mm(dz3b, wgo, "nt", "mm_gdn_out_dx")
    dwgo = _mm(o1, dz3b, "tn", "mm_gdn_out_dw")
    dyg, dh1_z, d_norm_g = _gdn_post_bwd(do1, yg, h1, small["norm_g"])
    du, dw, dqn, dk_p, dgc_p = _gdn_scan_bwd(qn, kn, w, vn, gc, gdn_st, dyg)
    dkn, dvg, dbeta, dgc = _gdn_chunk_bwd(kn, vg, beta, gc, tmat, du, dw, dk_p, dgc_p)
    dba, d_a_log, d_dt_bias = _gdn_scal_bwd(ba, small["a_log"], small["dt_bias"], g, dbeta, dgc)
    dacc = _gdn_conv_bwd_act(h1, conv_w, dqn, "q")
    dacc = _gdn_conv_bwd_act(h1, conv_w, dkn, "k", dacc)
    dacc = _gdn_conv_bwd_act(h1, conv_w, dvg, "v", dacc)
    dh1, d_conv_w = _gdn_conv_bwd_in(h1, conv_w, dacc, dh1_z)
    dx2_ba = _mm(dba, wba, "nt", "mm_gdn_ba_dx", epi="add", extra=dz3, scale=ALPHA)
    dx2 = _mm(dh1, wgi_main, "nt", "mm_gdn_in_dx", epi="add", extra=dx2_ba)
    dwgi_main = _mm(x2b, dh1, "tn", "mm_gdn_in_dw")
    dwba = _mm(x2b, dba, "tn", "mm_gdn_ba_dw")

    dz2, dz2b, d_lnf_g0, d_lnf_b0 = _ln_bwd(dx2, z2, small["ln_ffn_g"][0:1], "ln_ffn0_bwd")
    dhh0 = _mm(dz2b, w2[0], "nt", "mm_mlp0_down_dx", epi="drelu2", extra=hh0, out_dtype=BF16)
    dw2_0 = _mm(a0, dz2b, "tn", "mm_mlp0_down_dw")
    dx1 = _mm(dhh0, w1[0], "nt", "mm_mlp0_up_dx", epi="add", extra=dz2, scale=ALPHA)
    dw1_0 = _mm(x1b, dhh0, "tn", "mm_mlp0_up_dw", shard_major=True)
    dz1, dz1b, d_lnm_g0, d_lnm_b0 = _ln_bwd(dx1, z1, small["ln_mix_g"][0:1], "ln_mix0_bwd")
    do0 = _mm(dz1b, wro, "nt", "mm_ret_out_dx")
    dwro = _mm(o0, dz1b, "tn", "mm_ret_out_dw")
    dyr, dgate, d_gn_g = _ret_post_bwd(do0, yr, h0, small["ret_gn_g"])
    dq0, dk0, dv0 = _ret_bwd(qr, kr, vr, dyr, ret_st, rc, cos, sin)
    dh0 = jnp.concatenate([dq0, dk0, dv0, dgate], axis=1)
    grad_x = _mm(dh0, wri, "nt", "mm_ret_in_dx", epi="add", extra=dz1, scale=ALPHA)
    dwri = _mm(xb, dh0, "tn", "mm_ret_in_dw", shard_major=True)

    big = dict(ret_w_in=dwri, ret_w_out=dwro, gdn_w_in_main=dwgi_main, gdn_w_in_ba=dwba, gdn_w_out=dwgo,
               mlp_w1=(dw1_0, dw1_1), mlp_w2=(dw2_0, dw2_1))
    sm = dict(ret_gn_g=d_gn_g, a_log=d_a_log, dt_bias=d_dt_bias, norm_g=d_norm_g,
              ln_mix_g=jnp.concatenate([d_lnm_g0, d_lnm_g1], 0), ln_mix_b=jnp.concatenate([d_lnm_b0, d_lnm_b1], 0),
              ln_ffn_g=jnp.concatenate([d_lnf_g0, d_lnf_g1], 0), ln_ffn_b=jnp.concatenate([d_lnf_b0, d_lnf_b1], 0),
              conv_w=d_conv_w)
    return loss, grad_x, big, sm


def _coords():
    return lax.axis_index("x"), lax.axis_index("y"), lax.axis_index("c")


HBM_SPEC = pl.BlockSpec(memory_space=pl.ANY)


def _all_gather_chips(shards, split):
    n = len(shards)
    n_sem = sum(6 if s else 3 for s in split)

    def body(*refs):
        ins, outs = refs[:n], refs[n:2 * n]
        send_sems, recv_sems = refs[2 * n:]
        x, y, c = _coords()
        chips = [(1 - x, y), (x, 1 - y), (1 - x, 1 - y)]
        sibling = (x, y, 1 - c)
        me = 2 * x + y

        def piece(ref, p, core):
            if not split[p]:
                return ref
            half = shards[p].shape[0] // 2
            return ref.at[pl.ds(core * half, half)]

        def rcopy(k, src, dst, to):
            return pltpu.make_async_remote_copy(src_ref=src, dst_ref=dst, send_sem=send_sems.at[k],
                                                recv_sem=recv_sems.at[k], device_id=to, device_id_type=MESH)

        base, k = [], 0
        for p in range(n):
            base.append(k)
            k += 6 if split[p] else 3
        first = []
        for p in range(n):
            for j, chip in enumerate(chips):
                first.append(rcopy(base[p] + j, piece(ins[p], p, c), piece(outs[p].at[me], p, c), (*chip, c)))
        for cp in first:
            cp.start()
        passed = []
        for p in range(n):
            for j, chip in enumerate(chips):
                zone = piece(outs[p].at[2 * chip[0] + chip[1]], p, c)
                rcopy(base[p] + j, zone, zone, sibling).wait_recv()
                if split[p]:
                    fwd = rcopy(base[p] + 3 + j, zone, zone, sibling)
                    fwd.start()
                    passed.append(fwd)
        for p in range(n):
            if split[p]:
                for j, chip in enumerate(chips):
                    zone = piece(outs[p].at[2 * chip[0] + chip[1]], p, 1 - c)
                    rcopy(base[p] + 3 + j, zone, zone, sibling).wait_recv()
        for cp in first + passed:
            cp.wait_send()

    got = pl.pallas_call(
        body, name="all_gather_weights", in_specs=[HBM_SPEC] * n, out_specs=[HBM_SPEC] * n,
        out_shape=[SDS((N_CHIPS,) + s.shape, s.dtype) for s in shards],
        scratch_shapes=[pltpu.SemaphoreType.DMA((n_sem,)), pltpu.SemaphoreType.DMA((n_sem,))],
    )(*shards)
    me = 2 * lax.axis_index("x") + lax.axis_index("y")
    return [lax.dynamic_update_index_in_dim(g, s, me, 0) for g, s in zip(got, shards)]


def _exchange(name, arrays, out_shapes, plan):
    n = len(arrays)
    x0 = jnp.zeros((), jnp.int32)
    n_remote = len(plan((x0, x0, x0), [None] * n, [None] * n, dry=True))

    def body(*refs):
        ins, outs = refs[:n], refs[n:2 * n]
        send_sems, recv_sems = refs[2 * n:]
        rcs = [pltpu.make_async_remote_copy(src_ref=src, dst_ref=dst, send_sem=send_sems.at[i],
                                            recv_sem=recv_sems.at[i], device_id=to, device_id_type=MESH)
               for i, (src, dst, to) in enumerate(plan(_coords(), ins, outs, dry=False))]
        for cp in rcs:
            cp.start()
        for cp in rcs:
            cp.wait_recv()
        for cp in rcs:
            cp.wait_send()

    return pl.pallas_call(
        body, name=name, in_specs=[HBM_SPEC] * n, out_specs=[HBM_SPEC] * n, out_shape=out_shapes,
        scratch_shapes=[pltpu.SemaphoreType.DMA((n_remote,)), pltpu.SemaphoreType.DMA((n_remote,))],
    )(*arrays)


def _reduce_scatter(grads):
    n = len(grads)
    halves = [g.shape[1] // 2 for g in grads]
    chip = 2 * lax.axis_index("x") + lax.axis_index("y")

    def plan_a(xyc, ins, outs, dry):
        x, y, c = xyc
        return [(None if dry else ins[p].at[:, pl.ds((1 - c) * halves[p], halves[p])], None if dry else outs[p],
                 (x, y, 1 - c)) for p in range(n)]

    theirs = _exchange("rs_swap_halves", grads, [SDS((N_CHIPS, halves[p]) + g.shape[2:], F32)
                                                 for p, g in enumerate(grads)], plan_a)
    c = lax.axis_index("c")
    chip_sums = [_add_half(g, t, c, "rs_add_%d" % p) for p, (g, t) in enumerate(zip(grads, theirs))]

    def plan_c(xyc, ins, outs, dry):
        x, y, c = xyc
        me = 2 * x + y
        return [(None if dry else ins[p].at[2 * cx + cy], None if dry else outs[p].at[me], (cx, cy, c))
                for p in range(n) for cx, cy in [(1 - x, y), (x, 1 - y), (1 - x, 1 - y)]]

    parts = _exchange("rs_to_owner", chip_sums, [SDS(s.shape, s.dtype) for s in chip_sums], plan_c)
    parts = [lax.dynamic_update_index_in_dim(pt, lax.dynamic_index_in_dim(cs, chip, 0, keepdims=False), chip, 0)
             for pt, cs in zip(parts, chip_sums)]
    mine = [_sum_chips(pt, "rs_sum_%d" % p) for p, pt in enumerate(parts)]

    def plan_e(xyc, ins, outs, dry):
        x, y, c = xyc
        return [(None if dry else ins[p], None if dry else outs[p], (x, y, 1 - c)) for p in range(n)]

    other = _exchange("rs_swap_reduced", mine, [SDS(m.shape, F32) for m in mine], plan_e)
    return list(zip(mine, other))


def _add_half(g, theirs, c, name):
    _, R, C = g.shape
    half = R // 2
    tr = min(256, half)
    nb = half // tr

    def body(c_ref, g_ref, t_ref, o_ref):
        o_ref[...] = (g_ref[...] + t_ref[...]).astype(BF16)

    blk = pl.BlockSpec((None, tr, C), lambda s, i, c_ref: (s, i, 0))
    return pl.pallas_call(
        body, name=name,
        grid_spec=pltpu.PrefetchScalarGridSpec(
            num_scalar_prefetch=1, grid=(N_CHIPS, nb),
            in_specs=[pl.BlockSpec((None, tr, C), lambda s, i, c_ref: (s, c_ref[0] * nb + i, 0)), blk],
            out_specs=blk),
        out_shape=SDS((N_CHIPS, half, C), BF16), compiler_params=_cp("parallel", "parallel"),
    )(jnp.reshape(c, (1,)).astype(jnp.int32), g, theirs)


def _sum_chips(parts, name):
    _, r, C = parts.shape
    tr = min(256, r)

    def body(p_ref, o_ref):
        f = lambda s: p_ref[s].astype(F32)
        o_ref[...] = ((f(0) + f(1)) + f(2)) + f(3)

    return pl.pallas_call(
        body, name=name, grid=(r // tr,), in_specs=[pl.BlockSpec((N_CHIPS, tr, C), lambda i: (0, i, 0))],
        out_specs=pl.BlockSpec((tr, C), lambda i: (i, 0)), out_shape=SDS((r, C), F32), compiler_params=_cp("parallel"),
    )(parts)


def _all_reduce_small(buf):
    rows = buf.shape[0]

    def body(x_ref, o_ref, all_ref, send_sems, recv_sems):
        x, y, c = _coords()
        me = 4 * x + 2 * y + c
        all_ref[me] = x_ref[...]
        flips = [(fx, fy, fc) for fx in (0, 1) for fy in (0, 1) for fc in (0, 1)][1:]
        copies = []
        for k, (fx, fy, fc) in enumerate(flips):
            to = (x ^ fx, y ^ fy, c ^ fc)
            copies.append(pltpu.make_async_remote_copy(src_ref=x_ref, dst_ref=all_ref.at[me], send_sem=send_sems.at[k],
                                                       recv_sem=recv_sems.at[k], device_id=to, device_id_type=MESH))
        for cp in copies:
            cp.start()
        for cp in copies:
            cp.wait_recv()
        for cp in copies:
            cp.wait_send()
        acc = all_ref[0]
        for d in range(1, N_DEV):
            acc = acc + all_ref[d]
        o_ref[...] = acc

    vm = pl.BlockSpec(memory_space=pltpu.VMEM)
    return pl.pallas_call(
        body, name="all_reduce_small", in_specs=[vm], out_specs=vm, out_shape=SDS((rows, LANES), F32),
        scratch_shapes=[pltpu.VMEM((N_DEV, rows, LANES), F32), pltpu.SemaphoreType.DMA((N_DEV - 1,)),
                        pltpu.SemaphoreType.DMA((N_DEV - 1,))],
    )(buf)


def _adam_update(w, gv, m, v):
    mn = ADAM_B1 * m + (1.0 - ADAM_B1) * gv
    vn = ADAM_B2 * v + (1.0 - ADAM_B2) * (gv * gv)
    m_hat = mn / (1.0 - ADAM_B1 ** ADAM_STEP)
    v_hat = vn / (1.0 - ADAM_B2 ** ADAM_STEP)
    return -ADAM_LR * (m_hat / (jnp.sqrt(v_hat) + ADAM_EPS) + ADAM_WD * w), mn, vn


def _adamw_halves(w, mine, theirs, m, v, c, name):
    R, C = w.shape
    half = R // 2
    tr = min(128, half)
    nbh = half // tr

    def body(c_ref, w_ref, a_ref, b_ref, m_ref, v_ref, g_ref, d_ref, mo_ref, vo_ref):
        is_mine = (pl.program_id(0) // nbh) == c_ref[0]
        gv = jnp.where(is_mine, a_ref[...], b_ref[...])
        g_ref[...] = gv
        d_ref[...], mo_ref[...], vo_ref[...] = _adam_update(w_ref[...], gv, m_ref[...], v_ref[...])

    blk = pl.BlockSpec((tr, C), lambda i, c_ref: (i, 0))
    hblk = pl.BlockSpec((tr, C), lambda i, c_ref: (i % nbh, 0))
    return pl.pallas_call(
        body, name=name,
        grid_spec=pltpu.PrefetchScalarGridSpec(num_scalar_prefetch=1, grid=(R // tr,),
                                               in_specs=[blk, hblk, hblk, blk, blk], out_specs=(blk,) * 4),
        out_shape=(SDS((R, C), F32),) * 4, compiler_params=_cp("parallel"),
    )(jnp.reshape(c, (1,)).astype(jnp.int32), w, mine, theirs, m, v)


def _adamw(w, g, m, v, name):
    R, C = w.shape
    tr = min(256, R)
    assert R % tr == 0

    def body(w_ref, g_ref, m_ref, v_ref, d_ref, mo_ref, vo_ref):
        d_ref[...], mo_ref[...], vo_ref[...] = _adam_update(w_ref[...], g_ref[...], m_ref[...], v_ref[...])

    blk = pl.BlockSpec((tr, C), lambda i: (i, 0))
    return pl.pallas_call(
        body, name=name, grid=(R // tr,), in_specs=[blk] * 4, out_specs=(blk,) * 3,
        out_shape=(SDS((R, C), F32),) * 3, compiler_params=_cp("parallel"),
    )(w, g, m, v)


def _pack(arrs):
    rows = []
    for a in arrs:
        flat = a.reshape(-1).astype(F32)
        pad = (-flat.shape[0]) % LANES
        rows.append(jnp.pad(flat, (0, pad)).reshape(-1, LANES))
    buf = jnp.concatenate(rows, axis=0)
    pad_rows = (-buf.shape[0]) % 8
    return jnp.pad(buf, ((0, pad_rows), (0, 0)))


def _unpack(buf, shapes):
    out, r = [], 0
    for shp in shapes:
        size = int(np.prod(shp))
        nr = -(-size // LANES)
        out.append(buf[r:r + nr].reshape(-1)[:size].reshape(shp))
        r += nr
    return out


def _pad_lanes(a):
    return jnp.pad(a, ((0, 0), (0, LANES - a.shape[1])))


def kernel(x, ret_w_in, ret_gn_g, ret_w_out, gdn_w_in, gdn_conv_w, gdn_a_log, gdn_dt_bias, gdn_norm_g, gdn_w_out, ln_mix_g, ln_mix_b, mlp_w1, mlp_w2, ln_ffn_g, ln_ffn_b, loss_target, m_ret_w_in, m_ret_gn_g, m_ret_w_out, m_gdn_w_in, m_gdn_conv_w, m_gdn_a_log, m_gdn_dt_bias, m_gdn_norm_g, m_gdn_w_out, m_ln_mix_g, m_ln_mix_b, m_mlp_w1, m_mlp_w2, m_ln_ffn_g, m_ln_ffn_b, v_ret_w_in, v_ret_gn_g, v_ret_w_out, v_gdn_w_in, v_gdn_conv_w, v_gdn_a_log, v_gdn_dt_bias, v_gdn_norm_g, v_gdn_w_out, v_ln_mix_g, v_ln_mix_b, v_mlp_w1, v_mlp_w2, v_ln_ffn_g, v_ln_ffn_b):
    cx, cy = lax.axis_index("x"), lax.axis_index("y")
    chip = 2 * cx + cy

    shards = [ret_w_in[0].astype(BF16), ret_w_out[0].astype(BF16), gdn_w_in[0].astype(BF16),
              gdn_w_out[0].astype(BF16), mlp_w1.astype(BF16), mlp_w2.astype(BF16), gdn_conv_w[0]]
    g_ri, g_ro, g_gi, g_go, g_w1, g_w2, g_cv = _all_gather_chips(shards, [True] * 6 + [False])
    cols = lambda g: jnp.transpose(g, (1, 0, 2)).reshape(g.shape[1], -1)
    wri = cols(g_ri)
    wro = g_ro.reshape(4096, D_MODEL)
    wgi = cols(g_gi)
    wgi_main = wgi[:, :GDN_QKV + 4096]
    wba = jnp.pad(wgi[:, GDN_QKV + 4096:], ((0, 0), (0, LANES - 2 * GDN_HV)))
    wgo = g_go.reshape(4096, D_MODEL)
    w1 = [cols(g_w1[:, i]) for i in range(2)]
    w2 = [g_w2[:, i].reshape(4 * D_MODEL, D_MODEL) for i in range(2)]
    conv_w = cols(g_cv)
    small = dict(ret_gn_g=ret_gn_g, a_log=_pad_lanes(gdn_a_log), dt_bias=_pad_lanes(gdn_dt_bias), norm_g=gdn_norm_g,
                 ln_mix_g=ln_mix_g, ln_mix_b=ln_mix_b, ln_ffn_g=ln_ffn_g, ln_ffn_b=ln_ffn_b)

    loss, grad_x, big, sm = _local_step(x[0], loss_target[0], wri, wro, wgi_main, wba, wgo, w1, w2, conv_w, small)

    dwgi = jnp.concatenate([big["gdn_w_in_main"], big["gdn_w_in_ba"][:, :2 * GDN_HV]], axis=1)
    shard_major = lambda g: jnp.transpose(g.reshape(g.shape[0], N_CHIPS, -1), (1, 0, 2))
    to_reduce = [big["ret_w_in"], big["ret_w_out"].reshape(N_CHIPS, 1024, D_MODEL), shard_major(dwgi),
                 big["gdn_w_out"].reshape(N_CHIPS, 1024, D_MODEL),
                 jnp.concatenate(big["mlp_w1"], axis=1),
                 jnp.concatenate([g.reshape(N_CHIPS, D_MODEL, D_MODEL) for g in big["mlp_w2"]], axis=1)]
    reduced = _reduce_scatter(to_reduce)

    small_names = ["ret_gn_g", "a_log", "dt_bias", "norm_g", "ln_mix_g", "ln_mix_b", "ln_ffn_g", "ln_ffn_b", "conv_w"]
    small_shapes = [(1, 4096), (1, LANES), (1, LANES), (1, GDN_D), (2, D_MODEL), (2, D_MODEL), (2, D_MODEL),
                    (2, D_MODEL), (4, GDN_QKV)]
    red = _all_reduce_small(_pack([loss] + [sm[k] for k in small_names]))
    red_loss, *red_small = _unpack(red, [(1, 1)] + small_shapes)
    gs = dict(zip(small_names, red_small))
    g_conv = lax.dynamic_slice_in_dim(gs["conv_w"], chip * 2048, 2048, axis=1)
    g_a_log, g_dt_bias = gs["a_log"][:, :GDN_HV], gs["dt_bias"][:, :GDN_HV]

    big_w = [(ret_w_in, m_ret_w_in, v_ret_w_in), (ret_w_out, m_ret_w_out, v_ret_w_out),
             (gdn_w_in, m_gdn_w_in, v_gdn_w_in), (gdn_w_out, m_gdn_w_out, v_gdn_w_out),
             (mlp_w1, m_mlp_w1, v_mlp_w1), (mlp_w2, m_mlp_w2, v_mlp_w2)]
    core = lax.axis_index("c")
    big_out = []
    for i, ((w_, m_, v_), (mine, theirs)) in enumerate(zip(big_w, reduced)):
        two_d = lambda a: a.reshape(-1, a.shape[-1])
        res = _adamw_halves(two_d(w_), mine, theirs, two_d(m_), two_d(v_), core, "adamw_%d" % i)
        big_out.append(tuple(a.reshape(w_.shape) for a in res))
    sm_w = [(ret_gn_g, m_ret_gn_g, v_ret_gn_g, gs["ret_gn_g"]), (gdn_conv_w, m_gdn_conv_w, v_gdn_conv_w, g_conv),
            (gdn_a_log, m_gdn_a_log, v_gdn_a_log, g_a_log), (gdn_dt_bias, m_gdn_dt_bias, v_gdn_dt_bias, g_dt_bias),
            (gdn_norm_g, m_gdn_norm_g, v_gdn_norm_g, gs["norm_g"]), (ln_mix_g, m_ln_mix_g, v_ln_mix_g, gs["ln_mix_g"]),
            (ln_mix_b, m_ln_mix_b, v_ln_mix_b, gs["ln_mix_b"]), (ln_ffn_g, m_ln_ffn_g, v_ln_ffn_g, gs["ln_ffn_g"]),
            (ln_ffn_b, m_ln_ffn_b, v_ln_ffn_b, gs["ln_ffn_b"])]
    sm_shapes = [w_.shape for w_, _, _, _ in sm_w]
    d_s, nm_s, nv_s = _adamw(_pack([w_ for w_, _, _, _ in sm_w]), _pack([g_ for _, _, _, g_ in sm_w]),
                             _pack([m_ for _, m_, _, _ in sm_w]), _pack([v_ for _, _, v_, _ in sm_w]), "adamw_small")
    d_s, nm_s, nv_s = (_unpack(a, sm_shapes) for a in (d_s, nm_s, nv_s))
    sm_out = [(g_.reshape(w_.shape), d_s[i], nm_s[i], nv_s[i]) for i, (w_, _, _, g_) in enumerate(sm_w)]

    per_w = [big_out[0], sm_out[0], big_out[1], big_out[2], sm_out[1], sm_out[2], sm_out[3], sm_out[4], big_out[3],
             sm_out[5], sm_out[6], big_out[4], big_out[5], sm_out[7], sm_out[8]]
    outs = [red_loss.reshape(()), grad_x[None]]
    for kind in range(4):
        outs.extend(t[kind] for t in per_w)
    return tuple(outs)
```

```python
import functools

import numpy as np
import jax
import jax.numpy as jnp
from jax import lax
from jax.experimental import pallas as pl
from jax.experimental.pallas import tpu as pltpu

F32 = jnp.float32
BF16 = jnp.bfloat16
MESH = pl.DeviceIdType.MESH
SDS = jax.ShapeDtypeStruct

D_MODEL = 2048
CHUNK = 64
RET_HEADS, RET_DK, RET_DV = 8, 256, 512
GDN_HV, GDN_D = 32, 128
HP = 2
GDN_QKV = 8192
ALPHA = 4.0 ** 0.25
LN_EPS, GN_EPS, RMS_EPS, L2_EPS = 1e-5, 1e-6, 1e-6, 1e-6
ADAM_LR, ADAM_B1, ADAM_B2, ADAM_EPS, ADAM_WD, ADAM_STEP = 0.001, 0.9, 0.999, 1e-8, 0.01, 10

VMEM_LIMIT_BYTES = 56 * 1024 * 1024
RT = 256
CT = 256
ROWS = 256
LANES = 128
N_CHIPS = 4
N_DEV = 8


def _cp(*sem):
    return pltpu.CompilerParams(dimension_semantics=sem, vmem_limit_bytes=VMEM_LIMIT_BYTES)


def _dot(a, b):
    return jnp.dot(a, b, preferred_element_type=F32)


def _dot_nt(a, b):
    return lax.dot_general(a, b, (((1,), (1,)), ((), ())), preferred_element_type=F32)


def _dot_tn(a, b):
    return lax.dot_general(a, b, (((0,), (0,)), ((), ())), preferred_element_type=F32)


def _split2(x):
    hi = x.astype(BF16)
    lo = (x - hi.astype(F32)).astype(BF16)
    return hi, lo


def _dotx3(a, b):
    ah, al = _split2(a)
    bh, bl = _split2(b)
    return _dot(ah, bh) + (_dot(ah, bl) + _dot(al, bh))


def _dot_exact_l(l_bf16, x):
    hi = x.astype(BF16)
    r = x - hi.astype(F32)
    mid = r.astype(BF16)
    lo = (r - mid.astype(F32)).astype(BF16)
    return _dot(l_bf16, hi) + (_dot(l_bf16, mid) + _dot(l_bf16, lo))


def _sigmoid(x):
    return 1.0 / (1.0 + jnp.exp(-x))


def _iota(shape, dim):
    return lax.broadcasted_iota(jnp.int32, shape, dim)


class _Side:
    def __init__(self, arrays, out_shapes, phases, counts):
        self.arrays, self.out_shapes, self.phases, self.counts = list(arrays), list(out_shapes), phases, counts

    def sem_shapes(self):
        return [pltpu.SemaphoreType.DMA((sum(self.counts),))] * 2

    def ops(self, ins, outs, send_sems, recv_sems):
        def copies(ph):
            off = sum(self.counts[:ph])
            return [pltpu.make_async_remote_copy(src_ref=src, dst_ref=dst, send_sem=send_sems.at[off + i],
                                                 recv_sem=recv_sems.at[off + i], device_id=to, device_id_type=MESH)
                    for i, (src, dst, to) in enumerate(self.phases[ph](_coords(), ins, outs))]

        def start(ph):
            for cp in copies(ph):
                cp.start()

        def wait(ph):
            cps = copies(ph)
            for cp in cps:
                cp.wait_recv()
            for cp in cps:
                cp.wait_send()

        return start, wait


def _comm_call(name, side):
    n = len(side.arrays)

    def body(*refs):
        start, wait = side.ops(refs[:n], refs[n:2 * n], refs[2 * n], refs[2 * n + 1])
        for ph in range(len(side.phases)):
            start(ph)
            wait(ph)

    return pl.pallas_call(body, name=name, in_specs=[HBM_SPEC] * n, out_specs=[HBM_SPEC] * n,
                          out_shape=side.out_shapes, scratch_shapes=side.sem_shapes())(*side.arrays)


SIDE_SWITCH = 0.75


def _mm(a, b, mode, name, *, out_dtype=F32, tm=1024, tn=1024, tk=2048, epi=None, extra=None, scale=1.0,
        shard_major=False, side=None):
    if mode == "nn":
        (M, K), (K2, N) = a.shape, b.shape
    elif mode == "nt":
        (M, K), (N, K2) = a.shape, b.shape
    else:
        (K, M), (K2, N) = a.shape, b.shape
    assert K == K2, (a.shape, b.shape, mode)
    tm, tn, tk = min(tm, M), min(tn, N), min(tk, K)
    assert M % tm == 0 and N % tn == 0 and K % tk == 0, (M, N, K, tm, tn, tk)
    nk = K // tk
    dims = {"nn": (((1,), (0,)), ((), ())), "nt": (((1,), (1,)), ((), ())), "tn": (((0,), (0,)), ((), ()))}[mode]
    if mode == "tn":
        a_spec = pl.BlockSpec((tk, tm), lambda i, j, k: (k, i))
    else:
        a_spec = pl.BlockSpec((tm, tk), lambda i, j, k: (i, k))
    if mode == "nt":
        b_spec = pl.BlockSpec((tn, tk), lambda i, j, k: (j, k))
    else:
        b_spec = pl.BlockSpec((tk, tn), lambda i, j, k: (k, j))
    tile = pl.BlockSpec((tm, tn), lambda i, j, k: (i, j))
    in_specs, ins = [a_spec, b_spec], [a, b]
    if extra is not None:
        in_specs.append(tile)
        ins.append(extra)
    if epi == "relu2":
        out_shape = (SDS((M, N), F32), SDS((M, N), BF16))
        out_specs = (tile, tile)
    elif shard_major:
        per = (N // N_CHIPS) // tn
        assert per * tn * N_CHIPS == N
        out_shape = (SDS((N_CHIPS, M, N // N_CHIPS), out_dtype),)
        out_specs = (pl.BlockSpec((None, tm, tn), lambda i, j, k: (j // per, i, j % per)),)
    else:
        out_shape = (SDS((M, N), out_dtype),)
        out_specs = (tile,)
    n_out = len(out_shape)
    n_side = 0 if side is None else len(side.arrays)
    grid = (M // tm, N // tn, nk)
    steps = grid[0] * grid[1] * grid[2]

    def body(*refs):
        a_ref, b_ref = refs[0], refs[1]
        pos = 2
        x_ref = None
        if extra is not None:
            x_ref = refs[pos]
            pos += 1
        side_ins = refs[pos:pos + n_side]
        pos += n_side
        o_refs = refs[pos:pos + n_out]
        side_outs = refs[pos + n_out:pos + n_out + n_side]
        pos += n_out + n_side
        acc_ref = None
        if nk > 1:
            acc_ref = refs[pos]
            pos += 1
        if side is not None:
            step = (pl.program_id(0) * grid[1] + pl.program_id(1)) * grid[2] + pl.program_id(2)
            start, wait = side.ops(side_ins, side_outs, refs[pos], refs[pos + 1])
            n_ph = len(side.phases)
            assert n_ph in (1, 2) and steps >= 4
            pl.when(step == 0)(lambda: start(0))
            if n_ph == 2:
                @pl.when(step == int(steps * SIDE_SWITCH))
                def _():
                    wait(0)
                    start(1)

        def prod():
            av, bv = a_ref[...], b_ref[...]
            if av.dtype != BF16:
                av = av.astype(BF16)
            if bv.dtype != BF16:
                bv = bv.astype(BF16)
            return lax.dot_general(av, bv, dims, preferred_element_type=F32)

        def finish(acc):
            if epi == "relu2":
                o_refs[0][...] = acc
                r = jnp.maximum(acc, 0.0)
                o_refs[1][...] = (r * r).astype(BF16)
            elif epi == "drelu2":
                o_refs[0][...] = (acc * (2.0 * jnp.maximum(x_ref[...], 0.0))).astype(out_dtype)
            elif epi == "add":
                o_refs[0][...] = (acc + scale * x_ref[...]).astype(out_dtype)
            else:
                o_refs[0][...] = acc.astype(out_dtype)

        if nk == 1:
            finish(prod())
        else:
            k = pl.program_id(2)

            @pl.when(k == 0)
            def _():
                acc_ref[...] = prod()

            @pl.when(k > 0)
            def _():
                acc_ref[...] += prod()

            @pl.when(k == nk - 1)
            def _():
                finish(acc_ref[...])

        if side is not None:
            pl.when(step == steps - 1)(lambda: wait(len(side.phases) - 1))

    scratch = [pltpu.VMEM((tm, tn), F32)] if nk > 1 else []
    if side is None:
        out = pl.pallas_call(
            body, name=name, grid=grid, in_specs=in_specs, out_specs=out_specs, out_shape=out_shape,
            scratch_shapes=scratch, compiler_params=_cp("parallel", "parallel", "arbitrary"),
        )(*ins)
        return out if n_out > 1 else out[0]
    out = pl.pallas_call(
        body, name=name, grid=grid, in_specs=in_specs + [HBM_SPEC] * n_side,
        out_specs=tuple(out_specs) + (HBM_SPEC,) * n_side, out_shape=tuple(out_shape) + tuple(side.out_shapes),
        scratch_shapes=scratch + side.sem_shapes(), compiler_params=_cp("arbitrary", "arbitrary", "arbitrary"),
    )(*ins, *side.arrays)
    res = out[:n_out]
    return (res if n_out > 1 else res[0]), list(out[n_out:])


def _ln_stats(z):
    mu = jnp.mean(z, -1, keepdims=True)
    zc = z - mu
    var = jnp.mean(zc * zc, -1, keepdims=True)
    rstd = lax.rsqrt(var + LN_EPS)
    return zc * rstd, rstd


def _ln_fwd(xin, sub, g, b, name):
    S, Dm = xin.shape
    row = pl.BlockSpec((ROWS, Dm), lambda t: (t, 0))
    vec = pl.BlockSpec((1, Dm), lambda t: (0, 0))

    def body(x_ref, s_ref, g_ref, b_ref, o_ref, ob_ref, z_ref):
        z = ALPHA * x_ref[...] + s_ref[...]
        xh, _ = _ln_stats(z)
        o = xh * g_ref[...] + b_ref[...]
        o_ref[...] = o
        ob_ref[...] = o.astype(BF16)
        z_ref[...] = z

    return pl.pallas_call(
        body, name=name, grid=(S // ROWS,), in_specs=[row, row, vec, vec], out_specs=(row, row, row),
        out_shape=(SDS((S, Dm), F32), SDS((S, Dm), BF16), SDS((S, Dm), F32)), compiler_params=_cp("parallel"),
    )(xin, sub, g, b)


def _ln_bwd(dout, z, g, name):
    S, Dm = z.shape
    row = pl.BlockSpec((ROWS, Dm), lambda t: (t, 0))
    vec = pl.BlockSpec((1, Dm), lambda t: (0, 0))

    def body(d_ref, z_ref, g_ref, dz_ref, dzb_ref, dg_ref, db_ref):
        t = pl.program_id(0)
        xh, rstd = _ln_stats(z_ref[...])
        d = d_ref[...]
        dxh = d * g_ref[...]
        m1 = jnp.mean(dxh, -1, keepdims=True)
        m2 = jnp.mean(dxh * xh, -1, keepdims=True)
        dz = rstd * (dxh - m1 - xh * m2)
        dz_ref[...] = dz
        dzb_ref[...] = dz.astype(BF16)
        pg = jnp.sum(d * xh, axis=0, keepdims=True)
        pb = jnp.sum(d, axis=0, keepdims=True)

        @pl.when(t == 0)
        def _():
            dg_ref[...] = pg
            db_ref[...] = pb

        @pl.when(t > 0)
        def _():
            dg_ref[...] += pg
            db_ref[...] += pb

    return pl.pallas_call(
        body, name=name, grid=(S // ROWS,), in_specs=[row, row, vec], out_specs=(row, row, vec, vec),
        out_shape=(SDS((S, Dm), F32), SDS((S, Dm), BF16), SDS((1, Dm), F32), SDS((1, Dm), F32)),
        compiler_params=_cp("arbitrary"),
    )(dout, z, g)


def _loss_fwd_bwd(y, tgt):
    S, Dm = y.shape
    row = pl.BlockSpec((ROWS, Dm), lambda t: (t, 0))
    one = pl.BlockSpec((1, 1), lambda t: (0, 0))

    def body(y_ref, t_ref, dy_ref, l_ref):
        t = pl.program_id(0)
        diff = y_ref[...] - t_ref[...]
        dy_ref[...] = diff * (1.0 / Dm)
        part = jnp.sum(jnp.sum(diff * diff, axis=1, keepdims=True), axis=0, keepdims=True) * (0.5 / Dm)

        @pl.when(t == 0)
        def _():
            l_ref[...] = part

        @pl.when(t > 0)
        def _():
            l_ref[...] += part

    return pl.pallas_call(
        body, name="loss", grid=(S // ROWS,), in_specs=[row, row], out_specs=(row, one),
        out_shape=(SDS((S, Dm), F32), SDS((1, 1), F32)), compiler_params=_cp("arbitrary"),
    )(y, tgt)


def _ret_consts():
    h = np.arange(RET_HEADS, dtype=np.float64)
    lg = np.log1p(-np.exp2(-5.0 - h))
    return jnp.asarray(np.concatenate([lg, np.exp(lg * RT)]).astype(np.float32))


def _rope_tables(S):
    half = RET_DK // 2
    inv = 10000.0 ** (-jnp.arange(half, dtype=F32) / half)
    ang = jnp.arange(S).astype(F32)[:, None] * inv[None, :]
    return jnp.cos(ang), jnp.sin(ang)


def _ret_masks(lgh):
    ri, ci = _iota((RT, RT), 0), _iota((RT, RT), 1)
    visible = (ci >> 6) <= (ri >> 6)
    m = jnp.where(visible, jnp.exp(lgh * jnp.abs(ri - ci).astype(F32)), 0.0)
    pos = _iota((RT, 1), 0).astype(F32)
    return m, jnp.exp(lgh * (pos + 1.0)), jnp.exp(lgh * (RT - 1.0 - pos))


def _ret_rot(h0, cos, sin):
    S = h0.shape[0]
    half = RET_DK // 2

    def body(q_ref, k_ref, v_ref, c_ref, s_ref, qo_ref, ko_ref, vo_ref):
        c, s = c_ref[...], s_ref[...]

        def rot(t):
            t1, t2 = t[:, :half], t[:, half:]
            return jnp.concatenate([t1 * c - t2 * s, t1 * s + t2 * c], axis=-1)

        qo_ref[...] = rot(q_ref[...]).astype(BF16)
        ko_ref[...] = (rot(k_ref[...]) * (RET_DK ** -0.5)).astype(BF16)
        vo_ref[...] = v_ref[...].astype(BF16)

    qk = lambda off: pl.BlockSpec((RT, RET_DK), lambda t, h: (t, off + h))
    vv = lambda off: pl.BlockSpec((RT, RET_DV), lambda t, h: (t, off + h))
    tab = pl.BlockSpec((RT, half), lambda t, h: (t, 0))
    return pl.pallas_call(
        body, name="ret_rot", grid=(S // RT, RET_HEADS), in_specs=[qk(0), qk(RET_HEADS), vv(RET_HEADS), tab, tab],
        out_specs=(qk(0), qk(0), vv(0)),
        out_shape=(SDS((S, 2048), BF16), SDS((S, 2048), BF16), SDS((S, 4096), BF16)),
        compiler_params=_cp("parallel", "parallel"),
    )(h0, h0, h0, cos, sin)


def _ret_fwd(q, k, v, consts):
    S = q.shape[0]
    nt = S // RT

    def body(c_ref, q_ref, k_ref, v_ref, y_ref, st_ref, s_scr):
        h, t = pl.program_id(0), pl.program_id(1)

        @pl.when(t == 0)
        def _():
            s_scr[...] = jnp.zeros_like(s_scr)

        lgh, cdec = c_ref[h], c_ref[RET_HEADS + h]
        m, dq, dk = _ret_masks(lgh)
        qv, kv, vv = q_ref[...], k_ref[...], v_ref[...]
        p = (_dot_nt(qv, kv) * m).astype(BF16)
        sp = s_scr[...]
        spb = sp.astype(BF16)
        st_ref[...] = spb
        qd = (qv.astype(F32) * dq).astype(BF16)
        kd = (kv.astype(F32) * dk).astype(BF16)
        y_ref[...] = _dot(p, vv) + _dot(qd, spb)
        s_scr[...] = sp * cdec + _dot_tn(kd, vv)

    qk = pl.BlockSpec((RT, RET_DK), lambda h, t: (t, h))
    vs = pl.BlockSpec((RT, RET_DV), lambda h, t: (t, h))
    return pl.pallas_call(
        body, name="ret_fwd", grid=(RET_HEADS, nt),
        in_specs=[pl.BlockSpec(memory_space=pltpu.SMEM), qk, qk, vs],
        out_specs=(vs, pl.BlockSpec((None, None, RET_DK, RET_DV), lambda h, t: (h, t, 0, 0))),
        out_shape=(SDS((S, 4096), F32), SDS((RET_HEADS, nt, RET_DK, RET_DV), BF16)),
        scratch_shapes=[pltpu.VMEM((RET_DK, RET_DV), F32)], compiler_params=_cp("parallel", "arbitrary"),
    )(consts, q, k, v)


def _ret_bwd(q, k, v, dy, states, consts, cos, sin):
    S = q.shape[0]
    nt = S // RT
    half = RET_DK // 2

    def body(c_ref, q_ref, k_ref, v_ref, dy_ref, st_ref, cos_ref, sin_ref, dq_ref, dk_ref, dv_ref, ds_scr):
        h, t = pl.program_id(0), pl.program_id(1)

        @pl.when(t == 0)
        def _():
            ds_scr[...] = jnp.zeros_like(ds_scr)

        lgh, cdec = c_ref[h], c_ref[RET_HEADS + h]
        m, dqc, dkc = _ret_masks(lgh)
        qv, kv, vv, dyv, spb = q_ref[...], k_ref[...], v_ref[...], dy_ref[...], st_ref[...]
        p = (_dot_nt(qv, kv) * m).astype(BF16)
        qd = (qv.astype(F32) * dqc).astype(BF16)
        kd = (kv.astype(F32) * dkc).astype(BF16)
        dsn = ds_scr[...]
        dsb = dsn.astype(BF16)
        dsc = (_dot_nt(dyv, vv) * m).astype(BF16)
        dq = _dot(dsc, kv) + _dot_nt(dyv, spb) * dqc
        dk = _dot_tn(dsc, qv) + _dot_nt(vv, dsb) * dkc
        dv_ref[...] = (_dot_tn(p, dyv) + _dot(kd, dsb)).astype(BF16)
        ds_scr[...] = dsn * cdec + _dot_tn(qd, dyv)
        c, s = cos_ref[...], sin_ref[...]

        def unrot(d):
            d1, d2 = d[:, :half], d[:, half:]
            return jnp.concatenate([d1 * c + d2 * s, d2 * c - d1 * s], axis=-1)

        dq_ref[...] = unrot(dq).astype(BF16)
        dk_ref[...] = (unrot(dk) * (RET_DK ** -0.5)).astype(BF16)

    rev = lambda t: nt - 1 - t
    qk = pl.BlockSpec((RT, RET_DK), lambda h, t: (rev(t), h))
    vs = pl.BlockSpec((RT, RET_DV), lambda h, t: (rev(t), h))
    tab = pl.BlockSpec((RT, half), lambda h, t: (rev(t), 0))
    return pl.pallas_call(
        body, name="ret_bwd", grid=(RET_HEADS, nt),
        in_specs=[pl.BlockSpec(memory_space=pltpu.SMEM), qk, qk, vs, vs,
                  pl.BlockSpec((None, None, RET_DK, RET_DV), lambda h, t: (h, rev(t), 0, 0)), tab, tab],
        out_specs=(qk, qk, vs),
        out_shape=(SDS((S, 2048), BF16), SDS((S, 2048), BF16), SDS((S, 4096), BF16)),
        scratch_shapes=[pltpu.VMEM((RET_DK, RET_DV), F32)], compiler_params=_cp("parallel", "arbitrary"),
    )(consts, q, k, v, dy, states, cos, sin)


def _gn_stats(y):
    mu = jnp.mean(y, -1, keepdims=True)
    yc = y - mu
    var = jnp.mean(yc * yc, -1, keepdims=True)
    rstd = lax.rsqrt(var + GN_EPS)
    return yc * rstd, rstd


def _ret_post_fwd(y, h0, gn_g):
    S = y.shape[0]

    def body(y_ref, gate_ref, g_ref, o_ref):
        yn, _ = _gn_stats(y_ref[...])
        gate = gate_ref[...]
        o_ref[...] = (gate * _sigmoid(gate) * (yn * g_ref[...])).astype(BF16)

    vs = lambda off: pl.BlockSpec((RT, RET_DV), lambda h, t: (t, off + h))
    return pl.pallas_call(
        body, name="ret_post_fwd", grid=(RET_HEADS, S // RT),
        in_specs=[vs(0), vs(2 * RET_HEADS), pl.BlockSpec((1, RET_DV), lambda h, t: (0, h))], out_specs=vs(0),
        out_shape=SDS((S, 4096), BF16), compiler_params=_cp("parallel", "parallel"),
    )(y, h0, gn_g)


def _ret_post_bwd(do, y, h0, gn_g):
    S = y.shape[0]

    def body(do_ref, y_ref, gate_ref, g_ref, dy_ref, dgate_ref, dg_ref):
        t = pl.program_id(1)
        yn, rstd = _gn_stats(y_ref[...])
        gate, g, dov = gate_ref[...], g_ref[...], do_ref[...]
        sg = _sigmoid(gate)
        dgate_ref[...] = (dov * (yn * g) * (sg * (1.0 + gate * (1.0 - sg)))).astype(BF16)
        dyg = dov * (gate * sg)
        dyn = dyg * g
        m1 = jnp.mean(dyn, -1, keepdims=True)
        m2 = jnp.mean(dyn * yn, -1, keepdims=True)
        dy_ref[...] = (rstd * (dyn - m1 - yn * m2)).astype(BF16)
        pg = jnp.sum(dyg * yn, axis=0, keepdims=True)

        @pl.when(t == 0)
        def _():
            dg_ref[...] = pg

        @pl.when(t > 0)
        def _():
            dg_ref[...] += pg

    vs = lambda off: pl.BlockSpec((RT, RET_DV), lambda h, t: (t, off + h))
    vec = pl.BlockSpec((1, RET_DV), lambda h, t: (0, h))
    return pl.pallas_call(
        body, name="ret_post_bwd", grid=(RET_HEADS, S // RT), in_specs=[vs(0), vs(0), vs(2 * RET_HEADS), vec],
        out_specs=(vs(0), vs(0), vec),
        out_shape=(SDS((S, 4096), BF16), SDS((S, 4096), BF16), SDS((1, 4096), F32)),
        compiler_params=_cp("parallel", "arbitrary"),
    )(do, y, h0, gn_g)


def _conv_taps(x_ref, halo_ref, w_ref, ext_scr, t):
    ext_scr[0:8, :] = jnp.where(t == 0, 0.0, halo_ref[...])
    ext_scr[8:, :] = x_ref[...]
    w = w_ref[...]
    n = x_ref.shape[0]
    acc = w[3:4, :] * ext_scr[8:, :]
    for j in range(3):
        acc = acc + w[j:j + 1, :] * ext_scr[pl.ds(5 + j, n), :]
    return acc


def _gdn_conv_fwd(h1, conv_w, kind):
    S = h1.shape[0]
    base = {"q": 0, "k": 1, "v": 2}[kind]
    ncb = 2 if kind == "v" else 1
    C = 2048

    def body(x_ref, halo_ref, w_ref, o_ref, ext_scr):
        acc = _conv_taps(x_ref, halo_ref, w_ref, ext_scr, pl.program_id(0))
        c = acc * _sigmoid(acc)
        if kind == "v":
            o_ref[...] = c.astype(BF16)
        else:
            scale = GDN_D ** -0.5 if kind == "q" else 1.0
            for hh in range(C // GDN_D):
                ch = c[:, hh * GDN_D:(hh + 1) * GDN_D]
                r = lax.rsqrt(jnp.sum(ch * ch, -1, keepdims=True) + L2_EPS)
                o_ref[:, hh * GDN_D:(hh + 1) * GDN_D] = (ch * (r * scale)).astype(BF16)

    hb = ROWS // 8
    return pl.pallas_call(
        body, name="gdn_conv_fwd_" + kind, grid=(S // ROWS, ncb),
        in_specs=[pl.BlockSpec((ROWS, C), lambda t, j: (t, base + j)),
                  pl.BlockSpec((8, C), lambda t, j: (jnp.maximum(t * hb - 1, 0), base + j)),
                  pl.BlockSpec((4, C), lambda t, j: (0, base + j))],
        out_specs=pl.BlockSpec((ROWS, C), lambda t, j: (t, j)), out_shape=SDS((S, C * ncb), BF16),
        scratch_shapes=[pltpu.VMEM((ROWS + 8, C), F32)], compiler_params=_cp("parallel", "parallel"),
    )(h1, h1, conv_w)


def _gdn_conv_bwd_act(h1, conv_w, dn, kind, buf=None):
    S = h1.shape[0]
    base = {"q": 0, "k": 1, "v": 2}[kind]
    ncb = 2 if kind == "v" else 1
    C = 2048

    def body(x_ref, halo_ref, w_ref, dn_ref, *rest):
        o_ref, ext_scr = rest[-2], rest[-1]
        acc = _conv_taps(x_ref, halo_ref, w_ref, ext_scr, pl.program_id(0))
        sg = _sigmoid(acc)
        dsilu = sg * (1.0 + acc * (1.0 - sg))
        if kind == "v":
            o_ref[...] = dn_ref[...] * dsilu
        else:
            c = acc * sg
            scale = GDN_D ** -0.5 if kind == "q" else 1.0
            for hh in range(C // GDN_D):
                sl = slice(hh * GDN_D, (hh + 1) * GDN_D)
                ch, dnh = c[:, sl], dn_ref[:, sl]
                r = lax.rsqrt(jnp.sum(ch * ch, -1, keepdims=True) + L2_EPS)
                proj = jnp.sum(dnh * ch, -1, keepdims=True)
                o_ref[:, sl] = (scale * r) * (dnh - ch * (proj * r * r)) * dsilu[:, sl]

    hb = ROWS // 8
    return pl.pallas_call(
        body, name="gdn_conv_bwd_act_" + kind, grid=(S // ROWS, ncb),
        in_specs=[pl.BlockSpec((ROWS, C), lambda t, j: (t, base + j)),
                  pl.BlockSpec((8, C), lambda t, j: (jnp.maximum(t * hb - 1, 0), base + j)),
                  pl.BlockSpec((4, C), lambda t, j: (0, base + j)),
                  pl.BlockSpec((ROWS, C), lambda t, j: (t, j))] + ([] if buf is None else [HBM_SPEC]),
        out_specs=pl.BlockSpec((ROWS, C), lambda t, j: (t, base + j)), out_shape=SDS((S, GDN_QKV), F32),
        input_output_aliases={} if buf is None else {4: 0},
        scratch_shapes=[pltpu.VMEM((ROWS + 8, C), F32)], compiler_params=_cp("parallel", "parallel"),
    )(*((h1, h1, conv_w, dn) + (() if buf is None else (buf,))))


def _gdn_conv_bwd_in(h1, conv_w, dacc, dh1_buf):
    S = h1.shape[0]
    C = 2048
    nt = S // ROWS
    hb = ROWS // 8

    def body(x_ref, halo_ref, w_ref, d_ref, dhalo_ref, buf_ref, di_ref, dw_ref, ext_scr, dext_scr):
        t = pl.program_id(1)
        ext_scr[0:8, :] = jnp.where(t == 0, 0.0, halo_ref[...])
        ext_scr[8:, :] = x_ref[...]
        d = d_ref[...]
        dext_scr[0:ROWS, :] = d
        dext_scr[ROWS:, :] = jnp.where(t == nt - 1, 0.0, dhalo_ref[...])
        w = w_ref[...]
        di = w[3:4, :] * d
        for j in range(3):
            di = di + w[j:j + 1, :] * dext_scr[pl.ds(3 - j, ROWS), :]
        di_ref[...] = di.astype(BF16)
        rows = [jnp.sum(d * ext_scr[pl.ds(5 + j, ROWS), :], axis=0, keepdims=True) for j in range(4)]
        pw = jnp.concatenate(rows, axis=0)

        @pl.when(t == 0)
        def _():
            dw_ref[...] = pw

        @pl.when(t > 0)
        def _():
            dw_ref[...] += pw

    return pl.pallas_call(
        body, name="gdn_conv_bwd_in", grid=(GDN_QKV // C, nt),
        in_specs=[pl.BlockSpec((ROWS, C), lambda j, t: (t, j)),
                  pl.BlockSpec((8, C), lambda j, t: (jnp.maximum(t * hb - 1, 0), j)),
                  pl.BlockSpec((4, C), lambda j, t: (0, j)),
                  pl.BlockSpec((ROWS, C), lambda j, t: (t, j)),
                  pl.BlockSpec((8, C), lambda j, t: (jnp.minimum((t + 1) * hb, nt * hb - 1), j)), HBM_SPEC],
        out_specs=(pl.BlockSpec((ROWS, C), lambda j, t: (t, j)), pl.BlockSpec((4, C), lambda j, t: (0, j))),
        out_shape=(SDS(dh1_buf.shape, BF16), SDS((4, GDN_QKV), F32)), input_output_aliases={5: 0},
        scratch_shapes=[pltpu.VMEM((ROWS + 8, C), F32), pltpu.VMEM((ROWS + 8, C), F32)],
        compiler_params=_cp("parallel", "arbitrary"),
    )(h1, h1, conv_w, dacc, dacc, dh1_buf)


def _chunk_masks():
    ri, ci = _iota((CT, CT), 0), _iota((CT, CT), 1)
    same = (ri >> 6) == (ci >> 6)
    return same, same & (ri >= ci), same & (ri > ci), same & (ri <= ci), ri == ci


def _fold_dup(m):
    h = m[:, :LANES] + m[:, LANES:]
    return h + pltpu.roll(h, CHUNK, axis=1)


def _unfold_bd(d, same):
    return jnp.where(same, jnp.concatenate([d, d], axis=1), 0.0)


def _softplus(x):
    return jnp.maximum(x, 0.0) + jnp.log(1.0 + jnp.exp(-jnp.abs(x)))


def _gdn_scal_fwd(ba, a_log, dt_bias):
    S = ba.shape[0]

    def body(ba_ref, al_ref, dt_ref, beta_ref, g_ref, gc_ref):
        bav = ba_ref[...]
        beta_ref[...] = _sigmoid(bav)
        a = pltpu.roll(bav, LANES - GDN_HV, axis=1)
        g = -jnp.exp(al_ref[...]) * _softplus(a + dt_ref[...])
        g_ref[...] = g
        causal = _chunk_masks()[1]
        gc_ref[...] = _dot_exact_l(causal.astype(BF16), g)

    row = pl.BlockSpec((CT, LANES), lambda t: (t, 0))
    vec = pl.BlockSpec((1, LANES), lambda t: (0, 0))
    return pl.pallas_call(
        body, name="gdn_scal_fwd", grid=(S // CT,), in_specs=[row, vec, vec], out_specs=(row, row, row),
        out_shape=(SDS((S, LANES), F32),) * 3, compiler_params=_cp("parallel"),
    )(ba, a_log, dt_bias)


def _gdn_scal_bwd(ba, a_log, dt_bias, g, dbeta, dgc):
    S = ba.shape[0]

    def body(ba_ref, al_ref, dt_ref, g_ref, dbeta_ref, dgc_ref, dba_ref, dal_ref, ddt_ref):
        t = pl.program_id(0)
        bav = ba_ref[...]
        beta = _sigmoid(bav)
        db = dbeta_ref[...] * beta * (1.0 - beta)
        a = pltpu.roll(bav, LANES - GDN_HV, axis=1)
        dgv = _dot_exact_l(_chunk_masks()[3].astype(BF16), dgc_ref[...])
        da = dgv * (-jnp.exp(al_ref[...])) * _sigmoid(a + dt_ref[...])
        lane = _iota(bav.shape, 1)
        da_sh = pltpu.roll(da, GDN_HV, axis=1)
        dba = jnp.where(lane < GDN_HV, db, jnp.where(lane < 2 * GDN_HV, da_sh, 0.0))
        dba_ref[...] = dba.astype(BF16)
        keep = lane < GDN_HV
        pal = jnp.sum(jnp.where(keep, dgv * g_ref[...], 0.0), axis=0, keepdims=True)
        pdt = jnp.sum(jnp.where(keep, da, 0.0), axis=0, keepdims=True)

        @pl.when(t == 0)
        def _():
            dal_ref[...] = pal
            ddt_ref[...] = pdt

        @pl.when(t > 0)
        def _():
            dal_ref[...] += pal
            ddt_ref[...] += pdt

    row = pl.BlockSpec((CT, LANES), lambda t: (t, 0))
    vec = pl.BlockSpec((1, LANES), lambda t: (0, 0))
    return pl.pallas_call(
        body, name="gdn_scal_bwd", grid=(S // CT,), in_specs=[row, vec, vec, row, row, row],
        out_specs=(row, vec, vec), out_shape=(SDS((S, LANES), BF16), SDS((1, LANES), F32), SDS((1, LANES), F32)),
        compiler_params=_cp("arbitrary"),
    )(ba, a_log, dt_bias, g, dbeta, dgc)


def _sel_col(x, h):
    return jnp.sum(jnp.where(_iota(x.shape, 1) == h, x, 0.0), axis=1, keepdims=True)


def _decay(gcol, causal):
    gm = jnp.broadcast_to(gcol, (CT, CT))
    diff = gm - gm.T
    return jnp.where(causal, jnp.exp(jnp.where(causal, diff, 0.0)), 0.0)


def _gdn_chunk_fwd(k, v, beta, gc):
    S = k.shape[0]
    nt = S // CT

    def body(k_ref, v_ref, beta_ref, gc_ref, t_ref, u_ref, w_ref):
        kh = pl.program_id(1)
        same, causal, strict, _, _ = _chunk_masks()
        kv = k_ref[...]
        kf = kv.astype(F32)
        kk = _dot_nt(kv, kv)
        eye_dup = jnp.where((_iota((CT, LANES), 0) & (CHUNK - 1)) == (_iota((CT, LANES), 1) & (CHUNK - 1)), 1.0, 0.0)
        xs, xds, pds, cols = [], [], [], []
        for hp in range(HP):
            bcol, gcol = _sel_col(beta_ref[...], HP * kh + hp), _sel_col(gc_ref[...], HP * kh + hp)
            x = jnp.where(strict, -(kk * bcol) * _decay(gcol, causal), 0.0)
            xs.append(x)
            xds.append(_fold_dup(x))
            pds.append(eye_dup)
            cols.append((bcol, gcol))
        for m in range(6):
            for hp in range(HP):
                out = _dotx3(xs[hp], jnp.concatenate([xds[hp], pds[hp]], axis=1))
                pds[hp] = pds[hp] + out[:, LANES:]
                if m < 5:
                    xds[hp] = out[:, :LANES]
                    xs[hp] = _unfold_bd(xds[hp], same)
        for hp in range(HP):
            bcol, gcol = cols[hp]
            cs = slice(hp * GDN_D, (hp + 1) * GDN_D)
            t_ref[hp] = pds[hp]
            tb = _unfold_bd(pds[hp], same).astype(BF16)
            vb = (v_ref[:, cs].astype(F32) * bcol).astype(BF16)
            kg = (kf * (bcol * jnp.exp(gcol))).astype(BF16)
            uw = _dot(tb, jnp.concatenate([vb, kg], axis=1))
            u_ref[:, cs] = uw[:, :GDN_D]
            w_ref[:, cs] = uw[:, GDN_D:].astype(BF16)

    col = pl.BlockSpec((CT, LANES), lambda t, kh: (t, 0))
    hv = pl.BlockSpec((CT, HP * GDN_D), lambda t, kh: (t, kh))
    return pl.pallas_call(
        body, name="gdn_chunk_fwd", grid=(nt, GDN_HV // HP),
        in_specs=[pl.BlockSpec((CT, GDN_D), lambda t, kh: (t, kh)), hv, col, col],
        out_specs=(pl.BlockSpec((HP, None, CT, LANES), lambda t, kh: (kh, t, 0, 0)), hv, hv),
        out_shape=(SDS((GDN_HV, nt, CT, LANES), F32), SDS((S, 4096), F32), SDS((S, 4096), BF16)),
        compiler_params=_cp("parallel", "parallel"),
    )(k, v, beta, gc)


def _last_of_chunk(gcol, rows, c):
    return jnp.sum(jnp.where(rows == c * CHUNK + CHUNK - 1, gcol, 0.0), axis=0, keepdims=True)


def _gdn_scan_fwd(q, k, u, w, gc):
    S = q.shape[0]
    nt = S // CT
    ncs = CT // CHUNK

    def body(q_ref, k_ref, u_ref, w_ref, gc_ref, y_ref, vn_ref, st_ref, s_scr, vn_scr):
        kh, t = pl.program_id(0), pl.program_id(1)

        @pl.when(t == 0)
        def _():
            s_scr[...] = jnp.zeros_like(s_scr)

        causal = _chunk_masks()[1]
        qv, kv = q_ref[...], k_ref[...]
        qf, kf = qv.astype(F32), kv.astype(F32)
        qk = _dot_nt(qv, kv)
        rows = _iota((CT, 1), 0)
        heads = []
        for hp in range(HP):
            gcol = _sel_col(gc_ref[...], HP * kh + hp)
            heads.append((gcol, (qk * _decay(gcol, causal)).astype(BF16), (qf * jnp.exp(gcol)).astype(BF16)))
        vn_scr[...] = jnp.zeros_like(vn_scr)
        for c in range(ncs):
            r = slice(c * CHUNK, (c + 1) * CHUNK)
            for hp in range(HP):
                gcol, attn, qd = heads[hp]
                cs = slice(hp * GDN_D, (hp + 1) * GDN_D)
                s = s_scr[hp]
                st_ref[hp, c] = s
                sb = s.astype(BF16)
                gl = _last_of_chunk(gcol, rows, c)
                kd = (kf[r] * jnp.exp(gl - gcol[r])).astype(BF16)
                vn = (u_ref[r, cs] - _dot(w_ref[r, cs], sb)).astype(BF16)
                vn_scr[r, cs] = vn
                y_ref[r, cs] = _dot(qd[r], sb) + _dot(attn[r], vn_scr[:, cs])
                s_scr[hp] = s * jnp.exp(gl) + _dot_tn(kd, vn)
        vn_ref[...] = vn_scr[...]

    hk = pl.BlockSpec((CT, GDN_D), lambda kh, t: (t, kh))
    hv = pl.BlockSpec((CT, HP * GDN_D), lambda kh, t: (t, kh))
    col = pl.BlockSpec((CT, LANES), lambda kh, t: (t, 0))
    return pl.pallas_call(
        body, name="gdn_scan_fwd", grid=(GDN_HV // HP, nt), in_specs=[hk, hk, hv, hv, col],
        out_specs=(hv, hv, pl.BlockSpec((HP, ncs, GDN_D, GDN_D), lambda kh, t: (kh, t, 0, 0))),
        out_shape=(SDS((S, 4096), F32), SDS((S, 4096), BF16), SDS((GDN_HV, S // CHUNK, GDN_D, GDN_D), F32)),
        scratch_shapes=[pltpu.VMEM((HP, GDN_D, GDN_D), F32), pltpu.VMEM((CT, HP * GDN_D), BF16)],
        compiler_params=_cp("parallel", "arbitrary"),
    )(q, k, u, w, gc)


def _gdn_scan_bwd(q, k, w, vn, gc, states, dy):
    S = q.shape[0]
    nt = S // CT
    ncs = CT // CHUNK

    def body(q_ref, k_ref, w_ref, vn_ref, gc_ref, st_ref, dy_ref, du_ref, dw_ref, dq_ref, dk_ref, dgc_ref, ds_scr):
        kh, t = pl.program_id(0), pl.program_id(1)

        @pl.when(t == 0)
        def _():
            ds_scr[...] = jnp.zeros_like(ds_scr)

        _, causal, _, _, eye = _chunk_masks()
        qv, kv = q_ref[...], k_ref[...]
        qf, kf = qv.astype(F32), kv.astype(F32)
        qk = _dot_nt(qv, kv)
        rows = _iota((CT, 1), 0)
        heads = []
        for hp in range(HP):
            cs = slice(hp * GDN_D, (hp + 1) * GDN_D)
            gcol = _sel_col(gc_ref[...], HP * kh + hp)
            dm = _decay(gcol, causal)
            attn_f = qk * dm
            egc = jnp.exp(gcol)
            qd_f = qf * egc
            dyv, vnv = dy_ref[:, cs], vn_ref[:, cs]
            heads.append(dict(cs=cs, gcol=gcol, dm=dm, attn_f=attn_f, egc=egc, qd_f=qd_f, qd=qd_f.astype(BF16),
                              dy=dyv, vn=vnv, dattn=_dot_nt(dyv, vnv), at_dy=_dot_tn(attn_f.astype(BF16), dyv),
                              dgc=[None] * ncs))
        dq_ref[...] = jnp.zeros_like(dq_ref)
        dk_ref[...] = jnp.zeros_like(dk_ref)
        for c in reversed(range(ncs)):
            r = slice(c * CHUNK, (c + 1) * CHUNK)
            for hp in range(HP):
                hd = heads[hp]
                cs, gcol = hd["cs"], hd["gcol"]
                s = st_ref[hp, c]
                sb = s.astype(BF16)
                dsn = ds_scr[hp]
                dsb = dsn.astype(BF16)
                gl = _last_of_chunk(gcol, rows, c)
                cd = jnp.exp(gl)
                ekd = jnp.exp(gl - gcol[r])
                kd_f = kf[r] * ekd
                dvn = (hd["at_dy"][r] + _dot(kd_f.astype(BF16), dsb)).astype(BF16)
                dqd = _dot_nt(hd["dy"][r], sb)
                dkd = _dot_nt(hd["vn"][r], dsb)
                dcd = jnp.sum(jnp.sum(s * dsn, axis=1, keepdims=True), axis=0, keepdims=True)
                ds_scr[hp] = dsn * cd + _dot_tn(hd["qd"][r], hd["dy"][r]) - _dot_tn(w_ref[r, cs], dvn)
                du_ref[r, cs] = dvn
                dw_ref[r, cs] = (-_dot_nt(dvn, sb)).astype(BF16)
                dq_ref[r, :] += dqd * hd["egc"][r]
                dk_ref[r, :] += dkd * ekd
                rs_q = jnp.sum(dqd * hd["qd_f"][r], axis=1, keepdims=True)
                rs_k = jnp.sum(dkd * kd_f, axis=1, keepdims=True)
                tot = jnp.sum(rs_k, axis=0, keepdims=True) + dcd * cd
                hd["dgc"][c] = rs_q - rs_k + jnp.where(rows[r] == c * CHUNK + CHUNK - 1, tot, 0.0)
        for hp in range(HP):
            hd = heads[hp]
            dab = (hd["dattn"] * hd["dm"]).astype(BF16)
            dq_ref[...] += _dot(dab, kv)
            dk_ref[...] += _dot_tn(dab, qv)
            e1 = hd["dattn"] * hd["attn_f"]
            dgc = (jnp.concatenate(hd["dgc"], axis=0) + jnp.sum(e1, axis=1, keepdims=True)
                   - jnp.sum(e1.T, axis=1, keepdims=True))
            dgc_ref[hp] = jnp.sum(jnp.where(eye, jnp.broadcast_to(dgc, (CT, CT)), 0.0), axis=0, keepdims=True)

    rev = lambda t: nt - 1 - t
    hk = pl.BlockSpec((CT, GDN_D), lambda kh, t: (rev(t), kh))
    hv = pl.BlockSpec((CT, HP * GDN_D), lambda kh, t: (rev(t), kh))
    col = pl.BlockSpec((CT, LANES), lambda kh, t: (rev(t), 0))
    return pl.pallas_call(
        body, name="gdn_scan_bwd", grid=(GDN_HV // HP, nt),
        in_specs=[hk, hk, hv, hv, col, pl.BlockSpec((HP, ncs, GDN_D, GDN_D), lambda kh, t: (kh, rev(t), 0, 0)), hv],
        out_specs=(hv, hv, hk, hk, pl.BlockSpec((HP, 1, CT), lambda kh, t: (kh, 0, rev(t)))),
        out_shape=(SDS((S, 4096), BF16), SDS((S, 4096), BF16), SDS((S, 2048), F32), SDS((S, 2048), F32),
                   SDS((GDN_HV, 1, S), F32)),
        scratch_shapes=[pltpu.VMEM((HP, GDN_D, GDN_D), F32)], compiler_params=_cp("parallel", "arbitrary"),
    )(q, k, w, vn, gc, states, dy)


def _gdn_chunk_bwd(k, v, beta, gc, tmat, du, dw, dk_p, dgc_p):
    S = k.shape[0]
    nt = S // CT

    def body(k_ref, v_ref, beta_ref, gc_ref, t_ref, du_ref, dw_ref, dkp_ref, dgcp_ref,
             dk_ref, dv_ref, dbeta_ref, dgc_ref):
        kh = pl.program_id(1)
        same, causal, strict, _, eye = _chunk_masks()
        kv = k_ref[...]
        kf = kv.astype(F32)
        kk = _dot_nt(kv, kv)
        lane = _iota((CT, LANES), 1)

        @pl.when(kh == 0)
        def _():
            dbeta_ref[...] = jnp.zeros_like(dbeta_ref)
            dgc_ref[...] = jnp.zeros_like(dgc_ref)

        dk = dkp_ref[...]
        for hp in range(HP):
            h = HP * kh + hp
            cs = slice(hp * GDN_D, (hp + 1) * GDN_D)
            bcol, gcol = _sel_col(beta_ref[...], h), _sel_col(gc_ref[...], h)
            dm = _decay(gcol, causal)
            vf = v_ref[:, cs].astype(F32)
            kb = kf * bcol
            a = jnp.where(strict, (kk * bcol) * dm, 0.0)
            egc = jnp.exp(gcol)
            kg_f = kb * egc
            tb = _unfold_bd(t_ref[hp], same).astype(BF16)
            duw = jnp.concatenate([du_ref[:, cs], dw_ref[:, cs]], axis=1)
            dt = _dot_nt(duw, jnp.concatenate([(vf * bcol).astype(BF16), kg_f.astype(BF16)], axis=1))
            dvb_dkg = _dot_tn(tb, duw)
            dvb, dkg = dvb_dkg[:, :GDN_D], dvb_dkg[:, GDN_D:]
            da = -_dot_nt(_dot_tn(tb, dt.astype(BF16)).astype(BF16), tb)
            rm = jnp.where(strict, da, 0.0)
            rdb = (rm * dm).astype(BF16)
            dkb = _dot(rdb, kv) + dkg * egc
            dk = dk + _dot_tn(rdb, kb.astype(BF16)) + dkb * bcol
            e2 = rm * a
            dgc_in = jnp.sum(jnp.where(eye, jnp.broadcast_to(dgcp_ref[hp], (CT, CT)), 0.0), axis=1, keepdims=True)
            dgc = (jnp.sum(e2, axis=1, keepdims=True) - jnp.sum(e2.T, axis=1, keepdims=True)
                   + jnp.sum(dkg * kg_f, axis=1, keepdims=True) + dgc_in)
            dbeta = jnp.sum(dkb * kf, axis=1, keepdims=True) + jnp.sum(dvb * vf, axis=1, keepdims=True)
            dv_ref[:, cs] = dvb * bcol
            dbeta_ref[...] += jnp.where(lane == h, dbeta, 0.0)
            dgc_ref[...] += jnp.where(lane == h, dgc, 0.0)
        dk_ref[...] = dk

    hk = pl.BlockSpec((CT, GDN_D), lambda t, kh: (t, kh))
    hv = pl.BlockSpec((CT, HP * GDN_D), lambda t, kh: (t, kh))
    col = pl.BlockSpec((CT, LANES), lambda t, kh: (t, 0))
    return pl.pallas_call(
        body, name="gdn_chunk_bwd", grid=(nt, GDN_HV // HP),
        in_specs=[hk, hv, col, col, pl.BlockSpec((HP, None, CT, LANES), lambda t, kh: (kh, t, 0, 0)), hv, hv, hk,
                  pl.BlockSpec((HP, 1, CT), lambda t, kh: (kh, 0, t))],
        out_specs=(hk, hv, col, col),
        out_shape=(SDS((S, 2048), F32), SDS((S, 4096), F32), SDS((S, LANES), F32), SDS((S, LANES), F32)),
        compiler_params=_cp("parallel", "arbitrary"),
    )(k, v, beta, gc, tmat, du, dw, dk_p, dgc_p)


def _gdn_post_fwd(y, h1, norm_g):
    S = y.shape[0]

    def body(y_ref, z_ref, g_ref, o_ref):
        g = g_ref[...]
        for hh in range(GDN_HV):
            sl = slice(hh * GDN_D, (hh + 1) * GDN_D)
            yh, zh = y_ref[:, sl], z_ref[:, sl]
            yn = yh * lax.rsqrt(jnp.mean(yh * yh, -1, keepdims=True) + RMS_EPS)
            o_ref[:, sl] = (yn * g * (zh * _sigmoid(zh))).astype(BF16)

    row = lambda off: pl.BlockSpec((ROWS, 4096), lambda t: (t, off))
    return pl.pallas_call(
        body, name="gdn_post_fwd", grid=(S // ROWS,),
        in_specs=[row(0), row(2), pl.BlockSpec((1, GDN_D), lambda t: (0, 0))], out_specs=row(0),
        out_shape=SDS((S, 4096), BF16), compiler_params=_cp("parallel"),
    )(y, h1, norm_g)


def _gdn_post_bwd(do, y, h1, norm_g):
    S = y.shape[0]

    def body(do_ref, y_ref, z_ref, g_ref, dy_ref, dz_ref, dg_ref):
        t = pl.program_id(0)
        g = g_ref[...]
        pg = jnp.zeros((1, GDN_D), F32)
        for hh in range(GDN_HV):
            sl = slice(hh * GDN_D, (hh + 1) * GDN_D)
            yh, zh, doh = y_ref[:, sl], z_ref[:, sl], do_ref[:, sl]
            rstd = lax.rsqrt(jnp.mean(yh * yh, -1, keepdims=True) + RMS_EPS)
            yn = yh * rstd
            sg = _sigmoid(zh)
            dz_ref[:, sl] = (doh * (yn * g) * (sg * (1.0 + zh * (1.0 - sg)))).astype(BF16)
            dyg = doh * (zh * sg)
            dyn = dyg * g
            dy_ref[:, sl] = (rstd * (dyn - yn * jnp.mean(dyn * yn, -1, keepdims=True))).astype(BF16)
            pg = pg + jnp.sum(dyg * yn, axis=0, keepdims=True)

        @pl.when(t == 0)
        def _():
            dg_ref[...] = pg

        @pl.when(t > 0)
        def _():
            dg_ref[...] += pg

    row = lambda off: pl.BlockSpec((ROWS, 4096), lambda t: (t, off))
    vec = pl.BlockSpec((1, GDN_D), lambda t: (0, 0))
    return pl.pallas_call(
        body, name="gdn_post_bwd", grid=(S // ROWS,), in_specs=[row(0), row(0), row(2), vec],
        out_specs=(row(0), row(2), vec),
        out_shape=(SDS((S, 4096), BF16), SDS((S, 3 * 4096), BF16), SDS((1, GDN_D), F32)),
        compiler_params=_cp("arbitrary"),
    )(do, y, h1, norm_g)


def _cols(g):
    return jnp.transpose(g, (1, 0, 2)).reshape(g.shape[1], -1)


def _rows(g):
    return g.reshape(-1, g.shape[-1])


def _local_step(x, tgt, sh, small):
    S = x.shape[0]
    xb = x.astype(BF16)
    rc = _ret_consts()
    cos, sin = _rope_tables(S)

    lead = [sh["ret_w_in"]]
    wri = _cols(_ag_finish(_comm_call("ag_ret_in", _ag_side(lead, [True])), lead)[0])
    early = [sh["ret_w_out"], sh["mlp_w1_0"], sh["mlp_w2_0"], sh["gdn_w_out"], sh["conv_w"]]
    h0, got = _mm(xb, wri, "nn", "mm_ret_in", side=_ag_side(early, [True, True, True, True, False]))
    g_ro, g_w10, g_w20, g_go, g_cv = _ag_finish(got, early)
    wro, wgo, conv_w = _rows(g_ro), _rows(g_go), _cols(g_cv)
    w1, w2 = [_cols(g_w10), None], [_rows(g_w20), None]
    qr, kr, vr = _ret_rot(h0, cos, sin)
    yr, ret_st = _ret_fwd(qr, kr, vr, rc)
    o0 = _ret_post_fwd(yr, h0, small["ret_gn_g"])
    mix0 = _mm(o0, wro, "nn", "mm_ret_out")
    x1, x1b, z1 = _ln_fwd(x, mix0, small["ln_mix_g"][0:1], small["ln_mix_b"][0:1], "ln_mix0_fwd")
    mid = [sh["gdn_w_in"]]
    (hh0, a0), got = _mm(x1b, w1[0], "nn", "mm_mlp0_up", epi="relu2", side=_ag_side(mid, [True]))
    wgi = _cols(_ag_finish(got, mid)[0])
    wgi_main = wgi[:, :GDN_QKV + 4096]
    wba = jnp.pad(wgi[:, GDN_QKV + 4096:], ((0, 0), (0, LANES - 2 * GDN_HV)))
    late = [sh["mlp_w1_1"], sh["mlp_w2_1"]]
    m0, got = _mm(a0, w2[0], "nn", "mm_mlp0_down", side=_ag_side(late, [True, True]))
    g_w11, g_w21 = _ag_finish(got, late)
    w1[1], w2[1] = _cols(g_w11), _rows(g_w21)
    x2, x2b, z2 = _ln_fwd(x1, m0, small["ln_ffn_g"][0:1], small["ln_ffn_b"][0:1], "ln_ffn0_fwd")

    h1 = _mm(x2b, wgi_main, "nn", "mm_gdn_in")
    ba = _mm(x2b, wba, "nn", "mm_gdn_ba")
    qn = _gdn_conv_fwd(h1, conv_w, "q")
    kn = _gdn_conv_fwd(h1, conv_w, "k")
    vg = _gdn_conv_fwd(h1, conv_w, "v")
    beta, g, gc = _gdn_scal_fwd(ba, small["a_log"], small["dt_bias"])
    tmat, u, w = _gdn_chunk_fwd(kn, vg, beta, gc)
    yg, vn, gdn_st = _gdn_scan_fwd(qn, kn, u, w, gc)
    o1 = _gdn_post_fwd(yg, h1, small["norm_g"])
    mix1 = _mm(o1, wgo, "nn", "mm_gdn_out")
    x3, x3b, z3 = _ln_fwd(x2, mix1, small["ln_mix_g"][1:2], small["ln_mix_b"][1:2], "ln_mix1_fwd")
    hh1, a1 = _mm(x3b, w1[1], "nn", "mm_mlp1_up", epi="relu2")
    m1 = _mm(a1, w2[1], "nn", "mm_mlp1_down")
    x4, _, z4 = _ln_fwd(x3, m1, small["ln_ffn_g"][1:2], small["ln_ffn_b"][1:2], "ln_ffn1_fwd")

    dx4, loss = _loss_fwd_bwd(x4, tgt)

    dz4, dz4b, d_lnf_g1, d_lnf_b1 = _ln_bwd(dx4, z4, small["ln_ffn_g"][1:2], "ln_ffn1_bwd")
    dhh1 = _mm(dz4b, w2[1], "nt", "mm_mlp1_down_dx", epi="drelu2", extra=hh1, out_dtype=BF16)
    dw2_1 = _mm(a1, dz4b, "tn", "mm_mlp1_down_dw")
    dx3 = _mm(dhh1, w1[1], "nt", "mm_mlp1_up_dx", epi="add", extra=dz4, scale=ALPHA)
    dw1_1 = _mm(x3b, dhh1, "tn", "mm_mlp1_up_dw", shard_major=True)
    dz3, dz3b, d_lnm_g1, d_lnm_b1 = _ln_bwd(dx3, z3, small["ln_mix_g"][1:2], "ln_mix1_bwd")
    do1 = _mm(dz3b, wgo, "nt", "mm_gdn_out_dx")
    dwgo = _mm(o1, dz3b, "tn", "mm_gdn_out_dw")
    shards_of = lambda g: g.reshape(N_CHIPS, -1, g.shape[-1])
    sums_a = _rs_begin([dw1_1, shards_of(dw2_1), shards_of(dwgo)], "a")
    dyg, dh1_z, d_norm_g = _gdn_post_bwd(do1, yg, h1, small["norm_g"])
    du, dw, dqn, dk_p, dgc_p = _gdn_scan_bwd(qn, kn, w, vn, gc, gdn_st, dyg)
    dkn, dvg, dbeta, dgc = _gdn_chunk_bwd(kn, vg, beta, gc, tmat, du, dw, dk_p, dgc_p)
    dba, d_a_log, d_dt_bias = _gdn_scal_bwd(ba, small["a_log"], small["dt_bias"], g, dbeta, dgc)
    dacc = _gdn_conv_bwd_act(h1, conv_w, dqn, "q")
    dacc = _gdn_conv_bwd_act(h1, conv_w, dkn, "k", dacc)
    dacc = _gdn_conv_bwd_act(h1, conv_w, dvg, "v", dacc)
    dh1, d_conv_w = _gdn_conv_bwd_in(h1, conv_w, dacc, dh1_z)
    dx2_ba = _mm(dba, wba, "nt", "mm_gdn_ba_dx", epi="add", extra=dz3, scale=ALPHA)
    dx2, parts_a = _mm(dh1, wgi_main, "nt", "mm_gdn_in_dx", epi="add", extra=dx2_ba, side=_rs_owner_side(sums_a))
    red_w1_1, red_w2_1, red_go = _rs_finish(parts_a, sums_a, "a")
    dwgi_main = _mm(x2b, dh1, "tn", "mm_gdn_in_dw")
    dwba = _mm(x2b, dba, "tn", "mm_gdn_ba_dw")
    dwgi = jnp.concatenate([dwgi_main, dwba[:, :2 * GDN_HV]], axis=1)
    sums_b = _rs_begin([jnp.transpose(dwgi.reshape(dwgi.shape[0], N_CHIPS, -1), (1, 0, 2))], "b")

    dz2, dz2b, d_lnf_g0, d_lnf_b0 = _ln_bwd(dx2, z2, small["ln_ffn_g"][0:1], "ln_ffn0_bwd")
    dhh0, parts_b = _mm(dz2b, w2[0], "nt", "mm_mlp0_down_dx", epi="drelu2", extra=hh0, out_dtype=BF16,
                        side=_rs_owner_side(sums_b))
    (red_gi,) = _rs_finish(parts_b, sums_b, "b")
    dw2_0 = _mm(a0, dz2b, "tn", "mm_mlp0_down_dw")
    dx1 = _mm(dhh0, w1[0], "nt", "mm_mlp0_up_dx", epi="add", extra=dz2, scale=ALPHA)
    dw1_0 = _mm(x1b, dhh0, "tn", "mm_mlp0_up_dw", shard_major=True)
    dz1, dz1b, d_lnm_g0, d_lnm_b0 = _ln_bwd(dx1, z1, small["ln_mix_g"][0:1], "ln_mix0_bwd")
    do0 = _mm(dz1b, wro, "nt", "mm_ret_out_dx")
    dwro = _mm(o0, dz1b, "tn", "mm_ret_out_dw")
    sums_c = _rs_begin([dw1_0, shards_of(dw2_0), shards_of(dwro)], "c")
    dyr, dgate, d_gn_g = _ret_post_bwd(do0, yr, h0, small["ret_gn_g"])
    dq0, dk0, dv0 = _ret_bwd(qr, kr, vr, dyr, ret_st, rc, cos, sin)
    dh0 = jnp.concatenate([dq0, dk0, dv0, dgate], axis=1)
    grad_x, parts_c = _mm(dh0, wri, "nt", "mm_ret_in_dx", epi="add", extra=dz1, scale=ALPHA,
                          side=_rs_owner_side(sums_c))
    red_w1_0, red_w2_0, red_ro = _rs_finish(parts_c, sums_c, "c")
    dwri = _mm(xb, dh0, "tn", "mm_ret_in_dw", shard_major=True)
    sums_d = _rs_begin([dwri], "d")
    (red_ri,) = _rs_finish(_comm_call("rs_to_owner_d", _rs_owner_side(sums_d)), sums_d, "d")

    big = dict(ret_w_in=red_ri, ret_w_out=red_ro, gdn_w_in=red_gi, gdn_w_out=red_go,
               mlp_w1=(red_w1_0, red_w1_1), mlp_w2=(red_w2_0, red_w2_1))
    sm = dict(ret_gn_g=d_gn_g, a_log=d_a_log, dt_bias=d_dt_bias, norm_g=d_norm_g,
              ln_mix_g=jnp.concatenate([d_lnm_g0, d_lnm_g1], 0), ln_mix_b=jnp.concatenate([d_lnm_b0, d_lnm_b1], 0),
              ln_ffn_g=jnp.concatenate([d_lnf_g0, d_lnf_g1], 0), ln_ffn_b=jnp.concatenate([d_lnf_b0, d_lnf_b1], 0),
              conv_w=d_conv_w)
    return loss, grad_x, big, sm


def _coords():
    return lax.axis_index("x"), lax.axis_index("y"), lax.axis_index("c")


HBM_SPEC = pl.BlockSpec(memory_space=pl.ANY)


def _other_chips(x, y):
    return [(1 - x, y), (x, 1 - y), (1 - x, 1 - y)]


def _ag_side(shards, split):
    n = len(shards)

    def piece(ref, p, core):
        if not split[p]:
            return ref
        half = shards[p].shape[0] // 2
        return ref.at[pl.ds(core * half, half)]

    def over_ici(xyc, ins, outs):
        x, y, c = xyc
        return [(piece(ins[p], p, c), piece(outs[p].at[2 * x + y], p, c), (cx, cy, c))
                for p in range(n) for cx, cy in _other_chips(x, y)]

    def to_sibling(xyc, ins, outs):
        x, y, c = xyc
        zones = [piece(outs[p].at[2 * cx + cy], p, c) for p in range(n) if split[p] for cx, cy in _other_chips(x, y)]
        return [(z, z, (x, y, 1 - c)) for z in zones]

    n_split = sum(bool(s) for s in split)
    phases, counts = [over_ici], [3 * n]
    if n_split:
        phases, counts = phases + [to_sibling], counts + [3 * n_split]
    return _Side(shards, [SDS((N_CHIPS,) + s.shape, s.dtype) for s in shards], phases, counts)


def _ag_finish(got, shards):
    me = 2 * lax.axis_index("x") + lax.axis_index("y")
    return [lax.dynamic_update_index_in_dim(g, s, me, 0) for g, s in zip(got, shards)]


def _rs_begin(grads, tag):
    n = len(grads)
    halves = [g.shape[1] // 2 for g in grads]

    def swap(xyc, ins, outs):
        x, y, c = xyc
        return [(ins[p].at[:, pl.ds((1 - c) * halves[p], halves[p])], outs[p], (x, y, 1 - c)) for p in range(n)]

    theirs = _comm_call("rs_swap_halves_" + tag, _Side(
        grads, [SDS((N_CHIPS, halves[p]) + g.shape[2:], F32) for p, g in enumerate(grads)], [swap], [n]))
    c = lax.axis_index("c")
    return [_add_half(g, t, c, "rs_add_%s%d" % (tag, p)) for p, (g, t) in enumerate(zip(grads, theirs))]


def _rs_owner_side(chip_sums):
    n = len(chip_sums)

    def to_owner(xyc, ins, outs):
        x, y, c = xyc
        return [(ins[p].at[2 * cx + cy], outs[p].at[2 * x + y], (cx, cy, c))
                for p in range(n) for cx, cy in _other_chips(x, y)]

    return _Side(chip_sums, [SDS(s.shape, s.dtype) for s in chip_sums], [to_owner], [3 * n])


def _rs_finish(parts, chip_sums, tag):
    n = len(parts)
    chip = 2 * lax.axis_index("x") + lax.axis_index("y")
    parts = [lax.dynamic_update_index_in_dim(pt, lax.dynamic_index_in_dim(cs, chip, 0, keepdims=False), chip, 0)
             for pt, cs in zip(parts, chip_sums)]
    mine = [_sum_chips(pt, "rs_sum_%s%d" % (tag, p)) for p, pt in enumerate(parts)]

    def swap(xyc, ins, outs):
        x, y, c = xyc
        return [(ins[p], outs[p], (x, y, 1 - c)) for p in range(n)]

    other = _comm_call("rs_swap_reduced_" + tag, _Side(mine, [SDS(m.shape, F32) for m in mine], [swap], [n]))
    return list(zip(mine, other))


def _add_half(g, theirs, c, name):
    _, R, C = g.shape
    half = R // 2
    tr = min(256, half)
    nb = half // tr

    def body(c_ref, g_ref, t_ref, o_ref):
        o_ref[...] = (g_ref[...] + t_ref[...]).astype(BF16)

    blk = pl.BlockSpec((None, tr, C), lambda s, i, c_ref: (s, i, 0))
    return pl.pallas_call(
        body, name=name,
        grid_spec=pltpu.PrefetchScalarGridSpec(
            num_scalar_prefetch=1, grid=(N_CHIPS, nb),
            in_specs=[pl.BlockSpec((None, tr, C), lambda s, i, c_ref: (s, c_ref[0] * nb + i, 0)), blk],
            out_specs=blk),
        out_shape=SDS((N_CHIPS, half, C), BF16), compiler_params=_cp("parallel", "parallel"),
    )(jnp.reshape(c, (1,)).astype(jnp.int32), g, theirs)


def _sum_chips(parts, name):
    _, r, C = parts.shape
    tr = min(256, r)

    def body(p_ref, o_ref):
        f = lambda s: p_ref[s].astype(F32)
        o_ref[...] = ((f(0) + f(1)) + f(2)) + f(3)

    return pl.pallas_call(
        body, name=name, grid=(r // tr,), in_specs=[pl.BlockSpec((N_CHIPS, tr, C), lambda i: (0, i, 0))],
        out_specs=pl.BlockSpec((tr, C), lambda i: (i, 0)), out_shape=SDS((r, C), F32), compiler_params=_cp("parallel"),
    )(parts)


def _all_reduce_small(buf):
    rows = buf.shape[0]

    def body(x_ref, o_ref, all_ref, send_sems, recv_sems):
        x, y, c = _coords()
        me = 4 * x + 2 * y + c
        all_ref[me] = x_ref[...]
        flips = [(fx, fy, fc) for fx in (0, 1) for fy in (0, 1) for fc in (0, 1)][1:]
        copies = []
        for k, (fx, fy, fc) in enumerate(flips):
            to = (x ^ fx, y ^ fy, c ^ fc)
            copies.append(pltpu.make_async_remote_copy(src_ref=x_ref, dst_ref=all_ref.at[me], send_sem=send_sems.at[k],
                                                       recv_sem=recv_sems.at[k], device_id=to, device_id_type=MESH))
        for cp in copies:
            cp.start()
        for cp in copies:
            cp.wait_recv()
        for cp in copies:
            cp.wait_send()
        acc = all_ref[0]
        for d in range(1, N_DEV):
            acc = acc + all_ref[d]
        o_ref[...] = acc

    vm = pl.BlockSpec(memory_space=pltpu.VMEM)
    return pl.pallas_call(
        body, name="all_reduce_small", in_specs=[vm], out_specs=vm, out_shape=SDS((rows, LANES), F32),
        scratch_shapes=[pltpu.VMEM((N_DEV, rows, LANES), F32), pltpu.SemaphoreType.DMA((N_DEV - 1,)),
                        pltpu.SemaphoreType.DMA((N_DEV - 1,))],
    )(buf)


def _adam_update(w, gv, m, v):
    mn = ADAM_B1 * m + (1.0 - ADAM_B1) * gv
    vn = ADAM_B2 * v + (1.0 - ADAM_B2) * (gv * gv)
    m_hat = mn / (1.0 - ADAM_B1 ** ADAM_STEP)
    v_hat = vn / (1.0 - ADAM_B2 ** ADAM_STEP)
    return -ADAM_LR * (m_hat / (jnp.sqrt(v_hat) + ADAM_EPS) + ADAM_WD * w), mn, vn


def _adamw_halves(w, mine, theirs, m, v, c, name, row0=0, bufs=None):
    R, C = w.shape
    half = mine.shape[0]
    tr = min(128, half)
    nbh = half // tr
    b0 = row0 // tr
    assert row0 % tr == 0 and half % tr == 0

    def body(c_ref, w_ref, a_ref, b_ref, m_ref, v_ref, *rest):
        g_ref, d_ref, mo_ref, vo_ref = rest[-4:]
        is_mine = (pl.program_id(0) // nbh) == c_ref[0]
        gv = jnp.where(is_mine, a_ref[...], b_ref[...])
        g_ref[...] = gv
        d_ref[...], mo_ref[...], vo_ref[...] = _adam_update(w_ref[...], gv, m_ref[...], v_ref[...])

    blk = pl.BlockSpec((tr, C), lambda i, c_ref: (b0 + i, 0))
    hblk = pl.BlockSpec((tr, C), lambda i, c_ref: (i % nbh, 0))
    extra = [] if bufs is None else list(bufs)
    return pl.pallas_call(
        body, name=name,
        grid_spec=pltpu.PrefetchScalarGridSpec(num_scalar_prefetch=1, grid=(2 * nbh,),
                                               in_specs=[blk, hblk, hblk, blk, blk] + [HBM_SPEC] * len(extra),
                                               out_specs=(blk,) * 4),
        out_shape=(SDS((R, C), F32),) * 4, compiler_params=_cp("parallel"),
        input_output_aliases={6 + i: i for i in range(len(extra))},
    )(jnp.reshape(c, (1,)).astype(jnp.int32), w, mine, theirs, m, v, *extra)


def _adamw(w, g, m, v, name):
    R, C = w.shape
    tr = min(256, R)
    assert R % tr == 0

    def body(w_ref, g_ref, m_ref, v_ref, d_ref, mo_ref, vo_ref):
        d_ref[...], mo_ref[...], vo_ref[...] = _adam_update(w_ref[...], g_ref[...], m_ref[...], v_ref[...])

    blk = pl.BlockSpec((tr, C), lambda i: (i, 0))
    return pl.pallas_call(
        body, name=name, grid=(R // tr,), in_specs=[blk] * 4, out_specs=(blk,) * 3,
        out_shape=(SDS((R, C), F32),) * 3, compiler_params=_cp("parallel"),
    )(w, g, m, v)


def _pack(arrs):
    rows = []
    for a in arrs:
        flat = a.reshape(-1).astype(F32)
        pad = (-flat.shape[0]) % LANES
        rows.append(jnp.pad(flat, (0, pad)).reshape(-1, LANES))
    buf = jnp.concatenate(rows, axis=0)
    pad_rows = (-buf.shape[0]) % 8
    return jnp.pad(buf, ((0, pad_rows), (0, 0)))


def _unpack(buf, shapes):
    out, r = [], 0
    for shp in shapes:
        size = int(np.prod(shp))
        nr = -(-size // LANES)
        out.append(buf[r:r + nr].reshape(-1)[:size].reshape(shp))
        r += nr
    return out


def _pad_lanes(a):
    return jnp.pad(a, ((0, 0), (0, LANES - a.shape[1])))


def kernel(x, ret_w_in, ret_gn_g, ret_w_out, gdn_w_in, gdn_conv_w, gdn_a_log, gdn_dt_bias, gdn_norm_g, gdn_w_out, ln_mix_g, ln_mix_b, mlp_w1, mlp_w2, ln_ffn_g, ln_ffn_b, loss_target, m_ret_w_in, m_ret_gn_g, m_ret_w_out, m_gdn_w_in, m_gdn_conv_w, m_gdn_a_log, m_gdn_dt_bias, m_gdn_norm_g, m_gdn_w_out, m_ln_mix_g, m_ln_mix_b, m_mlp_w1, m_mlp_w2, m_ln_ffn_g, m_ln_ffn_b, v_ret_w_in, v_ret_gn_g, v_ret_w_out, v_gdn_w_in, v_gdn_conv_w, v_gdn_a_log, v_gdn_dt_bias, v_gdn_norm_g, v_gdn_w_out, v_ln_mix_g, v_ln_mix_b, v_mlp_w1, v_mlp_w2, v_ln_ffn_g, v_ln_ffn_b):
    cx, cy = lax.axis_index("x"), lax.axis_index("y")
    chip = 2 * cx + cy

    sh = dict(ret_w_in=ret_w_in[0].astype(BF16), ret_w_out=ret_w_out[0].astype(BF16),
              gdn_w_in=gdn_w_in[0].astype(BF16), gdn_w_out=gdn_w_out[0].astype(BF16),
              mlp_w1_0=mlp_w1[0].astype(BF16), mlp_w1_1=mlp_w1[1].astype(BF16),
              mlp_w2_0=mlp_w2[0].astype(BF16), mlp_w2_1=mlp_w2[1].astype(BF16), conv_w=gdn_conv_w[0])
    small = dict(ret_gn_g=ret_gn_g, a_log=_pad_lanes(gdn_a_log), dt_bias=_pad_lanes(gdn_dt_bias), norm_g=gdn_norm_g,
                 ln_mix_g=ln_mix_g, ln_mix_b=ln_mix_b, ln_ffn_g=ln_ffn_g, ln_ffn_b=ln_ffn_b)

    loss, grad_x, big, sm = _local_step(x[0], loss_target[0], sh, small)

    small_names = ["ret_gn_g", "a_log", "dt_bias", "norm_g", "ln_mix_g", "ln_mix_b", "ln_ffn_g", "ln_ffn_b", "conv_w"]
    small_shapes = [(1, 4096), (1, LANES), (1, LANES), (1, GDN_D), (2, D_MODEL), (2, D_MODEL), (2, D_MODEL),
                    (2, D_MODEL), (4, GDN_QKV)]
    red = _all_reduce_small(_pack([loss] + [sm[k] for k in small_names]))
    red_loss, *red_small = _unpack(red, [(1, 1)] + small_shapes)
    gs = dict(zip(small_names, red_small))
    g_conv = lax.dynamic_slice_in_dim(gs["conv_w"], chip * 2048, 2048, axis=1)
    g_a_log, g_dt_bias = gs["a_log"][:, :GDN_HV], gs["dt_bias"][:, :GDN_HV]

    big_w = [(ret_w_in, m_ret_w_in, v_ret_w_in, [big["ret_w_in"]]), (ret_w_out, m_ret_w_out, v_ret_w_out, [big["ret_w_out"]]),
             (gdn_w_in, m_gdn_w_in, v_gdn_w_in, [big["gdn_w_in"]]), (gdn_w_out, m_gdn_w_out, v_gdn_w_out, [big["gdn_w_out"]]),
             (mlp_w1, m_mlp_w1, v_mlp_w1, big["mlp_w1"]), (mlp_w2, m_mlp_w2, v_mlp_w2, big["mlp_w2"])]
    core = lax.axis_index("c")
    big_out = []
    for i, (w_, m_, v_, layers) in enumerate(big_w):
        two_d = lambda a: a.reshape(-1, a.shape[-1])
        res = None
        for j, (mine, theirs) in enumerate(layers):
            res = _adamw_halves(two_d(w_), mine, theirs, two_d(m_), two_d(v_), core, "adamw_%d_%d" % (i, j),
                                row0=j * 2 * mine.shape[0], bufs=res)
        big_out.append(tuple(a.reshape(w_.shape) for a in res))
    sm_w = [(ret_gn_g, m_ret_gn_g, v_ret_gn_g, gs["ret_gn_g"]), (gdn_conv_w, m_gdn_conv_w, v_gdn_conv_w, g_conv),
            (gdn_a_log, m_gdn_a_log, v_gdn_a_log, g_a_log), (gdn_dt_bias, m_gdn_dt_bias, v_gdn_dt_bias, g_dt_bias),
            (gdn_norm_g, m_gdn_norm_g, v_gdn_norm_g, gs["norm_g"]), (ln_mix_g, m_ln_mix_g, v_ln_mix_g, gs["ln_mix_g"]),
            (ln_mix_b, m_ln_mix_b, v_ln_mix_b, gs["ln_mix_b"]), (ln_ffn_g, m_ln_ffn_g, v_ln_ffn_g, gs["ln_ffn_g"]),
            (ln_ffn_b, m_ln_ffn_b, v_ln_ffn_b, gs["ln_ffn_b"])]
    sm_shapes = [w_.shape for w_, _, _, _ in sm_w]
    d_s, nm_s, nv_s = _adamw(_pack([w_ for w_, _, _, _ in sm_w]), _pack([g_ for _, _, _, g_ in sm_w]),
                             _pack([m_ for _, m_, _, _ in sm_w]), _pack([v_ for _, _, v_, _ in sm_w]), "adamw_small")
    d_s, nm_s, nv_s = (_unpack(a, sm_shapes) for a in (d_s, nm_s, nv_s))
    sm_out = [(g_.reshape(w_.shape), d_s[i], nm_s[i], nv_s[i]) for i, (w_, _, _, g_) in enumerate(sm_w)]

    per_w = [big_out[0], sm_out[0], big_out[1], big_out[2], sm_out[1], sm_out[2], sm_out[3], sm_out[4], big_out[3],
             sm_out[5], sm_out[6], big_out[4], big_out[5], sm_out[7], sm_out[8]]
    outs = [red_loss.reshape(()), grad_x[None]]
    for kind in range(4):
        outs.extend(t[kind] for t in per_w)
    return tuple(outs)
```

```python
import functools

import numpy as np
import jax
import jax.numpy as jnp
from jax import lax
from jax.experimental import pallas as pl
from jax.experimental.pallas import tpu as pltpu

F32 = jnp.float32
BF16 = jnp.bfloat16
MESH = pl.DeviceIdType.MESH
SDS = jax.ShapeDtypeStruct

D_MODEL = 2048
CHUNK = 64
RET_HEADS, RET_DK, RET_DV = 8, 256, 512
GDN_HV, GDN_D = 32, 128
HP = 2
GDN_QKV = 8192
ALPHA = 4.0 ** 0.25
LN_EPS, GN_EPS, RMS_EPS, L2_EPS = 1e-5, 1e-6, 1e-6, 1e-6
ADAM_LR, ADAM_B1, ADAM_B2, ADAM_EPS, ADAM_WD, ADAM_STEP = 0.001, 0.9, 0.999, 1e-8, 0.01, 10

VMEM_LIMIT_BYTES = 56 * 1024 * 1024
RT = 256
CT = 256
ROWS = 256
LANES = 128
N_CHIPS = 4
N_DEV = 8


def _cp(*sem):
    return pltpu.CompilerParams(dimension_semantics=sem, vmem_limit_bytes=VMEM_LIMIT_BYTES)


def _dot(a, b):
    return jnp.dot(a, b, preferred_element_type=F32)


def _dot_nt(a, b):
    return lax.dot_general(a, b, (((1,), (1,)), ((), ())), preferred_element_type=F32)


def _dot_tn(a, b):
    return lax.dot_general(a, b, (((0,), (0,)), ((), ())), preferred_element_type=F32)


def _split2(x):
    hi = x.astype(BF16)
    lo = (x - hi.astype(F32)).astype(BF16)
    return hi, lo


def _dotx3(a, b):
    ah, al = _split2(a)
    bh, bl = _split2(b)
    return _dot(ah, bh) + (_dot(ah, bl) + _dot(al, bh))


def _dot_exact_l(l_bf16, x):
    hi = x.astype(BF16)
    r = x - hi.astype(F32)
    mid = r.astype(BF16)
    lo = (r - mid.astype(F32)).astype(BF16)
    return _dot(l_bf16, hi) + (_dot(l_bf16, mid) + _dot(l_bf16, lo))


def _sigmoid(x):
    return 1.0 / (1.0 + jnp.exp(-x))


def _iota(shape, dim):
    return lax.broadcasted_iota(jnp.int32, shape, dim)


class _Side:
    def __init__(self, arrays, out_shapes, phases, counts):
        self.arrays, self.out_shapes, self.phases, self.counts = list(arrays), list(out_shapes), phases, counts

    def sem_shapes(self):
        return [pltpu.SemaphoreType.DMA((sum(self.counts),))] * 2

    def ops(self, ins, outs, send_sems, recv_sems):
        def copies(ph):
            off = sum(self.counts[:ph])
            return [pltpu.make_async_remote_copy(src_ref=src, dst_ref=dst, send_sem=send_sems.at[off + i],
                                                 recv_sem=recv_sems.at[off + i], device_id=to, device_id_type=MESH)
                    for i, (src, dst, to) in enumerate(self.phases[ph](_coords(), ins, outs))]

        def start(ph):
            for cp in copies(ph):
                cp.start()

        def wait(ph):
            cps = copies(ph)
            for cp in cps:
                cp.wait_recv()
            for cp in cps:
                cp.wait_send()

        return start, wait


def _comm_call(name, side):
    n = len(side.arrays)

    def body(*refs):
        start, wait = side.ops(refs[:n], refs[n:2 * n], refs[2 * n], refs[2 * n + 1])
        for ph in range(len(side.phases)):
            start(ph)
            wait(ph)

    return pl.pallas_call(body, name=name, in_specs=[HBM_SPEC] * n, out_specs=[HBM_SPEC] * n,
                          out_shape=side.out_shapes, scratch_shapes=side.sem_shapes())(*side.arrays)


SIDE_SWITCH = 0.85


def _pcall(body, args, *, name, grid, in_specs, out_specs, out_shape, sem, scratch_shapes=(), sides=()):
    single = not isinstance(out_shape, (tuple, list))
    if not sides:
        return pl.pallas_call(body, name=name, grid=grid, in_specs=list(in_specs), out_specs=out_specs,
                              out_shape=out_shape, scratch_shapes=list(scratch_shapes), compiler_params=_cp(*sem))(*args)
    o_shapes = (out_shape,) if single else tuple(out_shape)
    o_specs = (out_specs,) if single else tuple(out_specs)
    n_in, n_out, n_scr = len(args), len(o_shapes), len(scratch_shapes)
    ns = [len(s.arrays) for s in sides]
    steps = int(np.prod(grid))

    def carrier(*refs):
        pos = n_in
        s_ins = []
        for k in ns:
            s_ins.append(refs[pos:pos + k])
            pos += k
        outs = refs[pos:pos + n_out]
        pos += n_out
        s_outs = []
        for k in ns:
            s_outs.append(refs[pos:pos + k])
            pos += k
        scr = refs[pos:pos + n_scr]
        pos += n_scr
        step = pl.program_id(0)
        for d in range(1, len(grid)):
            step = step * grid[d] + pl.program_id(d)
        hooks = [s.ops(s_ins[i], s_outs[i], refs[pos + 2 * i], refs[pos + 2 * i + 1]) for i, s in enumerate(sides)]
        for (start, wait), s in zip(hooks, sides):
            pl.when(step == 0)(functools.partial(start, 0))
            if len(s.phases) == 2:
                def switch(start=start, wait=wait):
                    wait(0)
                    start(1)
                pl.when(step == int(steps * SIDE_SWITCH))(switch)
            else:
                assert len(s.phases) == 1
        body(*refs[:n_in], *outs, *scr)
        for (start, wait), s in zip(hooks, sides):
            pl.when(step == steps - 1)(functools.partial(wait, len(s.phases) - 1))

    res = pl.pallas_call(
        carrier, name=name, grid=grid, in_specs=list(in_specs) + [HBM_SPEC] * sum(ns),
        out_specs=o_specs + (HBM_SPEC,) * sum(ns),
        out_shape=o_shapes + tuple(sh for s in sides for sh in s.out_shapes),
        scratch_shapes=list(scratch_shapes) + [sm for s in sides for sm in s.sem_shapes()],
        compiler_params=_cp(*(("arbitrary",) * len(grid))),
    )(*args, *[a for s in sides for a in s.arrays])
    main, rest, side_res = res[:n_out], list(res[n_out:]), []
    for k in ns:
        side_res.append(rest[:k])
        rest = rest[k:]
    return (main[0] if single else tuple(main)), side_res


def _mm(a, b, mode, name, *, out_dtype=F32, tm=1024, tn=1024, tk=2048, epi=None, extra=None, scale=1.0,
        shard_major=False, sides=()):
    if mode == "nn":
        (M, K), (K2, N) = a.shape, b.shape
    elif mode == "nt":
        (M, K), (N, K2) = a.shape, b.shape
    else:
        (K, M), (K2, N) = a.shape, b.shape
    assert K == K2, (a.shape, b.shape, mode)
    tm, tn, tk = min(tm, M), min(tn, N), min(tk, K)
    assert M % tm == 0 and N % tn == 0 and K % tk == 0, (M, N, K, tm, tn, tk)
    nk = K // tk
    dims = {"nn": (((1,), (0,)), ((), ())), "nt": (((1,), (1,)), ((), ())), "tn": (((0,), (0,)), ((), ()))}[mode]
    if mode == "tn":
        a_spec = pl.BlockSpec((tk, tm), lambda i, j, k: (k, i))
    else:
        a_spec = pl.BlockSpec((tm, tk), lambda i, j, k: (i, k))
    if mode == "nt":
        b_spec = pl.BlockSpec((tn, tk), lambda i, j, k: (j, k))
    else:
        b_spec = pl.BlockSpec((tk, tn), lambda i, j, k: (k, j))
    tile = pl.BlockSpec((tm, tn), lambda i, j, k: (i, j))
    in_specs, ins = [a_spec, b_spec], [a, b]
    if extra is not None:
        in_specs.append(tile)
        ins.append(extra)
    if epi == "relu2":
        out_shape = (SDS((M, N), F32), SDS((M, N), BF16))
        out_specs = (tile, tile)
    elif shard_major:
        per = (N // N_CHIPS) // tn
        assert per * tn * N_CHIPS == N
        out_shape = (SDS((N_CHIPS, M, N // N_CHIPS), out_dtype),)
        out_specs = (pl.BlockSpec((None, tm, tn), lambda i, j, k: (j // per, i, j % per)),)
    else:
        out_shape = (SDS((M, N), out_dtype),)
        out_specs = (tile,)
    n_out = len(out_shape)

    def body(*refs):
        a_ref, b_ref = refs[0], refs[1]
        pos = 2
        x_ref = None
        if extra is not None:
            x_ref = refs[pos]
            pos += 1
        o_refs = refs[pos:pos + n_out]
        acc_ref = refs[pos + n_out] if nk > 1 else None

        def prod():
            av, bv = a_ref[...], b_ref[...]
            if av.dtype != BF16:
                av = av.astype(BF16)
            if bv.dtype != BF16:
                bv = bv.astype(BF16)
            return lax.dot_general(av, bv, dims, preferred_element_type=F32)

        def finish(acc):
            if epi == "relu2":
                o_refs[0][...] = acc
                r = jnp.maximum(acc, 0.0)
                o_refs[1][...] = (r * r).astype(BF16)
            elif epi == "drelu2":
                o_refs[0][...] = (acc * (2.0 * jnp.maximum(x_ref[...], 0.0))).astype(out_dtype)
            elif epi == "add":
                o_refs[0][...] = (acc + scale * x_ref[...]).astype(out_dtype)
            else:
                o_refs[0][...] = acc.astype(out_dtype)

        if nk == 1:
            finish(prod())
        else:
            k = pl.program_id(2)

            @pl.when(k == 0)
            def _():
                acc_ref[...] = prod()

            @pl.when(k > 0)
            def _():
                acc_ref[...] += prod()

            @pl.when(k == nk - 1)
            def _():
                finish(acc_ref[...])

    res = _pcall(body, ins, name=name, grid=(M // tm, N // tn, nk), in_specs=in_specs, out_specs=out_specs,
                 out_shape=out_shape, scratch_shapes=[pltpu.VMEM((tm, tn), F32)] if nk > 1 else [],
                 sem=("parallel", "parallel", "arbitrary"), sides=sides)
    main, side_res = res if sides else (res, None)
    main = main if n_out > 1 else main[0]
    return (main, side_res) if sides else main


def _ln_stats(z):
    mu = jnp.mean(z, -1, keepdims=True)
    zc = z - mu
    var = jnp.mean(zc * zc, -1, keepdims=True)
    rstd = lax.rsqrt(var + LN_EPS)
    return zc * rstd, rstd


def _ln_fwd(xin, sub, g, b, name):
    S, Dm = xin.shape
    row = pl.BlockSpec((ROWS, Dm), lambda t: (t, 0))
    vec = pl.BlockSpec((1, Dm), lambda t: (0, 0))

    def body(x_ref, s_ref, g_ref, b_ref, o_ref, ob_ref, z_ref):
        z = ALPHA * x_ref[...] + s_ref[...]
        xh, _ = _ln_stats(z)
        o = xh * g_ref[...] + b_ref[...]
        o_ref[...] = o
        ob_ref[...] = o.astype(BF16)
        z_ref[...] = z

    return pl.pallas_call(
        body, name=name, grid=(S // ROWS,), in_specs=[row, row, vec, vec], out_specs=(row, row, row),
        out_shape=(SDS((S, Dm), F32), SDS((S, Dm), BF16), SDS((S, Dm), F32)), compiler_params=_cp("parallel"),
    )(xin, sub, g, b)


def _ln_bwd(dout, z, g, name):
    S, Dm = z.shape
    row = pl.BlockSpec((ROWS, Dm), lambda t: (t, 0))
    vec = pl.BlockSpec((1, Dm), lambda t: (0, 0))

    def body(d_ref, z_ref, g_ref, dz_ref, dzb_ref, dg_ref, db_ref):
        t = pl.program_id(0)
        xh, rstd = _ln_stats(z_ref[...])
        d = d_ref[...]
        dxh = d * g_ref[...]
        m1 = jnp.mean(dxh, -1, keepdims=True)
        m2 = jnp.mean(dxh * xh, -1, keepdims=True)
        dz = rstd * (dxh - m1 - xh * m2)
        dz_ref[...] = dz
        dzb_ref[...] = dz.astype(BF16)
        pg = jnp.sum(d * xh, axis=0, keepdims=True)
        pb = jnp.sum(d, axis=0, keepdims=True)

        @pl.when(t == 0)
        def _():
            dg_ref[...] = pg
            db_ref[...] = pb

        @pl.when(t > 0)
        def _():
            dg_ref[...] += pg
            db_ref[...] += pb

    return pl.pallas_call(
        body, name=name, grid=(S // ROWS,), in_specs=[row, row, vec], out_specs=(row, row, vec, vec),
        out_shape=(SDS((S, Dm), F32), SDS((S, Dm), BF16), SDS((1, Dm), F32), SDS((1, Dm), F32)),
        compiler_params=_cp("arbitrary"),
    )(dout, z, g)


def _loss_fwd_bwd(y, tgt):
    S, Dm = y.shape
    row = pl.BlockSpec((ROWS, Dm), lambda t: (t, 0))
    one = pl.BlockSpec((1, 1), lambda t: (0, 0))

    def body(y_ref, t_ref, dy_ref, l_ref):
        t = pl.program_id(0)
        diff = y_ref[...] - t_ref[...]
        dy_ref[...] = diff * (1.0 / Dm)
        part = jnp.sum(jnp.sum(diff * diff, axis=1, keepdims=True), axis=0, keepdims=True) * (0.5 / Dm)

        @pl.when(t == 0)
        def _():
            l_ref[...] = part

        @pl.when(t > 0)
        def _():
            l_ref[...] += part

    return pl.pallas_call(
        body, name="loss", grid=(S // ROWS,), in_specs=[row, row], out_specs=(row, one),
        out_shape=(SDS((S, Dm), F32), SDS((1, 1), F32)), compiler_params=_cp("arbitrary"),
    )(y, tgt)


def _ret_consts():
    h = np.arange(RET_HEADS, dtype=np.float64)
    lg = np.log1p(-np.exp2(-5.0 - h))
    return jnp.asarray(np.concatenate([lg, np.exp(lg * RT)]).astype(np.float32))


def _rope_tables(S):
    half = RET_DK // 2
    inv = 10000.0 ** (-jnp.arange(half, dtype=F32) / half)
    ang = jnp.arange(S).astype(F32)[:, None] * inv[None, :]
    return jnp.cos(ang), jnp.sin(ang)


def _ret_masks(lgh):
    ri, ci = _iota((RT, RT), 0), _iota((RT, RT), 1)
    visible = (ci >> 6) <= (ri >> 6)
    m = jnp.where(visible, jnp.exp(lgh * jnp.abs(ri - ci).astype(F32)), 0.0)
    pos = _iota((RT, 1), 0).astype(F32)
    return m, jnp.exp(lgh * (pos + 1.0)), jnp.exp(lgh * (RT - 1.0 - pos))


def _ret_rot(h0, cos, sin, sides=()):
    S = h0.shape[0]
    half = RET_DK // 2

    def body(q_ref, k_ref, v_ref, c_ref, s_ref, qo_ref, ko_ref, vo_ref):
        c, s = c_ref[...], s_ref[...]

        def rot(t):
            t1, t2 = t[:, :half], t[:, half:]
            return jnp.concatenate([t1 * c - t2 * s, t1 * s + t2 * c], axis=-1)

        qo_ref[...] = rot(q_ref[...]).astype(BF16)
        ko_ref[...] = (rot(k_ref[...]) * (RET_DK ** -0.5)).astype(BF16)
        vo_ref[...] = v_ref[...].astype(BF16)

    qk = lambda off: pl.BlockSpec((RT, RET_DK), lambda t, h: (t, off + h))
    vv = lambda off: pl.BlockSpec((RT, RET_DV), lambda t, h: (t, off + h))
    tab = pl.BlockSpec((RT, half), lambda t, h: (t, 0))
    return _pcall(
        body, (h0, h0, h0, cos, sin), name="ret_rot", grid=(S // RT, RET_HEADS),
        in_specs=[qk(0), qk(RET_HEADS), vv(RET_HEADS), tab, tab], out_specs=(qk(0), qk(0), vv(0)),
        out_shape=(SDS((S, 2048), BF16), SDS((S, 2048), BF16), SDS((S, 4096), BF16)),
        sem=("parallel", "parallel"), sides=sides)


def _ret_fwd(q, k, v, consts, sides=()):
    S = q.shape[0]
    nt = S // RT

    def body(c_ref, q_ref, k_ref, v_ref, y_ref, st_ref, s_scr):
        h, t = pl.program_id(0), pl.program_id(1)

        @pl.when(t == 0)
        def _():
            s_scr[...] = jnp.zeros_like(s_scr)

        lgh, cdec = c_ref[h], c_ref[RET_HEADS + h]
        m, dq, dk = _ret_masks(lgh)
        qv, kv, vv = q_ref[...], k_ref[...], v_ref[...]
        p = (_dot_nt(qv, kv) * m).astype(BF16)
        sp = s_scr[...]
        spb = sp.astype(BF16)
        st_ref[...] = spb
        qd = (qv.astype(F32) * dq).astype(BF16)
        kd = (kv.astype(F32) * dk).astype(BF16)
        y_ref[...] = _dot(p, vv) + _dot(qd, spb)
        s_scr[...] = sp * cdec + _dot_tn(kd, vv)

    qk = pl.BlockSpec((RT, RET_DK), lambda h, t: (t, h))
    vs = pl.BlockSpec((RT, RET_DV), lambda h, t: (t, h))
    return _pcall(
        body, (consts, q, k, v), name="ret_fwd", grid=(RET_HEADS, nt),
        in_specs=[pl.BlockSpec(memory_space=pltpu.SMEM), qk, qk, vs],
        out_specs=(vs, pl.BlockSpec((None, None, RET_DK, RET_DV), lambda h, t: (h, t, 0, 0))),
        out_shape=(SDS((S, 4096), F32), SDS((RET_HEADS, nt, RET_DK, RET_DV), BF16)),
        scratch_shapes=[pltpu.VMEM((RET_DK, RET_DV), F32)], sem=("parallel", "arbitrary"), sides=sides)


def _ret_bwd(q, k, v, dy, states, consts, cos, sin, sides=()):
    S = q.shape[0]
    nt = S // RT
    half = RET_DK // 2

    def body(c_ref, q_ref, k_ref, v_ref, dy_ref, st_ref, cos_ref, sin_ref, dq_ref, dk_ref, dv_ref, ds_scr):
        h, t = pl.program_id(0), pl.program_id(1)

        @pl.when(t == 0)
        def _():
            ds_scr[...] = jnp.zeros_like(ds_scr)

        lgh, cdec = c_ref[h], c_ref[RET_HEADS + h]
        m, dqc, dkc = _ret_masks(lgh)
        qv, kv, vv, dyv, spb = q_ref[...], k_ref[...], v_ref[...], dy_ref[...], st_ref[...]
        p = (_dot_nt(qv, kv) * m).astype(BF16)
        qd = (qv.astype(F32) * dqc).astype(BF16)
        kd = (kv.astype(F32) * dkc).astype(BF16)
        dsn = ds_scr[...]
        dsb = dsn.astype(BF16)
        dsc = (_dot_nt(dyv, vv) * m).astype(BF16)
        dq = _dot(dsc, kv) + _dot_nt(dyv, spb) * dqc
        dk = _dot_tn(dsc, qv) + _dot_nt(vv, dsb) * dkc
        dv_ref[...] = (_dot_tn(p, dyv) + _dot(kd, dsb)).astype(BF16)
        ds_scr[...] = dsn * cdec + _dot_tn(qd, dyv)
        c, s = cos_ref[...], sin_ref[...]

        def unrot(d):
            d1, d2 = d[:, :half], d[:, half:]
            return jnp.concatenate([d1 * c + d2 * s, d2 * c - d1 * s], axis=-1)

        dq_ref[...] = unrot(dq).astype(BF16)
        dk_ref[...] = (unrot(dk) * (RET_DK ** -0.5)).astype(BF16)

    rev = lambda t: nt - 1 - t
    qk = pl.BlockSpec((RT, RET_DK), lambda h, t: (rev(t), h))
    vs = pl.BlockSpec((RT, RET_DV), lambda h, t: (rev(t), h))
    tab = pl.BlockSpec((RT, half), lambda h, t: (rev(t), 0))
    return _pcall(
        body, (consts, q, k, v, dy, states, cos, sin), name="ret_bwd", grid=(RET_HEADS, nt),
        in_specs=[pl.BlockSpec(memory_space=pltpu.SMEM), qk, qk, vs, vs,
                  pl.BlockSpec((None, None, RET_DK, RET_DV), lambda h, t: (h, rev(t), 0, 0)), tab, tab],
        out_specs=(qk, qk, vs),
        out_shape=(SDS((S, 2048), BF16), SDS((S, 2048), BF16), SDS((S, 4096), BF16)),
        scratch_shapes=[pltpu.VMEM((RET_DK, RET_DV), F32)], sem=("parallel", "arbitrary"), sides=sides)


def _gn_stats(y):
    mu = jnp.mean(y, -1, keepdims=True)
    yc = y - mu
    var = jnp.mean(yc * yc, -1, keepdims=True)
    rstd = lax.rsqrt(var + GN_EPS)
    return yc * rstd, rstd


def _ret_post_fwd(y, h0, gn_g, sides=()):
    S = y.shape[0]

    def body(y_ref, gate_ref, g_ref, o_ref):
        yn, _ = _gn_stats(y_ref[...])
        gate = gate_ref[...]
        o_ref[...] = (gate * _sigmoid(gate) * (yn * g_ref[...])).astype(BF16)

    vs = lambda off: pl.BlockSpec((RT, RET_DV), lambda h, t: (t, off + h))
    return _pcall(
        body, (y, h0, gn_g), name="ret_post_fwd", grid=(RET_HEADS, S // RT),
        in_specs=[vs(0), vs(2 * RET_HEADS), pl.BlockSpec((1, RET_DV), lambda h, t: (0, h))], out_specs=vs(0),
        out_shape=SDS((S, 4096), BF16), sem=("parallel", "parallel"), sides=sides)


def _ret_post_bwd(do, y, h0, gn_g, sides=()):
    S = y.shape[0]

    def body(do_ref, y_ref, gate_ref, g_ref, dy_ref, dgate_ref, dg_ref):
        t = pl.program_id(1)
        yn, rstd = _gn_stats(y_ref[...])
        gate, g, dov = gate_ref[...], g_ref[...], do_ref[...]
        sg = _sigmoid(gate)
        dgate_ref[...] = (dov * (yn * g) * (sg * (1.0 + gate * (1.0 - sg)))).astype(BF16)
        dyg = dov * (gate * sg)
        dyn = dyg * g
        m1 = jnp.mean(dyn, -1, keepdims=True)
        m2 = jnp.mean(dyn * yn, -1, keepdims=True)
        dy_ref[...] = (rstd * (dyn - m1 - yn * m2)).astype(BF16)
        pg = jnp.sum(dyg * yn, axis=0, keepdims=True)

        @pl.when(t == 0)
        def _():
            dg_ref[...] = pg

        @pl.when(t > 0)
        def _():
            dg_ref[...] += pg

    vs = lambda off: pl.BlockSpec((RT, RET_DV), lambda h, t: (t, off + h))
    vec = pl.BlockSpec((1, RET_DV), lambda h, t: (0, h))
    return _pcall(
        body, (do, y, h0, gn_g), name="ret_post_bwd", grid=(RET_HEADS, S // RT),
        in_specs=[vs(0), vs(0), vs(2 * RET_HEADS), vec], out_specs=(vs(0), vs(0), vec),
        out_shape=(SDS((S, 4096), BF16), SDS((S, 4096), BF16), SDS((1, 4096), F32)),
        sem=("parallel", "arbitrary"), sides=sides)


def _conv_taps(x_ref, halo_ref, w_ref, ext_scr, t):
    ext_scr[0:8, :] = jnp.where(t == 0, 0.0, halo_ref[...])
    ext_scr[8:, :] = x_ref[...]
    w = w_ref[...]
    n = x_ref.shape[0]
    acc = w[3:4, :] * ext_scr[8:, :]
    for j in range(3):
        acc = acc + w[j:j + 1, :] * ext_scr[pl.ds(5 + j, n), :]
    return acc


def _gdn_conv_fwd(h1, conv_w, kind):
    S = h1.shape[0]
    base = {"q": 0, "k": 1, "v": 2}[kind]
    ncb = 2 if kind == "v" else 1
    C = 2048

    def body(x_ref, halo_ref, w_ref, o_ref, ext_scr):
        acc = _conv_taps(x_ref, halo_ref, w_ref, ext_scr, pl.program_id(0))
        c = acc * _sigmoid(acc)
        if kind == "v":
            o_ref[...] = c.astype(BF16)
        else:
            scale = GDN_D ** -0.5 if kind == "q" else 1.0
            for hh in range(C // GDN_D):
                ch = c[:, hh * GDN_D:(hh + 1) * GDN_D]
                r = lax.rsqrt(jnp.sum(ch * ch, -1, keepdims=True) + L2_EPS)
                o_ref[:, hh * GDN_D:(hh + 1) * GDN_D] = (ch * (r * scale)).astype(BF16)

    hb = ROWS // 8
    return pl.pallas_call(
        body, name="gdn_conv_fwd_" + kind, grid=(S // ROWS, ncb),
        in_specs=[pl.BlockSpec((ROWS, C), lambda t, j: (t, base + j)),
                  pl.BlockSpec((8, C), lambda t, j: (jnp.maximum(t * hb - 1, 0), base + j)),
                  pl.BlockSpec((4, C), lambda t, j: (0, base + j))],
        out_specs=pl.BlockSpec((ROWS, C), lambda t, j: (t, j)), out_shape=SDS((S, C * ncb), BF16),
        scratch_shapes=[pltpu.VMEM((ROWS + 8, C), F32)], compiler_params=_cp("parallel", "parallel"),
    )(h1, h1, conv_w)


def _gdn_conv_bwd_act(h1, conv_w, dn, kind, buf=None):
    S = h1.shape[0]
    base = {"q": 0, "k": 1, "v": 2}[kind]
    ncb = 2 if kind == "v" else 1
    C = 2048

    def body(x_ref, halo_ref, w_ref, dn_ref, *rest):
        o_ref, ext_scr = rest[-2], rest[-1]
        acc = _conv_taps(x_ref, halo_ref, w_ref, ext_scr, pl.program_id(0))
        sg = _sigmoid(acc)
        dsilu = sg * (1.0 + acc * (1.0 - sg))
        if kind == "v":
            o_ref[...] = dn_ref[...] * dsilu
        else:
            c = acc * sg
            scale = GDN_D ** -0.5 if kind == "q" else 1.0
            for hh in range(C // GDN_D):
                sl = slice(hh * GDN_D, (hh + 1) * GDN_D)
                ch, dnh = c[:, sl], dn_ref[:, sl]
                r = lax.rsqrt(jnp.sum(ch * ch, -1, keepdims=True) + L2_EPS)
                proj = jnp.sum(dnh * ch, -1, keepdims=True)
                o_ref[:, sl] = (scale * r) * (dnh - ch * (proj * r * r)) * dsilu[:, sl]

    hb = ROWS // 8
    return pl.pallas_call(
        body, name="gdn_conv_bwd_act_" + kind, grid=(S // ROWS, ncb),
        in_specs=[pl.BlockSpec((ROWS, C), lambda t, j: (t, base + j)),
                  pl.BlockSpec((8, C), lambda t, j: (jnp.maximum(t * hb - 1, 0), base + j)),
                  pl.BlockSpec((4, C), lambda t, j: (0, base + j)),
                  pl.BlockSpec((ROWS, C), lambda t, j: (t, j))] + ([] if buf is None else [HBM_SPEC]),
        out_specs=pl.BlockSpec((ROWS, C), lambda t, j: (t, base + j)), out_shape=SDS((S, GDN_QKV), F32),
        input_output_aliases={} if buf is None else {4: 0},
        scratch_shapes=[pltpu.VMEM((ROWS + 8, C), F32)], compiler_params=_cp("parallel", "parallel"),
    )(*((h1, h1, conv_w, dn) + (() if buf is None else (buf,))))


def _gdn_conv_bwd_in(h1, conv_w, dacc, dh1_buf):
    S = h1.shape[0]
    C = 2048
    nt = S // ROWS
    hb = ROWS // 8

    def body(x_ref, halo_ref, w_ref, d_ref, dhalo_ref, buf_ref, di_ref, dw_ref, ext_scr, dext_scr):
        t = pl.program_id(1)
        ext_scr[0:8, :] = jnp.where(t == 0, 0.0, halo_ref[...])
        ext_scr[8:, :] = x_ref[...]
        d = d_ref[...]
        dext_scr[0:ROWS, :] = d
        dext_scr[ROWS:, :] = jnp.where(t == nt - 1, 0.0, dhalo_ref[...])
        w = w_ref[...]
        di = w[3:4, :] * d
        for j in range(3):
            di = di + w[j:j + 1, :] * dext_scr[pl.ds(3 - j, ROWS), :]
        di_ref[...] = di.astype(BF16)
        rows = [jnp.sum(d * ext_scr[pl.ds(5 + j, ROWS), :], axis=0, keepdims=True) for j in range(4)]
        pw = jnp.concatenate(rows, axis=0)

        @pl.when(t == 0)
        def _():
            dw_ref[...] = pw

        @pl.when(t > 0)
        def _():
            dw_ref[...] += pw

    return pl.pallas_call(
        body, name="gdn_conv_bwd_in", grid=(GDN_QKV // C, nt),
        in_specs=[pl.BlockSpec((ROWS, C), lambda j, t: (t, j)),
                  pl.BlockSpec((8, C), lambda j, t: (jnp.maximum(t * hb - 1, 0), j)),
                  pl.BlockSpec((4, C), lambda j, t: (0, j)),
                  pl.BlockSpec((ROWS, C), lambda j, t: (t, j)),
                  pl.BlockSpec((8, C), lambda j, t: (jnp.minimum((t + 1) * hb, nt * hb - 1), j)), HBM_SPEC],
        out_specs=(pl.BlockSpec((ROWS, C), lambda j, t: (t, j)), pl.BlockSpec((4, C), lambda j, t: (0, j))),
        out_shape=(SDS(dh1_buf.shape, BF16), SDS((4, GDN_QKV), F32)), input_output_aliases={5: 0},
        scratch_shapes=[pltpu.VMEM((ROWS + 8, C), F32), pltpu.VMEM((ROWS + 8, C), F32)],
        compiler_params=_cp("parallel", "arbitrary"),
    )(h1, h1, conv_w, dacc, dacc, dh1_buf)


def _chunk_masks():
    ri, ci = _iota((CT, CT), 0), _iota((CT, CT), 1)
    same = (ri >> 6) == (ci >> 6)
    return same, same & (ri >= ci), same & (ri > ci), same & (ri <= ci), ri == ci


def _fold_dup(m):
    h = m[:, :LANES] + m[:, LANES:]
    return h + pltpu.roll(h, CHUNK, axis=1)


def _unfold_bd(d, same):
    return jnp.where(same, jnp.concatenate([d, d], axis=1), 0.0)


def _softplus(x):
    return jnp.maximum(x, 0.0) + jnp.log(1.0 + jnp.exp(-jnp.abs(x)))


def _gdn_scal_fwd(ba, a_log, dt_bias):
    S = ba.shape[0]

    def body(ba_ref, al_ref, dt_ref, beta_ref, g_ref, gc_ref):
        bav = ba_ref[...]
        beta_ref[...] = _sigmoid(bav)
        a = pltpu.roll(bav, LANES - GDN_HV, axis=1)
        g = -jnp.exp(al_ref[...]) * _softplus(a + dt_ref[...])
        g_ref[...] = g
        causal = _chunk_masks()[1]
        gc_ref[...] = _dot_exact_l(causal.astype(BF16), g)

    row = pl.BlockSpec((CT, LANES), lambda t: (t, 0))
    vec = pl.BlockSpec((1, LANES), lambda t: (0, 0))
    return pl.pallas_call(
        body, name="gdn_scal_fwd", grid=(S // CT,), in_specs=[row, vec, vec], out_specs=(row, row, row),
        out_shape=(SDS((S, LANES), F32),) * 3, compiler_params=_cp("parallel"),
    )(ba, a_log, dt_bias)


def _gdn_scal_bwd(ba, a_log, dt_bias, g, dbeta, dgc):
    S = ba.shape[0]

    def body(ba_ref, al_ref, dt_ref, g_ref, dbeta_ref, dgc_ref, dba_ref, dal_ref, ddt_ref):
        t = pl.program_id(0)
        bav = ba_ref[...]
        beta = _sigmoid(bav)
        db = dbeta_ref[...] * beta * (1.0 - beta)
        a = pltpu.roll(bav, LANES - GDN_HV, axis=1)
        dgv = _dot_exact_l(_chunk_masks()[3].astype(BF16), dgc_ref[...])
        da = dgv * (-jnp.exp(al_ref[...])) * _sigmoid(a + dt_ref[...])
        lane = _iota(bav.shape, 1)
        da_sh = pltpu.roll(da, GDN_HV, axis=1)
        dba = jnp.where(lane < GDN_HV, db, jnp.where(lane < 2 * GDN_HV, da_sh, 0.0))
        dba_ref[...] = dba.astype(BF16)
        keep = lane < GDN_HV
        pal = jnp.sum(jnp.where(keep, dgv * g_ref[...], 0.0), axis=0, keepdims=True)
        pdt = jnp.sum(jnp.where(keep, da, 0.0), axis=0, keepdims=True)

        @pl.when(t == 0)
        def _():
            dal_ref[...] = pal
            ddt_ref[...] = pdt

        @pl.when(t > 0)
        def _():
            dal_ref[...] += pal
            ddt_ref[...] += pdt

    row = pl.BlockSpec((CT, LANES), lambda t: (t, 0))
    vec = pl.BlockSpec((1, LANES), lambda t: (0, 0))
    return pl.pallas_call(
        body, name="gdn_scal_bwd", grid=(S // CT,), in_specs=[row, vec, vec, row, row, row],
        out_specs=(row, vec, vec), out_shape=(SDS((S, LANES), BF16), SDS((1, LANES), F32), SDS((1, LANES), F32)),
        compiler_params=_cp("arbitrary"),
    )(ba, a_log, dt_bias, g, dbeta, dgc)


def _sel_col(x, h):
    return jnp.sum(jnp.where(_iota(x.shape, 1) == h, x, 0.0), axis=1, keepdims=True)


def _decay(gcol, causal):
    gm = jnp.broadcast_to(gcol, (CT, CT))
    diff = gm - gm.T
    return jnp.where(causal, jnp.exp(jnp.where(causal, diff, 0.0)), 0.0)


def _gdn_chunk_fwd(k, v, beta, gc):
    S = k.shape[0]
    nt = S // CT

    def body(k_ref, v_ref, beta_ref, gc_ref, t_ref, u_ref, w_ref):
        kh = pl.program_id(1)
        same, causal, strict, _, _ = _chunk_masks()
        kv = k_ref[...]
        kf = kv.astype(F32)
        kk = _dot_nt(kv, kv)
        eye_dup = jnp.where((_iota((CT, LANES), 0) & (CHUNK - 1)) == (_iota((CT, LANES), 1) & (CHUNK - 1)), 1.0, 0.0)
        xs, xds, pds, cols = [], [], [], []
        for hp in range(HP):
            bcol, gcol = _sel_col(beta_ref[...], HP * kh + hp), _sel_col(gc_ref[...], HP * kh + hp)
            x = jnp.where(strict, -(kk * bcol) * _decay(gcol, causal), 0.0)
            xs.append(x)
            xds.append(_fold_dup(x))
            pds.append(eye_dup)
            cols.append((bcol, gcol))
        for m in range(6):
            for hp in range(HP):
                out = _dotx3(xs[hp], jnp.concatenate([xds[hp], pds[hp]], axis=1))
                pds[hp] = pds[hp] + out[:, LANES:]
                if m < 5:
                    xds[hp] = out[:, :LANES]
                    xs[hp] = _unfold_bd(xds[hp], same)
        for hp in range(HP):
            bcol, gcol = cols[hp]
            cs = slice(hp * GDN_D, (hp + 1) * GDN_D)
            t_ref[hp] = pds[hp]
            tb = _unfold_bd(pds[hp], same).astype(BF16)
            vb = (v_ref[:, cs].astype(F32) * bcol).astype(BF16)
            kg = (kf * (bcol * jnp.exp(gcol))).astype(BF16)
            uw = _dot(tb, jnp.concatenate([vb, kg], axis=1))
            u_ref[:, cs] = uw[:, :GDN_D]
            w_ref[:, cs] = uw[:, GDN_D:].astype(BF16)

    col = pl.BlockSpec((CT, LANES), lambda t, kh: (t, 0))
    hv = pl.BlockSpec((CT, HP * GDN_D), lambda t, kh: (t, kh))
    return pl.pallas_call(
        body, name="gdn_chunk_fwd", grid=(nt, GDN_HV // HP),
        in_specs=[pl.BlockSpec((CT, GDN_D), lambda t, kh: (t, kh)), hv, col, col],
        out_specs=(pl.BlockSpec((HP, None, CT, LANES), lambda t, kh: (kh, t, 0, 0)), hv, hv),
        out_shape=(SDS((GDN_HV, nt, CT, LANES), F32), SDS((S, 4096), F32), SDS((S, 4096), BF16)),
        compiler_params=_cp("parallel", "parallel"),
    )(k, v, beta, gc)


def _last_of_chunk(gcol, rows, c):
    return jnp.sum(jnp.where(rows == c * CHUNK + CHUNK - 1, gcol, 0.0), axis=0, keepdims=True)


def _gdn_scan_fwd(q, k, u, w, gc):
    S = q.shape[0]
    nt = S // CT
    ncs = CT // CHUNK

    def body(q_ref, k_ref, u_ref, w_ref, gc_ref, y_ref, vn_ref, st_ref, s_scr, vn_scr):
        kh, t = pl.program_id(0), pl.program_id(1)

        @pl.when(t == 0)
        def _():
            s_scr[...] = jnp.zeros_like(s_scr)

        causal = _chunk_masks()[1]
        qv, kv = q_ref[...], k_ref[...]
        qf, kf = qv.astype(F32), kv.astype(F32)
        qk = _dot_nt(qv, kv)
        rows = _iota((CT, 1), 0)
        heads = []
        for hp in range(HP):
            gcol = _sel_col(gc_ref[...], HP * kh + hp)
            heads.append((gcol, (qk * _decay(gcol, causal)).astype(BF16), (qf * jnp.exp(gcol)).astype(BF16)))
        vn_scr[...] = jnp.zeros_like(vn_scr)
        for c in range(ncs):
            r = slice(c * CHUNK, (c + 1) * CHUNK)
            for hp in range(HP):
                gcol, attn, qd = heads[hp]
                cs = slice(hp * GDN_D, (hp + 1) * GDN_D)
                s = s_scr[hp]
                st_ref[hp, c] = s
                sb = s.astype(BF16)
                gl = _last_of_chunk(gcol, rows, c)
                kd = (kf[r] * jnp.exp(gl - gcol[r])).astype(BF16)
                vn = (u_ref[r, cs] - _dot(w_ref[r, cs], sb)).astype(BF16)
                vn_scr[r, cs] = vn
                y_ref[r, cs] = _dot(qd[r], sb) + _dot(attn[r], vn_scr[:, cs])
                s_scr[hp] = s * jnp.exp(gl) + _dot_tn(kd, vn)
        vn_ref[...] = vn_scr[...]

    hk = pl.BlockSpec((CT, GDN_D), lambda kh, t: (t, kh))
    hv = pl.BlockSpec((CT, HP * GDN_D), lambda kh, t: (t, kh))
    col = pl.BlockSpec((CT, LANES), lambda kh, t: (t, 0))
    return pl.pallas_call(
        body, name="gdn_scan_fwd", grid=(GDN_HV // HP, nt), in_specs=[hk, hk, hv, hv, col],
        out_specs=(hv, hv, pl.BlockSpec((HP, ncs, GDN_D, GDN_D), lambda kh, t: (kh, t, 0, 0))),
        out_shape=(SDS((S, 4096), F32), SDS((S, 4096), BF16), SDS((GDN_HV, S // CHUNK, GDN_D, GDN_D), F32)),
        scratch_shapes=[pltpu.VMEM((HP, GDN_D, GDN_D), F32), pltpu.VMEM((CT, HP * GDN_D), BF16)],
        compiler_params=_cp("parallel", "arbitrary"),
    )(q, k, u, w, gc)


def _gdn_scan_bwd(q, k, w, vn, gc, states, dy, sides=()):
    S = q.shape[0]
    nt = S // CT
    ncs = CT // CHUNK

    def body(q_ref, k_ref, w_ref, vn_ref, gc_ref, st_ref, dy_ref, du_ref, dw_ref, dq_ref, dk_ref, dgc_ref, ds_scr):
        kh, t = pl.program_id(0), pl.program_id(1)

        @pl.when(t == 0)
        def _():
            ds_scr[...] = jnp.zeros_like(ds_scr)

        _, causal, _, _, eye = _chunk_masks()
        qv, kv = q_ref[...], k_ref[...]
        qf, kf = qv.astype(F32), kv.astype(F32)
        qk = _dot_nt(qv, kv)
        rows = _iota((CT, 1), 0)
        heads = []
        for hp in range(HP):
            cs = slice(hp * GDN_D, (hp + 1) * GDN_D)
            gcol = _sel_col(gc_ref[...], HP * kh + hp)
            dm = _decay(gcol, causal)
            attn_f = qk * dm
            egc = jnp.exp(gcol)
            qd_f = qf * egc
            dyv, vnv = dy_ref[:, cs], vn_ref[:, cs]
            heads.append(dict(cs=cs, gcol=gcol, dm=dm, attn_f=attn_f, egc=egc, qd_f=qd_f, qd=qd_f.astype(BF16),
                              dy=dyv, vn=vnv, dattn=_dot_nt(dyv, vnv), at_dy=_dot_tn(attn_f.astype(BF16), dyv),
                              dgc=[None] * ncs))
        dq_ref[...] = jnp.zeros_like(dq_ref)
        dk_ref[...] = jnp.zeros_like(dk_ref)
        for c in reversed(range(ncs)):
            r = slice(c * CHUNK, (c + 1) * CHUNK)
            for hp in range(HP):
                hd = heads[hp]
                cs, gcol = hd["cs"], hd["gcol"]
                s = st_ref[hp, c]
                sb = s.astype(BF16)
                dsn = ds_scr[hp]
                dsb = dsn.astype(BF16)
                gl = _last_of_chunk(gcol, rows, c)
                cd = jnp.exp(gl)
                ekd = jnp.exp(gl - gcol[r])
                kd_f = kf[r] * ekd
                dvn = (hd["at_dy"][r] + _dot(kd_f.astype(BF16), dsb)).astype(BF16)
                dqd = _dot_nt(hd["dy"][r], sb)
                dkd = _dot_nt(hd["vn"][r], dsb)
                dcd = jnp.sum(jnp.sum(s * dsn, axis=1, keepdims=True), axis=0, keepdims=True)
                ds_scr[hp] = dsn * cd + _dot_tn(hd["qd"][r], hd["dy"][r]) - _dot_tn(w_ref[r, cs], dvn)
                du_ref[r, cs] = dvn
                dw_ref[r, cs] = (-_dot_nt(dvn, sb)).astype(BF16)
                dq_ref[r, :] += dqd * hd["egc"][r]
                dk_ref[r, :] += dkd * ekd
                rs_q = jnp.sum(dqd * hd["qd_f"][r], axis=1, keepdims=True)
                rs_k = jnp.sum(dkd * kd_f, axis=1, keepdims=True)
                tot = jnp.sum(rs_k, axis=0, keepdims=True) + dcd * cd
                hd["dgc"][c] = rs_q - rs_k + jnp.where(rows[r] == c * CHUNK + CHUNK - 1, tot, 0.0)
        for hp in range(HP):
            hd = heads[hp]
            dab = (hd["dattn"] * hd["dm"]).astype(BF16)
            dq_ref[...] += _dot(dab, kv)
            dk_ref[...] += _dot_tn(dab, qv)
            e1 = hd["dattn"] * hd["attn_f"]
            dgc = (jnp.concatenate(hd["dgc"], axis=0) + jnp.sum(e1, axis=1, keepdims=True)
                   - jnp.sum(e1.T, axis=1, keepdims=True))
            dgc_ref[hp] = jnp.sum(jnp.where(eye, jnp.broadcast_to(dgc, (CT, CT)), 0.0), axis=0, keepdims=True)

    rev = lambda t: nt - 1 - t
    hk = pl.BlockSpec((CT, GDN_D), lambda kh, t: (rev(t), kh))
    hv = pl.BlockSpec((CT, HP * GDN_D), lambda kh, t: (rev(t), kh))
    col = pl.BlockSpec((CT, LANES), lambda kh, t: (rev(t), 0))
    return _pcall(
        body, (q, k, w, vn, gc, states, dy), name="gdn_scan_bwd", grid=(GDN_HV // HP, nt),
        in_specs=[hk, hk, hv, hv, col, pl.BlockSpec((HP, ncs, GDN_D, GDN_D), lambda kh, t: (kh, rev(t), 0, 0)), hv],
        out_specs=(hv, hv, hk, hk, pl.BlockSpec((HP, 1, CT), lambda kh, t: (kh, 0, rev(t)))),
        out_shape=(SDS((S, 4096), BF16), SDS((S, 4096), BF16), SDS((S, 2048), F32), SDS((S, 2048), F32),
                   SDS((GDN_HV, 1, S), F32)),
        scratch_shapes=[pltpu.VMEM((HP, GDN_D, GDN_D), F32)], sem=("parallel", "arbitrary"), sides=sides)


def _gdn_chunk_bwd(k, v, beta, gc, tmat, du, dw, dk_p, dgc_p, sides=()):
    S = k.shape[0]
    nt = S // CT

    def body(k_ref, v_ref, beta_ref, gc_ref, t_ref, du_ref, dw_ref, dkp_ref, dgcp_ref,
             dk_ref, dv_ref, dbeta_ref, dgc_ref):
        kh = pl.program_id(1)
        same, causal, strict, _, eye = _chunk_masks()
        kv = k_ref[...]
        kf = kv.astype(F32)
        kk = _dot_nt(kv, kv)
        lane = _iota((CT, LANES), 1)

        @pl.when(kh == 0)
        def _():
            dbeta_ref[...] = jnp.zeros_like(dbeta_ref)
            dgc_ref[...] = jnp.zeros_like(dgc_ref)

        dk = dkp_ref[...]
        for hp in range(HP):
            h = HP * kh + hp
            cs = slice(hp * GDN_D, (hp + 1) * GDN_D)
            bcol, gcol = _sel_col(beta_ref[...], h), _sel_col(gc_ref[...], h)
            dm = _decay(gcol, causal)
            vf = v_ref[:, cs].astype(F32)
            kb = kf * bcol
            a = jnp.where(strict, (kk * bcol) * dm, 0.0)
            egc = jnp.exp(gcol)
            kg_f = kb * egc
            tb = _unfold_bd(t_ref[hp], same).astype(BF16)
            duw = jnp.concatenate([du_ref[:, cs], dw_ref[:, cs]], axis=1)
            dt = _dot_nt(duw, jnp.concatenate([(vf * bcol).astype(BF16), kg_f.astype(BF16)], axis=1))
            dvb_dkg = _dot_tn(tb, duw)
            dvb, dkg = dvb_dkg[:, :GDN_D], dvb_dkg[:, GDN_D:]
            da = -_dot_nt(_dot_tn(tb, dt.astype(BF16)).astype(BF16), tb)
            rm = jnp.where(strict, da, 0.0)
            rdb = (rm * dm).astype(BF16)
            dkb = _dot(rdb, kv) + dkg * egc
            dk = dk + _dot_tn(rdb, kb.astype(BF16)) + dkb * bcol
            e2 = rm * a
            dgc_in = jnp.sum(jnp.where(eye, jnp.broadcast_to(dgcp_ref[hp], (CT, CT)), 0.0), axis=1, keepdims=True)
            dgc = (jnp.sum(e2, axis=1, keepdims=True) - jnp.sum(e2.T, axis=1, keepdims=True)
                   + jnp.sum(dkg * kg_f, axis=1, keepdims=True) + dgc_in)
            dbeta = jnp.sum(dkb * kf, axis=1, keepdims=True) + jnp.sum(dvb * vf, axis=1, keepdims=True)
            dv_ref[:, cs] = dvb * bcol
            dbeta_ref[...] += jnp.where(lane == h, dbeta, 0.0)
            dgc_ref[...] += jnp.where(lane == h, dgc, 0.0)
        dk_ref[...] = dk

    hk = pl.BlockSpec((CT, GDN_D), lambda t, kh: (t, kh))
    hv = pl.BlockSpec((CT, HP * GDN_D), lambda t, kh: (t, kh))
    col = pl.BlockSpec((CT, LANES), lambda t, kh: (t, 0))
    return _pcall(
        body, (k, v, beta, gc, tmat, du, dw, dk_p, dgc_p), name="gdn_chunk_bwd", grid=(nt, GDN_HV // HP),
        in_specs=[hk, hv, col, col, pl.BlockSpec((HP, None, CT, LANES), lambda t, kh: (kh, t, 0, 0)), hv, hv, hk,
                  pl.BlockSpec((HP, 1, CT), lambda t, kh: (kh, 0, t))],
        out_specs=(hk, hv, col, col),
        out_shape=(SDS((S, 2048), F32), SDS((S, 4096), F32), SDS((S, LANES), F32), SDS((S, LANES), F32)),
        sem=("parallel", "arbitrary"), sides=sides)


def _gdn_post_fwd(y, h1, norm_g):
    S = y.shape[0]

    def body(y_ref, z_ref, g_ref, o_ref):
        g = g_ref[...]
        for hh in range(GDN_HV):
            sl = slice(hh * GDN_D, (hh + 1) * GDN_D)
            yh, zh = y_ref[:, sl], z_ref[:, sl]
            yn = yh * lax.rsqrt(jnp.mean(yh * yh, -1, keepdims=True) + RMS_EPS)
            o_ref[:, sl] = (yn * g * (zh * _sigmoid(zh))).astype(BF16)

    row = lambda off: pl.BlockSpec((ROWS, 4096), lambda t: (t, off))
    return pl.pallas_call(
        body, name="gdn_post_fwd", grid=(S // ROWS,),
        in_specs=[row(0), row(2), pl.BlockSpec((1, GDN_D), lambda t: (0, 0))], out_specs=row(0),
        out_shape=SDS((S, 4096), BF16), compiler_params=_cp("parallel"),
    )(y, h1, norm_g)


def _gdn_post_bwd(do, y, h1, norm_g, sides=()):
    S = y.shape[0]

    def body(do_ref, y_ref, z_ref, g_ref, dy_ref, dz_ref, dg_ref):
        t = pl.program_id(0)
        g = g_ref[...]
        pg = jnp.zeros((1, GDN_D), F32)
        for hh in range(GDN_HV):
            sl = slice(hh * GDN_D, (hh + 1) * GDN_D)
            yh, zh, doh = y_ref[:, sl], z_ref[:, sl], do_ref[:, sl]
            rstd = lax.rsqrt(jnp.mean(yh * yh, -1, keepdims=True) + RMS_EPS)
            yn = yh * rstd
            sg = _sigmoid(zh)
            dz_ref[:, sl] = (doh * (yn * g) * (sg * (1.0 + zh * (1.0 - sg)))).astype(BF16)
            dyg = doh * (zh * sg)
            dyn = dyg * g
            dy_ref[:, sl] = (rstd * (dyn - yn * jnp.mean(dyn * yn, -1, keepdims=True))).astype(BF16)
            pg = pg + jnp.sum(dyg * yn, axis=0, keepdims=True)

        @pl.when(t == 0)
        def _():
            dg_ref[...] = pg

        @pl.when(t > 0)
        def _():
            dg_ref[...] += pg

    row = lambda off: pl.BlockSpec((ROWS, 4096), lambda t: (t, off))
    vec = pl.BlockSpec((1, GDN_D), lambda t: (0, 0))
    return _pcall(
        body, (do, y, h1, norm_g), name="gdn_post_bwd", grid=(S // ROWS,), in_specs=[row(0), row(0), row(2), vec],
        out_specs=(row(0), row(2), vec),
        out_shape=(SDS((S, 4096), BF16), SDS((S, 3 * 4096), BF16), SDS((1, GDN_D), F32)),
        sem=("arbitrary",), sides=sides)


def _cols(g):
    return jnp.transpose(g, (1, 0, 2)).reshape(g.shape[1], -1)


def _rows(g):
    return g.reshape(-1, g.shape[-1])


def _local_step(x, tgt, sh, small):
    S = x.shape[0]
    xb = x.astype(BF16)
    rc = _ret_consts()
    cos, sin = _rope_tables(S)

    def gather(names):
        shards = [sh[k] for k in names]
        return shards, _ag_side(shards, [k != "conv_w" for k in names])

    lead, side = gather(["ret_w_in"])
    wri = _cols(_ag_finish(_comm_call("ag_ret_in", side), lead)[0])
    shards, side = gather(["ret_w_out", "mlp_w1_0"])
    h0, (got,) = _mm(xb, wri, "nn", "mm_ret_in", sides=[side])
    g_ro, g_w10 = _ag_finish(got, shards)
    wro, w1 = _rows(g_ro), [_cols(g_w10), None]
    shards, side = gather(["mlp_w2_0"])
    (qr, kr, vr), (got,) = _ret_rot(h0, cos, sin, sides=[side])
    w2 = [_rows(_ag_finish(got, shards)[0]), None]
    shards, side = gather(["gdn_w_out", "conv_w"])
    (yr, ret_st), (got,) = _ret_fwd(qr, kr, vr, rc, sides=[side])
    g_go, g_cv = _ag_finish(got, shards)
    wgo, conv_w = _rows(g_go), _cols(g_cv)
    shards, side = gather(["mlp_w1_1"])
    o0, (got,) = _ret_post_fwd(yr, h0, small["ret_gn_g"], sides=[side])
    w1[1] = _cols(_ag_finish(got, shards)[0])
    mix0 = _mm(o0, wro, "nn", "mm_ret_out")
    x1, x1b, z1 = _ln_fwd(x, mix0, small["ln_mix_g"][0:1], small["ln_mix_b"][0:1], "ln_mix0_fwd")
    shards, side = gather(["gdn_w_in"])
    (hh0, a0), (got,) = _mm(x1b, w1[0], "nn", "mm_mlp0_up", epi="relu2", sides=[side])
    wgi = _cols(_ag_finish(got, shards)[0])
    wgi_main = wgi[:, :GDN_QKV + 4096]
    wba = jnp.pad(wgi[:, GDN_QKV + 4096:], ((0, 0), (0, LANES - 2 * GDN_HV)))
    shards, side = gather(["mlp_w2_1"])
    m0, (got,) = _mm(a0, w2[0], "nn", "mm_mlp0_down", sides=[side])
    w2[1] = _rows(_ag_finish(got, shards)[0])
    x2, x2b, z2 = _ln_fwd(x1, m0, small["ln_ffn_g"][0:1], small["ln_ffn_b"][0:1], "ln_ffn0_fwd")

    h1 = _mm(x2b, wgi_main, "nn", "mm_gdn_in")
    ba = _mm(x2b, wba, "nn", "mm_gdn_ba")
    qn = _gdn_conv_fwd(h1, conv_w, "q")
    kn = _gdn_conv_fwd(h1, conv_w, "k")
    vg = _gdn_conv_fwd(h1, conv_w, "v")
    beta, g, gc = _gdn_scal_fwd(ba, small["a_log"], small["dt_bias"])
    tmat, u, w = _gdn_chunk_fwd(kn, vg, beta, gc)
    yg, vn, gdn_st = _gdn_scan_fwd(qn, kn, u, w, gc)
    o1 = _gdn_post_fwd(yg, h1, small["norm_g"])
    mix1 = _mm(o1, wgo, "nn", "mm_gdn_out")
    x3, x3b, z3 = _ln_fwd(x2, mix1, small["ln_mix_g"][1:2], small["ln_mix_b"][1:2], "ln_mix1_fwd")
    hh1, a1 = _mm(x3b, w1[1], "nn", "mm_mlp1_up", epi="relu2")
    m1 = _mm(a1, w2[1], "nn", "mm_mlp1_down")
    x4, _, z4 = _ln_fwd(x3, m1, small["ln_ffn_g"][1:2], small["ln_ffn_b"][1:2], "ln_ffn1_fwd")

    dx4, loss = _loss_fwd_bwd(x4, tgt)

    dz4, dz4b, d_lnf_g1, d_lnf_b1 = _ln_bwd(dx4, z4, small["ln_ffn_g"][1:2], "ln_ffn1_bwd")
    dhh1 = _mm(dz4b, w2[1], "nt", "mm_mlp1_down_dx", epi="drelu2", extra=hh1, out_dtype=BF16)
    dw2_1 = _mm(a1, dz4b, "tn", "mm_mlp1_down_dw")
    dx3 = _mm(dhh1, w1[1], "nt", "mm_mlp1_up_dx", epi="add", extra=dz4, scale=ALPHA)
    dw1_1 = _mm(x3b, dhh1, "tn", "mm_mlp1_up_dw", shard_major=True)
    dz3, dz3b, d_lnm_g1, d_lnm_b1 = _ln_bwd(dx3, z3, small["ln_mix_g"][1:2], "ln_mix1_bwd")
    shards_of = lambda g: g.reshape(N_CHIPS, -1, g.shape[-1])
    g_a = [dw1_1, shards_of(dw2_1)]
    do1, (th_a,) = _mm(dz3b, wgo, "nt", "mm_gdn_out_dx", sides=[_rs_swap_side(g_a)])
    sums_a = _rs_add(g_a, th_a, "a")
    dwgo = _mm(o1, dz3b, "tn", "mm_gdn_out_dw")
    dyg, dh1_z, d_norm_g = _gdn_post_bwd(do1, yg, h1, small["norm_g"])
    du, dw, dqn, dk_p, dgc_p = _gdn_scan_bwd(qn, kn, w, vn, gc, gdn_st, dyg)
    dkn, dvg, dbeta, dgc = _gdn_chunk_bwd(kn, vg, beta, gc, tmat, du, dw, dk_p, dgc_p)
    dba, d_a_log, d_dt_bias = _gdn_scal_bwd(ba, small["a_log"], small["dt_bias"], g, dbeta, dgc)
    dacc = _gdn_conv_bwd_act(h1, conv_w, dqn, "q")
    dacc = _gdn_conv_bwd_act(h1, conv_w, dkn, "k", dacc)
    dacc = _gdn_conv_bwd_act(h1, conv_w, dvg, "v", dacc)
    dh1, d_conv_w = _gdn_conv_bwd_in(h1, conv_w, dacc, dh1_z)
    g_go = [shards_of(dwgo)]
    dx2_ba, (th_go,) = _mm(dba, wba, "nt", "mm_gdn_ba_dx", epi="add", extra=dz3, scale=ALPHA,
                           sides=[_rs_swap_side(g_go)])
    sums_a = sums_a + _rs_add(g_go, th_go, "go")
    dx2, (parts_a,) = _mm(dh1, wgi_main, "nt", "mm_gdn_in_dx", epi="add", extra=dx2_ba,
                          sides=[_rs_owner_side(sums_a)])
    mine_a = _rs_sum(parts_a, sums_a, "a")
    dwgi_main, (back_a,) = _mm(x2b, dh1, "tn", "mm_gdn_in_dw", sides=[_rs_back_side(mine_a)])
    red_w1_1, red_w2_1, red_go = zip(mine_a, back_a)
    dwba = _mm(x2b, dba, "tn", "mm_gdn_ba_dw")
    dwgi = jnp.concatenate([dwgi_main, dwba[:, :2 * GDN_HV]], axis=1)
    g_b = [jnp.transpose(dwgi.reshape(dwgi.shape[0], N_CHIPS, -1), (1, 0, 2))]

    dz2, dz2b, d_lnf_g0, d_lnf_b0 = _ln_bwd(dx2, z2, small["ln_ffn_g"][0:1], "ln_ffn0_bwd")
    dhh0, (th_b,) = _mm(dz2b, w2[0], "nt", "mm_mlp0_down_dx", epi="drelu2", extra=hh0, out_dtype=BF16,
                        sides=[_rs_swap_side(g_b)])
    sums_b = _rs_add(g_b, th_b, "b")
    dw2_0 = _mm(a0, dz2b, "tn", "mm_mlp0_down_dw")
    dx1, (parts_b,) = _mm(dhh0, w1[0], "nt", "mm_mlp0_up_dx", epi="add", extra=dz2, scale=ALPHA,
                          sides=[_rs_owner_side(sums_b)])
    mine_b = _rs_sum(parts_b, sums_b, "b")
    dw1_0 = _mm(x1b, dhh0, "tn", "mm_mlp0_up_dw", shard_major=True)
    dz1, dz1b, d_lnm_g0, d_lnm_b0 = _ln_bwd(dx1, z1, small["ln_mix_g"][0:1], "ln_mix0_bwd")
    g_c = [dw1_0, shards_of(dw2_0)]
    do0, (th_c, back_b) = _mm(dz1b, wro, "nt", "mm_ret_out_dx", sides=[_rs_swap_side(g_c), _rs_back_side(mine_b)])
    (red_gi,) = zip(mine_b, back_b)
    sums_c = _rs_add(g_c, th_c, "c")
    dwro = _mm(o0, dz1b, "tn", "mm_ret_out_dw")
    g_ro = [shards_of(dwro)]
    (dyr, dgate, d_gn_g), (th_ro, parts_c) = _ret_post_bwd(do0, yr, h0, small["ret_gn_g"],
                                                          sides=[_rs_swap_side(g_ro), _rs_owner_side(sums_c)])
    sums_ro = _rs_add(g_ro, th_ro, "ro")
    mine_c = _rs_sum(parts_c, sums_c, "c")
    (dq0, dk0, dv0), (parts_ro, back_c) = _ret_bwd(qr, kr, vr, dyr, ret_st, rc, cos, sin,
                                                   sides=[_rs_owner_side(sums_ro), _rs_back_side(mine_c)])
    red_w1_0, red_w2_0 = zip(mine_c, back_c)
    mine_ro = _rs_sum(parts_ro, sums_ro, "ro")
    dh0 = jnp.concatenate([dq0, dk0, dv0, dgate], axis=1)
    dwri, (back_ro,) = _mm(xb, dh0, "tn", "mm_ret_in_dw", shard_major=True, sides=[_rs_back_side(mine_ro)])
    (red_ro,) = zip(mine_ro, back_ro)
    g_d = [dwri]
    sums_d = _rs_add(g_d, _comm_call("rs_swap_halves_d", _rs_swap_side(g_d)), "d")
    grad_x, (parts_d,) = _mm(dh0, wri, "nt", "mm_ret_in_dx", epi="add", extra=dz1, scale=ALPHA,
                             sides=[_rs_owner_side(sums_d)])
    mine_d = _rs_sum(parts_d, sums_d, "d")
    (red_ri,) = zip(mine_d, _comm_call("rs_swap_reduced_d", _rs_back_side(mine_d)))

    big = dict(ret_w_in=red_ri, ret_w_out=red_ro, gdn_w_in=red_gi, gdn_w_out=red_go,
               mlp_w1=(red_w1_0, red_w1_1), mlp_w2=(red_w2_0, red_w2_1))
    sm = dict(ret_gn_g=d_gn_g, a_log=d_a_log, dt_bias=d_dt_bias, norm_g=d_norm_g,
              ln_mix_g=jnp.concatenate([d_lnm_g0, d_lnm_g1], 0), ln_mix_b=jnp.concatenate([d_lnm_b0, d_lnm_b1], 0),
              ln_ffn_g=jnp.concatenate([d_lnf_g0, d_lnf_g1], 0), ln_ffn_b=jnp.concatenate([d_lnf_b0, d_lnf_b1], 0),
              conv_w=d_conv_w)
    return loss, grad_x, big, sm


def _coords():
    return lax.axis_index("x"), lax.axis_index("y"), lax.axis_index("c")


HBM_SPEC = pl.BlockSpec(memory_space=pl.ANY)


def _other_chips(x, y):
    return [(1 - x, y), (x, 1 - y), (1 - x, 1 - y)]


def _ag_side(shards, split):
    n = len(shards)

    def piece(ref, p, core):
        if not split[p]:
            return ref
        half = shards[p].shape[0] // 2
        return ref.at[pl.ds(core * half, half)]

    def over_ici(xyc, ins, outs):
        x, y, c = xyc
        return [(piece(ins[p], p, c), piece(outs[p].at[2 * x + y], p, c), (cx, cy, c))
                for p in range(n) for cx, cy in _other_chips(x, y)]

    def to_sibling(xyc, ins, outs):
        x, y, c = xyc
        zones = [piece(outs[p].at[2 * cx + cy], p, c) for p in range(n) if split[p] for cx, cy in _other_chips(x, y)]
        return [(z, z, (x, y, 1 - c)) for z in zones]

    n_split = sum(bool(s) for s in split)
    phases, counts = [over_ici], [3 * n]
    if n_split:
        phases, counts = phases + [to_sibling], counts + [3 * n_split]
    return _Side(shards, [SDS((N_CHIPS,) + s.shape, s.dtype) for s in shards], phases, counts)


def _ag_finish(got, shards):
    me = 2 * lax.axis_index("x") + lax.axis_index("y")
    return [lax.dynamic_update_index_in_dim(g, s, me, 0) for g, s in zip(got, shards)]


def _rs_swap_side(grads):
    n = len(grads)
    halves = [g.shape[1] // 2 for g in grads]

    def swap(xyc, ins, outs):
        x, y, c = xyc
        return [(ins[p].at[:, pl.ds((1 - c) * halves[p], halves[p])], outs[p], (x, y, 1 - c)) for p in range(n)]

    return _Side(grads, [SDS((N_CHIPS, halves[p]) + g.shape[2:], F32) for p, g in enumerate(grads)], [swap], [n])


def _rs_add(grads, theirs, tag):
    c = lax.axis_index("c")
    return [_add_half(g, t, c, "rs_add_%s%d" % (tag, p)) for p, (g, t) in enumerate(zip(grads, theirs))]


def _rs_owner_side(chip_sums):
    n = len(chip_sums)

    def to_owner(xyc, ins, outs):
        x, y, c = xyc
        return [(ins[p].at[2 * cx + cy], outs[p].at[2 * x + y], (cx, cy, c))
                for p in range(n) for cx, cy in _other_chips(x, y)]

    return _Side(chip_sums, [SDS(s.shape, s.dtype) for s in chip_sums], [to_owner], [3 * n])


def _rs_sum(parts, chip_sums, tag):
    chip = 2 * lax.axis_index("x") + lax.axis_index("y")
    parts = [lax.dynamic_update_index_in_dim(pt, lax.dynamic_index_in_dim(cs, chip, 0, keepdims=False), chip, 0)
             for pt, cs in zip(parts, chip_sums)]
    return [_sum_chips(pt, "rs_sum_%s%d" % (tag, p)) for p, pt in enumerate(parts)]


def _rs_back_side(mine):
    n = len(mine)

    def swap(xyc, ins, outs):
        x, y, c = xyc
        return [(ins[p], outs[p], (x, y, 1 - c)) for p in range(n)]

    return _Side(mine, [SDS(m.shape, F32) for m in mine], [swap], [n])


def _add_half(g, theirs, c, name):
    _, R, C = g.shape
    half = R // 2
    tr = min(256, half)
    nb = half // tr

    def body(c_ref, g_ref, t_ref, o_ref):
        o_ref[...] = (g_ref[...] + t_ref[...]).astype(BF16)

    blk = pl.BlockSpec((None, tr, C), lambda s, i, c_ref: (s, i, 0))
    return pl.pallas_call(
        body, name=name,
        grid_spec=pltpu.PrefetchScalarGridSpec(
            num_scalar_prefetch=1, grid=(N_CHIPS, nb),
            in_specs=[pl.BlockSpec((None, tr, C), lambda s, i, c_ref: (s, c_ref[0] * nb + i, 0)), blk],
            out_specs=blk),
        out_shape=SDS((N_CHIPS, half, C), BF16), compiler_params=_cp("parallel", "parallel"),
    )(jnp.reshape(c, (1,)).astype(jnp.int32), g, theirs)


def _sum_chips(parts, name):
    _, r, C = parts.shape
    tr = min(256, r)

    def body(p_ref, o_ref):
        f = lambda s: p_ref[s].astype(F32)
        o_ref[...] = ((f(0) + f(1)) + f(2)) + f(3)

    return pl.pallas_call(
        body, name=name, grid=(r // tr,), in_specs=[pl.BlockSpec((N_CHIPS, tr, C), lambda i: (0, i, 0))],
        out_specs=pl.BlockSpec((tr, C), lambda i: (i, 0)), out_shape=SDS((r, C), F32), compiler_params=_cp("parallel"),
    )(parts)


def _all_reduce_small(buf):
    rows = buf.shape[0]

    def body(x_ref, o_ref, all_ref, send_sems, recv_sems):
        x, y, c = _coords()
        me = 4 * x + 2 * y + c
        all_ref[me] = x_ref[...]
        flips = [(fx, fy, fc) for fx in (0, 1) for fy in (0, 1) for fc in (0, 1)][1:]
        copies = []
        for k, (fx, fy, fc) in enumerate(flips):
            to = (x ^ fx, y ^ fy, c ^ fc)
            copies.append(pltpu.make_async_remote_copy(src_ref=x_ref, dst_ref=all_ref.at[me], send_sem=send_sems.at[k],
                                                       recv_sem=recv_sems.at[k], device_id=to, device_id_type=MESH))
        for cp in copies:
            cp.start()
        for cp in copies:
            cp.wait_recv()
        for cp in copies:
            cp.wait_send()
        acc = all_ref[0]
        for d in range(1, N_DEV):
            acc = acc + all_ref[d]
        o_ref[...] = acc

    vm = pl.BlockSpec(memory_space=pltpu.VMEM)
    return pl.pallas_call(
        body, name="all_reduce_small", in_specs=[vm], out_specs=vm, out_shape=SDS((rows, LANES), F32),
        scratch_shapes=[pltpu.VMEM((N_DEV, rows, LANES), F32), pltpu.SemaphoreType.DMA((N_DEV - 1,)),
                        pltpu.SemaphoreType.DMA((N_DEV - 1,))],
    )(buf)


def _adam_update(w, gv, m, v):
    mn = ADAM_B1 * m + (1.0 - ADAM_B1) * gv
    vn = ADAM_B2 * v + (1.0 - ADAM_B2) * (gv * gv)
    m_hat = mn / (1.0 - ADAM_B1 ** ADAM_STEP)
    v_hat = vn / (1.0 - ADAM_B2 ** ADAM_STEP)
    return -ADAM_LR * (m_hat / (jnp.sqrt(v_hat) + ADAM_EPS) + ADAM_WD * w), mn, vn


def _adamw_halves(w, mine, theirs, m, v, c, name, row0=0, bufs=None):
    R, C = w.shape
    half = mine.shape[0]
    tr = min(128, half)
    nbh = half // tr
    b0 = row0 // tr
    assert row0 % tr == 0 and half % tr == 0

    def body(c_ref, w_ref, a_ref, b_ref, m_ref, v_ref, *rest):
        g_ref, d_ref, mo_ref, vo_ref = rest[-4:]
        is_mine = (pl.program_id(0) // nbh) == c_ref[0]
        gv = jnp.where(is_mine, a_ref[...], b_ref[...])
        g_ref[...] = gv
        d_ref[...], mo_ref[...], vo_ref[...] = _adam_update(w_ref[...], gv, m_ref[...], v_ref[...])

    blk = pl.BlockSpec((tr, C), lambda i, c_ref: (b0 + i, 0))
    hblk = pl.BlockSpec((tr, C), lambda i, c_ref: (i % nbh, 0))
    extra = [] if bufs is None else list(bufs)
    return pl.pallas_call(
        body, name=name,
        grid_spec=pltpu.PrefetchScalarGridSpec(num_scalar_prefetch=1, grid=(2 * nbh,),
                                               in_specs=[blk, hblk, hblk, blk, blk] + [HBM_SPEC] * len(extra),
                                               out_specs=(blk,) * 4),
        out_shape=(SDS((R, C), F32),) * 4, compiler_params=_cp("parallel"),
        input_output_aliases={6 + i: i for i in range(len(extra))},
    )(jnp.reshape(c, (1,)).astype(jnp.int32), w, mine, theirs, m, v, *extra)


def _adamw(w, g, m, v, name):
    R, C = w.shape
    tr = min(256, R)
    assert R % tr == 0

    def body(w_ref, g_ref, m_ref, v_ref, d_ref, mo_ref, vo_ref):
        d_ref[...], mo_ref[...], vo_ref[...] = _adam_update(w_ref[...], g_ref[...], m_ref[...], v_ref[...])

    blk = pl.BlockSpec((tr, C), lambda i: (i, 0))
    return pl.pallas_call(
        body, name=name, grid=(R // tr,), in_specs=[blk] * 4, out_specs=(blk,) * 3,
        out_shape=(SDS((R, C), F32),) * 3, compiler_params=_cp("parallel"),
    )(w, g, m, v)


def _pack(arrs):
    rows = []
    for a in arrs:
        flat = a.reshape(-1).astype(F32)
        pad = (-flat.shape[0]) % LANES
        rows.append(jnp.pad(flat, (0, pad)).reshape(-1, LANES))
    buf = jnp.concatenate(rows, axis=0)
    pad_rows = (-buf.shape[0]) % 8
    return jnp.pad(buf, ((0, pad_rows), (0, 0)))


def _unpack(buf, shapes):
    out, r = [], 0
    for shp in shapes:
        size = int(np.prod(shp))
        nr = -(-size // LANES)
        out.append(buf[r:r + nr].reshape(-1)[:size].reshape(shp))
        r += nr
    return out


def _pad_lanes(a):
    return jnp.pad(a, ((0, 0), (0, LANES - a.shape[1])))


def kernel(x, ret_w_in, ret_gn_g, ret_w_out, gdn_w_in, gdn_conv_w, gdn_a_log, gdn_dt_bias, gdn_norm_g, gdn_w_out, ln_mix_g, ln_mix_b, mlp_w1, mlp_w2, ln_ffn_g, ln_ffn_b, loss_target, m_ret_w_in, m_ret_gn_g, m_ret_w_out, m_gdn_w_in, m_gdn_conv_w, m_gdn_a_log, m_gdn_dt_bias, m_gdn_norm_g, m_gdn_w_out, m_ln_mix_g, m_ln_mix_b, m_mlp_w1, m_mlp_w2, m_ln_ffn_g, m_ln_ffn_b, v_ret_w_in, v_ret_gn_g, v_ret_w_out, v_gdn_w_in, v_gdn_conv_w, v_gdn_a_log, v_gdn_dt_bias, v_gdn_norm_g, v_gdn_w_out, v_ln_mix_g, v_ln_mix_b, v_mlp_w1, v_mlp_w2, v_ln_ffn_g, v_ln_ffn_b):
    cx, cy = lax.axis_index("x"), lax.axis_index("y")
    chip = 2 * cx + cy

    sh = dict(ret_w_in=ret_w_in[0].astype(BF16), ret_w_out=ret_w_out[0].astype(BF16),
              gdn_w_in=gdn_w_in[0].astype(BF16), gdn_w_out=gdn_w_out[0].astype(BF16),
              mlp_w1_0=mlp_w1[0].astype(BF16), mlp_w1_1=mlp_w1[1].astype(BF16),
              mlp_w2_0=mlp_w2[0].astype(BF16), mlp_w2_1=mlp_w2[1].astype(BF16), conv_w=gdn_conv_w[0])
    small = dict(ret_gn_g=ret_gn_g, a_log=_pad_lanes(gdn_a_log), dt_bias=_pad_lanes(gdn_dt_bias), norm_g=gdn_norm_g,
                 ln_mix_g=ln_mix_g, ln_mix_b=ln_mix_b, ln_ffn_g=ln_ffn_g, ln_ffn_b=ln_ffn_b)

    loss, grad_x, big, sm = _local_step(x[0], loss_target[0], sh, small)

    small_names = ["ret_gn_g", "a_log", "dt_bias", "norm_g", "ln_mix_g", "ln_mix_b", "ln_ffn_g", "ln_ffn_b", "conv_w"]
    small_shapes = [(1, 4096), (1, LANES), (1, LANES), (1, GDN_D), (2, D_MODEL), (2, D_MODEL), (2, D_MODEL),
                    (2, D_MODEL), (4, GDN_QKV)]
    red = _all_reduce_small(_pack([loss] + [sm[k] for k in small_names]))
    red_loss, *red_small = _unpack(red, [(1, 1)] + small_shapes)
    gs = dict(zip(small_names, red_small))
    g_conv = lax.dynamic_slice_in_dim(gs["conv_w"], chip * 2048, 2048, axis=1)
    g_a_log, g_dt_bias = gs["a_log"][:, :GDN_HV], gs["dt_bias"][:, :GDN_HV]

    big_w = [(ret_w_in, m_ret_w_in, v_ret_w_in, [big["ret_w_in"]]), (ret_w_out, m_ret_w_out, v_ret_w_out, [big["ret_w_out"]]),
             (gdn_w_in, m_gdn_w_in, v_gdn_w_in, [big["gdn_w_in"]]), (gdn_w_out, m_gdn_w_out, v_gdn_w_out, [big["gdn_w_out"]]),
             (mlp_w1, m_mlp_w1, v_mlp_w1, big["mlp_w1"]), (mlp_w2, m_mlp_w2, v_mlp_w2, big["mlp_w2"])]
    core = lax.axis_index("c")
    big_out = []
    for i, (w_, m_, v_, layers) in enumerate(big_w):
        two_d = lambda a: a.reshape(-1, a.shape[-1])
        res = None
        for j, (mine, theirs) in enumerate(layers):
            res = _adamw_halves(two_d(w_), mine, theirs, two_d(m_), two_d(v_), core, "adamw_%d_%d" % (i, j),
                                row0=j * 2 * mine.shape[0], bufs=res)
        big_out.append(tuple(a.reshape(w_.shape) for a in res))
    sm_w = [(ret_gn_g, m_ret_gn_g, v_ret_gn_g, gs["ret_gn_g"]), (gdn_conv_w, m_gdn_conv_w, v_gdn_conv_w, g_conv),
            (gdn_a_log, m_gdn_a_log, v_gdn_a_log, g_a_log), (gdn_dt_bias, m_gdn_dt_bias, v_gdn_dt_bias, g_dt_bias),
            (gdn_norm_g, m_gdn_norm_g, v_gdn_norm_g, gs["norm_g"]), (ln_mix_g, m_ln_mix_g, v_ln_mix_g, gs["ln_mix_g"]),
            (ln_mix_b, m_ln_mix_b, v_ln_mix_b, gs["ln_mix_b"]), (ln_ffn_g, m_ln_ffn_g, v_ln_ffn_g, gs["ln_ffn_g"]),
            (ln_ffn_b, m_ln_ffn_b, v_ln_ffn_b, gs["ln_ffn_b"])]
    sm_shapes = [w_.shape for w_, _, _, _ in sm_w]
    d_s, nm_s, nv_s = _adamw(_pack([w_ for w_, _, _, _ in sm_w]), _pack([g_ for _, _, _, g_ in sm_w]),
                             _pack([m_ for _, m_, _, _ in sm_w]), _pack([v_ for _, _, v_, _ in sm_w]), "adamw_small")
    d_s, nm_s, nv_s = (_unpack(a, sm_shapes) for a in (d_s, nm_s, nv_s))
    sm_out = [(g_.reshape(w_.shape), d_s[i], nm_s[i], nv_s[i]) for i, (w_, _, _, g_) in enumerate(sm_w)]

    per_w = [big_out[0], sm_out[0], big_out[1], big_out[2], sm_out[1], sm_out[2], sm_out[3], sm_out[4], big_out[3],
             sm_out[5], sm_out[6], big_out[4], big_out[5], sm_out[7], sm_out[8]]
    outs = [red_loss.reshape(()), grad_x[None]]
    for kind in range(4):
        outs.extend(t[kind] for t in per_w)
    return tuple(outs)
```

```python
import functools

import numpy as np
import jax
import jax.numpy as jnp
from jax import lax
from jax.experimental import pallas as pl
from jax.experimental.pallas import tpu as pltpu

F32 = jnp.float32
BF16 = jnp.bfloat16
MESH = pl.DeviceIdType.MESH
SDS = jax.ShapeDtypeStruct

D_MODEL = 2048
CHUNK = 64
RET_HEADS, RET_DK, RET_DV = 8, 256, 512
GDN_HV, GDN_D = 32, 128
HP = 2
KB = 2
NH = KB * HP
GDN_QKV = 8192
ALPHA = 4.0 ** 0.25
LN_EPS, GN_EPS, RMS_EPS, L2_EPS = 1e-5, 1e-6, 1e-6, 1e-6
ADAM_LR, ADAM_B1, ADAM_B2, ADAM_EPS, ADAM_WD, ADAM_STEP = 0.001, 0.9, 0.999, 1e-8, 0.01, 10

VMEM_LIMIT_BYTES = 56 * 1024 * 1024
RT = 256
CT = 256
ROWS = 256
LANES = 128
N_CHIPS = 4
N_DEV = 8


def _cp(*sem):
    return pltpu.CompilerParams(dimension_semantics=sem, vmem_limit_bytes=VMEM_LIMIT_BYTES)


def _dot(a, b):
    return jnp.dot(a, b, preferred_element_type=F32)


def _dot_nt(a, b):
    return lax.dot_general(a, b, (((1,), (1,)), ((), ())), preferred_element_type=F32)


def _dot_tn(a, b):
    return lax.dot_general(a, b, (((0,), (0,)), ((), ())), preferred_element_type=F32)


def _split2(x):
    hi = x.astype(BF16)
    lo = (x - hi.astype(F32)).astype(BF16)
    return hi, lo


def _dotx3(a, b):
    ah, al = _split2(a)
    bh, bl = _split2(b)
    return _dot(ah, bh) + (_dot(ah, bl) + _dot(al, bh))


def _dot_exact_l(l_bf16, x):
    hi = x.astype(BF16)
    r = x - hi.astype(F32)
    mid = r.astype(BF16)
    lo = (r - mid.astype(F32)).astype(BF16)
    return _dot(l_bf16, hi) + (_dot(l_bf16, mid) + _dot(l_bf16, lo))


def _sigmoid(x):
    return 1.0 / (1.0 + jnp.exp(-x))


def _iota(shape, dim):
    return lax.broadcasted_iota(jnp.int32, shape, dim)


class _Side:
    def __init__(self, arrays, out_shapes, phases, counts):
        self.arrays, self.out_shapes, self.phases, self.counts = list(arrays), list(out_shapes), phases, counts

    def sem_shapes(self):
        return [pltpu.SemaphoreType.DMA((sum(self.counts),))] * 2

    def ops(self, ins, outs, send_sems, recv_sems):
        def copies(ph):
            off = sum(self.counts[:ph])
            return [pltpu.make_async_remote_copy(src_ref=src, dst_ref=dst, send_sem=send_sems.at[off + i],
                                                 recv_sem=recv_sems.at[off + i], device_id=to, device_id_type=MESH)
                    for i, (src, dst, to) in enumerate(self.phases[ph](_coords(), ins, outs))]

        def start(ph):
            for cp in copies(ph):
                cp.start()

        def wait(ph):
            cps = copies(ph)
            for cp in cps:
                cp.wait_recv()
            for cp in cps:
                cp.wait_send()

        return start, wait


def _comm_call(name, side):
    n = len(side.arrays)

    def body(*refs):
        start, wait = side.ops(refs[:n], refs[n:2 * n], refs[2 * n], refs[2 * n + 1])
        for ph in range(len(side.phases)):
            start(ph)
            wait(ph)

    return pl.pallas_call(body, name=name, in_specs=[HBM_SPEC] * n, out_specs=[HBM_SPEC] * n,
                          out_shape=side.out_shapes, scratch_shapes=side.sem_shapes())(*side.arrays)


SIDE_SWITCH = 0.85


def _pcall(body, args, *, name, grid, in_specs, out_specs, out_shape, sem, scratch_shapes=(), sides=()):
    single = not isinstance(out_shape, (tuple, list))
    if not sides:
        return pl.pallas_call(body, name=name, grid=grid, in_specs=list(in_specs), out_specs=out_specs,
                              out_shape=out_shape, scratch_shapes=list(scratch_shapes), compiler_params=_cp(*sem))(*args)
    o_shapes = (out_shape,) if single else tuple(out_shape)
    o_specs = (out_specs,) if single else tuple(out_specs)
    n_in, n_out, n_scr = len(args), len(o_shapes), len(scratch_shapes)
    ns = [len(s.arrays) for s in sides]
    steps = int(np.prod(grid))

    def carrier(*refs):
        pos = n_in
        s_ins = []
        for k in ns:
            s_ins.append(refs[pos:pos + k])
            pos += k
        outs = refs[pos:pos + n_out]
        pos += n_out
        s_outs = []
        for k in ns:
            s_outs.append(refs[pos:pos + k])
            pos += k
        scr = refs[pos:pos + n_scr]
        pos += n_scr
        step = pl.program_id(0)
        for d in range(1, len(grid)):
            step = step * grid[d] + pl.program_id(d)
        hooks = [s.ops(s_ins[i], s_outs[i], refs[pos + 2 * i], refs[pos + 2 * i + 1]) for i, s in enumerate(sides)]
        for (start, wait), s in zip(hooks, sides):
            pl.when(step == 0)(functools.partial(start, 0))
            if len(s.phases) == 2:
                def switch(start=start, wait=wait):
                    wait(0)
                    start(1)
                pl.when(step == int(steps * SIDE_SWITCH))(switch)
            else:
                assert len(s.phases) == 1
        body(*refs[:n_in], *outs, *scr)
        for (start, wait), s in zip(hooks, sides):
            pl.when(step == steps - 1)(functools.partial(wait, len(s.phases) - 1))

    res = pl.pallas_call(
        carrier, name=name, grid=grid, in_specs=list(in_specs) + [HBM_SPEC] * sum(ns),
        out_specs=o_specs + (HBM_SPEC,) * sum(ns),
        out_shape=o_shapes + tuple(sh for s in sides for sh in s.out_shapes),
        scratch_shapes=list(scratch_shapes) + [sm for s in sides for sm in s.sem_shapes()],
        compiler_params=_cp(*(("arbitrary",) * len(grid))),
    )(*args, *[a for s in sides for a in s.arrays])
    main, rest, side_res = res[:n_out], list(res[n_out:]), []
    for k in ns:
        side_res.append(rest[:k])
        rest = rest[k:]
    return (main[0] if single else tuple(main)), side_res


def _mm(a, b, mode, name, *, out_dtype=F32, tm=1024, tn=1024, tk=2048, epi=None, extra=None, scale=1.0,
        shard_major=False, sides=()):
    if mode == "nn":
        (M, K), (K2, N) = a.shape, b.shape
    elif mode == "nt":
        (M, K), (N, K2) = a.shape, b.shape
    else:
        (K, M), (K2, N) = a.shape, b.shape
    assert K == K2, (a.shape, b.shape, mode)
    tm, tn, tk = min(tm, M), min(tn, N), min(tk, K)
    assert M % tm == 0 and N % tn == 0 and K % tk == 0, (M, N, K, tm, tn, tk)
    nk = K // tk
    dims = {"nn": (((1,), (0,)), ((), ())), "nt": (((1,), (1,)), ((), ())), "tn": (((0,), (0,)), ((), ()))}[mode]
    if mode == "tn":
        a_spec = pl.BlockSpec((tk, tm), lambda i, j, k: (k, i))
    else:
        a_spec = pl.BlockSpec((tm, tk), lambda i, j, k: (i, k))
    if mode == "nt":
        b_spec = pl.BlockSpec((tn, tk), lambda i, j, k: (j, k))
    else:
        b_spec = pl.BlockSpec((tk, tn), lambda i, j, k: (k, j))
    tile = pl.BlockSpec((tm, tn), lambda i, j, k: (i, j))
    in_specs, ins = [a_spec, b_spec], [a, b]
    if epi == "rope":
        half = RET_DK // 2
        assert tn % RET_DK == 0 and (2 * RET_HEADS * RET_DK) % tn == 0
        in_specs += [pl.BlockSpec((tm, half), lambda i, j, k: (i, 0))] * 2
        ins += list(extra)
    elif extra is not None:
        in_specs.append(tile)
        ins.append(extra)
    n_extra = len(ins) - 2
    if epi == "relu2":
        out_shape = (SDS((M, N), F32), SDS((M, N), BF16))
        out_specs = (tile, tile)
    elif shard_major:
        per = (N // N_CHIPS) // tn
        assert per * tn * N_CHIPS == N
        out_shape = (SDS((N_CHIPS, M, N // N_CHIPS), out_dtype),)
        out_specs = (pl.BlockSpec((None, tm, tn), lambda i, j, k: (j // per, i, j % per)),)
    else:
        out_shape = (SDS((M, N), out_dtype),)
        out_specs = (tile,)
    n_out = len(out_shape)

    def body(*refs):
        a_ref, b_ref = refs[0], refs[1]
        x_ref = refs[2] if n_extra else None
        pos = 2 + n_extra
        o_refs = refs[pos:pos + n_out]
        acc_ref = refs[pos + n_out] if nk > 1 else None

        def prod():
            av, bv = a_ref[...], b_ref[...]
            if av.dtype != BF16:
                av = av.astype(BF16)
            if bv.dtype != BF16:
                bv = bv.astype(BF16)
            return lax.dot_general(av, bv, dims, preferred_element_type=F32)

        def finish(acc):
            if epi == "relu2":
                o_refs[0][...] = acc
                r = jnp.maximum(acc, 0.0)
                o_refs[1][...] = (r * r).astype(BF16)
            elif epi == "drelu2":
                o_refs[0][...] = (acc * (2.0 * jnp.maximum(x_ref[...], 0.0))).astype(out_dtype)
            elif epi == "add":
                o_refs[0][...] = (acc + scale * x_ref[...]).astype(out_dtype)
            elif epi == "rope":
                j = pl.program_id(1)
                qk_tiles = 2 * RET_HEADS * RET_DK // tn

                @pl.when(j < qk_tiles)
                def _():
                    c, s = refs[2][...], refs[3][...]
                    sc = jnp.where(j >= qk_tiles // 2, RET_DK ** -0.5, 1.0)
                    for hh in range(tn // RET_DK):
                        lo = slice(hh * RET_DK, hh * RET_DK + half)
                        hi = slice(hh * RET_DK + half, (hh + 1) * RET_DK)
                        t1, t2 = acc[:, lo], acc[:, hi]
                        o_refs[0][:, lo] = ((t1 * c - t2 * s) * sc).astype(out_dtype)
                        o_refs[0][:, hi] = ((t1 * s + t2 * c) * sc).astype(out_dtype)

                @pl.when(j >= qk_tiles)
                def _():
                    o_refs[0][...] = acc.astype(out_dtype)
            else:
                o_refs[0][...] = acc.astype(out_dtype)

        if nk == 1:
            finish(prod())
        else:
            k = pl.program_id(2)

            @pl.when(k == 0)
            def _():
                acc_ref[...] = prod()

            @pl.when(k > 0)
            def _():
                acc_ref[...] += prod()

            @pl.when(k == nk - 1)
            def _():
                finish(acc_ref[...])

    res = _pcall(body, ins, name=name, grid=(M // tm, N // tn, nk), in_specs=in_specs, out_specs=out_specs,
                 out_shape=out_shape, scratch_shapes=[pltpu.VMEM((tm, tn), F32)] if nk > 1 else [],
                 sem=("parallel", "parallel", "arbitrary"), sides=sides)
    main, side_res = res if sides else (res, None)
    main = main if n_out > 1 else main[0]
    return (main, side_res) if sides else main


def _ln_stats(z):
    mu = jnp.mean(z, -1, keepdims=True)
    zc = z - mu
    var = jnp.mean(zc * zc, -1, keepdims=True)
    rstd = lax.rsqrt(var + LN_EPS)
    return zc * rstd, rstd


def _ln_fwd(xin, sub, g, b, name):
    S, Dm = xin.shape
    row = pl.BlockSpec((ROWS, Dm), lambda t: (t, 0))
    vec = pl.BlockSpec((1, Dm), lambda t: (0, 0))

    def body(x_ref, s_ref, g_ref, b_ref, o_ref, ob_ref, z_ref):
        z = ALPHA * x_ref[...] + s_ref[...]
        xh, _ = _ln_stats(z)
        o = xh * g_ref[...] + b_ref[...]
        o_ref[...] = o
        ob_ref[...] = o.astype(BF16)
        z_ref[...] = z

    return pl.pallas_call(
        body, name=name, grid=(S // ROWS,), in_specs=[row, row, vec, vec], out_specs=(row, row, row),
        out_shape=(SDS((S, Dm), F32), SDS((S, Dm), BF16), SDS((S, Dm), F32)), compiler_params=_cp("parallel"),
    )(xin, sub, g, b)


def _ln_bwd(dout, z, g, name):
    S, Dm = z.shape
    row = pl.BlockSpec((ROWS, Dm), lambda t: (t, 0))
    vec = pl.BlockSpec((1, Dm), lambda t: (0, 0))

    def body(d_ref, z_ref, g_ref, dz_ref, dzb_ref, dg_ref, db_ref):
        t = pl.program_id(0)
        xh, rstd = _ln_stats(z_ref[...])
        d = d_ref[...]
        dxh = d * g_ref[...]
        m1 = jnp.mean(dxh, -1, keepdims=True)
        m2 = jnp.mean(dxh * xh, -1, keepdims=True)
        dz = rstd * (dxh - m1 - xh * m2)
        dz_ref[...] = dz
        dzb_ref[...] = dz.astype(BF16)
        pg = jnp.sum(d * xh, axis=0, keepdims=True)
        pb = jnp.sum(d, axis=0, keepdims=True)

        @pl.when(t == 0)
        def _():
            dg_ref[...] = pg
            db_ref[...] = pb

        @pl.when(t > 0)
        def _():
            dg_ref[...] += pg
            db_ref[...] += pb

    return pl.pallas_call(
        body, name=name, grid=(S // ROWS,), in_specs=[row, row, vec], out_specs=(row, row, vec, vec),
        out_shape=(SDS((S, Dm), F32), SDS((S, Dm), BF16), SDS((1, Dm), F32), SDS((1, Dm), F32)),
        compiler_params=_cp("arbitrary"),
    )(dout, z, g)


def _loss_ln_bwd(z, g, b, tgt, name):
    S, Dm = z.shape
    row = pl.BlockSpec((ROWS, Dm), lambda t: (t, 0))
    vec = pl.BlockSpec((1, Dm), lambda t: (0, 0))
    one = pl.BlockSpec((1, 1), lambda t: (0, 0))

    def body(z_ref, g_ref, b_ref, t_ref, l_ref, dz_ref, dzb_ref, dg_ref, db_ref):
        t = pl.program_id(0)
        xh, rstd = _ln_stats(z_ref[...])
        diff = xh * g_ref[...] + b_ref[...] - t_ref[...]
        part = jnp.sum(jnp.sum(diff * diff, axis=1, keepdims=True), axis=0, keepdims=True) * (0.5 / Dm)
        d = diff * (1.0 / Dm)
        dxh = d * g_ref[...]
        m1 = jnp.mean(dxh, -1, keepdims=True)
        m2 = jnp.mean(dxh * xh, -1, keepdims=True)
        dz = rstd * (dxh - m1 - xh * m2)
        dz_ref[...] = dz
        dzb_ref[...] = dz.astype(BF16)
        pg = jnp.sum(d * xh, axis=0, keepdims=True)
        pb = jnp.sum(d, axis=0, keepdims=True)

        @pl.when(t == 0)
        def _():
            l_ref[...] = part
            dg_ref[...] = pg
            db_ref[...] = pb

        @pl.when(t > 0)
        def _():
            l_ref[...] += part
            dg_ref[...] += pg
            db_ref[...] += pb

    return pl.pallas_call(
        body, name=name, grid=(S // ROWS,), in_specs=[row, vec, vec, row], out_specs=(one, row, row, vec, vec),
        out_shape=(SDS((1, 1), F32), SDS((S, Dm), F32), SDS((S, Dm), BF16), SDS((1, Dm), F32), SDS((1, Dm), F32)),
        compiler_params=_cp("arbitrary"),
    )(z, g, b, tgt)


def _ret_consts():
    h = np.arange(RET_HEADS, dtype=np.float64)
    lg = np.log1p(-np.exp2(-5.0 - h))
    return jnp.asarray(np.concatenate([lg, np.exp(lg * RT)]).astype(np.float32))


def _rope_tables(S):
    half = RET_DK // 2
    inv = 10000.0 ** (-jnp.arange(half, dtype=F32) / half)
    ang = jnp.arange(S).astype(F32)[:, None] * inv[None, :]
    return jnp.cos(ang), jnp.sin(ang)


def _ret_masks(lgh):
    ri, ci = _iota((RT, RT), 0), _iota((RT, RT), 1)
    visible = (ci >> 6) <= (ri >> 6)
    m = jnp.where(visible, jnp.exp(lgh * jnp.abs(ri - ci).astype(F32)), 0.0)
    pos = _iota((RT, 1), 0).astype(F32)
    return m, jnp.exp(lgh * (pos + 1.0)), jnp.exp(lgh * (RT - 1.0 - pos))


def _ret_fwd(h0, gn_g, consts, sides=()):
    S = h0.shape[0]
    nt = S // RT

    def body(c_ref, q_ref, k_ref, v_ref, gate_ref, g_ref, y_ref, o_ref, st_ref, s_scr):
        h, t = pl.program_id(0), pl.program_id(1)

        @pl.when(t == 0)
        def _():
            s_scr[...] = jnp.zeros_like(s_scr)

        lgh, cdec = c_ref[h], c_ref[RET_HEADS + h]
        m, dq, dk = _ret_masks(lgh)
        qv, kv, vv = q_ref[...], k_ref[...], v_ref[...]
        p = (_dot_nt(qv, kv) * m).astype(BF16)
        sp = s_scr[...]
        spb = sp.astype(BF16)
        st_ref[...] = spb
        qd = (qv.astype(F32) * dq).astype(BF16)
        kd = (kv.astype(F32) * dk).astype(BF16)
        y = _dot(p, vv) + _dot(qd, spb)
        y_ref[...] = y
        s_scr[...] = sp * cdec + _dot_tn(kd, vv)
        yn, _ = _gn_stats(y)
        gate = gate_ref[...].astype(F32)
        o_ref[...] = (gate * _sigmoid(gate) * (yn * g_ref[...])).astype(BF16)

    qk = lambda off: pl.BlockSpec((RT, RET_DK), lambda h, t: (t, off + h))
    vs = lambda off: pl.BlockSpec((RT, RET_DV), lambda h, t: (t, off + h))
    return _pcall(
        body, (consts, h0, h0, h0, h0, gn_g), name="ret_fwd", grid=(RET_HEADS, nt),
        in_specs=[pl.BlockSpec(memory_space=pltpu.SMEM), qk(0), qk(RET_HEADS), vs(RET_HEADS), vs(2 * RET_HEADS),
                  pl.BlockSpec((1, RET_DV), lambda h, t: (0, h))],
        out_specs=(vs(0), vs(0), pl.BlockSpec((None, None, RET_DK, RET_DV), lambda h, t: (h, t, 0, 0))),
        out_shape=(SDS((S, 4096), F32), SDS((S, 4096), BF16), SDS((RET_HEADS, nt, RET_DK, RET_DV), BF16)),
        scratch_shapes=[pltpu.VMEM((RET_DK, RET_DV), F32)], sem=("parallel", "arbitrary"), sides=sides)


def _ret_bwd(h0, dy, states, consts, cos, sin, sides=()):
    S = h0.shape[0]
    nt = S // RT
    half = RET_DK // 2

    def body(c_ref, q_ref, k_ref, v_ref, dy_ref, st_ref, cos_ref, sin_ref, dq_ref, dk_ref, dv_ref, ds_scr):
        h, t = pl.program_id(0), pl.program_id(1)

        @pl.when(t == 0)
        def _():
            ds_scr[...] = jnp.zeros_like(ds_scr)

        lgh, cdec = c_ref[h], c_ref[RET_HEADS + h]
        m, dqc, dkc = _ret_masks(lgh)
        qv, kv, vv, dyv, spb = q_ref[...], k_ref[...], v_ref[...], dy_ref[...], st_ref[...]
        p = (_dot_nt(qv, kv) * m).astype(BF16)
        qd = (qv.astype(F32) * dqc).astype(BF16)
        kd = (kv.astype(F32) * dkc).astype(BF16)
        dsn = ds_scr[...]
        dsb = dsn.astype(BF16)
        dsc = (_dot_nt(dyv, vv) * m).astype(BF16)
        dq = _dot(dsc, kv) + _dot_nt(dyv, spb) * dqc
        dk = _dot_tn(dsc, qv) + _dot_nt(vv, dsb) * dkc
        dv_ref[...] = (_dot_tn(p, dyv) + _dot(kd, dsb)).astype(BF16)
        ds_scr[...] = dsn * cdec + _dot_tn(qd, dyv)
        c, s = cos_ref[...], sin_ref[...]

        def unrot(d):
            d1, d2 = d[:, :half], d[:, half:]
            return jnp.concatenate([d1 * c + d2 * s, d2 * c - d1 * s], axis=-1)

        dq_ref[...] = unrot(dq).astype(BF16)
        dk_ref[...] = (unrot(dk) * (RET_DK ** -0.5)).astype(BF16)

    rev = lambda t: nt - 1 - t
    qkb = lambda off: pl.BlockSpec((RT, RET_DK), lambda h, t: (rev(t), off + h))
    vsb = lambda off: pl.BlockSpec((RT, RET_DV), lambda h, t: (rev(t), off + h))
    qk, vs = qkb(0), vsb(0)
    tab = pl.BlockSpec((RT, half), lambda h, t: (rev(t), 0))
    return _pcall(
        body, (consts, h0, h0, h0, dy, states, cos, sin), name="ret_bwd", grid=(RET_HEADS, nt),
        in_specs=[pl.BlockSpec(memory_space=pltpu.SMEM), qk, qkb(RET_HEADS), vsb(RET_HEADS), vs,
                  pl.BlockSpec((None, None, RET_DK, RET_DV), lambda h, t: (h, rev(t), 0, 0)), tab, tab],
        out_specs=(qk, qk, vs),
        out_shape=(SDS((S, 2048), BF16), SDS((S, 2048), BF16), SDS((S, 4096), BF16)),
        scratch_shapes=[pltpu.VMEM((RET_DK, RET_DV), F32)], sem=("parallel", "arbitrary"), sides=sides)


def _gn_stats(y):
    mu = jnp.mean(y, -1, keepdims=True)
    yc = y - mu
    var = jnp.mean(yc * yc, -1, keepdims=True)
    rstd = lax.rsqrt(var + GN_EPS)
    return yc * rstd, rstd


def _ret_post_bwd(do, y, h0, gn_g, sides=()):
    S = y.shape[0]

    def body(do_ref, y_ref, gate_ref, g_ref, dy_ref, dgate_ref, dg_ref):
        t = pl.program_id(1)
        yn, rstd = _gn_stats(y_ref[...])
        gate, g, dov = gate_ref[...].astype(F32), g_ref[...], do_ref[...]
        sg = _sigmoid(gate)
        dgate_ref[...] = (dov * (yn * g) * (sg * (1.0 + gate * (1.0 - sg)))).astype(BF16)
        dyg = dov * (gate * sg)
        dyn = dyg * g
        m1 = jnp.mean(dyn, -1, keepdims=True)
        m2 = jnp.mean(dyn * yn, -1, keepdims=True)
        dy_ref[...] = (rstd * (dyn - m1 - yn * m2)).astype(BF16)
        pg = jnp.sum(dyg * yn, axis=0, keepdims=True)

        @pl.when(t == 0)
        def _():
            dg_ref[...] = pg

        @pl.when(t > 0)
        def _():
            dg_ref[...] += pg

    vs = lambda off: pl.BlockSpec((RT, RET_DV), lambda h, t: (t, off + h))
    vec = pl.BlockSpec((1, RET_DV), lambda h, t: (0, h))
    return _pcall(
        body, (do, y, h0, gn_g), name="ret_post_bwd", grid=(RET_HEADS, S // RT),
        in_specs=[vs(0), vs(0), vs(2 * RET_HEADS), vec], out_specs=(vs(0), vs(0), vec),
        out_shape=(SDS((S, 4096), BF16), SDS((S, 4096), BF16), SDS((1, 4096), F32)),
        sem=("parallel", "arbitrary"), sides=sides)


def _conv_taps(x_ref, halo_ref, w_ref, ext_scr, t):
    ext_scr[0:8, :] = jnp.where(t == 0, 0.0, halo_ref[...])
    ext_scr[8:, :] = x_ref[...]
    w = w_ref[...]
    n = x_ref.shape[0]
    acc = w[3:4, :] * ext_scr[8:, :]
    for j in range(3):
        acc = acc + w[j:j + 1, :] * ext_scr[pl.ds(5 + j, n), :]
    return acc


def _gdn_conv_fwd(h1, conv_w, kind):
    S = h1.shape[0]
    base = {"q": 0, "k": 1, "v": 2}[kind]
    ncb = 2 if kind == "v" else 1
    C = 2048

    def body(x_ref, halo_ref, w_ref, o_ref, ext_scr):
        acc = _conv_taps(x_ref, halo_ref, w_ref, ext_scr, pl.program_id(0))
        c = acc * _sigmoid(acc)
        if kind == "v":
            o_ref[...] = c.astype(BF16)
        else:
            scale = GDN_D ** -0.5 if kind == "q" else 1.0
            for hh in range(C // GDN_D):
                ch = c[:, hh * GDN_D:(hh + 1) * GDN_D]
                r = lax.rsqrt(jnp.sum(ch * ch, -1, keepdims=True) + L2_EPS)
                o_ref[:, hh * GDN_D:(hh + 1) * GDN_D] = (ch * (r * scale)).astype(BF16)

    hb = ROWS // 8
    return pl.pallas_call(
        body, name="gdn_conv_fwd_" + kind, grid=(S // ROWS, ncb),
        in_specs=[pl.BlockSpec((ROWS, C), lambda t, j: (t, base + j)),
                  pl.BlockSpec((8, C), lambda t, j: (jnp.maximum(t * hb - 1, 0), base + j)),
                  pl.BlockSpec((4, C), lambda t, j: (0, base + j))],
        out_specs=pl.BlockSpec((ROWS, C), lambda t, j: (t, j)), out_shape=SDS((S, C * ncb), BF16),
        scratch_shapes=[pltpu.VMEM((ROWS + 8, C), F32)], compiler_params=_cp("parallel", "parallel"),
    )(h1, h1, conv_w)


def _gdn_conv_bwd_act(h1, conv_w, dn, kind, buf=None):
    S = h1.shape[0]
    base = {"q": 0, "k": 1, "v": 2}[kind]
    ncb = 2 if kind == "v" else 1
    C = 2048

    def body(x_ref, halo_ref, w_ref, dn_ref, *rest):
        o_ref, ext_scr = rest[-2], rest[-1]
        acc = _conv_taps(x_ref, halo_ref, w_ref, ext_scr, pl.program_id(0))
        sg = _sigmoid(acc)
        dsilu = sg * (1.0 + acc * (1.0 - sg))
        if kind == "v":
            o_ref[...] = dn_ref[...] * dsilu
        else:
            c = acc * sg
            scale = GDN_D ** -0.5 if kind == "q" else 1.0
            for hh in range(C // GDN_D):
                sl = slice(hh * GDN_D, (hh + 1) * GDN_D)
                ch, dnh = c[:, sl], dn_ref[:, sl]
                r = lax.rsqrt(jnp.sum(ch * ch, -1, keepdims=True) + L2_EPS)
                proj = jnp.sum(dnh * ch, -1, keepdims=True)
                o_ref[:, sl] = (scale * r) * (dnh - ch * (proj * r * r)) * dsilu[:, sl]

    hb = ROWS // 8
    return pl.pallas_call(
        body, name="gdn_conv_bwd_act_" + kind, grid=(S // ROWS, ncb),
        in_specs=[pl.BlockSpec((ROWS, C), lambda t, j: (t, base + j)),
                  pl.BlockSpec((8, C), lambda t, j: (jnp.maximum(t * hb - 1, 0), base + j)),
                  pl.BlockSpec((4, C), lambda t, j: (0, base + j)),
                  pl.BlockSpec((ROWS, C), lambda t, j: (t, j))] + ([] if buf is None else [HBM_SPEC]),
        out_specs=pl.BlockSpec((ROWS, C), lambda t, j: (t, base + j)), out_shape=SDS((S, GDN_QKV), F32),
        input_output_aliases={} if buf is None else {4: 0},
        scratch_shapes=[pltpu.VMEM((ROWS + 8, C), F32)], compiler_params=_cp("parallel", "parallel"),
    )(*((h1, h1, conv_w, dn) + (() if buf is None else (buf,))))


def _gdn_conv_bwd_in(h1, conv_w, dacc, dh1_buf):
    S = h1.shape[0]
    C = 2048
    nt = S // ROWS
    hb = ROWS // 8

    def body(x_ref, halo_ref, w_ref, d_ref, dhalo_ref, buf_ref, di_ref, dw_ref, ext_scr, dext_scr):
        t = pl.program_id(1)
        ext_scr[0:8, :] = jnp.where(t == 0, 0.0, halo_ref[...])
        ext_scr[8:, :] = x_ref[...]
        d = d_ref[...]
        dext_scr[0:ROWS, :] = d
        dext_scr[ROWS:, :] = jnp.where(t == nt - 1, 0.0, dhalo_ref[...])
        w = w_ref[...]
        di = w[3:4, :] * d
        for j in range(3):
            di = di + w[j:j + 1, :] * dext_scr[pl.ds(3 - j, ROWS), :]
        di_ref[...] = di.astype(BF16)
        rows = [jnp.sum(d * ext_scr[pl.ds(5 + j, ROWS), :], axis=0, keepdims=True) for j in range(4)]
        pw = jnp.concatenate(rows, axis=0)

        @pl.when(t == 0)
        def _():
            dw_ref[...] = pw

        @pl.when(t > 0)
        def _():
            dw_ref[...] += pw

    return pl.pallas_call(
        body, name="gdn_conv_bwd_in", grid=(GDN_QKV // C, nt),
        in_specs=[pl.BlockSpec((ROWS, C), lambda j, t: (t, j)),
                  pl.BlockSpec((8, C), lambda j, t: (jnp.maximum(t * hb - 1, 0), j)),
                  pl.BlockSpec((4, C), lambda j, t: (0, j)),
                  pl.BlockSpec((ROWS, C), lambda j, t: (t, j)),
                  pl.BlockSpec((8, C), lambda j, t: (jnp.minimum((t + 1) * hb, nt * hb - 1), j)), HBM_SPEC],
        out_specs=(pl.BlockSpec((ROWS, C), lambda j, t: (t, j)), pl.BlockSpec((4, C), lambda j, t: (0, j))),
        out_shape=(SDS(dh1_buf.shape, BF16), SDS((4, GDN_QKV), F32)), input_output_aliases={5: 0},
        scratch_shapes=[pltpu.VMEM((ROWS + 8, C), F32), pltpu.VMEM((ROWS + 8, C), F32)],
        compiler_params=_cp("parallel", "arbitrary"),
    )(h1, h1, conv_w, dacc, dacc, dh1_buf)


def _chunk_masks():
    ri, ci = _iota((CT, CT), 0), _iota((CT, CT), 1)
    same = (ri >> 6) == (ci >> 6)
    return same, same & (ri >= ci), same & (ri > ci), same & (ri <= ci), ri == ci


def _fold_dup(m):
    h = m[:, :LANES] + m[:, LANES:]
    return h + pltpu.roll(h, CHUNK, axis=1)


def _unfold_bd(d, same):
    return jnp.where(same, jnp.concatenate([d, d], axis=1), 0.0)


def _softplus(x):
    return jnp.maximum(x, 0.0) + jnp.log(1.0 + jnp.exp(-jnp.abs(x)))


def _gdn_scal_fwd(ba, a_log, dt_bias):
    S = ba.shape[0]

    def body(ba_ref, al_ref, dt_ref, beta_ref, g_ref, gc_ref):
        bav = ba_ref[...]
        beta_ref[...] = _sigmoid(bav)
        a = pltpu.roll(bav, LANES - GDN_HV, axis=1)
        g = -jnp.exp(al_ref[...]) * _softplus(a + dt_ref[...])
        g_ref[...] = g
        causal = _chunk_masks()[1]
        gc_ref[...] = _dot_exact_l(causal.astype(BF16), g)

    row = pl.BlockSpec((CT, LANES), lambda t: (t, 0))
    vec = pl.BlockSpec((1, LANES), lambda t: (0, 0))
    return pl.pallas_call(
        body, name="gdn_scal_fwd", grid=(S // CT,), in_specs=[row, vec, vec], out_specs=(row, row, row),
        out_shape=(SDS((S, LANES), F32),) * 3, compiler_params=_cp("parallel"),
    )(ba, a_log, dt_bias)


def _gdn_scal_bwd(ba, a_log, dt_bias, g, dbeta, dgc):
    S = ba.shape[0]

    def body(ba_ref, al_ref, dt_ref, g_ref, dbeta_ref, dgc_ref, dba_ref, dal_ref, ddt_ref):
        t = pl.program_id(0)
        bav = ba_ref[...]
        beta = _sigmoid(bav)
        db = dbeta_ref[...] * beta * (1.0 - beta)
        a = pltpu.roll(bav, LANES - GDN_HV, axis=1)
        dgv = _dot_exact_l(_chunk_masks()[3].astype(BF16), dgc_ref[...])
        da = dgv * (-jnp.exp(al_ref[...])) * _sigmoid(a + dt_ref[...])
        lane = _iota(bav.shape, 1)
        da_sh = pltpu.roll(da, GDN_HV, axis=1)
        dba = jnp.where(lane < GDN_HV, db, jnp.where(lane < 2 * GDN_HV, da_sh, 0.0))
        dba_ref[...] = dba.astype(BF16)
        keep = lane < GDN_HV
        pal = jnp.sum(jnp.where(keep, dgv * g_ref[...], 0.0), axis=0, keepdims=True)
        pdt = jnp.sum(jnp.where(keep, da, 0.0), axis=0, keepdims=True)

        @pl.when(t == 0)
        def _():
            dal_ref[...] = pal
            ddt_ref[...] = pdt

        @pl.when(t > 0)
        def _():
            dal_ref[...] += pal
            ddt_ref[...] += pdt

    row = pl.BlockSpec((CT, LANES), lambda t: (t, 0))
    vec = pl.BlockSpec((1, LANES), lambda t: (0, 0))
    return pl.pallas_call(
        body, name="gdn_scal_bwd", grid=(S // CT,), in_specs=[row, vec, vec, row, row, row],
        out_specs=(row, vec, vec), out_shape=(SDS((S, LANES), BF16), SDS((1, LANES), F32), SDS((1, LANES), F32)),
        compiler_params=_cp("arbitrary"),
    )(ba, a_log, dt_bias, g, dbeta, dgc)


def _sel_col(x, h):
    return jnp.sum(jnp.where(_iota(x.shape, 1) == h, x, 0.0), axis=1, keepdims=True)


def _decay(gcol, causal):
    gm = jnp.broadcast_to(gcol, (CT, CT))
    diff = gm - gm.T
    return jnp.where(causal, jnp.exp(jnp.where(causal, diff, 0.0)), 0.0)


def _gdn_chunk_fwd(k, v, beta, gc):
    S = k.shape[0]
    nt = S // CT

    def body(k_ref, v_ref, beta_ref, gc_ref, t_ref, u_ref, w_ref):
        kg_id = pl.program_id(1)
        same, causal, strict, _, _ = _chunk_masks()
        eye_dup = jnp.where((_iota((CT, LANES), 0) & (CHUNK - 1)) == (_iota((CT, LANES), 1) & (CHUNK - 1)), 1.0, 0.0)
        kfs, xs, xds, pds, cols = [], [], [], [], []
        for kb in range(KB):
            kv = k_ref[:, kb * GDN_D:(kb + 1) * GDN_D]
            kfs.append(kv.astype(F32))
            kk = _dot_nt(kv, kv)
            for hp in range(HP):
                h = NH * kg_id + kb * HP + hp
                bcol, gcol = _sel_col(beta_ref[...], h), _sel_col(gc_ref[...], h)
                x = jnp.where(strict, -(kk * bcol) * _decay(gcol, causal), 0.0)
                xs.append(x)
                xds.append(_fold_dup(x))
                pds.append(eye_dup)
                cols.append((bcol, gcol))
        for m in range(6):
            for hi in range(NH):
                out = _dotx3(xs[hi], jnp.concatenate([xds[hi], pds[hi]], axis=1))
                pds[hi] = pds[hi] + out[:, LANES:]
                if m < 5:
                    xds[hi] = out[:, :LANES]
                    xs[hi] = _unfold_bd(xds[hi], same)
        for hi in range(NH):
            bcol, gcol = cols[hi]
            cs = slice(hi * GDN_D, (hi + 1) * GDN_D)
            t_ref[hi] = pds[hi]
            tb = _unfold_bd(pds[hi], same).astype(BF16)
            vb = (v_ref[:, cs].astype(F32) * bcol).astype(BF16)
            kg = (kfs[hi // HP] * (bcol * jnp.exp(gcol))).astype(BF16)
            uw = _dot(tb, jnp.concatenate([vb, kg], axis=1))
            u_ref[:, cs] = uw[:, :GDN_D]
            w_ref[:, cs] = uw[:, GDN_D:].astype(BF16)

    col = pl.BlockSpec((CT, LANES), lambda t, kg: (t, 0))
    hv = pl.BlockSpec((CT, NH * GDN_D), lambda t, kg: (t, kg))
    return pl.pallas_call(
        body, name="gdn_chunk_fwd", grid=(nt, GDN_HV // NH),
        in_specs=[pl.BlockSpec((CT, KB * GDN_D), lambda t, kg: (t, kg)), hv, col, col],
        out_specs=(pl.BlockSpec((NH, None, CT, LANES), lambda t, kg: (kg, t, 0, 0)), hv, hv),
        out_shape=(SDS((GDN_HV, nt, CT, LANES), F32), SDS((S, 4096), F32), SDS((S, 4096), BF16)),
        compiler_params=_cp("parallel", "parallel"),
    )(k, v, beta, gc)


def _last_of_chunk(gcol, rows, c):
    return jnp.sum(jnp.where(rows == c * CHUNK + CHUNK - 1, gcol, 0.0), axis=0, keepdims=True)


def _gdn_scan_fwd(q, k, u, w, gc):
    S = q.shape[0]
    nt = S // CT
    ncs = CT // CHUNK

    def body(q_ref, k_ref, u_ref, w_ref, gc_ref, y_ref, vn_ref, st_ref, s_scr, vn_scr):
        kg_id, t = pl.program_id(0), pl.program_id(1)

        @pl.when(t == 0)
        def _():
            s_scr[...] = jnp.zeros_like(s_scr)

        causal = _chunk_masks()[1]
        rows = _iota((CT, 1), 0)
        heads = []
        for kb in range(KB):
            ks = slice(kb * GDN_D, (kb + 1) * GDN_D)
            qv, kv = q_ref[:, ks], k_ref[:, ks]
            qf, kf = qv.astype(F32), kv.astype(F32)
            qk = _dot_nt(qv, kv)
            for hp in range(HP):
                gcol = _sel_col(gc_ref[...], NH * kg_id + kb * HP + hp)
                heads.append((gcol, (qk * _decay(gcol, causal)).astype(BF16), (qf * jnp.exp(gcol)).astype(BF16), kf))
        vn_scr[...] = jnp.zeros_like(vn_scr)
        for c in range(ncs):
            r = slice(c * CHUNK, (c + 1) * CHUNK)
            for hi in range(NH):
                gcol, attn, qd, kf = heads[hi]
                cs = slice(hi * GDN_D, (hi + 1) * GDN_D)
                s = s_scr[hi]
                st_ref[hi, c] = s
                sb = s.astype(BF16)
                gl = _last_of_chunk(gcol, rows, c)
                kd = (kf[r] * jnp.exp(gl - gcol[r])).astype(BF16)
                vn = (u_ref[r, cs] - _dot(w_ref[r, cs], sb)).astype(BF16)
                vn_scr[r, cs] = vn
                y_ref[r, cs] = _dot(qd[r], sb) + _dot(attn[r], vn_scr[:, cs])
                s_scr[hi] = s * jnp.exp(gl) + _dot_tn(kd, vn)
        vn_ref[...] = vn_scr[...]

    hk = pl.BlockSpec((CT, KB * GDN_D), lambda kg, t: (t, kg))
    hv = pl.BlockSpec((CT, NH * GDN_D), lambda kg, t: (t, kg))
    col = pl.BlockSpec((CT, LANES), lambda kg, t: (t, 0))
    return pl.pallas_call(
        body, name="gdn_scan_fwd", grid=(GDN_HV // NH, nt), in_specs=[hk, hk, hv, hv, col],
        out_specs=(hv, hv, pl.BlockSpec((NH, ncs, GDN_D, GDN_D), lambda kg, t: (kg, t, 0, 0))),
        out_shape=(SDS((S, 4096), F32), SDS((S, 4096), BF16), SDS((GDN_HV, S // CHUNK, GDN_D, GDN_D), F32)),
        scratch_shapes=[pltpu.VMEM((NH, GDN_D, GDN_D), F32), pltpu.VMEM((CT, NH * GDN_D), BF16)],
        compiler_params=_cp("parallel", "arbitrary"),
    )(q, k, u, w, gc)


def _gdn_scan_bwd(q, k, w, vn, gc, states, dy, sides=()):
    S = q.shape[0]
    nt = S // CT
    ncs = CT // CHUNK

    def body(q_ref, k_ref, w_ref, vn_ref, gc_ref, st_ref, dy_ref, du_ref, dw_ref, dq_ref, dk_ref, dgc_ref, ds_scr):
        kg_id, t = pl.program_id(0), pl.program_id(1)

        @pl.when(t == 0)
        def _():
            ds_scr[...] = jnp.zeros_like(ds_scr)

        _, causal, _, _, eye = _chunk_masks()
        rows = _iota((CT, 1), 0)
        heads = []
        for kb in range(KB):
            ks = slice(kb * GDN_D, (kb + 1) * GDN_D)
            qv, kv = q_ref[:, ks], k_ref[:, ks]
            qf, kf = qv.astype(F32), kv.astype(F32)
            qk = _dot_nt(qv, kv)
            for hp in range(HP):
                hi = kb * HP + hp
                cs = slice(hi * GDN_D, (hi + 1) * GDN_D)
                gcol = _sel_col(gc_ref[...], NH * kg_id + hi)
                dm = _decay(gcol, causal)
                attn_f = qk * dm
                egc = jnp.exp(gcol)
                qd_f = qf * egc
                dyv, vnv = dy_ref[:, cs], vn_ref[:, cs]
                heads.append(dict(cs=cs, ks=ks, gcol=gcol, dm=dm, attn_f=attn_f, egc=egc, qd_f=qd_f, kf=kf, qv=qv, kv=kv,
                                  qd=qd_f.astype(BF16), dy=dyv, vn=vnv, dattn=_dot_nt(dyv, vnv),
                                  at_dy=_dot_tn(attn_f.astype(BF16), dyv), dgc=[None] * ncs))
        dq_ref[...] = jnp.zeros_like(dq_ref)
        dk_ref[...] = jnp.zeros_like(dk_ref)
        for c in reversed(range(ncs)):
            r = slice(c * CHUNK, (c + 1) * CHUNK)
            for hi in range(NH):
                hd = heads[hi]
                cs, ks, gcol = hd["cs"], hd["ks"], hd["gcol"]
                s = st_ref[hi, c]
                sb = s.astype(BF16)
                dsn = ds_scr[hi]
                dsb = dsn.astype(BF16)
                gl = _last_of_chunk(gcol, rows, c)
                cd = jnp.exp(gl)
                ekd = jnp.exp(gl - gcol[r])
                kd_f = hd["kf"][r] * ekd
                dvn = (hd["at_dy"][r] + _dot(kd_f.astype(BF16), dsb)).astype(BF16)
                dqd = _dot_nt(hd["dy"][r], sb)
                dkd = _dot_nt(hd["vn"][r], dsb)
                dcd = jnp.sum(jnp.sum(s * dsn, axis=1, keepdims=True), axis=0, keepdims=True)
                ds_scr[hi] = dsn * cd + _dot_tn(hd["qd"][r], hd["dy"][r]) - _dot_tn(w_ref[r, cs], dvn)
                du_ref[r, cs] = dvn
                dw_ref[r, cs] = (-_dot_nt(dvn, sb)).astype(BF16)
                dq_ref[r, ks] += dqd * hd["egc"][r]
                dk_ref[r, ks] += dkd * ekd
                rs_q = jnp.sum(dqd * hd["qd_f"][r], axis=1, keepdims=True)
                rs_k = jnp.sum(dkd * kd_f, axis=1, keepdims=True)
                tot = jnp.sum(rs_k, axis=0, keepdims=True) + dcd * cd
                hd["dgc"][c] = rs_q - rs_k + jnp.where(rows[r] == c * CHUNK + CHUNK - 1, tot, 0.0)
        for hi in range(NH):
            hd = heads[hi]
            ks = hd["ks"]
            dab = (hd["dattn"] * hd["dm"]).astype(BF16)
            dq_ref[:, ks] += _dot(dab, hd["kv"])
            dk_ref[:, ks] += _dot_tn(dab, hd["qv"])
            e1 = hd["dattn"] * hd["attn_f"]
            dgc = (jnp.concatenate(hd["dgc"], axis=0) + jnp.sum(e1, axis=1, keepdims=True)
                   - jnp.sum(e1.T, axis=1, keepdims=True))
            dgc_ref[hi] = jnp.sum(jnp.where(eye, jnp.broadcast_to(dgc, (CT, CT)), 0.0), axis=0, keepdims=True)

    rev = lambda t: nt - 1 - t
    hk = pl.BlockSpec((CT, KB * GDN_D), lambda kg, t: (rev(t), kg))
    hv = pl.BlockSpec((CT, NH * GDN_D), lambda kg, t: (rev(t), kg))
    col = pl.BlockSpec((CT, LANES), lambda kg, t: (rev(t), 0))
    return _pcall(
        body, (q, k, w, vn, gc, states, dy), name="gdn_scan_bwd", grid=(GDN_HV // NH, nt),
        in_specs=[hk, hk, hv, hv, col, pl.BlockSpec((NH, ncs, GDN_D, GDN_D), lambda kg, t: (kg, rev(t), 0, 0)), hv],
        out_specs=(hv, hv, hk, hk, pl.BlockSpec((NH, 1, CT), lambda kg, t: (kg, 0, rev(t)))),
        out_shape=(SDS((S, 4096), BF16), SDS((S, 4096), BF16), SDS((S, 2048), F32), SDS((S, 2048), F32),
                   SDS((GDN_HV, 1, S), F32)),
        scratch_shapes=[pltpu.VMEM((NH, GDN_D, GDN_D), F32)], sem=("parallel", "arbitrary"), sides=sides)


def _gdn_chunk_bwd(k, v, beta, gc, tmat, du, dw, dk_p, dgc_p, sides=()):
    S = k.shape[0]
    nt = S // CT

    def body(k_ref, v_ref, beta_ref, gc_ref, t_ref, du_ref, dw_ref, dkp_ref, dgcp_ref,
             dk_ref, dv_ref, dbeta_ref, dgc_ref):
        kg_id = pl.program_id(1)
        same, causal, strict, _, eye = _chunk_masks()
        lane = _iota((CT, LANES), 1)

        @pl.when(kg_id == 0)
        def _():
            dbeta_ref[...] = jnp.zeros_like(dbeta_ref)
            dgc_ref[...] = jnp.zeros_like(dgc_ref)

        for kb_i in range(KB):
            ks = slice(kb_i * GDN_D, (kb_i + 1) * GDN_D)
            kv = k_ref[:, ks]
            kf = kv.astype(F32)
            kk = _dot_nt(kv, kv)
            dk = dkp_ref[:, ks]
            for hp in range(HP):
                hi = kb_i * HP + hp
                h = NH * kg_id + hi
                cs = slice(hi * GDN_D, (hi + 1) * GDN_D)
                bcol, gcol = _sel_col(beta_ref[...], h), _sel_col(gc_ref[...], h)
                dm = _decay(gcol, causal)
                vf = v_ref[:, cs].astype(F32)
                kb = kf * bcol
                a = jnp.where(strict, (kk * bcol) * dm, 0.0)
                egc = jnp.exp(gcol)
                kg_f = kb * egc
                tb = _unfold_bd(t_ref[hi], same).astype(BF16)
                duw = jnp.concatenate([du_ref[:, cs], dw_ref[:, cs]], axis=1)
                dt = _dot_nt(duw, jnp.concatenate([(vf * bcol).astype(BF16), kg_f.astype(BF16)], axis=1))
                dvb_dkg = _dot_tn(tb, duw)
                dvb, dkg = dvb_dkg[:, :GDN_D], dvb_dkg[:, GDN_D:]
                da = -_dot_nt(_dot_tn(tb, dt.astype(BF16)).astype(BF16), tb)
                rm = jnp.where(strict, da, 0.0)
                rdb = (rm * dm).astype(BF16)
                dkb = _dot(rdb, kv) + dkg * egc
                dk = dk + _dot_tn(rdb, kb.astype(BF16)) + dkb * bcol
                e2 = rm * a
                dgc_in = jnp.sum(jnp.where(eye, jnp.broadcast_to(dgcp_ref[hi], (CT, CT)), 0.0), axis=1, keepdims=True)
                dgc = (jnp.sum(e2, axis=1, keepdims=True) - jnp.sum(e2.T, axis=1, keepdims=True)
                       + jnp.sum(dkg * kg_f, axis=1, keepdims=True) + dgc_in)
                dbeta = jnp.sum(dkb * kf, axis=1, keepdims=True) + jnp.sum(dvb * vf, axis=1, keepdims=True)
                dv_ref[:, cs] = dvb * bcol
                dbeta_ref[...] += jnp.where(lane == h, dbeta, 0.0)
                dgc_ref[...] += jnp.where(lane == h, dgc, 0.0)
            dk_ref[:, ks] = dk

    hk = pl.BlockSpec((CT, KB * GDN_D), lambda t, kg: (t, kg))
    hv = pl.BlockSpec((CT, NH * GDN_D), lambda t, kg: (t, kg))
    col = pl.BlockSpec((CT, LANES), lambda t, kg: (t, 0))
    return _pcall(
        body, (k, v, beta, gc, tmat, du, dw, dk_p, dgc_p), name="gdn_chunk_bwd", grid=(nt, GDN_HV // NH),
        in_specs=[hk, hv, col, col, pl.BlockSpec((NH, None, CT, LANES), lambda t, kg: (kg, t, 0, 0)), hv, hv, hk,
                  pl.BlockSpec((NH, 1, CT), lambda t, kg: (kg, 0, t))],
        out_specs=(hk, hv, col, col),
        out_shape=(SDS((S, 2048), F32), SDS((S, 4096), F32), SDS((S, LANES), F32), SDS((S, LANES), F32)),
        sem=("parallel", "arbitrary"), sides=sides)


def _gdn_post_fwd(y, h1, norm_g):
    S = y.shape[0]

    def body(y_ref, z_ref, g_ref, o_ref):
        g = g_ref[...]
        for hh in range(GDN_HV):
            sl = slice(hh * GDN_D, (hh + 1) * GDN_D)
            yh, zh = y_ref[:, sl], z_ref[:, sl]
            yn = yh * lax.rsqrt(jnp.mean(yh * yh, -1, keepdims=True) + RMS_EPS)
            o_ref[:, sl] = (yn * g * (zh * _sigmoid(zh))).astype(BF16)

    row = lambda off: pl.BlockSpec((ROWS, 4096), lambda t: (t, off))
    return pl.pallas_call(
        body, name="gdn_post_fwd", grid=(S // ROWS,),
        in_specs=[row(0), row(2), pl.BlockSpec((1, GDN_D), lambda t: (0, 0))], out_specs=row(0),
        out_shape=SDS((S, 4096), BF16), compiler_params=_cp("parallel"),
    )(y, h1, norm_g)


def _gdn_post_bwd(do, y, h1, norm_g, sides=()):
    S = y.shape[0]

    def body(do_ref, y_ref, z_ref, g_ref, dy_ref, dz_ref, dg_ref):
        t = pl.program_id(0)
        g = g_ref[...]
        pg = jnp.zeros((1, GDN_D), F32)
        for hh in range(GDN_HV):
            sl = slice(hh * GDN_D, (hh + 1) * GDN_D)
            yh, zh, doh = y_ref[:, sl], z_ref[:, sl], do_ref[:, sl]
            rstd = lax.rsqrt(jnp.mean(yh * yh, -1, keepdims=True) + RMS_EPS)
            yn = yh * rstd
            sg = _sigmoid(zh)
            dz_ref[:, sl] = (doh * (yn * g) * (sg * (1.0 + zh * (1.0 - sg)))).astype(BF16)
            dyg = doh * (zh * sg)
            dyn = dyg * g
            dy_ref[:, sl] = (rstd * (dyn - yn * jnp.mean(dyn * yn, -1, keepdims=True))).astype(BF16)
            pg = pg + jnp.sum(dyg * yn, axis=0, keepdims=True)

        @pl.when(t == 0)
        def _():
            dg_ref[...] = pg

        @pl.when(t > 0)
        def _():
            dg_ref[...] += pg

    row = lambda off: pl.BlockSpec((ROWS, 4096), lambda t: (t, off))
    vec = pl.BlockSpec((1, GDN_D), lambda t: (0, 0))
    return _pcall(
        body, (do, y, h1, norm_g), name="gdn_post_bwd", grid=(S // ROWS,), in_specs=[row(0), row(0), row(2), vec],
        out_specs=(row(0), row(2), vec),
        out_shape=(SDS((S, 4096), BF16), SDS((S, 3 * 4096), BF16), SDS((1, GDN_D), F32)),
        sem=("arbitrary",), sides=sides)


def _cols(g):
    return jnp.transpose(g, (1, 0, 2)).reshape(g.shape[1], -1)


def _rows(g):
    return g.reshape(-1, g.shape[-1])


def _local_step(x, tgt, sh, small):
    S = x.shape[0]
    xb = x.astype(BF16)
    rc = _ret_consts()
    cos, sin = _rope_tables(S)

    def gather(names):
        shards = [sh[k] for k in names]
        return shards, _ag_side(shards, [k != "conv_w" for k in names])

    lead, side = gather(["ret_w_in"])
    wri = _cols(_ag_finish(_comm_call("ag_ret_in", side), lead)[0])
    shards, side = gather(["ret_w_out", "mlp_w1_0", "gdn_w_out"])
    h0, (got,) = _mm(xb, wri, "nn", "mm_ret_in", out_dtype=BF16, epi="rope", extra=(cos, sin), sides=[side])
    g_ro, g_w10, g_go = _ag_finish(got, shards)
    wro, wgo, w1 = _rows(g_ro), _rows(g_go), [_cols(g_w10), None]
    shards, side = gather(["mlp_w2_0", "conv_w"])
    (yr, o0, ret_st), (got,) = _ret_fwd(h0, small["ret_gn_g"], rc, sides=[side])
    g_w20, g_cv = _ag_finish(got, shards)
    w2, conv_w = [_rows(g_w20), None], _cols(g_cv)
    mix0 = _mm(o0, wro, "nn", "mm_ret_out")
    x1, x1b, z1 = _ln_fwd(x, mix0, small["ln_mix_g"][0:1], small["ln_mix_b"][0:1], "ln_mix0_fwd")
    shards, side = gather(["gdn_w_in"])
    (hh0, a0), (got,) = _mm(x1b, w1[0], "nn", "mm_mlp0_up", epi="relu2", sides=[side])
    wgi = _cols(_ag_finish(got, shards)[0])
    wgi_main = wgi[:, :GDN_QKV + 4096]
    wba = jnp.pad(wgi[:, GDN_QKV + 4096:], ((0, 0), (0, LANES - 2 * GDN_HV)))
    shards, side = gather(["mlp_w1_1"])
    m0, (got,) = _mm(a0, w2[0], "nn", "mm_mlp0_down", sides=[side])
    w1[1] = _cols(_ag_finish(got, shards)[0])
    x2, x2b, z2 = _ln_fwd(x1, m0, small["ln_ffn_g"][0:1], small["ln_ffn_b"][0:1], "ln_ffn0_fwd")

    shards, side = gather(["mlp_w2_1"])
    h1, (got,) = _mm(x2b, wgi_main, "nn", "mm_gdn_in", sides=[side])
    w2[1] = _rows(_ag_finish(got, shards)[0])
    ba = _mm(x2b, wba, "nn", "mm_gdn_ba")
    qn = _gdn_conv_fwd(h1, conv_w, "q")
    kn = _gdn_conv_fwd(h1, conv_w, "k")
    vg = _gdn_conv_fwd(h1, conv_w, "v")
    beta, g, gc = _gdn_scal_fwd(ba, small["a_log"], small["dt_bias"])
    tmat, u, w = _gdn_chunk_fwd(kn, vg, beta, gc)
    yg, vn, gdn_st = _gdn_scan_fwd(qn, kn, u, w, gc)
    o1 = _gdn_post_fwd(yg, h1, small["norm_g"])
    mix1 = _mm(o1, wgo, "nn", "mm_gdn_out")
    x3, x3b, z3 = _ln_fwd(x2, mix1, small["ln_mix_g"][1:2], small["ln_mix_b"][1:2], "ln_mix1_fwd")
    hh1, a1 = _mm(x3b, w1[1], "nn", "mm_mlp1_up", epi="relu2")
    z4 = _mm(a1, w2[1], "nn", "mm_mlp1_down", epi="add", extra=x3, scale=ALPHA)

    loss, dz4, dz4b, d_lnf_g1, d_lnf_b1 = _loss_ln_bwd(z4, small["ln_ffn_g"][1:2], small["ln_ffn_b"][1:2], tgt,
                                                       "loss_ln_ffn1_bwd")
    dhh1 = _mm(dz4b, w2[1], "nt", "mm_mlp1_down_dx", epi="drelu2", extra=hh1, out_dtype=BF16)
    dw2_1 = _mm(a1, dz4b, "tn", "mm_mlp1_down_dw")
    dx3 = _mm(dhh1, w1[1], "nt", "mm_mlp1_up_dx", epi="add", extra=dz4, scale=ALPHA)
    dw1_1 = _mm(x3b, dhh1, "tn", "mm_mlp1_up_dw", shard_major=True)
    dz3, dz3b, d_lnm_g1, d_lnm_b1 = _ln_bwd(dx3, z3, small["ln_mix_g"][1:2], "ln_mix1_bwd")
    shards_of = lambda g: g.reshape(N_CHIPS, -1, g.shape[-1])
    g_a = [dw1_1, shards_of(dw2_1)]
    do1, (th_a,) = _mm(dz3b, wgo, "nt", "mm_gdn_out_dx", sides=[_rs_swap_side(g_a)])
    sums_a = _rs_add(g_a, th_a, "a")
    dwgo = _mm(o1, dz3b, "tn", "mm_gdn_out_dw")
    dyg, dh1_z, d_norm_g = _gdn_post_bwd(do1, yg, h1, small["norm_g"])
    du, dw, dqn, dk_p, dgc_p = _gdn_scan_bwd(qn, kn, w, vn, gc, gdn_st, dyg)
    dkn, dvg, dbeta, dgc = _gdn_chunk_bwd(kn, vg, beta, gc, tmat, du, dw, dk_p, dgc_p)
    dba, d_a_log, d_dt_bias = _gdn_scal_bwd(ba, small["a_log"], small["dt_bias"], g, dbeta, dgc)
    dacc = _gdn_conv_bwd_act(h1, conv_w, dqn, "q")
    dacc = _gdn_conv_bwd_act(h1, conv_w, dkn, "k", dacc)
    dacc = _gdn_conv_bwd_act(h1, conv_w, dvg, "v", dacc)
    dh1, d_conv_w = _gdn_conv_bwd_in(h1, conv_w, dacc, dh1_z)
    g_go = [shards_of(dwgo)]
    dx2_ba, (th_go,) = _mm(dba, wba, "nt", "mm_gdn_ba_dx", epi="add", extra=dz3, scale=ALPHA,
                           sides=[_rs_swap_side(g_go)])
    sums_a = sums_a + _rs_add(g_go, th_go, "go")
    dx2, (parts_a,) = _mm(dh1, wgi_main, "nt", "mm_gdn_in_dx", epi="add", extra=dx2_ba,
                          sides=[_rs_owner_side(sums_a)])
    mine_a = _rs_sum(parts_a, sums_a, "a")
    dwgi_main, (back_a,) = _mm(x2b, dh1, "tn", "mm_gdn_in_dw", sides=[_rs_back_side(mine_a)])
    red_w1_1, red_w2_1, red_go = zip(mine_a, back_a)
    dwba = _mm(x2b, dba, "tn", "mm_gdn_ba_dw")
    dwgi = jnp.concatenate([dwgi_main, dwba[:, :2 * GDN_HV]], axis=1)
    g_b = [jnp.transpose(dwgi.reshape(dwgi.shape[0], N_CHIPS, -1), (1, 0, 2))]

    dz2, dz2b, d_lnf_g0, d_lnf_b0 = _ln_bwd(dx2, z2, small["ln_ffn_g"][0:1], "ln_ffn0_bwd")
    dhh0, (th_b,) = _mm(dz2b, w2[0], "nt", "mm_mlp0_down_dx", epi="drelu2", extra=hh0, out_dtype=BF16,
                        sides=[_rs_swap_side(g_b)])
    sums_b = _rs_add(g_b, th_b, "b")
    dw2_0 = _mm(a0, dz2b, "tn", "mm_mlp0_down_dw")
    dx1, (parts_b,) = _mm(dhh0, w1[0], "nt", "mm_mlp0_up_dx", epi="add", extra=dz2, scale=ALPHA,
                          sides=[_rs_owner_side(sums_b)])
    mine_b = _rs_sum(parts_b, sums_b, "b")
    dw1_0 = _mm(x1b, dhh0, "tn", "mm_mlp0_up_dw", shard_major=True)
    dz1, dz1b, d_lnm_g0, d_lnm_b0 = _ln_bwd(dx1, z1, small["ln_mix_g"][0:1], "ln_mix0_bwd")
    g_c = [dw1_0, shards_of(dw2_0)]
    do0, (th_c, back_b) = _mm(dz1b, wro, "nt", "mm_ret_out_dx", sides=[_rs_swap_side(g_c), _rs_back_side(mine_b)])
    (red_gi,) = zip(mine_b, back_b)
    sums_c = _rs_add(g_c, th_c, "c")
    dwro = _mm(o0, dz1b, "tn", "mm_ret_out_dw")
    g_ro = [shards_of(dwro)]
    (dyr, dgate, d_gn_g), (th_ro, parts_c) = _ret_post_bwd(do0, yr, h0, small["ret_gn_g"],
                                                          sides=[_rs_swap_side(g_ro), _rs_owner_side(sums_c)])
    sums_ro = _rs_add(g_ro, th_ro, "ro")
    mine_c = _rs_sum(parts_c, sums_c, "c")
    (dq0, dk0, dv0), (parts_ro, back_c) = _ret_bwd(h0, dyr, ret_st, rc, cos, sin,
                                                   sides=[_rs_owner_side(sums_ro), _rs_back_side(mine_c)])
    red_w1_0, red_w2_0 = zip(mine_c, back_c)
    mine_ro = _rs_sum(parts_ro, sums_ro, "ro")
    dh0 = jnp.concatenate([dq0, dk0, dv0, dgate], axis=1)
    dwri, (back_ro,) = _mm(xb, dh0, "tn", "mm_ret_in_dw", shard_major=True, sides=[_rs_back_side(mine_ro)])
    (red_ro,) = zip(mine_ro, back_ro)
    g_d = [dwri]
    sums_d = _rs_add(g_d, _comm_call("rs_swap_halves_d", _rs_swap_side(g_d)), "d")
    grad_x, (parts_d,) = _mm(dh0, wri, "nt", "mm_ret_in_dx", epi="add", extra=dz1, scale=ALPHA,
                             sides=[_rs_owner_side(sums_d)])
    mine_d = _rs_sum(parts_d, sums_d, "d")
    (red_ri,) = zip(mine_d, _comm_call("rs_swap_reduced_d", _rs_back_side(mine_d)))

    big = dict(ret_w_in=red_ri, ret_w_out=red_ro, gdn_w_in=red_gi, gdn_w_out=red_go,
               mlp_w1=(red_w1_0, red_w1_1), mlp_w2=(red_w2_0, red_w2_1))
    sm = dict(ret_gn_g=d_gn_g, a_log=d_a_log, dt_bias=d_dt_bias, norm_g=d_norm_g,
              ln_mix_g=jnp.concatenate([d_lnm_g0, d_lnm_g1], 0), ln_mix_b=jnp.concatenate([d_lnm_b0, d_lnm_b1], 0),
              ln_ffn_g=jnp.concatenate([d_lnf_g0, d_lnf_g1], 0), ln_ffn_b=jnp.concatenate([d_lnf_b0, d_lnf_b1], 0),
              conv_w=d_conv_w)
    return loss, grad_x, big, sm


def _coords():
    return lax.axis_index("x"), lax.axis_index("y"), lax.axis_index("c")


HBM_SPEC = pl.BlockSpec(memory_space=pl.ANY)


def _other_chips(x, y):
    return [(1 - x, y), (x, 1 - y), (1 - x, 1 - y)]


def _ag_side(shards, split):
    n = len(shards)

    def piece(ref, p, core):
        if not split[p]:
            return ref
        half = shards[p].shape[0] // 2
        return ref.at[pl.ds(core * half, half)]

    def over_ici(xyc, ins, outs):
        x, y, c = xyc
        return [(piece(ins[p], p, c), piece(outs[p].at[2 * x + y], p, c), (cx, cy, c))
                for p in range(n) for cx, cy in _other_chips(x, y)]

    def to_sibling(xyc, ins, outs):
        x, y, c = xyc
        zones = [piece(outs[p].at[2 * cx + cy], p, c) for p in range(n) if split[p] for cx, cy in _other_chips(x, y)]
        return [(z, z, (x, y, 1 - c)) for z in zones]

    n_split = sum(bool(s) for s in split)
    phases, counts = [over_ici], [3 * n]
    if n_split:
        phases, counts = phases + [to_sibling], counts + [3 * n_split]
    return _Side(shards, [SDS((N_CHIPS,) + s.shape, s.dtype) for s in shards], phases, counts)


def _ag_finish(got, shards):
    me = 2 * lax.axis_index("x") + lax.axis_index("y")
    return [lax.dynamic_update_index_in_dim(g, s, me, 0) for g, s in zip(got, shards)]


def _rs_swap_side(grads):
    n = len(grads)
    halves = [g.shape[1] // 2 for g in grads]

    def swap(xyc, ins, outs):
        x, y, c = xyc
        return [(ins[p].at[:, pl.ds((1 - c) * halves[p], halves[p])], outs[p], (x, y, 1 - c)) for p in range(n)]

    return _Side(grads, [SDS((N_CHIPS, halves[p]) + g.shape[2:], F32) for p, g in enumerate(grads)], [swap], [n])


def _rs_add(grads, theirs, tag):
    c = lax.axis_index("c")
    return [_add_half(g, t, c, "rs_add_%s%d" % (tag, p)) for p, (g, t) in enumerate(zip(grads, theirs))]


def _rs_owner_side(chip_sums):
    n = len(chip_sums)

    def to_owner(xyc, ins, outs):
        x, y, c = xyc
        return [(ins[p].at[2 * cx + cy], outs[p].at[2 * x + y], (cx, cy, c))
                for p in range(n) for cx, cy in _other_chips(x, y)]

    return _Side(chip_sums, [SDS(s.shape, s.dtype) for s in chip_sums], [to_owner], [3 * n])


def _rs_sum(parts, chip_sums, tag):
    chip = 2 * lax.axis_index("x") + lax.axis_index("y")
    parts = [lax.dynamic_update_index_in_dim(pt, lax.dynamic_index_in_dim(cs, chip, 0, keepdims=False), chip, 0)
             for pt, cs in zip(parts, chip_sums)]
    return [_sum_chips(pt, "rs_sum_%s%d" % (tag, p)) for p, pt in enumerate(parts)]


def _rs_back_side(mine):
    n = len(mine)

    def swap(xyc, ins, outs):
        x, y, c = xyc
        return [(ins[p], outs[p], (x, y, 1 - c)) for p in range(n)]

    return _Side(mine, [SDS(m.shape, F32) for m in mine], [swap], [n])


def _add_half(g, theirs, c, name):
    _, R, C = g.shape
    half = R // 2
    tr = min(256, half)
    nb = half // tr

    def body(c_ref, g_ref, t_ref, o_ref):
        o_ref[...] = (g_ref[...] + t_ref[...]).astype(BF16)

    blk = pl.BlockSpec((None, tr, C), lambda s, i, c_ref: (s, i, 0))
    return pl.pallas_call(
        body, name=name,
        grid_spec=pltpu.PrefetchScalarGridSpec(
            num_scalar_prefetch=1, grid=(N_CHIPS, nb),
            in_specs=[pl.BlockSpec((None, tr, C), lambda s, i, c_ref: (s, c_ref[0] * nb + i, 0)), blk],
            out_specs=blk),
        out_shape=SDS((N_CHIPS, half, C), BF16), compiler_params=_cp("parallel", "parallel"),
    )(jnp.reshape(c, (1,)).astype(jnp.int32), g, theirs)


def _sum_chips(parts, name):
    _, r, C = parts.shape
    tr = min(256, r)

    def body(p_ref, o_ref):
        f = lambda s: p_ref[s].astype(F32)
        o_ref[...] = ((f(0) + f(1)) + f(2)) + f(3)

    return pl.pallas_call(
        body, name=name, grid=(r // tr,), in_specs=[pl.BlockSpec((N_CHIPS, tr, C), lambda i: (0, i, 0))],
        out_specs=pl.BlockSpec((tr, C), lambda i: (i, 0)), out_shape=SDS((r, C), F32), compiler_params=_cp("parallel"),
    )(parts)


def _all_reduce_small(buf):
    rows = buf.shape[0]

    def body(x_ref, o_ref, all_ref, send_sems, recv_sems):
        x, y, c = _coords()
        me = 4 * x + 2 * y + c
        all_ref[me] = x_ref[...]
        flips = [(fx, fy, fc) for fx in (0, 1) for fy in (0, 1) for fc in (0, 1)][1:]
        copies = []
        for k, (fx, fy, fc) in enumerate(flips):
            to = (x ^ fx, y ^ fy, c ^ fc)
            copies.append(pltpu.make_async_remote_copy(src_ref=x_ref, dst_ref=all_ref.at[me], send_sem=send_sems.at[k],
                                                       recv_sem=recv_sems.at[k], device_id=to, device_id_type=MESH))
        for cp in copies:
            cp.start()
        for cp in copies:
            cp.wait_recv()
        for cp in copies:
            cp.wait_send()
        acc = all_ref[0]
        for d in range(1, N_DEV):
            acc = acc + all_ref[d]
        o_ref[...] = acc

    vm = pl.BlockSpec(memory_space=pltpu.VMEM)
    return pl.pallas_call(
        body, name="all_reduce_small", in_specs=[vm], out_specs=vm, out_shape=SDS((rows, LANES), F32),
        scratch_shapes=[pltpu.VMEM((N_DEV, rows, LANES), F32), pltpu.SemaphoreType.DMA((N_DEV - 1,)),
                        pltpu.SemaphoreType.DMA((N_DEV - 1,))],
    )(buf)


def _adam_update(w, gv, m, v):
    mn = ADAM_B1 * m + (1.0 - ADAM_B1) * gv
    vn = ADAM_B2 * v + (1.0 - ADAM_B2) * (gv * gv)
    m_hat = mn / (1.0 - ADAM_B1 ** ADAM_STEP)
    v_hat = vn / (1.0 - ADAM_B2 ** ADAM_STEP)
    return -ADAM_LR * (m_hat / (jnp.sqrt(v_hat) + ADAM_EPS) + ADAM_WD * w), mn, vn


def _adamw_halves(w, mine, theirs, m, v, c, name, row0=0, bufs=None):
    R, C = w.shape
    half = mine.shape[0]
    tr = min(128, half)
    nbh = half // tr
    b0 = row0 // tr
    assert row0 % tr == 0 and half % tr == 0

    def body(c_ref, w_ref, a_ref, b_ref, m_ref, v_ref, *rest):
        g_ref, d_ref, mo_ref, vo_ref = rest[-4:]
        is_mine = (pl.program_id(0) // nbh) == c_ref[0]
        gv = jnp.where(is_mine, a_ref[...], b_ref[...])
        g_ref[...] = gv
        d_ref[...], mo_ref[...], vo_ref[...] = _adam_update(w_ref[...], gv, m_ref[...], v_ref[...])

    blk = pl.BlockSpec((tr, C), lambda i, c_ref: (b0 + i, 0))
    hblk = pl.BlockSpec((tr, C), lambda i, c_ref: (i % nbh, 0))
    extra = [] if bufs is None else list(bufs)
    return pl.pallas_call(
        body, name=name,
        grid_spec=pltpu.PrefetchScalarGridSpec(num_scalar_prefetch=1, grid=(2 * nbh,),
                                               in_specs=[blk, hblk, hblk, blk, blk] + [HBM_SPEC] * len(extra),
                                               out_specs=(blk,) * 4),
        out_shape=(SDS((R, C), F32),) * 4, compiler_params=_cp("parallel"),
        input_output_aliases={6 + i: i for i in range(len(extra))},
    )(jnp.reshape(c, (1,)).astype(jnp.int32), w, mine, theirs, m, v, *extra)


def _adamw(w, g, m, v, name):
    R, C = w.shape
    tr = min(256, R)
    assert R % tr == 0

    def body(w_ref, g_ref, m_ref, v_ref, d_ref, mo_ref, vo_ref):
        d_ref[...], mo_ref[...], vo_ref[...] = _adam_update(w_ref[...], g_ref[...], m_ref[...], v_ref[...])

    blk = pl.BlockSpec((tr, C), lambda i: (i, 0))
    return pl.pallas_call(
        body, name=name, grid=(R // tr,), in_specs=[blk] * 4, out_specs=(blk,) * 3,
        out_shape=(SDS((R, C), F32),) * 3, compiler_params=_cp("parallel"),
    )(w, g, m, v)


def _pack(arrs):
    rows = []
    for a in arrs:
        flat = a.reshape(-1).astype(F32)
        pad = (-flat.shape[0]) % LANES
        rows.append(jnp.pad(flat, (0, pad)).reshape(-1, LANES))
    buf = jnp.concatenate(rows, axis=0)
    pad_rows = (-buf.shape[0]) % 8
    return jnp.pad(buf, ((0, pad_rows), (0, 0)))


def _unpack(buf, shapes):
    out, r = [], 0
    for shp in shapes:
        size = int(np.prod(shp))
        nr = -(-size // LANES)
        out.append(buf[r:r + nr].reshape(-1)[:size].reshape(shp))
        r += nr
    return out


def _pad_lanes(a):
    return jnp.pad(a, ((0, 0), (0, LANES - a.shape[1])))


def kernel(x, ret_w_in, ret_gn_g, ret_w_out, gdn_w_in, gdn_conv_w, gdn_a_log, gdn_dt_bias, gdn_norm_g, gdn_w_out, ln_mix_g, ln_mix_b, mlp_w1, mlp_w2, ln_ffn_g, ln_ffn_b, loss_target, m_ret_w_in, m_ret_gn_g, m_ret_w_out, m_gdn_w_in, m_gdn_conv_w, m_gdn_a_log, m_gdn_dt_bias, m_gdn_norm_g, m_gdn_w_out, m_ln_mix_g, m_ln_mix_b, m_mlp_w1, m_mlp_w2, m_ln_ffn_g, m_ln_ffn_b, v_ret_w_in, v_ret_gn_g, v_ret_w_out, v_gdn_w_in, v_gdn_conv_w, v_gdn_a_log, v_gdn_dt_bias, v_gdn_norm_g, v_gdn_w_out, v_ln_mix_g, v_ln_mix_b, v_mlp_w1, v_mlp_w2, v_ln_ffn_g, v_ln_ffn_b):
    cx, cy = lax.axis_index("x"), lax.axis_index("y")
    chip = 2 * cx + cy

    sh = dict(ret_w_in=ret_w_in[0].astype(BF16), ret_w_out=ret_w_out[0].astype(BF16),
              gdn_w_in=gdn_w_in[0].astype(BF16), gdn_w_out=gdn_w_out[0].astype(BF16),
              mlp_w1_0=mlp_w1[0].astype(BF16), mlp_w1_1=mlp_w1[1].astype(BF16),
              mlp_w2_0=mlp_w2[0].astype(BF16), mlp_w2_1=mlp_w2[1].astype(BF16), conv_w=gdn_conv_w[0])
    small = dict(ret_gn_g=ret_gn_g, a_log=_pad_lanes(gdn_a_log), dt_bias=_pad_lanes(gdn_dt_bias), norm_g=gdn_norm_g,
                 ln_mix_g=ln_mix_g, ln_mix_b=ln_mix_b, ln_ffn_g=ln_ffn_g, ln_ffn_b=ln_ffn_b)

    loss, grad_x, big, sm = _local_step(x[0], loss_target[0], sh, small)

    small_names = ["ret_gn_g", "a_log", "dt_bias", "norm_g", "ln_mix_g", "ln_mix_b", "ln_ffn_g", "ln_ffn_b", "conv_w"]
    small_shapes = [(1, 4096), (1, LANES), (1, LANES), (1, GDN_D), (2, D_MODEL), (2, D_MODEL), (2, D_MODEL),
                    (2, D_MODEL), (4, GDN_QKV)]
    red = _all_reduce_small(_pack([loss] + [sm[k] for k in small_names]))
    red_loss, *red_small = _unpack(red, [(1, 1)] + small_shapes)
    gs = dict(zip(small_names, red_small))
    g_conv = lax.dynamic_slice_in_dim(gs["conv_w"], chip * 2048, 2048, axis=1)
    g_a_log, g_dt_bias = gs["a_log"][:, :GDN_HV], gs["dt_bias"][:, :GDN_HV]

    big_w = [(ret_w_in, m_ret_w_in, v_ret_w_in, [big["ret_w_in"]]), (ret_w_out, m_ret_w_out, v_ret_w_out, [big["ret_w_out"]]),
             (gdn_w_in, m_gdn_w_in, v_gdn_w_in, [big["gdn_w_in"]]), (gdn_w_out, m_gdn_w_out, v_gdn_w_out, [big["gdn_w_out"]]),
             (mlp_w1, m_mlp_w1, v_mlp_w1, big["mlp_w1"]), (mlp_w2, m_mlp_w2, v_mlp_w2, big["mlp_w2"])]
    core = lax.axis_index("c")
    big_out = []
    for i, (w_, m_, v_, layers) in enumerate(big_w):
        two_d = lambda a: a.reshape(-1, a.shape[-1])
        res = None
        for j, (mine, theirs) in enumerate(layers):
            res = _adamw_halves(two_d(w_), mine, theirs, two_d(m_), two_d(v_), core, "adamw_%d_%d" % (i, j),
                                row0=j * 2 * mine.shape[0], bufs=res)
        big_out.append(tuple(a.reshape(w_.shape) for a in res))
    sm_w = [(ret_gn_g, m_ret_gn_g, v_ret_gn_g, gs["ret_gn_g"]), (gdn_conv_w, m_gdn_conv_w, v_gdn_conv_w, g_conv),
            (gdn_a_log, m_gdn_a_log, v_gdn_a_log, g_a_log), (gdn_dt_bias, m_gdn_dt_bias, v_gdn_dt_bias, g_dt_bias),
            (gdn_norm_g, m_gdn_norm_g, v_gdn_norm_g, gs["norm_g"]), (ln_mix_g, m_ln_mix_g, v_ln_mix_g, gs["ln_mix_g"]),
            (ln_mix_b, m_ln_mix_b, v_ln_mix_b, gs["ln_mix_b"]), (ln_ffn_g, m_ln_ffn_g, v_ln_ffn_g, gs["ln_ffn_g"]),
            (ln_ffn_b, m_ln_ffn_b, v_ln_ffn_b, gs["ln_ffn_b"])]
    sm_shapes = [w_.shape for w_, _, _, _ in sm_w]
    d_s, nm_s, nv_s = _adamw(_pack([w_ for w_, _, _, _ in sm_w]), _pack([g_ for _, _, _, g_ in sm_w]),
                             _pack([m_ for _, m_, _, _ in sm_w]), _pack([v_ for _, _, v_, _ in sm_w]), "adamw_small")
    d_s, nm_s, nv_s = (_unpack(a, sm_shapes) for a in (d_s, nm_s, nv_s))
    sm_out = [(g_.reshape(w_.shape), d_s[i], nm_s[i], nv_s[i]) for i, (w_, _, _, g_) in enumerate(sm_w)]

    per_w = [big_out[0], sm_out[0], big_out[1], big_out[2], sm_out[1], sm_out[2], sm_out[3], sm_out[4], big_out[3],
             sm_out[5], sm_out[6], big_out[4], big_out[5], sm_out[7], sm_out[8]]
    outs = [red_loss.reshape(()), grad_x[None]]
    for kind in range(4):
        outs.extend(t[kind] for t in per_w)
    return tuple(outs)
```

```python
import functools

import numpy as np
import jax
import jax.numpy as jnp
from jax import lax
from jax.experimental import pallas as pl
from jax.experimental.pallas import tpu as pltpu

F32 = jnp.float32
BF16 = jnp.bfloat16
MESH = pl.DeviceIdType.MESH
SDS = jax.ShapeDtypeStruct

D_MODEL = 2048
CHUNK = 64
RET_HEADS, RET_DK, RET_DV = 8, 256, 512
GDN_HV, GDN_D = 32, 128
HP = 2
KB = 2
NH = KB * HP
GDN_QKV = 8192
ALPHA = 4.0 ** 0.25
LN_EPS, GN_EPS, RMS_EPS, L2_EPS = 1e-5, 1e-6, 1e-6, 1e-6
ADAM_LR, ADAM_B1, ADAM_B2, ADAM_EPS, ADAM_WD, ADAM_STEP = 0.001, 0.9, 0.999, 1e-8, 0.01, 10

VMEM_LIMIT_BYTES = 56 * 1024 * 1024
RT = 256
CT = 256
ROWS = 256
LANES = 128
N_CHIPS = 4
N_DEV = 8


def _cp(*sem):
    return pltpu.CompilerParams(dimension_semantics=sem, vmem_limit_bytes=VMEM_LIMIT_BYTES)


def _dot(a, b):
    return jnp.dot(a, b, preferred_element_type=F32)


def _dot_nt(a, b):
    return lax.dot_general(a, b, (((1,), (1,)), ((), ())), preferred_element_type=F32)


def _dot_tn(a, b):
    return lax.dot_general(a, b, (((0,), (0,)), ((), ())), preferred_element_type=F32)


def _split2(x):
    hi = x.astype(BF16)
    lo = (x - hi.astype(F32)).astype(BF16)
    return hi, lo


def _dotx3(a, b):
    ah, al = _split2(a)
    bh, bl = _split2(b)
    return _dot(ah, bh) + (_dot(ah, bl) + _dot(al, bh))


def _dot_exact_l(l_bf16, x):
    hi = x.astype(BF16)
    r = x - hi.astype(F32)
    mid = r.astype(BF16)
    lo = (r - mid.astype(F32)).astype(BF16)
    return _dot(l_bf16, hi) + (_dot(l_bf16, mid) + _dot(l_bf16, lo))


def _sigmoid(x):
    return 1.0 / (1.0 + jnp.exp(-x))


def _iota(shape, dim):
    return lax.broadcasted_iota(jnp.int32, shape, dim)


class _Side:
    def __init__(self, arrays, out_shapes, phases, counts):
        self.arrays, self.out_shapes, self.phases, self.counts = list(arrays), list(out_shapes), phases, counts

    def sem_shapes(self):
        return [pltpu.SemaphoreType.DMA((sum(self.counts),))] * 2

    def ops(self, ins, outs, send_sems, recv_sems):
        def copies(ph):
            off = sum(self.counts[:ph])
            return [pltpu.make_async_remote_copy(src_ref=src, dst_ref=dst, send_sem=send_sems.at[off + i],
                                                 recv_sem=recv_sems.at[off + i], device_id=to, device_id_type=MESH)
                    for i, (src, dst, to) in enumerate(self.phases[ph](_coords(), ins, outs))]

        def start(ph):
            for cp in copies(ph):
                cp.start()

        def wait(ph):
            cps = copies(ph)
            for cp in cps:
                cp.wait_recv()
            for cp in cps:
                cp.wait_send()

        return start, wait


def _comm_call(name, side):
    n = len(side.arrays)

    def body(*refs):
        start, wait = side.ops(refs[:n], refs[n:2 * n], refs[2 * n], refs[2 * n + 1])
        for ph in range(len(side.phases)):
            start(ph)
            wait(ph)

    return pl.pallas_call(body, name=name, in_specs=[HBM_SPEC] * n, out_specs=[HBM_SPEC] * n,
                          out_shape=side.out_shapes, scratch_shapes=side.sem_shapes())(*side.arrays)


SIDE_SWITCH = 0.85


def _pcall(body, args, *, name, grid, in_specs, out_specs, out_shape, sem, scratch_shapes=(), sides=()):
    single = not isinstance(out_shape, (tuple, list))
    if not sides:
        return pl.pallas_call(body, name=name, grid=grid, in_specs=list(in_specs), out_specs=out_specs,
                              out_shape=out_shape, scratch_shapes=list(scratch_shapes), compiler_params=_cp(*sem))(*args)
    o_shapes = (out_shape,) if single else tuple(out_shape)
    o_specs = (out_specs,) if single else tuple(out_specs)
    n_in, n_out, n_scr = len(args), len(o_shapes), len(scratch_shapes)
    ns = [len(s.arrays) for s in sides]
    steps = int(np.prod(grid))

    def carrier(*refs):
        pos = n_in
        s_ins = []
        for k in ns:
            s_ins.append(refs[pos:pos + k])
            pos += k
        outs = refs[pos:pos + n_out]
        pos += n_out
        s_outs = []
        for k in ns:
            s_outs.append(refs[pos:pos + k])
            pos += k
        scr = refs[pos:pos + n_scr]
        pos += n_scr
        step = pl.program_id(0)
        for d in range(1, len(grid)):
            step = step * grid[d] + pl.program_id(d)
        hooks = [s.ops(s_ins[i], s_outs[i], refs[pos + 2 * i], refs[pos + 2 * i + 1]) for i, s in enumerate(sides)]
        for (start, wait), s in zip(hooks, sides):
            pl.when(step == 0)(functools.partial(start, 0))
            if len(s.phases) == 2:
                def switch(start=start, wait=wait):
                    wait(0)
                    start(1)
                pl.when(step == int(steps * SIDE_SWITCH))(switch)
            else:
                assert len(s.phases) == 1
        body(*refs[:n_in], *outs, *scr)
        for (start, wait), s in zip(hooks, sides):
            pl.when(step == steps - 1)(functools.partial(wait, len(s.phases) - 1))

    res = pl.pallas_call(
        carrier, name=name, grid=grid, in_specs=list(in_specs) + [HBM_SPEC] * sum(ns),
        out_specs=o_specs + (HBM_SPEC,) * sum(ns),
        out_shape=o_shapes + tuple(sh for s in sides for sh in s.out_shapes),
        scratch_shapes=list(scratch_shapes) + [sm for s in sides for sm in s.sem_shapes()],
        compiler_params=_cp(*(("arbitrary",) * len(grid))),
    )(*args, *[a for s in sides for a in s.arrays])
    main, rest, side_res = res[:n_out], list(res[n_out:]), []
    for k in ns:
        side_res.append(rest[:k])
        rest = rest[k:]
    return (main[0] if single else tuple(main)), side_res


def _mm(a, b, mode, name, *, out_dtype=F32, tm=1024, tn=1024, tk=2048, epi=None, extra=None, scale=1.0,
        shard_major=False, sides=()):
    if mode == "nn":
        (M, K), (K2, N) = a.shape, b.shape
    elif mode == "nt":
        (M, K), (N, K2) = a.shape, b.shape
    else:
        (K, M), (K2, N) = a.shape, b.shape
    assert K == K2, (a.shape, b.shape, mode)
    tm, tn, tk = min(tm, M), min(tn, N), min(tk, K)
    assert M % tm == 0 and N % tn == 0 and K % tk == 0, (M, N, K, tm, tn, tk)
    nk = K // tk
    dims = {"nn": (((1,), (0,)), ((), ())), "nt": (((1,), (1,)), ((), ())), "tn": (((0,), (0,)), ((), ()))}[mode]
    if mode == "tn":
        a_spec = pl.BlockSpec((tk, tm), lambda i, j, k: (k, i))
    else:
        a_spec = pl.BlockSpec((tm, tk), lambda i, j, k: (i, k))
    if mode == "nt":
        b_spec = pl.BlockSpec((tn, tk), lambda i, j, k: (j, k))
    else:
        b_spec = pl.BlockSpec((tk, tn), lambda i, j, k: (k, j))
    tile = pl.BlockSpec((tm, tn), lambda i, j, k: (i, j))
    in_specs, ins = [a_spec, b_spec], [a, b]
    if epi == "rope":
        half = RET_DK // 2
        assert tn % RET_DK == 0 and (2 * RET_HEADS * RET_DK) % tn == 0
        in_specs += [pl.BlockSpec((tm, half), lambda i, j, k: (i, 0))] * 2
        ins += list(extra)
    elif extra is not None:
        in_specs.append(tile)
        ins.append(extra)
    n_extra = len(ins) - 2
    if epi == "relu2":
        out_shape = (SDS((M, N), F32), SDS((M, N), BF16))
        out_specs = (tile, tile)
    elif shard_major:
        per = (N // N_CHIPS) // tn
        assert per * tn * N_CHIPS == N
        out_shape = (SDS((N_CHIPS, M, N // N_CHIPS), out_dtype),)
        out_specs = (pl.BlockSpec((None, tm, tn), lambda i, j, k: (j // per, i, j % per)),)
    else:
        out_shape = (SDS((M, N), out_dtype),)
        out_specs = (tile,)
    n_out = len(out_shape)

    def body(*refs):
        a_ref, b_ref = refs[0], refs[1]
        x_ref = refs[2] if n_extra else None
        pos = 2 + n_extra
        o_refs = refs[pos:pos + n_out]
        acc_ref = refs[pos + n_out] if nk > 1 else None

        def prod():
            av, bv = a_ref[...], b_ref[...]
            if av.dtype != BF16:
                av = av.astype(BF16)
            if bv.dtype != BF16:
                bv = bv.astype(BF16)
            return lax.dot_general(av, bv, dims, preferred_element_type=F32)

        def finish(acc):
            if epi == "relu2":
                o_refs[0][...] = acc
                r = jnp.maximum(acc, 0.0)
                o_refs[1][...] = (r * r).astype(BF16)
            elif epi == "drelu2":
                o_refs[0][...] = (acc * (2.0 * jnp.maximum(x_ref[...], 0.0))).astype(out_dtype)
            elif epi == "add":
                o_refs[0][...] = (acc + scale * x_ref[...]).astype(out_dtype)
            elif epi == "rope":
                j = pl.program_id(1)
                qk_tiles = 2 * RET_HEADS * RET_DK // tn

                @pl.when(j < qk_tiles)
                def _():
                    c, s = refs[2][...], refs[3][...]
                    sc = jnp.where(j >= qk_tiles // 2, RET_DK ** -0.5, 1.0)
                    for hh in range(tn // RET_DK):
                        lo = slice(hh * RET_DK, hh * RET_DK + half)
                        hi = slice(hh * RET_DK + half, (hh + 1) * RET_DK)
                        t1, t2 = acc[:, lo], acc[:, hi]
                        o_refs[0][:, lo] = ((t1 * c - t2 * s) * sc).astype(out_dtype)
                        o_refs[0][:, hi] = ((t1 * s + t2 * c) * sc).astype(out_dtype)

                @pl.when(j >= qk_tiles)
                def _():
                    o_refs[0][...] = acc.astype(out_dtype)
            else:
                o_refs[0][...] = acc.astype(out_dtype)

        if nk == 1:
            finish(prod())
        else:
            k = pl.program_id(2)

            @pl.when(k == 0)
            def _():
                acc_ref[...] = prod()

            @pl.when(k > 0)
            def _():
                acc_ref[...] += prod()

            @pl.when(k == nk - 1)
            def _():
                finish(acc_ref[...])

    res = _pcall(body, ins, name=name, grid=(M // tm, N // tn, nk), in_specs=in_specs, out_specs=out_specs,
                 out_shape=out_shape, scratch_shapes=[pltpu.VMEM((tm, tn), F32)] if nk > 1 else [],
                 sem=("parallel", "parallel", "arbitrary"), sides=sides)
    main, side_res = res if sides else (res, None)
    main = main if n_out > 1 else main[0]
    return (main, side_res) if sides else main


def _ln_stats(z):
    mu = jnp.mean(z, -1, keepdims=True)
    zc = z - mu
    var = jnp.mean(zc * zc, -1, keepdims=True)
    rstd = lax.rsqrt(var + LN_EPS)
    return zc * rstd, rstd


def _ln_fwd(xin, sub, g, b, name):
    S, Dm = xin.shape
    row = pl.BlockSpec((ROWS, Dm), lambda t: (t, 0))
    vec = pl.BlockSpec((1, Dm), lambda t: (0, 0))

    def body(x_ref, s_ref, g_ref, b_ref, o_ref, ob_ref, z_ref):
        z = ALPHA * x_ref[...] + s_ref[...]
        xh, _ = _ln_stats(z)
        o = xh * g_ref[...] + b_ref[...]
        o_ref[...] = o
        ob_ref[...] = o.astype(BF16)
        z_ref[...] = z

    return pl.pallas_call(
        body, name=name, grid=(S // ROWS,), in_specs=[row, row, vec, vec], out_specs=(row, row, row),
        out_shape=(SDS((S, Dm), F32), SDS((S, Dm), BF16), SDS((S, Dm), F32)), compiler_params=_cp("parallel"),
    )(xin, sub, g, b)


def _ln_bwd(dout, z, g, name):
    S, Dm = z.shape
    row = pl.BlockSpec((ROWS, Dm), lambda t: (t, 0))
    vec = pl.BlockSpec((1, Dm), lambda t: (0, 0))

    def body(d_ref, z_ref, g_ref, dz_ref, dzb_ref, dg_ref, db_ref):
        t = pl.program_id(0)
        xh, rstd = _ln_stats(z_ref[...])
        d = d_ref[...]
        dxh = d * g_ref[...]
        m1 = jnp.mean(dxh, -1, keepdims=True)
        m2 = jnp.mean(dxh * xh, -1, keepdims=True)
        dz = rstd * (dxh - m1 - xh * m2)
        dz_ref[...] = dz
        dzb_ref[...] = dz.astype(BF16)
        pg = jnp.sum(d * xh, axis=0, keepdims=True)
        pb = jnp.sum(d, axis=0, keepdims=True)

        @pl.when(t == 0)
        def _():
            dg_ref[...] = pg
            db_ref[...] = pb

        @pl.when(t > 0)
        def _():
            dg_ref[...] += pg
            db_ref[...] += pb

    return pl.pallas_call(
        body, name=name, grid=(S // ROWS,), in_specs=[row, row, vec], out_specs=(row, row, vec, vec),
        out_shape=(SDS((S, Dm), F32), SDS((S, Dm), BF16), SDS((1, Dm), F32), SDS((1, Dm), F32)),
        compiler_params=_cp("arbitrary"),
    )(dout, z, g)


def _loss_ln_bwd(z, g, b, tgt, name):
    S, Dm = z.shape
    row = pl.BlockSpec((ROWS, Dm), lambda t: (t, 0))
    vec = pl.BlockSpec((1, Dm), lambda t: (0, 0))
    one = pl.BlockSpec((1, 1), lambda t: (0, 0))

    def body(z_ref, g_ref, b_ref, t_ref, l_ref, dz_ref, dzb_ref, dg_ref, db_ref):
        t = pl.program_id(0)
        xh, rstd = _ln_stats(z_ref[...])
        diff = xh * g_ref[...] + b_ref[...] - t_ref[...]
        part = jnp.sum(jnp.sum(diff * diff, axis=1, keepdims=True), axis=0, keepdims=True) * (0.5 / Dm)
        d = diff * (1.0 / Dm)
        dxh = d * g_ref[...]
        m1 = jnp.mean(dxh, -1, keepdims=True)
        m2 = jnp.mean(dxh * xh, -1, keepdims=True)
        dz = rstd * (dxh - m1 - xh * m2)
        dz_ref[...] = dz
        dzb_ref[...] = dz.astype(BF16)
        pg = jnp.sum(d * xh, axis=0, keepdims=True)
        pb = jnp.sum(d, axis=0, keepdims=True)

        @pl.when(t == 0)
        def _():
            l_ref[...] = part
            dg_ref[...] = pg
            db_ref[...] = pb

        @pl.when(t > 0)
        def _():
            l_ref[...] += part
            dg_ref[...] += pg
            db_ref[...] += pb

    return pl.pallas_call(
        body, name=name, grid=(S // ROWS,), in_specs=[row, vec, vec, row], out_specs=(one, row, row, vec, vec),
        out_shape=(SDS((1, 1), F32), SDS((S, Dm), F32), SDS((S, Dm), BF16), SDS((1, Dm), F32), SDS((1, Dm), F32)),
        compiler_params=_cp("arbitrary"),
    )(z, g, b, tgt)


def _ret_consts():
    h = np.arange(RET_HEADS, dtype=np.float64)
    lg = np.log1p(-np.exp2(-5.0 - h))
    return jnp.asarray(np.concatenate([lg, np.exp(lg * RT)]).astype(np.float32))


def _rope_tables(S):
    half = RET_DK // 2
    inv = 10000.0 ** (-jnp.arange(half, dtype=F32) / half)
    ang = jnp.arange(S).astype(F32)[:, None] * inv[None, :]
    return jnp.cos(ang), jnp.sin(ang)


def _ret_masks(lgh):
    ri, ci = _iota((RT, RT), 0), _iota((RT, RT), 1)
    visible = (ci >> 6) <= (ri >> 6)
    m = jnp.where(visible, jnp.exp(lgh * jnp.abs(ri - ci).astype(F32)), 0.0)
    pos = _iota((RT, 1), 0).astype(F32)
    return m, jnp.exp(lgh * (pos + 1.0)), jnp.exp(lgh * (RT - 1.0 - pos))


def _ret_fwd(h0, gn_g, consts, sides=()):
    S = h0.shape[0]
    nt = S // RT

    def body(c_ref, q_ref, k_ref, v_ref, gate_ref, g_ref, y_ref, o_ref, st_ref, s_scr):
        h, t = pl.program_id(0), pl.program_id(1)

        @pl.when(t == 0)
        def _():
            s_scr[...] = jnp.zeros_like(s_scr)

        lgh, cdec = c_ref[h], c_ref[RET_HEADS + h]
        m, dq, dk = _ret_masks(lgh)
        qv, kv, vv = q_ref[...], k_ref[...], v_ref[...]
        p = (_dot_nt(qv, kv) * m).astype(BF16)
        sp = s_scr[...]
        spb = sp.astype(BF16)
        st_ref[...] = spb
        qd = (qv.astype(F32) * dq).astype(BF16)
        kd = (kv.astype(F32) * dk).astype(BF16)
        y = _dot(p, vv) + _dot(qd, spb)
        y_ref[...] = y
        s_scr[...] = sp * cdec + _dot_tn(kd, vv)
        yn, _ = _gn_stats(y)
        gate = gate_ref[...].astype(F32)
        o_ref[...] = (gate * _sigmoid(gate) * (yn * g_ref[...])).astype(BF16)

    qk = lambda off: pl.BlockSpec((RT, RET_DK), lambda h, t: (t, off + h))
    vs = lambda off: pl.BlockSpec((RT, RET_DV), lambda h, t: (t, off + h))
    return _pcall(
        body, (consts, h0, h0, h0, h0, gn_g), name="ret_fwd", grid=(RET_HEADS, nt),
        in_specs=[pl.BlockSpec(memory_space=pltpu.SMEM), qk(0), qk(RET_HEADS), vs(RET_HEADS), vs(2 * RET_HEADS),
                  pl.BlockSpec((1, RET_DV), lambda h, t: (0, h))],
        out_specs=(vs(0), vs(0), pl.BlockSpec((None, None, RET_DK, RET_DV), lambda h, t: (h, t, 0, 0))),
        out_shape=(SDS((S, 4096), F32), SDS((S, 4096), BF16), SDS((RET_HEADS, nt, RET_DK, RET_DV), BF16)),
        scratch_shapes=[pltpu.VMEM((RET_DK, RET_DV), F32)], sem=("parallel", "arbitrary"), sides=sides)


def _ret_bwd(h0, do, y, gn_g, states, consts, cos, sin, sides=()):
    S = h0.shape[0]
    nt = S // RT
    half = RET_DK // 2

    def body(c_ref, q_ref, k_ref, v_ref, gate_ref, do_ref, y_ref, g_ref, st_ref, cos_ref, sin_ref,
             dq_ref, dk_ref, dv_ref, dgate_ref, dg_ref, ds_scr):
        h, t = pl.program_id(0), pl.program_id(1)

        @pl.when(t == 0)
        def _():
            ds_scr[...] = jnp.zeros_like(ds_scr)
            dg_ref[...] = jnp.zeros_like(dg_ref)

        yn, rstd = _gn_stats(y_ref[...])
        gate, g, dov = gate_ref[...].astype(F32), g_ref[...], do_ref[...]
        sg = _sigmoid(gate)
        dgate_ref[...] = (dov * (yn * g) * (sg * (1.0 + gate * (1.0 - sg)))).astype(BF16)
        dyg = dov * (gate * sg)
        dyn = dyg * g
        m1 = jnp.mean(dyn, -1, keepdims=True)
        m2 = jnp.mean(dyn * yn, -1, keepdims=True)
        dyv = (rstd * (dyn - m1 - yn * m2)).astype(BF16)
        dg_ref[...] += jnp.sum(dyg * yn, axis=0, keepdims=True)

        lgh, cdec = c_ref[h], c_ref[RET_HEADS + h]
        m, dqc, dkc = _ret_masks(lgh)
        qv, kv, vv, spb = q_ref[...], k_ref[...], v_ref[...], st_ref[...]
        p = (_dot_nt(qv, kv) * m).astype(BF16)
        qd = (qv.astype(F32) * dqc).astype(BF16)
        kd = (kv.astype(F32) * dkc).astype(BF16)
        dsn = ds_scr[...]
        dsb = dsn.astype(BF16)
        dsc = (_dot_nt(dyv, vv) * m).astype(BF16)
        dq = _dot(dsc, kv) + _dot_nt(dyv, spb) * dqc
        dk = _dot_tn(dsc, qv) + _dot_nt(vv, dsb) * dkc
        dv_ref[...] = (_dot_tn(p, dyv) + _dot(kd, dsb)).astype(BF16)
        ds_scr[...] = dsn * cdec + _dot_tn(qd, dyv)
        c, s = cos_ref[...], sin_ref[...]

        def unrot(d):
            d1, d2 = d[:, :half], d[:, half:]
            return jnp.concatenate([d1 * c + d2 * s, d2 * c - d1 * s], axis=-1)

        dq_ref[...] = unrot(dq).astype(BF16)
        dk_ref[...] = (unrot(dk) * (RET_DK ** -0.5)).astype(BF16)

    rev = lambda t: nt - 1 - t
    qkb = lambda off: pl.BlockSpec((RT, RET_DK), lambda h, t: (rev(t), off + h))
    vsb = lambda off: pl.BlockSpec((RT, RET_DV), lambda h, t: (rev(t), off + h))
    qk, vs = qkb(0), vsb(0)
    tab = pl.BlockSpec((RT, half), lambda h, t: (rev(t), 0))
    vec = pl.BlockSpec((1, RET_DV), lambda h, t: (0, h))
    return _pcall(
        body, (consts, h0, h0, h0, h0, do, y, gn_g, states, cos, sin), name="ret_bwd", grid=(RET_HEADS, nt),
        in_specs=[pl.BlockSpec(memory_space=pltpu.SMEM), qk, qkb(RET_HEADS), vsb(RET_HEADS), vsb(2 * RET_HEADS), vs, vs,
                  vec, pl.BlockSpec((None, None, RET_DK, RET_DV), lambda h, t: (h, rev(t), 0, 0)), tab, tab],
        out_specs=(qk, qk, vs, vs, vec),
        out_shape=(SDS((S, 2048), BF16), SDS((S, 2048), BF16), SDS((S, 4096), BF16), SDS((S, 4096), BF16),
                   SDS((1, 4096), F32)),
        scratch_shapes=[pltpu.VMEM((RET_DK, RET_DV), F32)], sem=("parallel", "arbitrary"), sides=sides)


def _gn_stats(y):
    mu = jnp.mean(y, -1, keepdims=True)
    yc = y - mu
    var = jnp.mean(yc * yc, -1, keepdims=True)
    rstd = lax.rsqrt(var + GN_EPS)
    return yc * rstd, rstd


CONV_RB, CONV_CB = 32, 512


def _conv_tiles(rows, cols):
    return [(r0, slice(c0, c0 + CONV_CB)) for r0 in range(0, rows, CONV_RB) for c0 in range(0, cols, CONV_CB)]


def _conv_fill(x_ref, halo_ref, ext_scr, t):
    ext_scr[0:8, :] = jnp.where(t == 0, 0.0, halo_ref[...])
    ext_scr[8:, :] = x_ref[...]


def _conv_tile(ext_scr, w, r0, cs):
    acc = w[3:4, cs] * ext_scr[pl.ds(8 + r0, CONV_RB), cs]
    for j in range(3):
        acc = acc + w[j:j + 1, cs] * ext_scr[pl.ds(5 + j + r0, CONV_RB), cs]
    return acc


def _gdn_conv_fwd(h1, conv_w, kind):
    S = h1.shape[0]
    base = {"q": 0, "k": 1, "v": 2}[kind]
    ncb = 2 if kind == "v" else 1
    C = 2048

    def body(x_ref, halo_ref, w_ref, o_ref, ext_scr):
        _conv_fill(x_ref, halo_ref, ext_scr, pl.program_id(0))
        w = w_ref[...]
        scale = GDN_D ** -0.5 if kind == "q" else 1.0
        for r0, cs in _conv_tiles(ROWS, C):
            rs = pl.ds(r0, CONV_RB)
            acc = _conv_tile(ext_scr, w, r0, cs)
            c = acc * _sigmoid(acc)
            if kind == "v":
                o_ref[rs, cs] = c.astype(BF16)
            else:
                for hh in range(CONV_CB // GDN_D):
                    ch = c[:, hh * GDN_D:(hh + 1) * GDN_D]
                    r = lax.rsqrt(jnp.sum(ch * ch, -1, keepdims=True) + L2_EPS)
                    o_ref[rs, pl.ds(cs.start + hh * GDN_D, GDN_D)] = (ch * (r * scale)).astype(BF16)

    hb = ROWS // 8
    return pl.pallas_call(
        body, name="gdn_conv_fwd_" + kind, grid=(S // ROWS, ncb),
        in_specs=[pl.BlockSpec((ROWS, C), lambda t, j: (t, base + j)),
                  pl.BlockSpec((8, C), lambda t, j: (jnp.maximum(t * hb - 1, 0), base + j)),
                  pl.BlockSpec((4, C), lambda t, j: (0, base + j))],
        out_specs=pl.BlockSpec((ROWS, C), lambda t, j: (t, j)), out_shape=SDS((S, C * ncb), BF16),
        scratch_shapes=[pltpu.VMEM((ROWS + 8, C), F32)], compiler_params=_cp("parallel", "parallel"),
    )(h1, h1, conv_w)


def _gdn_conv_bwd_act(h1, conv_w, dn, kind, buf=None):
    S = h1.shape[0]
    base = {"q": 0, "k": 1, "v": 2}[kind]
    ncb = 2 if kind == "v" else 1
    C = 2048

    def body(x_ref, halo_ref, w_ref, dn_ref, *rest):
        o_ref, ext_scr = rest[-2], rest[-1]
        _conv_fill(x_ref, halo_ref, ext_scr, pl.program_id(0))
        w = w_ref[...]
        scale = GDN_D ** -0.5 if kind == "q" else 1.0
        for r0, cs in _conv_tiles(ROWS, C):
            rs = pl.ds(r0, CONV_RB)
            acc = _conv_tile(ext_scr, w, r0, cs)
            sg = _sigmoid(acc)
            dsilu = sg * (1.0 + acc * (1.0 - sg))
            if kind == "v":
                o_ref[rs, cs] = dn_ref[rs, cs] * dsilu
            else:
                c = acc * sg
                for hh in range(CONV_CB // GDN_D):
                    sl = slice(hh * GDN_D, (hh + 1) * GDN_D)
                    gl = pl.ds(cs.start + hh * GDN_D, GDN_D)
                    ch, dnh = c[:, sl], dn_ref[rs, gl]
                    r = lax.rsqrt(jnp.sum(ch * ch, -1, keepdims=True) + L2_EPS)
                    proj = jnp.sum(dnh * ch, -1, keepdims=True)
                    o_ref[rs, gl] = (scale * r) * (dnh - ch * (proj * r * r)) * dsilu[:, sl]

    hb = ROWS // 8
    return pl.pallas_call(
        body, name="gdn_conv_bwd_act_" + kind, grid=(S // ROWS, ncb),
        in_specs=[pl.BlockSpec((ROWS, C), lambda t, j: (t, base + j)),
                  pl.BlockSpec((8, C), lambda t, j: (jnp.maximum(t * hb - 1, 0), base + j)),
                  pl.BlockSpec((4, C), lambda t, j: (0, base + j)),
                  pl.BlockSpec((ROWS, C), lambda t, j: (t, j))] + ([] if buf is None else [HBM_SPEC]),
        out_specs=pl.BlockSpec((ROWS, C), lambda t, j: (t, base + j)), out_shape=SDS((S, GDN_QKV), F32),
        input_output_aliases={} if buf is None else {4: 0},
        scratch_shapes=[pltpu.VMEM((ROWS + 8, C), F32)], compiler_params=_cp("parallel", "parallel"),
    )(*((h1, h1, conv_w, dn) + (() if buf is None else (buf,))))


def _gdn_conv_bwd_in(h1, conv_w, dacc, dh1_buf):
    S = h1.shape[0]
    C = 2048
    nt = S // ROWS
    hb = ROWS // 8

    def body(x_ref, halo_ref, w_ref, d_ref, dhalo_ref, buf_ref, di_ref, dw_ref, ext_scr, dext_scr):
        t = pl.program_id(1)
        _conv_fill(x_ref, halo_ref, ext_scr, t)
        dext_scr[0:ROWS, :] = d_ref[...]
        dext_scr[ROWS:, :] = jnp.where(t == nt - 1, 0.0, dhalo_ref[...])
        w = w_ref[...]

        @pl.when(t == 0)
        def _():
            dw_ref[...] = jnp.zeros_like(dw_ref)

        for c0 in range(0, C, CONV_CB):
            cs = slice(c0, c0 + CONV_CB)
            pw = [jnp.zeros((1, CONV_CB), F32) for _ in range(4)]
            for r0 in range(0, ROWS, CONV_RB):
                d = dext_scr[pl.ds(r0, CONV_RB), cs]
                di = w[3:4, cs] * d
                for j in range(3):
                    di = di + w[j:j + 1, cs] * dext_scr[pl.ds(3 - j + r0, CONV_RB), cs]
                di_ref[pl.ds(r0, CONV_RB), cs] = di.astype(BF16)
                for j in range(4):
                    pw[j] = pw[j] + jnp.sum(d * ext_scr[pl.ds(5 + j + r0, CONV_RB), cs], axis=0, keepdims=True)
            dw_ref[:, cs] += jnp.concatenate(pw, axis=0)

    return pl.pallas_call(
        body, name="gdn_conv_bwd_in", grid=(GDN_QKV // C, nt),
        in_specs=[pl.BlockSpec((ROWS, C), lambda j, t: (t, j)),
                  pl.BlockSpec((8, C), lambda j, t: (jnp.maximum(t * hb - 1, 0), j)),
                  pl.BlockSpec((4, C), lambda j, t: (0, j)),
                  pl.BlockSpec((ROWS, C), lambda j, t: (t, j)),
                  pl.BlockSpec((8, C), lambda j, t: (jnp.minimum((t + 1) * hb, nt * hb - 1), j)), HBM_SPEC],
        out_specs=(pl.BlockSpec((ROWS, C), lambda j, t: (t, j)), pl.BlockSpec((4, C), lambda j, t: (0, j))),
        out_shape=(SDS(dh1_buf.shape, BF16), SDS((4, GDN_QKV), F32)), input_output_aliases={5: 0},
        scratch_shapes=[pltpu.VMEM((ROWS + 8, C), F32), pltpu.VMEM((ROWS + 8, C), F32)],
        compiler_params=_cp("parallel", "arbitrary"),
    )(h1, h1, conv_w, dacc, dacc, dh1_buf)


def _chunk_masks():
    ri, ci = _iota((CT, CT), 0), _iota((CT, CT), 1)
    same = (ri >> 6) == (ci >> 6)
    return same, same & (ri >= ci), same & (ri > ci), same & (ri <= ci), ri == ci


def _fold_dup(m):
    h = m[:, :LANES] + m[:, LANES:]
    return h + pltpu.roll(h, CHUNK, axis=1)


def _unfold_bd(d, same):
    return jnp.where(same, jnp.concatenate([d, d], axis=1), 0.0)


def _softplus(x):
    return jnp.maximum(x, 0.0) + jnp.log(1.0 + jnp.exp(-jnp.abs(x)))


def _gdn_scal_fwd(ba, a_log, dt_bias):
    S = ba.shape[0]

    def body(ba_ref, al_ref, dt_ref, beta_ref, g_ref, gc_ref):
        bav = ba_ref[...]
        beta_ref[...] = _sigmoid(bav)
        a = pltpu.roll(bav, LANES - GDN_HV, axis=1)
        g = -jnp.exp(al_ref[...]) * _softplus(a + dt_ref[...])
        g_ref[...] = g
        causal = _chunk_masks()[1]
        gc_ref[...] = _dot_exact_l(causal.astype(BF16), g)

    row = pl.BlockSpec((CT, LANES), lambda t: (t, 0))
    vec = pl.BlockSpec((1, LANES), lambda t: (0, 0))
    return pl.pallas_call(
        body, name="gdn_scal_fwd", grid=(S // CT,), in_specs=[row, vec, vec], out_specs=(row, row, row),
        out_shape=(SDS((S, LANES), F32),) * 3, compiler_params=_cp("parallel"),
    )(ba, a_log, dt_bias)


def _gdn_scal_bwd(ba, a_log, dt_bias, g, dbeta, dgc):
    S = ba.shape[0]

    def body(ba_ref, al_ref, dt_ref, g_ref, dbeta_ref, dgc_ref, dba_ref, dal_ref, ddt_ref):
        t = pl.program_id(0)
        bav = ba_ref[...]
        beta = _sigmoid(bav)
        db = dbeta_ref[...] * beta * (1.0 - beta)
        a = pltpu.roll(bav, LANES - GDN_HV, axis=1)
        dgv = _dot_exact_l(_chunk_masks()[3].astype(BF16), dgc_ref[...])
        da = dgv * (-jnp.exp(al_ref[...])) * _sigmoid(a + dt_ref[...])
        lane = _iota(bav.shape, 1)
        da_sh = pltpu.roll(da, GDN_HV, axis=1)
        dba = jnp.where(lane < GDN_HV, db, jnp.where(lane < 2 * GDN_HV, da_sh, 0.0))
        dba_ref[...] = dba.astype(BF16)
        keep = lane < GDN_HV
        pal = jnp.sum(jnp.where(keep, dgv * g_ref[...], 0.0), axis=0, keepdims=True)
        pdt = jnp.sum(jnp.where(keep, da, 0.0), axis=0, keepdims=True)

        @pl.when(t == 0)
        def _():
            dal_ref[...] = pal
            ddt_ref[...] = pdt

        @pl.when(t > 0)
        def _():
            dal_ref[...] += pal
            ddt_ref[...] += pdt

    row = pl.BlockSpec((CT, LANES), lambda t: (t, 0))
    vec = pl.BlockSpec((1, LANES), lambda t: (0, 0))
    return pl.pallas_call(
        body, name="gdn_scal_bwd", grid=(S // CT,), in_specs=[row, vec, vec, row, row, row],
        out_specs=(row, vec, vec), out_shape=(SDS((S, LANES), BF16), SDS((1, LANES), F32), SDS((1, LANES), F32)),
        compiler_params=_cp("arbitrary"),
    )(ba, a_log, dt_bias, g, dbeta, dgc)


def _sel_col(x, h):
    return jnp.sum(jnp.where(_iota(x.shape, 1) == h, x, 0.0), axis=1, keepdims=True)


def _decay(gcol, causal):
    gm = jnp.broadcast_to(gcol, (CT, CT))
    diff = gm - gm.T
    return jnp.where(causal, jnp.exp(jnp.where(causal, diff, 0.0)), 0.0)


def _gdn_chunk_fwd(k, v, beta, gc):
    S = k.shape[0]
    nt = S // CT

    def body(k_ref, v_ref, beta_ref, gc_ref, t_ref, u_ref, w_ref):
        kg_id = pl.program_id(1)
        same, causal, strict, _, _ = _chunk_masks()
        eye_dup = jnp.where((_iota((CT, LANES), 0) & (CHUNK - 1)) == (_iota((CT, LANES), 1) & (CHUNK - 1)), 1.0, 0.0)
        kfs, xds, pds, cols = [], [], [], []
        for kb in range(KB):
            kv = k_ref[:, kb * GDN_D:(kb + 1) * GDN_D]
            kfs.append(kv.astype(F32))
            kk = _dot_nt(kv, kv)
            for hp in range(HP):
                h = NH * kg_id + kb * HP + hp
                bcol, gcol = _sel_col(beta_ref[...], h), _sel_col(gc_ref[...], h)
                x = jnp.where(strict, -(kk * bcol) * _decay(gcol, causal), 0.0)
                xds.append(_fold_dup(x))
                pds.append(eye_dup)
                cols.append((bcol, gcol))
        for m in range(6):
            for hi in range(NH):
                xh, xl = _split2(xds[hi])
                ph, pl_ = _split2(pds[hi])
                lh, ll = _unfold_bd(xh, same), _unfold_bd(xl, same)
                rh, rl = jnp.concatenate([xh, ph], axis=1), jnp.concatenate([xl, pl_], axis=1)
                out = _dot(lh, rh) + (_dot(lh, rl) + _dot(ll, rh))
                pds[hi] = pds[hi] + out[:, LANES:]
                if m < 5:
                    xds[hi] = out[:, :LANES]
        for hi in range(NH):
            bcol, gcol = cols[hi]
            cs = slice(hi * GDN_D, (hi + 1) * GDN_D)
            t_ref[hi] = pds[hi]
            tb = _unfold_bd(pds[hi], same).astype(BF16)
            vb = (v_ref[:, cs].astype(F32) * bcol).astype(BF16)
            kg = (kfs[hi // HP] * (bcol * jnp.exp(gcol))).astype(BF16)
            uw = _dot(tb, jnp.concatenate([vb, kg], axis=1))
            u_ref[:, cs] = uw[:, :GDN_D]
            w_ref[:, cs] = uw[:, GDN_D:].astype(BF16)

    col = pl.BlockSpec((CT, LANES), lambda t, kg: (t, 0))
    hv = pl.BlockSpec((CT, NH * GDN_D), lambda t, kg: (t, kg))
    return pl.pallas_call(
        body, name="gdn_chunk_fwd", grid=(nt, GDN_HV // NH),
        in_specs=[pl.BlockSpec((CT, KB * GDN_D), lambda t, kg: (t, kg)), hv, col, col],
        out_specs=(pl.BlockSpec((NH, None, CT, LANES), lambda t, kg: (kg, t, 0, 0)), hv, hv),
        out_shape=(SDS((GDN_HV, nt, CT, LANES), F32), SDS((S, 4096), F32), SDS((S, 4096), BF16)),
        compiler_params=_cp("parallel", "parallel"),
    )(k, v, beta, gc)


def _last_of_chunk(gcol, rows, c):
    return jnp.sum(jnp.where(rows == c * CHUNK + CHUNK - 1, gcol, 0.0), axis=0, keepdims=True)


def _gdn_scan_fwd(q, k, u, w, gc):
    S = q.shape[0]
    nt = S // CT
    ncs = CT // CHUNK

    def body(q_ref, k_ref, u_ref, w_ref, gc_ref, y_ref, vn_ref, st_ref, s_scr, vn_scr):
        kg_id, t = pl.program_id(0), pl.program_id(1)

        @pl.when(t == 0)
        def _():
            s_scr[...] = jnp.zeros_like(s_scr)

        causal = _chunk_masks()[1]
        rows = _iota((CT, 1), 0)
        heads = []
        for kb in range(KB):
            ks = slice(kb * GDN_D, (kb + 1) * GDN_D)
            qv, kv = q_ref[:, ks], k_ref[:, ks]
            qf, kf = qv.astype(F32), kv.astype(F32)
            qk = _dot_nt(qv, kv)
            for hp in range(HP):
                gcol = _sel_col(gc_ref[...], NH * kg_id + kb * HP + hp)
                heads.append((gcol, (qk * _decay(gcol, causal)).astype(BF16), (qf * jnp.exp(gcol)).astype(BF16), kf))
        vn_scr[...] = jnp.zeros_like(vn_scr)
        for c in range(ncs):
            r = slice(c * CHUNK, (c + 1) * CHUNK)
            for hi in range(NH):
                gcol, attn, qd, kf = heads[hi]
                cs = slice(hi * GDN_D, (hi + 1) * GDN_D)
                s = s_scr[hi]
                st_ref[hi, c] = s
                sb = s.astype(BF16)
                gl = _last_of_chunk(gcol, rows, c)
                kd = (kf[r] * jnp.exp(gl - gcol[r])).astype(BF16)
                vn = (u_ref[r, cs] - _dot(w_ref[r, cs], sb)).astype(BF16)
                vn_scr[r, cs] = vn
                y_ref[r, cs] = _dot(qd[r], sb) + _dot(attn[r], vn_scr[:, cs])
                s_scr[hi] = s * jnp.exp(gl) + _dot_tn(kd, vn)
        vn_ref[...] = vn_scr[...]

    hk = pl.BlockSpec((CT, KB * GDN_D), lambda kg, t: (t, kg))
    hv = pl.BlockSpec((CT, NH * GDN_D), lambda kg, t: (t, kg))
    col = pl.BlockSpec((CT, LANES), lambda kg, t: (t, 0))
    return pl.pallas_call(
        body, name="gdn_scan_fwd", grid=(GDN_HV // NH, nt), in_specs=[hk, hk, hv, hv, col],
        out_specs=(hv, hv, pl.BlockSpec((NH, ncs, GDN_D, GDN_D), lambda kg, t: (kg, t, 0, 0))),
        out_shape=(SDS((S, 4096), F32), SDS((S, 4096), BF16), SDS((GDN_HV, S // CHUNK, GDN_D, GDN_D), F32)),
        scratch_shapes=[pltpu.VMEM((NH, GDN_D, GDN_D), F32), pltpu.VMEM((CT, NH * GDN_D), BF16)],
        compiler_params=_cp("parallel", "arbitrary"),
    )(q, k, u, w, gc)


def _gdn_scan_bwd(q, k, w, vn, gc, states, dy, sides=()):
    S = q.shape[0]
    nt = S // CT
    ncs = CT // CHUNK

    def body(q_ref, k_ref, w_ref, vn_ref, gc_ref, st_ref, dy_ref, du_ref, dw_ref, dq_ref, dk_ref, dgc_ref, ds_scr):
        kg_id, t = pl.program_id(0), pl.program_id(1)

        @pl.when(t == 0)
        def _():
            ds_scr[...] = jnp.zeros_like(ds_scr)

        _, causal, _, _, eye = _chunk_masks()
        rows = _iota((CT, 1), 0)
        heads = []
        for kb in range(KB):
            ks = slice(kb * GDN_D, (kb + 1) * GDN_D)
            qv, kv = q_ref[:, ks], k_ref[:, ks]
            qf, kf = qv.astype(F32), kv.astype(F32)
            qk = _dot_nt(qv, kv)
            for hp in range(HP):
                hi = kb * HP + hp
                cs = slice(hi * GDN_D, (hi + 1) * GDN_D)
                gcol = _sel_col(gc_ref[...], NH * kg_id + hi)
                dm = _decay(gcol, causal)
                attn_f = qk * dm
                egc = jnp.exp(gcol)
                qd_f = qf * egc
                dyv, vnv = dy_ref[:, cs], vn_ref[:, cs]
                heads.append(dict(cs=cs, ks=ks, gcol=gcol, dm=dm, attn_f=attn_f, egc=egc, qd_f=qd_f, kf=kf, qv=qv, kv=kv,
                                  qd=qd_f.astype(BF16), dy=dyv, vn=vnv, dattn=_dot_nt(dyv, vnv),
                                  at_dy=_dot_tn(attn_f.astype(BF16), dyv), dgc=[None] * ncs))
        dq_ref[...] = jnp.zeros_like(dq_ref)
        dk_ref[...] = jnp.zeros_like(dk_ref)
        for c in reversed(range(ncs)):
            r = slice(c * CHUNK, (c + 1) * CHUNK)
            for hi in range(NH):
                hd = heads[hi]
                cs, ks, gcol = hd["cs"], hd["ks"], hd["gcol"]
                s = st_ref[hi, c]
                sb = s.astype(BF16)
                dsn = ds_scr[hi]
                dsb = dsn.astype(BF16)
                gl = _last_of_chunk(gcol, rows, c)
                cd = jnp.exp(gl)
                ekd = jnp.exp(gl - gcol[r])
                kd_f = hd["kf"][r] * ekd
                dvn = (hd["at_dy"][r] + _dot(kd_f.astype(BF16), dsb)).astype(BF16)
                dqd = _dot_nt(hd["dy"][r], sb)
                dkd = _dot_nt(hd["vn"][r], dsb)
                dcd = jnp.sum(jnp.sum(s * dsn, axis=1, keepdims=True), axis=0, keepdims=True)
                ds_scr[hi] = dsn * cd + _dot_tn(hd["qd"][r], hd["dy"][r]) - _dot_tn(w_ref[r, cs], dvn)
                du_ref[r, cs] = dvn
                dw_ref[r, cs] = (-_dot_nt(dvn, sb)).astype(BF16)
                dq_ref[r, ks] += dqd * hd["egc"][r]
                dk_ref[r, ks] += dkd * ekd
                rs_q = jnp.sum(dqd * hd["qd_f"][r], axis=1, keepdims=True)
                rs_k = jnp.sum(dkd * kd_f, axis=1, keepdims=True)
                tot = jnp.sum(rs_k, axis=0, keepdims=True) + dcd * cd
                hd["dgc"][c] = rs_q - rs_k + jnp.where(rows[r] == c * CHUNK + CHUNK - 1, tot, 0.0)
        for hi in range(NH):
            hd = heads[hi]
            ks = hd["ks"]
            dab = (hd["dattn"] * hd["dm"]).astype(BF16)
            dq_ref[:, ks] += _dot(dab, hd["kv"])
            dk_ref[:, ks] += _dot_tn(dab, hd["qv"])
            e1 = hd["dattn"] * hd["attn_f"]
            dgc = (jnp.concatenate(hd["dgc"], axis=0) + jnp.sum(e1, axis=1, keepdims=True)
                   - jnp.sum(e1.T, axis=1, keepdims=True))
            dgc_ref[hi] = jnp.sum(jnp.where(eye, jnp.broadcast_to(dgc, (CT, CT)), 0.0), axis=0, keepdims=True)

    rev = lambda t: nt - 1 - t
    hk = pl.BlockSpec((CT, KB * GDN_D), lambda kg, t: (rev(t), kg))
    hv = pl.BlockSpec((CT, NH * GDN_D), lambda kg, t: (rev(t), kg))
    col = pl.BlockSpec((CT, LANES), lambda kg, t: (rev(t), 0))
    return _pcall(
        body, (q, k, w, vn, gc, states, dy), name="gdn_scan_bwd", grid=(GDN_HV // NH, nt),
        in_specs=[hk, hk, hv, hv, col, pl.BlockSpec((NH, ncs, GDN_D, GDN_D), lambda kg, t: (kg, rev(t), 0, 0)), hv],
        out_specs=(hv, hv, hk, hk, pl.BlockSpec((NH, 1, CT), lambda kg, t: (kg, 0, rev(t)))),
        out_shape=(SDS((S, 4096), BF16), SDS((S, 4096), BF16), SDS((S, 2048), F32), SDS((S, 2048), F32),
                   SDS((GDN_HV, 1, S), F32)),
        scratch_shapes=[pltpu.VMEM((NH, GDN_D, GDN_D), F32)], sem=("parallel", "arbitrary"), sides=sides)


def _gdn_chunk_bwd(k, v, beta, gc, tmat, du, dw, dk_p, dgc_p, sides=()):
    S = k.shape[0]
    nt = S // CT

    def body(k_ref, v_ref, beta_ref, gc_ref, t_ref, du_ref, dw_ref, dkp_ref, dgcp_ref,
             dk_ref, dv_ref, dbeta_ref, dgc_ref):
        kg_id = pl.program_id(1)
        same, causal, strict, _, eye = _chunk_masks()
        lane = _iota((CT, LANES), 1)

        @pl.when(kg_id == 0)
        def _():
            dbeta_ref[...] = jnp.zeros_like(dbeta_ref)
            dgc_ref[...] = jnp.zeros_like(dgc_ref)

        for kb_i in range(KB):
            ks = slice(kb_i * GDN_D, (kb_i + 1) * GDN_D)
            kv = k_ref[:, ks]
            kf = kv.astype(F32)
            kk = _dot_nt(kv, kv)
            dk = dkp_ref[:, ks]
            for hp in range(HP):
                hi = kb_i * HP + hp
                h = NH * kg_id + hi
                cs = slice(hi * GDN_D, (hi + 1) * GDN_D)
                bcol, gcol = _sel_col(beta_ref[...], h), _sel_col(gc_ref[...], h)
                dm = _decay(gcol, causal)
                vf = v_ref[:, cs].astype(F32)
                kb = kf * bcol
                a = jnp.where(strict, (kk * bcol) * dm, 0.0)
                egc = jnp.exp(gcol)
                kg_f = kb * egc
                tb = _unfold_bd(t_ref[hi], same).astype(BF16)
                duw = jnp.concatenate([du_ref[:, cs], dw_ref[:, cs]], axis=1)
                dt = _dot_nt(duw, jnp.concatenate([(vf * bcol).astype(BF16), kg_f.astype(BF16)], axis=1))
                dvb_dkg = _dot_tn(tb, duw)
                dvb, dkg = dvb_dkg[:, :GDN_D], dvb_dkg[:, GDN_D:]
                da = -_dot_nt(_dot_tn(tb, dt.astype(BF16)).astype(BF16), tb)
                rm = jnp.where(strict, da, 0.0)
                rdb = (rm * dm).astype(BF16)
                dkb = _dot(rdb, kv) + dkg * egc
                dk = dk + _dot_tn(rdb, kb.astype(BF16)) + dkb * bcol
                e2 = rm * a
                dgc_in = jnp.sum(jnp.where(eye, jnp.broadcast_to(dgcp_ref[hi], (CT, CT)), 0.0), axis=1, keepdims=True)
                dgc = (jnp.sum(e2, axis=1, keepdims=True) - jnp.sum(e2.T, axis=1, keepdims=True)
                       + jnp.sum(dkg * kg_f, axis=1, keepdims=True) + dgc_in)
                dbeta = jnp.sum(dkb * kf, axis=1, keepdims=True) + jnp.sum(dvb * vf, axis=1, keepdims=True)
                dv_ref[:, cs] = dvb * bcol
                dbeta_ref[...] += jnp.where(lane == h, dbeta, 0.0)
                dgc_ref[...] += jnp.where(lane == h, dgc, 0.0)
            dk_ref[:, ks] = dk

    hk = pl.BlockSpec((CT, KB * GDN_D), lambda t, kg: (t, kg))
    hv = pl.BlockSpec((CT, NH * GDN_D), lambda t, kg: (t, kg))
    col = pl.BlockSpec((CT, LANES), lambda t, kg: (t, 0))
    return _pcall(
        body, (k, v, beta, gc, tmat, du, dw, dk_p, dgc_p), name="gdn_chunk_bwd", grid=(nt, GDN_HV // NH),
        in_specs=[hk, hv, col, col, pl.BlockSpec((NH, None, CT, LANES), lambda t, kg: (kg, t, 0, 0)), hv, hv, hk,
                  pl.BlockSpec((NH, 1, CT), lambda t, kg: (kg, 0, t))],
        out_specs=(hk, hv, col, col),
        out_shape=(SDS((S, 2048), F32), SDS((S, 4096), F32), SDS((S, LANES), F32), SDS((S, LANES), F32)),
        sem=("parallel", "arbitrary"), sides=sides)


def _gdn_post_fwd(y, h1, norm_g):
    S = y.shape[0]

    def body(y_ref, z_ref, g_ref, o_ref):
        g = g_ref[...]
        for hh in range(GDN_HV):
            sl = slice(hh * GDN_D, (hh + 1) * GDN_D)
            yh, zh = y_ref[:, sl], z_ref[:, sl]
            yn = yh * lax.rsqrt(jnp.mean(yh * yh, -1, keepdims=True) + RMS_EPS)
            o_ref[:, sl] = (yn * g * (zh * _sigmoid(zh))).astype(BF16)

    row = lambda off: pl.BlockSpec((ROWS, 4096), lambda t: (t, off))
    return pl.pallas_call(
        body, name="gdn_post_fwd", grid=(S // ROWS,),
        in_specs=[row(0), row(2), pl.BlockSpec((1, GDN_D), lambda t: (0, 0))], out_specs=row(0),
        out_shape=SDS((S, 4096), BF16), compiler_params=_cp("parallel"),
    )(y, h1, norm_g)


def _gdn_post_bwd(do, y, h1, norm_g, sides=()):
    S = y.shape[0]

    def body(do_ref, y_ref, z_ref, g_ref, dy_ref, dz_ref, dg_ref):
        t = pl.program_id(0)
        g = g_ref[...]
        pg = jnp.zeros((1, GDN_D), F32)
        for hh in range(GDN_HV):
            sl = slice(hh * GDN_D, (hh + 1) * GDN_D)
            yh, zh, doh = y_ref[:, sl], z_ref[:, sl], do_ref[:, sl]
            rstd = lax.rsqrt(jnp.mean(yh * yh, -1, keepdims=True) + RMS_EPS)
            yn = yh * rstd
            sg = _sigmoid(zh)
            dz_ref[:, sl] = (doh * (yn * g) * (sg * (1.0 + zh * (1.0 - sg)))).astype(BF16)
            dyg = doh * (zh * sg)
            dyn = dyg * g
            dy_ref[:, sl] = (rstd * (dyn - yn * jnp.mean(dyn * yn, -1, keepdims=True))).astype(BF16)
            pg = pg + jnp.sum(dyg * yn, axis=0, keepdims=True)

        @pl.when(t == 0)
        def _():
            dg_ref[...] = pg

        @pl.when(t > 0)
        def _():
            dg_ref[...] += pg

    row = lambda off: pl.BlockSpec((ROWS, 4096), lambda t: (t, off))
    vec = pl.BlockSpec((1, GDN_D), lambda t: (0, 0))
    return _pcall(
        body, (do, y, h1, norm_g), name="gdn_post_bwd", grid=(S // ROWS,), in_specs=[row(0), row(0), row(2), vec],
        out_specs=(row(0), row(2), vec),
        out_shape=(SDS((S, 4096), BF16), SDS((S, 3 * 4096), BF16), SDS((1, GDN_D), F32)),
        sem=("arbitrary",), sides=sides)


def _cols(g):
    return jnp.transpose(g, (1, 0, 2)).reshape(g.shape[1], -1)


def _rows(g):
    return g.reshape(-1, g.shape[-1])


def _local_step(x, tgt, sh, small):
    S = x.shape[0]
    xb = x.astype(BF16)
    rc = _ret_consts()
    cos, sin = _rope_tables(S)

    def gather(names):
        shards = [sh[k] for k in names]
        return shards, _ag_side(shards, [k != "conv_w" for k in names])

    lead, side = gather(["ret_w_in"])
    wri = _cols(_ag_finish(_comm_call("ag_ret_in", side), lead)[0])
    shards, side = gather(["ret_w_out", "mlp_w1_0", "gdn_w_out"])
    h0, (got,) = _mm(xb, wri, "nn", "mm_ret_in", out_dtype=BF16, epi="rope", extra=(cos, sin), sides=[side])
    g_ro, g_w10, g_go = _ag_finish(got, shards)
    wro, wgo, w1 = _rows(g_ro), _rows(g_go), [_cols(g_w10), None]
    shards, side = gather(["mlp_w2_0", "conv_w"])
    (yr, o0, ret_st), (got,) = _ret_fwd(h0, small["ret_gn_g"], rc, sides=[side])
    g_w20, g_cv = _ag_finish(got, shards)
    w2, conv_w = [_rows(g_w20), None], _cols(g_cv)
    mix0 = _mm(o0, wro, "nn", "mm_ret_out")
    x1, x1b, z1 = _ln_fwd(x, mix0, small["ln_mix_g"][0:1], small["ln_mix_b"][0:1], "ln_mix0_fwd")
    shards, side = gather(["gdn_w_in"])
    (hh0, a0), (got,) = _mm(x1b, w1[0], "nn", "mm_mlp0_up", epi="relu2", sides=[side])
    wgi = _cols(_ag_finish(got, shards)[0])
    wgi_main = wgi[:, :GDN_QKV + 4096]
    wba = jnp.pad(wgi[:, GDN_QKV + 4096:], ((0, 0), (0, LANES - 2 * GDN_HV)))
    shards, side = gather(["mlp_w1_1"])
    m0, (got,) = _mm(a0, w2[0], "nn", "mm_mlp0_down", sides=[side])
    w1[1] = _cols(_ag_finish(got, shards)[0])
    x2, x2b, z2 = _ln_fwd(x1, m0, small["ln_ffn_g"][0:1], small["ln_ffn_b"][0:1], "ln_ffn0_fwd")

    shards, side = gather(["mlp_w2_1"])
    h1, (got,) = _mm(x2b, wgi_main, "nn", "mm_gdn_in", sides=[side])
    w2[1] = _rows(_ag_finish(got, shards)[0])
    ba = _mm(x2b, wba, "nn", "mm_gdn_ba")
    qn = _gdn_conv_fwd(h1, conv_w, "q")
    kn = _gdn_conv_fwd(h1, conv_w, "k")
    vg = _gdn_conv_fwd(h1, conv_w, "v")
    beta, g, gc = _gdn_scal_fwd(ba, small["a_log"], small["dt_bias"])
    tmat, u, w = _gdn_chunk_fwd(kn, vg, beta, gc)
    yg, vn, gdn_st = _gdn_scan_fwd(qn, kn, u, w, gc)
    o1 = _gdn_post_fwd(yg, h1, small["norm_g"])
    mix1 = _mm(o1, wgo, "nn", "mm_gdn_out")
    x3, x3b, z3 = _ln_fwd(x2, mix1, small["ln_mix_g"][1:2], small["ln_mix_b"][1:2], "ln_mix1_fwd")
    hh1, a1 = _mm(x3b, w1[1], "nn", "mm_mlp1_up", epi="relu2")
    z4 = _mm(a1, w2[1], "nn", "mm_mlp1_down", epi="add", extra=x3, scale=ALPHA)

    loss, dz4, dz4b, d_lnf_g1, d_lnf_b1 = _loss_ln_bwd(z4, small["ln_ffn_g"][1:2], small["ln_ffn_b"][1:2], tgt,
                                                       "loss_ln_ffn1_bwd")
    dhh1 = _mm(dz4b, w2[1], "nt", "mm_mlp1_down_dx", epi="drelu2", extra=hh1, out_dtype=BF16)
    dw2_1 = _mm(a1, dz4b, "tn", "mm_mlp1_down_dw")
    dx3 = _mm(dhh1, w1[1], "nt", "mm_mlp1_up_dx", epi="add", extra=dz4, scale=ALPHA)
    dw1_1 = _mm(x3b, dhh1, "tn", "mm_mlp1_up_dw", shard_major=True)
    dz3, dz3b, d_lnm_g1, d_lnm_b1 = _ln_bwd(dx3, z3, small["ln_mix_g"][1:2], "ln_mix1_bwd")
    shards_of = lambda g: g.reshape(N_CHIPS, -1, g.shape[-1])
    g_a = [dw1_1, shards_of(dw2_1)]
    do1, (th_a,) = _mm(dz3b, wgo, "nt", "mm_gdn_out_dx", sides=[_rs_swap_side(g_a)])
    sums_a = _rs_add(g_a, th_a, "a")
    dwgo = _mm(o1, dz3b, "tn", "mm_gdn_out_dw")
    dyg, dh1_z, d_norm_g = _gdn_post_bwd(do1, yg, h1, small["norm_g"])
    du, dw, dqn, dk_p, dgc_p = _gdn_scan_bwd(qn, kn, w, vn, gc, gdn_st, dyg)
    dkn, dvg, dbeta, dgc = _gdn_chunk_bwd(kn, vg, beta, gc, tmat, du, dw, dk_p, dgc_p)
    dba, d_a_log, d_dt_bias = _gdn_scal_bwd(ba, small["a_log"], small["dt_bias"], g, dbeta, dgc)
    dacc = _gdn_conv_bwd_act(h1, conv_w, dqn, "q")
    dacc = _gdn_conv_bwd_act(h1, conv_w, dkn, "k", dacc)
    dacc = _gdn_conv_bwd_act(h1, conv_w, dvg, "v", dacc)
    dh1, d_conv_w = _gdn_conv_bwd_in(h1, conv_w, dacc, dh1_z)
    g_go = [shards_of(dwgo)]
    dx2_ba, (th_go,) = _mm(dba, wba, "nt", "mm_gdn_ba_dx", epi="add", extra=dz3, scale=ALPHA,
                           sides=[_rs_swap_side(g_go)])
    sums_a = sums_a + _rs_add(g_go, th_go, "go")
    dx2, (parts_a,) = _mm(dh1, wgi_main, "nt", "mm_gdn_in_dx", epi="add", extra=dx2_ba,
                          sides=[_rs_owner_side(sums_a)])
    mine_a = _rs_sum(parts_a, sums_a, "a")
    dwgi_main, (back_a,) = _mm(x2b, dh1, "tn", "mm_gdn_in_dw", sides=[_rs_back_side(mine_a)])
    red_w1_1, red_w2_1, red_go = zip(mine_a, back_a)
    dwba = _mm(x2b, dba, "tn", "mm_gdn_ba_dw")
    dwgi = jnp.concatenate([dwgi_main, dwba[:, :2 * GDN_HV]], axis=1)
    g_b = [jnp.transpose(dwgi.reshape(dwgi.shape[0], N_CHIPS, -1), (1, 0, 2))]

    dz2, dz2b, d_lnf_g0, d_lnf_b0 = _ln_bwd(dx2, z2, small["ln_ffn_g"][0:1], "ln_ffn0_bwd")
    dhh0, (th_b,) = _mm(dz2b, w2[0], "nt", "mm_mlp0_down_dx", epi="drelu2", extra=hh0, out_dtype=BF16,
                        sides=[_rs_swap_side(g_b)])
    sums_b = _rs_add(g_b, th_b, "b")
    dw2_0 = _mm(a0, dz2b, "tn", "mm_mlp0_down_dw")
    dx1, (parts_b,) = _mm(dhh0, w1[0], "nt", "mm_mlp0_up_dx", epi="add", extra=dz2, scale=ALPHA,
                          sides=[_rs_owner_side(sums_b)])
    mine_b = _rs_sum(parts_b, sums_b, "b")
    dw1_0 = _mm(x1b, dhh0, "tn", "mm_mlp0_up_dw", shard_major=True)
    dz1, dz1b, d_lnm_g0, d_lnm_b0 = _ln_bwd(dx1, z1, small["ln_mix_g"][0:1], "ln_mix0_bwd")
    g_c = [dw1_0, shards_of(dw2_0)]
    do0, (th_c, back_b) = _mm(dz1b, wro, "nt", "mm_ret_out_dx", sides=[_rs_swap_side(g_c), _rs_back_side(mine_b)])
    (red_gi,) = zip(mine_b, back_b)
    sums_c = _rs_add(g_c, th_c, "c")
    dwro = _mm(o0, dz1b, "tn", "mm_ret_out_dw")
    g_ro = [shards_of(dwro)]
    (dq0, dk0, dv0, dgate, d_gn_g), (th_ro, parts_c) = _ret_bwd(
        h0, do0, yr, small["ret_gn_g"], ret_st, rc, cos, sin, sides=[_rs_swap_side(g_ro), _rs_owner_side(sums_c)])
    sums_ro = _rs_add(g_ro, th_ro, "ro")
    mine_c = _rs_sum(parts_c, sums_c, "c")
    dh0 = jnp.concatenate([dq0, dk0, dv0, dgate], axis=1)
    dwri, (parts_ro, back_c) = _mm(xb, dh0, "tn", "mm_ret_in_dw", shard_major=True,
                                   sides=[_rs_owner_side(sums_ro), _rs_back_side(mine_c)])
    red_w1_0, red_w2_0 = zip(mine_c, back_c)
    mine_ro = _rs_sum(parts_ro, sums_ro, "ro")
    g_d = [dwri]
    sums_d = _rs_add(g_d, _comm_call("rs_swap_halves_d", _rs_swap_side(g_d)), "d")
    grad_x, (parts_d, back_ro) = _mm(dh0, wri, "nt", "mm_ret_in_dx", epi="add", extra=dz1, scale=ALPHA,
                                     sides=[_rs_owner_side(sums_d), _rs_back_side(mine_ro)])
    (red_ro,) = zip(mine_ro, back_ro)
    mine_d = _rs_sum(parts_d, sums_d, "d")
    (red_ri,) = zip(mine_d, _comm_call("rs_swap_reduced_d", _rs_back_side(mine_d)))

    big = dict(ret_w_in=red_ri, ret_w_out=red_ro, gdn_w_in=red_gi, gdn_w_out=red_go,
               mlp_w1=(red_w1_0, red_w1_1), mlp_w2=(red_w2_0, red_w2_1))
    sm = dict(ret_gn_g=d_gn_g, a_log=d_a_log, dt_bias=d_dt_bias, norm_g=d_norm_g,
              ln_mix_g=jnp.concatenate([d_lnm_g0, d_lnm_g1], 0), ln_mix_b=jnp.concatenate([d_lnm_b0, d_lnm_b1], 0),
              ln_ffn_g=jnp.concatenate([d_lnf_g0, d_lnf_g1], 0), ln_ffn_b=jnp.concatenate([d_lnf_b0, d_lnf_b1], 0),
              conv_w=d_conv_w)
    return loss, grad_x, big, sm


def _coords():
    return lax.axis_index("x"), lax.axis_index("y"), lax.axis_index("c")


HBM_SPEC = pl.BlockSpec(memory_space=pl.ANY)


def _other_chips(x, y):
    return [(1 - x, y), (x, 1 - y), (1 - x, 1 - y)]


def _ag_side(shards, split):
    n = len(shards)

    def piece(ref, p, core):
        if not split[p]:
            return ref
        half = shards[p].shape[0] // 2
        return ref.at[pl.ds(core * half, half)]

    def over_ici(xyc, ins, outs):
        x, y, c = xyc
        return [(piece(ins[p], p, c), piece(outs[p].at[2 * x + y], p, c), (cx, cy, c))
                for p in range(n) for cx, cy in _other_chips(x, y)]

    def to_sibling(xyc, ins, outs):
        x, y, c = xyc
        zones = [piece(outs[p].at[2 * cx + cy], p, c) for p in range(n) if split[p] for cx, cy in _other_chips(x, y)]
        return [(z, z, (x, y, 1 - c)) for z in zones]

    n_split = sum(bool(s) for s in split)
    phases, counts = [over_ici], [3 * n]
    if n_split:
        phases, counts = phases + [to_sibling], counts + [3 * n_split]
    return _Side(shards, [SDS((N_CHIPS,) + s.shape, s.dtype) for s in shards], phases, counts)


def _ag_finish(got, shards):
    me = 2 * lax.axis_index("x") + lax.axis_index("y")
    return [lax.dynamic_update_index_in_dim(g, s, me, 0) for g, s in zip(got, shards)]


def _rs_swap_side(grads):
    n = len(grads)
    halves = [g.shape[1] // 2 for g in grads]

    def swap(xyc, ins, outs):
        x, y, c = xyc
        return [(ins[p].at[:, pl.ds((1 - c) * halves[p], halves[p])], outs[p], (x, y, 1 - c)) for p in range(n)]

    return _Side(grads, [SDS((N_CHIPS, halves[p]) + g.shape[2:], F32) for p, g in enumerate(grads)], [swap], [n])


def _rs_add(grads, theirs, tag):
    c = lax.axis_index("c")
    return [_add_half(g, t, c, "rs_add_%s%d" % (tag, p)) for p, (g, t) in enumerate(zip(grads, theirs))]


def _rs_owner_side(chip_sums):
    n = len(chip_sums)

    def to_owner(xyc, ins, outs):
        x, y, c = xyc
        return [(ins[p].at[2 * cx + cy], outs[p].at[2 * x + y], (cx, cy, c))
                for p in range(n) for cx, cy in _other_chips(x, y)]

    return _Side(chip_sums, [SDS(s.shape, s.dtype) for s in chip_sums], [to_owner], [3 * n])


def _rs_sum(parts, chip_sums, tag):
    chip = 2 * lax.axis_index("x") + lax.axis_index("y")
    parts = [lax.dynamic_update_index_in_dim(pt, lax.dynamic_index_in_dim(cs, chip, 0, keepdims=False), chip, 0)
             for pt, cs in zip(parts, chip_sums)]
    return [_sum_chips(pt, "rs_sum_%s%d" % (tag, p)) for p, pt in enumerate(parts)]


def _rs_back_side(mine):
    n = len(mine)

    def swap(xyc, ins, outs):
        x, y, c = xyc
        return [(ins[p], outs[p], (x, y, 1 - c)) for p in range(n)]

    return _Side(mine, [SDS(m.shape, F32) for m in mine], [swap], [n])


def _add_half(g, theirs, c, name):
    _, R, C = g.shape
    half = R // 2
    tr = min(256, half)
    nb = half // tr

    def body(c_ref, g_ref, t_ref, o_ref):
        o_ref[...] = (g_ref[...] + t_ref[...]).astype(BF16)

    blk = pl.BlockSpec((None, tr, C), lambda s, i, c_ref: (s, i, 0))
    return pl.pallas_call(
        body, name=name,
        grid_spec=pltpu.PrefetchScalarGridSpec(
            num_scalar_prefetch=1, grid=(N_CHIPS, nb),
            in_specs=[pl.BlockSpec((None, tr, C), lambda s, i, c_ref: (s, c_ref[0] * nb + i, 0)), blk],
            out_specs=blk),
        out_shape=SDS((N_CHIPS, half, C), BF16), compiler_params=_cp("parallel", "parallel"),
    )(jnp.reshape(c, (1,)).astype(jnp.int32), g, theirs)


def _sum_chips(parts, name):
    _, r, C = parts.shape
    tr = min(256, r)

    def body(p_ref, o_ref):
        f = lambda s: p_ref[s].astype(F32)
        o_ref[...] = ((f(0) + f(1)) + f(2)) + f(3)

    return pl.pallas_call(
        body, name=name, grid=(r // tr,), in_specs=[pl.BlockSpec((N_CHIPS, tr, C), lambda i: (0, i, 0))],
        out_specs=pl.BlockSpec((tr, C), lambda i: (i, 0)), out_shape=SDS((r, C), F32), compiler_params=_cp("parallel"),
    )(parts)


def _all_reduce_small(buf):
    rows = buf.shape[0]

    def body(x_ref, o_ref, all_ref, send_sems, recv_sems):
        x, y, c = _coords()
        me = 4 * x + 2 * y + c
        all_ref[me] = x_ref[...]
        flips = [(fx, fy, fc) for fx in (0, 1) for fy in (0, 1) for fc in (0, 1)][1:]
        copies = []
        for k, (fx, fy, fc) in enumerate(flips):
            to = (x ^ fx, y ^ fy, c ^ fc)
            copies.append(pltpu.make_async_remote_copy(src_ref=x_ref, dst_ref=all_ref.at[me], send_sem=send_sems.at[k],
                                                       recv_sem=recv_sems.at[k], device_id=to, device_id_type=MESH))
        for cp in copies:
            cp.start()
        for cp in copies:
            cp.wait_recv()
        for cp in copies:
            cp.wait_send()
        acc = all_ref[0]
        for d in range(1, N_DEV):
            acc = acc + all_ref[d]
        o_ref[...] = acc

    vm = pl.BlockSpec(memory_space=pltpu.VMEM)
    return pl.pallas_call(
        body, name="all_reduce_small", in_specs=[vm], out_specs=vm, out_shape=SDS((rows, LANES), F32),
        scratch_shapes=[pltpu.VMEM((N_DEV, rows, LANES), F32), pltpu.SemaphoreType.DMA((N_DEV - 1,)),
                        pltpu.SemaphoreType.DMA((N_DEV - 1,))],
    )(buf)


def _adam_update(w, gv, m, v):
    mn = ADAM_B1 * m + (1.0 - ADAM_B1) * gv
    vn = ADAM_B2 * v + (1.0 - ADAM_B2) * (gv * gv)
    m_hat = mn / (1.0 - ADAM_B1 ** ADAM_STEP)
    v_hat = vn / (1.0 - ADAM_B2 ** ADAM_STEP)
    return -ADAM_LR * (m_hat / (jnp.sqrt(v_hat) + ADAM_EPS) + ADAM_WD * w), mn, vn


def _adamw_halves(w, mine, theirs, m, v, c, name, row0=0, bufs=None):
    R, C = w.shape
    half = mine.shape[0]
    tr = min(128, half)
    nbh = half // tr
    b0 = row0 // tr
    assert row0 % tr == 0 and half % tr == 0

    def body(c_ref, w_ref, a_ref, b_ref, m_ref, v_ref, *rest):
        g_ref, d_ref, mo_ref, vo_ref = rest[-4:]
        is_mine = (pl.program_id(0) // nbh) == c_ref[0]
        gv = jnp.where(is_mine, a_ref[...], b_ref[...])
        g_ref[...] = gv
        d_ref[...], mo_ref[...], vo_ref[...] = _adam_update(w_ref[...], gv, m_ref[...], v_ref[...])

    blk = pl.BlockSpec((tr, C), lambda i, c_ref: (b0 + i, 0))
    hblk = pl.BlockSpec((tr, C), lambda i, c_ref: (i % nbh, 0))
    extra = [] if bufs is None else list(bufs)
    return pl.pallas_call(
        body, name=name,
        grid_spec=pltpu.PrefetchScalarGridSpec(num_scalar_prefetch=1, grid=(2 * nbh,),
                                               in_specs=[blk, hblk, hblk, blk, blk] + [HBM_SPEC] * len(extra),
                                               out_specs=(blk,) * 4),
        out_shape=(SDS((R, C), F32),) * 4, compiler_params=_cp("parallel"),
        input_output_aliases={6 + i: i for i in range(len(extra))},
    )(jnp.reshape(c, (1,)).astype(jnp.int32), w, mine, theirs, m, v, *extra)


def _adamw(w, g, m, v, name):
    R, C = w.shape
    tr = min(256, R)
    assert R % tr == 0

    def body(w_ref, g_ref, m_ref, v_ref, d_ref, mo_ref, vo_ref):
        d_ref[...], mo_ref[...], vo_ref[...] = _adam_update(w_ref[...], g_ref[...], m_ref[...], v_ref[...])

    blk = pl.BlockSpec((tr, C), lambda i: (i, 0))
    return pl.pallas_call(
        body, name=name, grid=(R // tr,), in_specs=[blk] * 4, out_specs=(blk,) * 3,
        out_shape=(SDS((R, C), F32),) * 3, compiler_params=_cp("parallel"),
    )(w, g, m, v)


def _pack(arrs):
    rows = []
    for a in arrs:
        flat = a.reshape(-1).astype(F32)
        pad = (-flat.shape[0]) % LANES
        rows.append(jnp.pad(flat, (0, pad)).reshape(-1, LANES))
    buf = jnp.concatenate(rows, axis=0)
    pad_rows = (-buf.shape[0]) % 8
    return jnp.pad(buf, ((0, pad_rows), (0, 0)))


def _unpack(buf, shapes):
    out, r = [], 0
    for shp in shapes:
        size = int(np.prod(shp))
        nr = -(-size // LANES)
        out.append(buf[r:r + nr].reshape(-1)[:size].reshape(shp))
        r += nr
    return out


def _pad_lanes(a):
    return jnp.pad(a, ((0, 0), (0, LANES - a.shape[1])))


def kernel(x, ret_w_in, ret_gn_g, ret_w_out, gdn_w_in, gdn_conv_w, gdn_a_log, gdn_dt_bias, gdn_norm_g, gdn_w_out, ln_mix_g, ln_mix_b, mlp_w1, mlp_w2, ln_ffn_g, ln_ffn_b, loss_target, m_ret_w_in, m_ret_gn_g, m_ret_w_out, m_gdn_w_in, m_gdn_conv_w, m_gdn_a_log, m_gdn_dt_bias, m_gdn_norm_g, m_gdn_w_out, m_ln_mix_g, m_ln_mix_b, m_mlp_w1, m_mlp_w2, m_ln_ffn_g, m_ln_ffn_b, v_ret_w_in, v_ret_gn_g, v_ret_w_out, v_gdn_w_in, v_gdn_conv_w, v_gdn_a_log, v_gdn_dt_bias, v_gdn_norm_g, v_gdn_w_out, v_ln_mix_g, v_ln_mix_b, v_mlp_w1, v_mlp_w2, v_ln_ffn_g, v_ln_ffn_b):
    cx, cy = lax.axis_index("x"), lax.axis_index("y")
    chip = 2 * cx + cy

    sh = dict(ret_w_in=ret_w_in[0].astype(BF16), ret_w_out=ret_w_out[0].astype(BF16),
              gdn_w_in=gdn_w_in[0].astype(BF16), gdn_w_out=gdn_w_out[0].astype(BF16),
              mlp_w1_0=mlp_w1[0].astype(BF16), mlp_w1_1=mlp_w1[1].astype(BF16),
              mlp_w2_0=mlp_w2[0].astype(BF16), mlp_w2_1=mlp_w2[1].astype(BF16), conv_w=gdn_conv_w[0])
    small = dict(ret_gn_g=ret_gn_g, a_log=_pad_lanes(gdn_a_log), dt_bias=_pad_lanes(gdn_dt_bias), norm_g=gdn_norm_g,
                 ln_mix_g=ln_mix_g, ln_mix_b=ln_mix_b, ln_ffn_g=ln_ffn_g, ln_ffn_b=ln_ffn_b)

    loss, grad_x, big, sm = _local_step(x[0], loss_target[0], sh, small)

    small_names = ["ret_gn_g", "a_log", "dt_bias", "norm_g", "ln_mix_g", "ln_mix_b", "ln_ffn_g", "ln_ffn_b", "conv_w"]
    small_shapes = [(1, 4096), (1, LANES), (1, LANES), (1, GDN_D), (2, D_MODEL), (2, D_MODEL), (2, D_MODEL),
                    (2, D_MODEL), (4, GDN_QKV)]
    red = _all_reduce_small(_pack([loss] + [sm[k] for k in small_names]))
    red_loss, *red_small = _unpack(red, [(1, 1)] + small_shapes)
    gs = dict(zip(small_names, red_small))
    g_conv = lax.dynamic_slice_in_dim(gs["conv_w"], chip * 2048, 2048, axis=1)
    g_a_log, g_dt_bias = gs["a_log"][:, :GDN_HV], gs["dt_bias"][:, :GDN_HV]

    big_w = [(ret_w_in, m_ret_w_in, v_ret_w_in, [big["ret_w_in"]]), (ret_w_out, m_ret_w_out, v_ret_w_out, [big["ret_w_out"]]),
             (gdn_w_in, m_gdn_w_in, v_gdn_w_in, [big["gdn_w_in"]]), (gdn_w_out, m_gdn_w_out, v_gdn_w_out, [big["gdn_w_out"]]),
             (mlp_w1, m_mlp_w1, v_mlp_w1, big["mlp_w1"]), (mlp_w2, m_mlp_w2, v_mlp_w2, big["mlp_w2"])]
    core = lax.axis_index("c")
    big_out = []
    for i, (w_, m_, v_, layers) in enumerate(big_w):
        two_d = lambda a: a.reshape(-1, a.shape[-1])
        res = None
        for j, (mine, theirs) in enumerate(layers):
            res = _adamw_halves(two_d(w_), mine, theirs, two_d(m_), two_d(v_), core, "adamw_%d_%d" % (i, j),
                                row0=j * 2 * mine.shape[0], bufs=res)
        big_out.append(tuple(a.reshape(w_.shape) for a in res))
    sm_w = [(ret_gn_g, m_ret_gn_g, v_ret_gn_g, gs["ret_gn_g"]), (gdn_conv_w, m_gdn_conv_w, v_gdn_conv_w, g_conv),
            (gdn_a_log, m_gdn_a_log, v_gdn_a_log, g_a_log), (gdn_dt_bias, m_gdn_dt_bias, v_gdn_dt_bias, g_dt_bias),
            (gdn_norm_g, m_gdn_norm_g, v_gdn_norm_g, gs["norm_g"]), (ln_mix_g, m_ln_mix_g, v_ln_mix_g, gs["ln_mix_g"]),
            (ln_mix_b, m_ln_mix_b, v_ln_mix_b, gs["ln_mix_b"]), (ln_ffn_g, m_ln_ffn_g, v_ln_ffn_g, gs["ln_ffn_g"]),
            (ln_ffn_b, m_ln_ffn_b, v_ln_ffn_b, gs["ln_ffn_b"])]
    sm_shapes = [w_.shape for w_, _, _, _ in sm_w]
    d_s, nm_s, nv_s = _adamw(_pack([w_ for w_, _, _, _ in sm_w]), _pack([g_ for _, _, _, g_ in sm_w]),
                             _pack([m_ for _, m_, _, _ in sm_w]), _pack([v_ for _, _, v_, _ in sm_w]), "adamw_small")
    d_s, nm_s, nv_s = (_unpack(a, sm_shapes) for a in (d_s, nm_s, nv_s))
    sm_out = [(g_.reshape(w_.shape), d_s[i], nm_s[i], nv_s[i]) for i, (w_, _, _, g_) in enumerate(sm_w)]

    per_w = [big_out[0], sm_out[0], big_out[1], big_out[2], sm_out[1], sm_out[2], sm_out[3], sm_out[4], big_out[3],
             sm_out[5], sm_out[6], big_out[4], big_out[5], sm_out[7], sm_out[8]]
    outs = [red_loss.reshape(()), grad_x[None]]
    for kind in range(4):
        outs.extend(t[kind] for t in per_w)
    return tuple(outs)
```

```python
import functools

import numpy as np
import jax
import jax.numpy as jnp
from jax import lax
from jax.experimental import pallas as pl
from jax.experimental.pallas import tpu as pltpu

F32 = jnp.float32
BF16 = jnp.bfloat16
MESH = pl.DeviceIdType.MESH
SDS = jax.ShapeDtypeStruct

D_MODEL = 2048
CHUNK = 64
RET_HEADS, RET_DK, RET_DV = 8, 256, 512
GDN_HV, GDN_D = 32, 128
HP = 2
KB = 2
NH = KB * HP
GDN_QKV = 8192
ALPHA = 4.0 ** 0.25
LN_EPS, GN_EPS, RMS_EPS, L2_EPS = 1e-5, 1e-6, 1e-6, 1e-6
ADAM_LR, ADAM_B1, ADAM_B2, ADAM_EPS, ADAM_WD, ADAM_STEP = 0.001, 0.9, 0.999, 1e-8, 0.01, 10

VMEM_LIMIT_BYTES = 56 * 1024 * 1024
RT = 256
CT = 256
ROWS = 256
LANES = 128
N_CHIPS = 4
N_DEV = 8


def _cp(*sem):
    return pltpu.CompilerParams(dimension_semantics=sem, vmem_limit_bytes=VMEM_LIMIT_BYTES)


def _dot(a, b):
    return jnp.dot(a, b, preferred_element_type=F32)


def _dot_nt(a, b):
    return lax.dot_general(a, b, (((1,), (1,)), ((), ())), preferred_element_type=F32)


def _dot_tn(a, b):
    return lax.dot_general(a, b, (((0,), (0,)), ((), ())), preferred_element_type=F32)


def _split2(x):
    hi = x.astype(BF16)
    lo = (x - hi.astype(F32)).astype(BF16)
    return hi, lo


def _dotx3(a, b):
    ah, al = _split2(a)
    bh, bl = _split2(b)
    return _dot(ah, bh) + (_dot(ah, bl) + _dot(al, bh))


def _dot_exact_l(l_bf16, x):
    hi = x.astype(BF16)
    r = x - hi.astype(F32)
    mid = r.astype(BF16)
    lo = (r - mid.astype(F32)).astype(BF16)
    return _dot(l_bf16, hi) + (_dot(l_bf16, mid) + _dot(l_bf16, lo))


def _sigmoid(x):
    return 1.0 / (1.0 + jnp.exp(-x))


def _iota(shape, dim):
    return lax.broadcasted_iota(jnp.int32, shape, dim)


class _Side:
    def __init__(self, arrays, out_shapes, phases, counts, local=None, n_local=0):
        self.arrays, self.out_shapes, self.phases, self.counts = list(arrays), list(out_shapes), phases, counts
        self.local, self.n_local = local, n_local

    def sem_shapes(self):
        sems = [pltpu.SemaphoreType.DMA((sum(self.counts),))] * 2
        return sems + ([pltpu.SemaphoreType.DMA((self.n_local,))] if self.n_local else [])

    def ops(self, ins, outs, sems):
        send_sems, recv_sems = sems[0], sems[1]

        def copies(ph):
            if ph < 0:
                return [pltpu.make_async_copy(src, dst, sems[2].at[i])
                        for i, (src, dst) in enumerate(self.local(_coords(), ins, outs))] if self.n_local else []
            off = sum(self.counts[:ph])
            return [pltpu.make_async_remote_copy(src_ref=src, dst_ref=dst, send_sem=send_sems.at[off + i],
                                                 recv_sem=recv_sems.at[off + i], device_id=to, device_id_type=MESH)
                    for i, (src, dst, to) in enumerate(self.phases[ph](_coords(), ins, outs))]

        def start(ph):
            for cp in copies(ph):
                cp.start()

        def wait(ph):
            cps = copies(ph)
            if ph < 0:
                for cp in cps:
                    cp.wait()
                return
            for cp in cps:
                cp.wait_recv()
            for cp in cps:
                cp.wait_send()

        return start, wait


def _comm_call(name, side):
    n = len(side.arrays)

    def body(*refs):
        start, wait = side.ops(refs[:n], refs[n:2 * n], refs[2 * n:])
        start(-1)
        for ph in range(len(side.phases)):
            start(ph)
            wait(ph)
        wait(-1)

    return pl.pallas_call(body, name=name, in_specs=[HBM_SPEC] * n, out_specs=[HBM_SPEC] * n,
                          out_shape=side.out_shapes, scratch_shapes=side.sem_shapes())(*side.arrays)


SIDE_SWITCH = 0.85


def _pcall(body, args, *, name, grid, in_specs, out_specs, out_shape, sem, scratch_shapes=(), sides=()):
    single = not isinstance(out_shape, (tuple, list))
    if not sides:
        return pl.pallas_call(body, name=name, grid=grid, in_specs=list(in_specs), out_specs=out_specs,
                              out_shape=out_shape, scratch_shapes=list(scratch_shapes), compiler_params=_cp(*sem))(*args)
    o_shapes = (out_shape,) if single else tuple(out_shape)
    o_specs = (out_specs,) if single else tuple(out_specs)
    n_in, n_out, n_scr = len(args), len(o_shapes), len(scratch_shapes)
    ns = [len(s.arrays) for s in sides]
    steps = int(np.prod(grid))

    def carrier(*refs):
        pos = n_in
        s_ins = []
        for k in ns:
            s_ins.append(refs[pos:pos + k])
            pos += k
        outs = refs[pos:pos + n_out]
        pos += n_out
        s_outs = []
        for k in ns:
            s_outs.append(refs[pos:pos + k])
            pos += k
        scr = refs[pos:pos + n_scr]
        pos += n_scr
        step = pl.program_id(0)
        for d in range(1, len(grid)):
            step = step * grid[d] + pl.program_id(d)
        hooks = []
        for i, s in enumerate(sides):
            k = len(s.sem_shapes())
            hooks.append(s.ops(s_ins[i], s_outs[i], refs[pos:pos + k]))
            pos += k
        for (start, wait), s in zip(hooks, sides):
            def first(start=start):
                start(-1)
                start(0)
            pl.when(step == 0)(first)
            if len(s.phases) == 2:
                def switch(start=start, wait=wait):
                    wait(0)
                    start(1)
                pl.when(step == int(steps * SIDE_SWITCH))(switch)
            else:
                assert len(s.phases) == 1
        body(*refs[:n_in], *outs, *scr)
        for (start, wait), s in zip(hooks, sides):
            def last(wait=wait, n_ph=len(s.phases)):
                wait(n_ph - 1)
                wait(-1)
            pl.when(step == steps - 1)(last)

    res = pl.pallas_call(
        carrier, name=name, grid=grid, in_specs=list(in_specs) + [HBM_SPEC] * sum(ns),
        out_specs=o_specs + (HBM_SPEC,) * sum(ns),
        out_shape=o_shapes + tuple(sh for s in sides for sh in s.out_shapes),
        scratch_shapes=list(scratch_shapes) + [sm for s in sides for sm in s.sem_shapes()],
        compiler_params=_cp(*(("arbitrary",) * len(grid))),
    )(*args, *[a for s in sides for a in s.arrays])
    main, rest, side_res = res[:n_out], list(res[n_out:]), []
    for k in ns:
        side_res.append(rest[:k])
        rest = rest[k:]
    return (main[0] if single else tuple(main)), side_res


def _mm(a, b, mode, name, *, out_dtype=F32, tm=1024, tn=1024, tk=2048, epi=None, extra=None, scale=1.0,
        shard_major=False, sides=()):
    if mode == "nn":
        (M, K), (K2, N) = a.shape, b.shape
    elif mode == "nt":
        (M, K), (N, K2) = a.shape, b.shape
    else:
        (K, M), (K2, N) = a.shape, b.shape
    assert K == K2, (a.shape, b.shape, mode)
    tm, tn, tk = min(tm, M), min(tn, N), min(tk, K)
    assert M % tm == 0 and N % tn == 0 and K % tk == 0, (M, N, K, tm, tn, tk)
    nk = K // tk
    dims = {"nn": (((1,), (0,)), ((), ())), "nt": (((1,), (1,)), ((), ())), "tn": (((0,), (0,)), ((), ()))}[mode]
    if mode == "tn":
        a_spec = pl.BlockSpec((tk, tm), lambda i, j, k: (k, i))
    else:
        a_spec = pl.BlockSpec((tm, tk), lambda i, j, k: (i, k))
    if mode == "nt":
        b_spec = pl.BlockSpec((tn, tk), lambda i, j, k: (j, k))
    else:
        b_spec = pl.BlockSpec((tk, tn), lambda i, j, k: (k, j))
    tile = pl.BlockSpec((tm, tn), lambda i, j, k: (i, j))
    in_specs, ins = [a_spec, b_spec], [a, b]
    if epi == "rope":
        half = RET_DK // 2
        assert tn % RET_DK == 0 and (2 * RET_HEADS * RET_DK) % tn == 0
        in_specs += [pl.BlockSpec((tm, half), lambda i, j, k: (i, 0))] * 2
        ins += list(extra)
    elif extra is not None:
        in_specs.append(tile)
        ins.append(extra)
    n_extra = len(ins) - 2
    if epi == "relu2":
        out_shape = (SDS((M, N), F32), SDS((M, N), BF16))
        out_specs = (tile, tile)
    elif shard_major:
        per = (N // N_CHIPS) // tn
        assert per * tn * N_CHIPS == N
        out_shape = (SDS((N_CHIPS, M, N // N_CHIPS), out_dtype),)
        out_specs = (pl.BlockSpec((None, tm, tn), lambda i, j, k: (j // per, i, j % per)),)
    else:
        out_shape = (SDS((M, N), out_dtype),)
        out_specs = (tile,)
    n_out = len(out_shape)

    def body(*refs):
        a_ref, b_ref = refs[0], refs[1]
        x_ref = refs[2] if n_extra else None
        pos = 2 + n_extra
        o_refs = refs[pos:pos + n_out]
        acc_ref = refs[pos + n_out] if nk > 1 else None

        def prod():
            av, bv = a_ref[...], b_ref[...]
            if av.dtype != BF16:
                av = av.astype(BF16)
            if bv.dtype != BF16:
                bv = bv.astype(BF16)
            return lax.dot_general(av, bv, dims, preferred_element_type=F32)

        def finish(acc):
            if epi == "relu2":
                o_refs[0][...] = acc
                r = jnp.maximum(acc, 0.0)
                o_refs[1][...] = (r * r).astype(BF16)
            elif epi == "drelu2":
                o_refs[0][...] = (acc * (2.0 * jnp.maximum(x_ref[...], 0.0))).astype(out_dtype)
            elif epi == "add":
                o_refs[0][...] = (acc + scale * x_ref[...]).astype(out_dtype)
            elif epi == "rope":
                j = pl.program_id(1)
                qk_tiles = 2 * RET_HEADS * RET_DK // tn

                @pl.when(j < qk_tiles)
                def _():
                    c, s = refs[2][...], refs[3][...]
                    sc = jnp.where(j >= qk_tiles // 2, RET_DK ** -0.5, 1.0)
                    for hh in range(tn // RET_DK):
                        lo = slice(hh * RET_DK, hh * RET_DK + half)
                        hi = slice(hh * RET_DK + half, (hh + 1) * RET_DK)
                        t1, t2 = acc[:, lo], acc[:, hi]
                        o_refs[0][:, lo] = ((t1 * c - t2 * s) * sc).astype(out_dtype)
                        o_refs[0][:, hi] = ((t1 * s + t2 * c) * sc).astype(out_dtype)

                @pl.when(j >= qk_tiles)
                def _():
                    o_refs[0][...] = acc.astype(out_dtype)
            else:
                o_refs[0][...] = acc.astype(out_dtype)

        if nk == 1:
            finish(prod())
        else:
            k = pl.program_id(2)

            @pl.when(k == 0)
            def _():
                acc_ref[...] = prod()

            @pl.when(k > 0)
            def _():
                acc_ref[...] += prod()

            @pl.when(k == nk - 1)
            def _():
                finish(acc_ref[...])

    res = _pcall(body, ins, name=name, grid=(M // tm, N // tn, nk), in_specs=in_specs, out_specs=out_specs,
                 out_shape=out_shape, scratch_shapes=[pltpu.VMEM((tm, tn), F32)] if nk > 1 else [],
                 sem=("parallel", "parallel", "arbitrary"), sides=sides)
    main, side_res = res if sides else (res, None)
    main = main if n_out > 1 else main[0]
    return (main, side_res) if sides else main


def _ln_stats(z):
    mu = jnp.mean(z, -1, keepdims=True)
    zc = z - mu
    var = jnp.mean(zc * zc, -1, keepdims=True)
    rstd = lax.rsqrt(var + LN_EPS)
    return zc * rstd, rstd


def _ln_fwd(xin, sub, g, b, name):
    S, Dm = xin.shape
    row = pl.BlockSpec((ROWS, Dm), lambda t: (t, 0))
    vec = pl.BlockSpec((1, Dm), lambda t: (0, 0))

    def body(x_ref, s_ref, g_ref, b_ref, o_ref, ob_ref, z_ref):
        z = ALPHA * x_ref[...] + s_ref[...]
        xh, _ = _ln_stats(z)
        o = xh * g_ref[...] + b_ref[...]
        o_ref[...] = o
        ob_ref[...] = o.astype(BF16)
        z_ref[...] = z

    return pl.pallas_call(
        body, name=name, grid=(S // ROWS,), in_specs=[row, row, vec, vec], out_specs=(row, row, row),
        out_shape=(SDS((S, Dm), F32), SDS((S, Dm), BF16), SDS((S, Dm), F32)), compiler_params=_cp("parallel"),
    )(xin, sub, g, b)


def _ln_bwd(dout, z, g, name):
    S, Dm = z.shape
    row = pl.BlockSpec((ROWS, Dm), lambda t: (t, 0))
    vec = pl.BlockSpec((1, Dm), lambda t: (0, 0))

    def body(d_ref, z_ref, g_ref, dz_ref, dzb_ref, dg_ref, db_ref):
        t = pl.program_id(0)
        xh, rstd = _ln_stats(z_ref[...])
        d = d_ref[...]
        dxh = d * g_ref[...]
        m1 = jnp.mean(dxh, -1, keepdims=True)
        m2 = jnp.mean(dxh * xh, -1, keepdims=True)
        dz = rstd * (dxh - m1 - xh * m2)
        dz_ref[...] = dz
        dzb_ref[...] = dz.astype(BF16)
        pg = jnp.sum(d * xh, axis=0, keepdims=True)
        pb = jnp.sum(d, axis=0, keepdims=True)

        @pl.when(t == 0)
        def _():
            dg_ref[...] = pg
            db_ref[...] = pb

        @pl.when(t > 0)
        def _():
            dg_ref[...] += pg
            db_ref[...] += pb

    return pl.pallas_call(
        body, name=name, grid=(S // ROWS,), in_specs=[row, row, vec], out_specs=(row, row, vec, vec),
        out_shape=(SDS((S, Dm), F32), SDS((S, Dm), BF16), SDS((1, Dm), F32), SDS((1, Dm), F32)),
        compiler_params=_cp("arbitrary"),
    )(dout, z, g)


def _loss_ln_bwd(z, g, b, tgt, name):
    S, Dm = z.shape
    row = pl.BlockSpec((ROWS, Dm), lambda t: (t, 0))
    vec = pl.BlockSpec((1, Dm), lambda t: (0, 0))
    one = pl.BlockSpec((1, 1), lambda t: (0, 0))

    def body(z_ref, g_ref, b_ref, t_ref, l_ref, dz_ref, dzb_ref, dg_ref, db_ref):
        t = pl.program_id(0)
        xh, rstd = _ln_stats(z_ref[...])
        diff = xh * g_ref[...] + b_ref[...] - t_ref[...]
        part = jnp.sum(jnp.sum(diff * diff, axis=1, keepdims=True), axis=0, keepdims=True) * (0.5 / Dm)
        d = diff * (1.0 / Dm)
        dxh = d * g_ref[...]
        m1 = jnp.mean(dxh, -1, keepdims=True)
        m2 = jnp.mean(dxh * xh, -1, keepdims=True)
        dz = rstd * (dxh - m1 - xh * m2)
        dz_ref[...] = dz
        dzb_ref[...] = dz.astype(BF16)
        pg = jnp.sum(d * xh, axis=0, keepdims=True)
        pb = jnp.sum(d, axis=0, keepdims=True)

        @pl.when(t == 0)
        def _():
            l_ref[...] = part
            dg_ref[...] = pg
            db_ref[...] = pb

        @pl.when(t > 0)
        def _():
            l_ref[...] += part
            dg_ref[...] += pg
            db_ref[...] += pb

    return pl.pallas_call(
        body, name=name, grid=(S // ROWS,), in_specs=[row, vec, vec, row], out_specs=(one, row, row, vec, vec),
        out_shape=(SDS((1, 1), F32), SDS((S, Dm), F32), SDS((S, Dm), BF16), SDS((1, Dm), F32), SDS((1, Dm), F32)),
        compiler_params=_cp("arbitrary"),
    )(z, g, b, tgt)


def _ret_consts():
    h = np.arange(RET_HEADS, dtype=np.float64)
    lg = np.log1p(-np.exp2(-5.0 - h))
    return jnp.asarray(np.concatenate([lg, np.exp(lg * RT)]).astype(np.float32))


def _rope_tables(S):
    half = RET_DK // 2
    inv = 10000.0 ** (-jnp.arange(half, dtype=F32) / half)
    ang = jnp.arange(S).astype(F32)[:, None] * inv[None, :]
    return jnp.cos(ang), jnp.sin(ang)


def _ret_masks(lgh):
    ri, ci = _iota((RT, RT), 0), _iota((RT, RT), 1)
    visible = (ci >> 6) <= (ri >> 6)
    m = jnp.where(visible, jnp.exp(lgh * jnp.abs(ri - ci).astype(F32)), 0.0)
    pos = _iota((RT, 1), 0).astype(F32)
    return m, jnp.exp(lgh * (pos + 1.0)), jnp.exp(lgh * (RT - 1.0 - pos))


def _ret_fwd(h0, gn_g, consts, sides=()):
    S = h0.shape[0]
    nt = S // RT

    def body(c_ref, q_ref, k_ref, v_ref, gate_ref, g_ref, y_ref, o_ref, st_ref, s_scr):
        h, t = pl.program_id(0), pl.program_id(1)

        @pl.when(t == 0)
        def _():
            s_scr[...] = jnp.zeros_like(s_scr)

        lgh, cdec = c_ref[h], c_ref[RET_HEADS + h]
        m, dq, dk = _ret_masks(lgh)
        qv, kv, vv = q_ref[...], k_ref[...], v_ref[...]
        p = (_dot_nt(qv, kv) * m).astype(BF16)
        sp = s_scr[...]
        spb = sp.astype(BF16)
        st_ref[...] = spb
        qd = (qv.astype(F32) * dq).astype(BF16)
        kd = (kv.astype(F32) * dk).astype(BF16)
        y = _dot(p, vv) + _dot(qd, spb)
        y_ref[...] = y
        s_scr[...] = sp * cdec + _dot_tn(kd, vv)
        yn, _ = _gn_stats(y)
        gate = gate_ref[...].astype(F32)
        o_ref[...] = (gate * _sigmoid(gate) * (yn * g_ref[...])).astype(BF16)

    qk = lambda off: pl.BlockSpec((RT, RET_DK), lambda h, t: (t, off + h))
    vs = lambda off: pl.BlockSpec((RT, RET_DV), lambda h, t: (t, off + h))
    return _pcall(
        body, (consts, h0, h0, h0, h0, gn_g), name="ret_fwd", grid=(RET_HEADS, nt),
        in_specs=[pl.BlockSpec(memory_space=pltpu.SMEM), qk(0), qk(RET_HEADS), vs(RET_HEADS), vs(2 * RET_HEADS),
                  pl.BlockSpec((1, RET_DV), lambda h, t: (0, h))],
        out_specs=(vs(0), vs(0), pl.BlockSpec((None, None, RET_DK, RET_DV), lambda h, t: (h, t, 0, 0))),
        out_shape=(SDS((S, 4096), F32), SDS((S, 4096), BF16), SDS((RET_HEADS, nt, RET_DK, RET_DV), BF16)),
        scratch_shapes=[pltpu.VMEM((RET_DK, RET_DV), F32)], sem=("parallel", "arbitrary"), sides=sides)


def _ret_bwd(h0, do, y, gn_g, states, consts, cos, sin, sides=()):
    S = h0.shape[0]
    nt = S // RT
    half = RET_DK // 2

    def body(c_ref, q_ref, k_ref, v_ref, gate_ref, do_ref, y_ref, g_ref, st_ref, cos_ref, sin_ref,
             dq_ref, dk_ref, dv_ref, dgate_ref, dg_ref, ds_scr):
        h, t = pl.program_id(0), pl.program_id(1)

        @pl.when(t == 0)
        def _():
            ds_scr[...] = jnp.zeros_like(ds_scr)
            dg_ref[...] = jnp.zeros_like(dg_ref)

        yn, rstd = _gn_stats(y_ref[...])
        gate, g, dov = gate_ref[...].astype(F32), g_ref[...], do_ref[...]
        sg = _sigmoid(gate)
        dgate_ref[...] = (dov * (yn * g) * (sg * (1.0 + gate * (1.0 - sg)))).astype(BF16)
        dyg = dov * (gate * sg)
        dyn = dyg * g
        m1 = jnp.mean(dyn, -1, keepdims=True)
        m2 = jnp.mean(dyn * yn, -1, keepdims=True)
        dyv = (rstd * (dyn - m1 - yn * m2)).astype(BF16)
        dg_ref[...] += jnp.sum(dyg * yn, axis=0, keepdims=True)

        lgh, cdec = c_ref[h], c_ref[RET_HEADS + h]
        m, dqc, dkc = _ret_masks(lgh)
        qv, kv, vv, spb = q_ref[...], k_ref[...], v_ref[...], st_ref[...]
        p = (_dot_nt(qv, kv) * m).astype(BF16)
        qd = (qv.astype(F32) * dqc).astype(BF16)
        kd = (kv.astype(F32) * dkc).astype(BF16)
        dsn = ds_scr[...]
        dsb = dsn.astype(BF16)
        dsc = (_dot_nt(dyv, vv) * m).astype(BF16)
        dq = _dot(dsc, kv) + _dot_nt(dyv, spb) * dqc
        dk = _dot_tn(dsc, qv) + _dot_nt(vv, dsb) * dkc
        dv_ref[...] = (_dot_tn(p, dyv) + _dot(kd, dsb)).astype(BF16)
        ds_scr[...] = dsn * cdec + _dot_tn(qd, dyv)
        c, s = cos_ref[...], sin_ref[...]

        def unrot(d):
            d1, d2 = d[:, :half], d[:, half:]
            return jnp.concatenate([d1 * c + d2 * s, d2 * c - d1 * s], axis=-1)

        dq_ref[...] = unrot(dq).astype(BF16)
        dk_ref[...] = (unrot(dk) * (RET_DK ** -0.5)).astype(BF16)

    rev = lambda t: nt - 1 - t
    qkb = lambda off: pl.BlockSpec((RT, RET_DK), lambda h, t: (rev(t), off + h))
    vsb = lambda off: pl.BlockSpec((RT, RET_DV), lambda h, t: (rev(t), off + h))
    qk, vs = qkb(0), vsb(0)
    tab = pl.BlockSpec((RT, half), lambda h, t: (rev(t), 0))
    vec = pl.BlockSpec((1, RET_DV), lambda h, t: (0, h))
    return _pcall(
        body, (consts, h0, h0, h0, h0, do, y, gn_g, states, cos, sin), name="ret_bwd", grid=(RET_HEADS, nt),
        in_specs=[pl.BlockSpec(memory_space=pltpu.SMEM), qk, qkb(RET_HEADS), vsb(RET_HEADS), vsb(2 * RET_HEADS), vs, vs,
                  vec, pl.BlockSpec((None, None, RET_DK, RET_DV), lambda h, t: (h, rev(t), 0, 0)), tab, tab],
        out_specs=(qk, qk, vs, vs, vec),
        out_shape=(SDS((S, 2048), BF16), SDS((S, 2048), BF16), SDS((S, 4096), BF16), SDS((S, 4096), BF16),
                   SDS((1, 4096), F32)),
        scratch_shapes=[pltpu.VMEM((RET_DK, RET_DV), F32)], sem=("parallel", "arbitrary"), sides=sides)


def _gn_stats(y):
    mu = jnp.mean(y, -1, keepdims=True)
    yc = y - mu
    var = jnp.mean(yc * yc, -1, keepdims=True)
    rstd = lax.rsqrt(var + GN_EPS)
    return yc * rstd, rstd


CONV_RB, CONV_CB = 32, 512


def _conv_tiles(rows, cols):
    return [(r0, slice(c0, c0 + CONV_CB)) for r0 in range(0, rows, CONV_RB) for c0 in range(0, cols, CONV_CB)]


def _conv_fill(x_ref, halo_ref, ext_scr, t):
    ext_scr[0:8, :] = jnp.where(t == 0, 0.0, halo_ref[...])
    ext_scr[8:, :] = x_ref[...]


def _conv_tile(ext_scr, w, r0, cs):
    acc = w[3:4, cs] * ext_scr[pl.ds(8 + r0, CONV_RB), cs]
    for j in range(3):
        acc = acc + w[j:j + 1, cs] * ext_scr[pl.ds(5 + j + r0, CONV_RB), cs]
    return acc


def _gdn_conv_fwd(h1, conv_w, kind):
    S = h1.shape[0]
    base = {"q": 0, "k": 1, "v": 2}[kind]
    ncb = 2 if kind == "v" else 1
    C = 2048

    def body(x_ref, halo_ref, w_ref, o_ref, ext_scr):
        _conv_fill(x_ref, halo_ref, ext_scr, pl.program_id(0))
        w = w_ref[...]
        scale = GDN_D ** -0.5 if kind == "q" else 1.0
        for r0, cs in _conv_tiles(ROWS, C):
            rs = pl.ds(r0, CONV_RB)
            acc = _conv_tile(ext_scr, w, r0, cs)
            c = acc * _sigmoid(acc)
            if kind == "v":
                o_ref[rs, cs] = c.astype(BF16)
            else:
                for hh in range(CONV_CB // GDN_D):
                    ch = c[:, hh * GDN_D:(hh + 1) * GDN_D]
                    r = lax.rsqrt(jnp.sum(ch * ch, -1, keepdims=True) + L2_EPS)
                    o_ref[rs, pl.ds(cs.start + hh * GDN_D, GDN_D)] = (ch * (r * scale)).astype(BF16)

    hb = ROWS // 8
    return pl.pallas_call(
        body, name="gdn_conv_fwd_" + kind, grid=(S // ROWS, ncb),
        in_specs=[pl.BlockSpec((ROWS, C), lambda t, j: (t, base + j)),
                  pl.BlockSpec((8, C), lambda t, j: (jnp.maximum(t * hb - 1, 0), base + j)),
                  pl.BlockSpec((4, C), lambda t, j: (0, base + j))],
        out_specs=pl.BlockSpec((ROWS, C), lambda t, j: (t, j)), out_shape=SDS((S, C * ncb), BF16),
        scratch_shapes=[pltpu.VMEM((ROWS + 8, C), F32)], compiler_params=_cp("parallel", "parallel"),
    )(h1, h1, conv_w)


def _gdn_conv_bwd_act(h1, conv_w, dn, kind, buf=None):
    S = h1.shape[0]
    base = {"q": 0, "k": 1, "v": 2}[kind]
    ncb = 2 if kind == "v" else 1
    C = 2048

    def body(x_ref, halo_ref, w_ref, dn_ref, *rest):
        o_ref, ext_scr = rest[-2], rest[-1]
        _conv_fill(x_ref, halo_ref, ext_scr, pl.program_id(0))
        w = w_ref[...]
        scale = GDN_D ** -0.5 if kind == "q" else 1.0
        for r0, cs in _conv_tiles(ROWS, C):
            rs = pl.ds(r0, CONV_RB)
            acc = _conv_tile(ext_scr, w, r0, cs)
            sg = _sigmoid(acc)
            dsilu = sg * (1.0 + acc * (1.0 - sg))
            if kind == "v":
                o_ref[rs, cs] = dn_ref[rs, cs] * dsilu
            else:
                c = acc * sg
                for hh in range(CONV_CB // GDN_D):
                    sl = slice(hh * GDN_D, (hh + 1) * GDN_D)
                    gl = pl.ds(cs.start + hh * GDN_D, GDN_D)
                    ch, dnh = c[:, sl], dn_ref[rs, gl]
                    r = lax.rsqrt(jnp.sum(ch * ch, -1, keepdims=True) + L2_EPS)
                    proj = jnp.sum(dnh * ch, -1, keepdims=True)
                    o_ref[rs, gl] = (scale * r) * (dnh - ch * (proj * r * r)) * dsilu[:, sl]

    hb = ROWS // 8
    return pl.pallas_call(
        body, name="gdn_conv_bwd_act_" + kind, grid=(S // ROWS, ncb),
        in_specs=[pl.BlockSpec((ROWS, C), lambda t, j: (t, base + j)),
                  pl.BlockSpec((8, C), lambda t, j: (jnp.maximum(t * hb - 1, 0), base + j)),
                  pl.BlockSpec((4, C), lambda t, j: (0, base + j)),
                  pl.BlockSpec((ROWS, C), lambda t, j: (t, j))] + ([] if buf is None else [HBM_SPEC]),
        out_specs=pl.BlockSpec((ROWS, C), lambda t, j: (t, base + j)), out_shape=SDS((S, GDN_QKV), F32),
        input_output_aliases={} if buf is None else {4: 0},
        scratch_shapes=[pltpu.VMEM((ROWS + 8, C), F32)], compiler_params=_cp("parallel", "parallel"),
    )(*((h1, h1, conv_w, dn) + (() if buf is None else (buf,))))


def _gdn_conv_bwd_in(h1, conv_w, dacc, dh1_buf):
    S = h1.shape[0]
    C = 2048
    nt = S // ROWS
    hb = ROWS // 8

    def body(x_ref, halo_ref, w_ref, d_ref, dhalo_ref, buf_ref, di_ref, dw_ref, ext_scr, dext_scr):
        t = pl.program_id(1)
        _conv_fill(x_ref, halo_ref, ext_scr, t)
        dext_scr[0:ROWS, :] = d_ref[...]
        dext_scr[ROWS:, :] = jnp.where(t == nt - 1, 0.0, dhalo_ref[...])
        w = w_ref[...]

        @pl.when(t == 0)
        def _():
            dw_ref[...] = jnp.zeros_like(dw_ref)

        for c0 in range(0, C, CONV_CB):
            cs = slice(c0, c0 + CONV_CB)
            pw = [jnp.zeros((1, CONV_CB), F32) for _ in range(4)]
            for r0 in range(0, ROWS, CONV_RB):
                d = dext_scr[pl.ds(r0, CONV_RB), cs]
                di = w[3:4, cs] * d
                for j in range(3):
                    di = di + w[j:j + 1, cs] * dext_scr[pl.ds(3 - j + r0, CONV_RB), cs]
                di_ref[pl.ds(r0, CONV_RB), cs] = di.astype(BF16)
                for j in range(4):
                    pw[j] = pw[j] + jnp.sum(d * ext_scr[pl.ds(5 + j + r0, CONV_RB), cs], axis=0, keepdims=True)
            dw_ref[:, cs] += jnp.concatenate(pw, axis=0)

    return pl.pallas_call(
        body, name="gdn_conv_bwd_in", grid=(GDN_QKV // C, nt),
        in_specs=[pl.BlockSpec((ROWS, C), lambda j, t: (t, j)),
                  pl.BlockSpec((8, C), lambda j, t: (jnp.maximum(t * hb - 1, 0), j)),
                  pl.BlockSpec((4, C), lambda j, t: (0, j)),
                  pl.BlockSpec((ROWS, C), lambda j, t: (t, j)),
                  pl.BlockSpec((8, C), lambda j, t: (jnp.minimum((t + 1) * hb, nt * hb - 1), j)), HBM_SPEC],
        out_specs=(pl.BlockSpec((ROWS, C), lambda j, t: (t, j)), pl.BlockSpec((4, C), lambda j, t: (0, j))),
        out_shape=(SDS(dh1_buf.shape, BF16), SDS((4, GDN_QKV), F32)), input_output_aliases={5: 0},
        scratch_shapes=[pltpu.VMEM((ROWS + 8, C), F32), pltpu.VMEM((ROWS + 8, C), F32)],
        compiler_params=_cp("parallel", "arbitrary"),
    )(h1, h1, conv_w, dacc, dacc, dh1_buf)


def _chunk_masks():
    ri, ci = _iota((CT, CT), 0), _iota((CT, CT), 1)
    same = (ri >> 6) == (ci >> 6)
    return same, same & (ri >= ci), same & (ri > ci), same & (ri <= ci), ri == ci


def _fold_dup(m):
    h = m[:, :LANES] + m[:, LANES:]
    return h + pltpu.roll(h, CHUNK, axis=1)


def _unfold_bd(d, same):
    return jnp.where(same, jnp.concatenate([d, d], axis=1), 0.0)


def _softplus(x):
    return jnp.maximum(x, 0.0) + jnp.log(1.0 + jnp.exp(-jnp.abs(x)))


def _gdn_scal_fwd(ba, a_log, dt_bias):
    S = ba.shape[0]

    def body(ba_ref, al_ref, dt_ref, beta_ref, g_ref, gc_ref):
        bav = ba_ref[...]
        beta_ref[...] = _sigmoid(bav)
        a = pltpu.roll(bav, LANES - GDN_HV, axis=1)
        g = -jnp.exp(al_ref[...]) * _softplus(a + dt_ref[...])
        g_ref[...] = g
        causal = _chunk_masks()[1]
        gc_ref[...] = _dot_exact_l(causal.astype(BF16), g)

    row = pl.BlockSpec((CT, LANES), lambda t: (t, 0))
    vec = pl.BlockSpec((1, LANES), lambda t: (0, 0))
    return pl.pallas_call(
        body, name="gdn_scal_fwd", grid=(S // CT,), in_specs=[row, vec, vec], out_specs=(row, row, row),
        out_shape=(SDS((S, LANES), F32),) * 3, compiler_params=_cp("parallel"),
    )(ba, a_log, dt_bias)


def _gdn_scal_bwd(ba, a_log, dt_bias, g, dbeta, dgc):
    S = ba.shape[0]

    def body(ba_ref, al_ref, dt_ref, g_ref, dbeta_ref, dgc_ref, dba_ref, dal_ref, ddt_ref):
        t = pl.program_id(0)
        bav = ba_ref[...]
        beta = _sigmoid(bav)
        db = dbeta_ref[...] * beta * (1.0 - beta)
        a = pltpu.roll(bav, LANES - GDN_HV, axis=1)
        dgv = _dot_exact_l(_chunk_masks()[3].astype(BF16), dgc_ref[...])
        da = dgv * (-jnp.exp(al_ref[...])) * _sigmoid(a + dt_ref[...])
        lane = _iota(bav.shape, 1)
        da_sh = pltpu.roll(da, GDN_HV, axis=1)
        dba = jnp.where(lane < GDN_HV, db, jnp.where(lane < 2 * GDN_HV, da_sh, 0.0))
        dba_ref[...] = dba.astype(BF16)
        keep = lane < GDN_HV
        pal = jnp.sum(jnp.where(keep, dgv * g_ref[...], 0.0), axis=0, keepdims=True)
        pdt = jnp.sum(jnp.where(keep, da, 0.0), axis=0, keepdims=True)

        @pl.when(t == 0)
        def _():
            dal_ref[...] = pal
            ddt_ref[...] = pdt

        @pl.when(t > 0)
        def _():
            dal_ref[...] += pal
            ddt_ref[...] += pdt

    row = pl.BlockSpec((CT, LANES), lambda t: (t, 0))
    vec = pl.BlockSpec((1, LANES), lambda t: (0, 0))
    return pl.pallas_call(
        body, name="gdn_scal_bwd", grid=(S // CT,), in_specs=[row, vec, vec, row, row, row],
        out_specs=(row, vec, vec), out_shape=(SDS((S, LANES), BF16), SDS((1, LANES), F32), SDS((1, LANES), F32)),
        compiler_params=_cp("arbitrary"),
    )(ba, a_log, dt_bias, g, dbeta, dgc)


def _sel_col(x, h):
    return jnp.sum(jnp.where(_iota(x.shape, 1) == h, x, 0.0), axis=1, keepdims=True)


def _decay(gcol, causal):
    gm = jnp.broadcast_to(gcol, (CT, CT))
    diff = gm - gm.T
    return jnp.where(causal, jnp.exp(jnp.where(causal, diff, 0.0)), 0.0)


def _gdn_chunk_fwd(k, v, beta, gc):
    S = k.shape[0]
    nt = S // CT

    def body(k_ref, v_ref, beta_ref, gc_ref, t_ref, u_ref, w_ref):
        kg_id = pl.program_id(1)
        same, causal, strict, _, _ = _chunk_masks()
        eye_dup = jnp.where((_iota((CT, LANES), 0) & (CHUNK - 1)) == (_iota((CT, LANES), 1) & (CHUNK - 1)), 1.0, 0.0)
        kfs, xds, pds, cols = [], [], [], []
        for kb in range(KB):
            kv = k_ref[:, kb * GDN_D:(kb + 1) * GDN_D]
            kfs.append(kv.astype(F32))
            kk = _dot_nt(kv, kv)
            for hp in range(HP):
                h = NH * kg_id + kb * HP + hp
                bcol, gcol = _sel_col(beta_ref[...], h), _sel_col(gc_ref[...], h)
                x = jnp.where(strict, -(kk * bcol) * _decay(gcol, causal), 0.0)
                xds.append(_fold_dup(x))
                pds.append(eye_dup)
                cols.append((bcol, gcol))
        for m in range(6):
            for hi in range(NH):
                xh, xl = _split2(xds[hi])
                ph, pl_ = _split2(pds[hi])
                lh, ll = _unfold_bd(xh, same), _unfold_bd(xl, same)
                rh, rl = jnp.concatenate([xh, ph], axis=1), jnp.concatenate([xl, pl_], axis=1)
                out = _dot(lh, rh) + (_dot(lh, rl) + _dot(ll, rh))
                pds[hi] = pds[hi] + out[:, LANES:]
                if m < 5:
                    xds[hi] = out[:, :LANES]
        for hi in range(NH):
            bcol, gcol = cols[hi]
            cs = slice(hi * GDN_D, (hi + 1) * GDN_D)
            t_ref[hi] = pds[hi]
            tb = _unfold_bd(pds[hi], same).astype(BF16)
            vb = (v_ref[:, cs].astype(F32) * bcol).astype(BF16)
            kg = (kfs[hi // HP] * (bcol * jnp.exp(gcol))).astype(BF16)
            uw = _dot(tb, jnp.concatenate([vb, kg], axis=1))
            u_ref[:, cs] = uw[:, :GDN_D]
            w_ref[:, cs] = uw[:, GDN_D:].astype(BF16)

    col = pl.BlockSpec((CT, LANES), lambda t, kg: (t, 0))
    hv = pl.BlockSpec((CT, NH * GDN_D), lambda t, kg: (t, kg))
    return pl.pallas_call(
        body, name="gdn_chunk_fwd", grid=(nt, GDN_HV // NH),
        in_specs=[pl.BlockSpec((CT, KB * GDN_D), lambda t, kg: (t, kg)), hv, col, col],
        out_specs=(pl.BlockSpec((NH, None, CT, LANES), lambda t, kg: (kg, t, 0, 0)), hv, hv),
        out_shape=(SDS((GDN_HV, nt, CT, LANES), F32), SDS((S, 4096), F32), SDS((S, 4096), BF16)),
        compiler_params=_cp("parallel", "parallel"),
    )(k, v, beta, gc)


def _last_of_chunk(gcol, rows, c):
    return jnp.sum(jnp.where(rows == c * CHUNK + CHUNK - 1, gcol, 0.0), axis=0, keepdims=True)


def _gdn_scan_fwd(q, k, u, w, gc):
    S = q.shape[0]
    nt = S // CT
    ncs = CT // CHUNK

    def body(q_ref, k_ref, u_ref, w_ref, gc_ref, y_ref, vn_ref, st_ref, s_scr, vn_scr):
        kg_id, t = pl.program_id(0), pl.program_id(1)

        @pl.when(t == 0)
        def _():
            s_scr[...] = jnp.zeros_like(s_scr)

        causal = _chunk_masks()[1]
        rows = _iota((CT, 1), 0)
        heads = []
        for kb in range(KB):
            ks = slice(kb * GDN_D, (kb + 1) * GDN_D)
            qv, kv = q_ref[:, ks], k_ref[:, ks]
            qf, kf = qv.astype(F32), kv.astype(F32)
            qk = _dot_nt(qv, kv)
            for hp in range(HP):
                gcol = _sel_col(gc_ref[...], NH * kg_id + kb * HP + hp)
                heads.append((gcol, (qk * _decay(gcol, causal)).astype(BF16), (qf * jnp.exp(gcol)).astype(BF16), kf))
        vn_scr[...] = jnp.zeros_like(vn_scr)
        for c in range(ncs):
            r = slice(c * CHUNK, (c + 1) * CHUNK)
            for hi in range(NH):
                gcol, attn, qd, kf = heads[hi]
                cs = slice(hi * GDN_D, (hi + 1) * GDN_D)
                s = s_scr[hi]
                st_ref[hi, c] = s
                sb = s.astype(BF16)
                gl = _last_of_chunk(gcol, rows, c)
                kd = (kf[r] * jnp.exp(gl - gcol[r])).astype(BF16)
                vn = (u_ref[r, cs] - _dot(w_ref[r, cs], sb)).astype(BF16)
                vn_scr[r, cs] = vn
                y_ref[r, cs] = _dot(qd[r], sb) + _dot(attn[r], vn_scr[:, cs])
                s_scr[hi] = s * jnp.exp(gl) + _dot_tn(kd, vn)
        vn_ref[...] = vn_scr[...]

    hk = pl.BlockSpec((CT, KB * GDN_D), lambda kg, t: (t, kg))
    hv = pl.BlockSpec((CT, NH * GDN_D), lambda kg, t: (t, kg))
    col = pl.BlockSpec((CT, LANES), lambda kg, t: (t, 0))
    return pl.pallas_call(
        body, name="gdn_scan_fwd", grid=(GDN_HV // NH, nt), in_specs=[hk, hk, hv, hv, col],
        out_specs=(hv, hv, pl.BlockSpec((NH, ncs, GDN_D, GDN_D), lambda kg, t: (kg, t, 0, 0))),
        out_shape=(SDS((S, 4096), F32), SDS((S, 4096), BF16), SDS((GDN_HV, S // CHUNK, GDN_D, GDN_D), F32)),
        scratch_shapes=[pltpu.VMEM((NH, GDN_D, GDN_D), F32), pltpu.VMEM((CT, NH * GDN_D), BF16)],
        compiler_params=_cp("parallel", "arbitrary"),
    )(q, k, u, w, gc)


def _gdn_scan_bwd(q, k, w, vn, gc, states, dy, sides=()):
    S = q.shape[0]
    nt = S // CT
    ncs = CT // CHUNK

    def body(q_ref, k_ref, w_ref, vn_ref, gc_ref, st_ref, dy_ref, du_ref, dw_ref, dq_ref, dk_ref, dgc_ref, ds_scr):
        kg_id, t = pl.program_id(0), pl.program_id(1)

        @pl.when(t == 0)
        def _():
            ds_scr[...] = jnp.zeros_like(ds_scr)

        _, causal, _, _, eye = _chunk_masks()
        rows = _iota((CT, 1), 0)
        heads = []
        for kb in range(KB):
            ks = slice(kb * GDN_D, (kb + 1) * GDN_D)
            qv, kv = q_ref[:, ks], k_ref[:, ks]
            qf, kf = qv.astype(F32), kv.astype(F32)
            qk = _dot_nt(qv, kv)
            for hp in range(HP):
                hi = kb * HP + hp
                cs = slice(hi * GDN_D, (hi + 1) * GDN_D)
                gcol = _sel_col(gc_ref[...], NH * kg_id + hi)
                dm = _decay(gcol, causal)
                attn_f = qk * dm
                egc = jnp.exp(gcol)
                qd_f = qf * egc
                dyv, vnv = dy_ref[:, cs], vn_ref[:, cs]
                heads.append(dict(cs=cs, ks=ks, gcol=gcol, dm=dm, attn_f=attn_f, egc=egc, qd_f=qd_f, kf=kf, qv=qv, kv=kv,
                                  qd=qd_f.astype(BF16), dy=dyv, vn=vnv, dattn=_dot_nt(dyv, vnv),
                                  at_dy=_dot_tn(attn_f.astype(BF16), dyv), dgc=[None] * ncs))
        dq_ref[...] = jnp.zeros_like(dq_ref)
        dk_ref[...] = jnp.zeros_like(dk_ref)
        for c in reversed(range(ncs)):
            r = slice(c * CHUNK, (c + 1) * CHUNK)
            for hi in range(NH):
                hd = heads[hi]
                cs, ks, gcol = hd["cs"], hd["ks"], hd["gcol"]
                s = st_ref[hi, c]
                sb = s.astype(BF16)
                dsn = ds_scr[hi]
                dsb = dsn.astype(BF16)
                gl = _last_of_chunk(gcol, rows, c)
                cd = jnp.exp(gl)
                ekd = jnp.exp(gl - gcol[r])
                kd_f = hd["kf"][r] * ekd
                dvn = (hd["at_dy"][r] + _dot(kd_f.astype(BF16), dsb)).astype(BF16)
                dqd = _dot_nt(hd["dy"][r], sb)
                dkd = _dot_nt(hd["vn"][r], dsb)
                dcd = jnp.sum(jnp.sum(s * dsn, axis=1, keepdims=True), axis=0, keepdims=True)
                ds_scr[hi] = dsn * cd + _dot_tn(hd["qd"][r], hd["dy"][r]) - _dot_tn(w_ref[r, cs], dvn)
                du_ref[r, cs] = dvn
                dw_ref[r, cs] = (-_dot_nt(dvn, sb)).astype(BF16)
                dq_ref[r, ks] += dqd * hd["egc"][r]
                dk_ref[r, ks] += dkd * ekd
                rs_q = jnp.sum(dqd * hd["qd_f"][r], axis=1, keepdims=True)
                rs_k = jnp.sum(dkd * kd_f, axis=1, keepdims=True)
                tot = jnp.sum(rs_k, axis=0, keepdims=True) + dcd * cd
                hd["dgc"][c] = rs_q - rs_k + jnp.where(rows[r] == c * CHUNK + CHUNK - 1, tot, 0.0)
        for hi in range(NH):
            hd = heads[hi]
            ks = hd["ks"]
            dab = (hd["dattn"] * hd["dm"]).astype(BF16)
            dq_ref[:, ks] += _dot(dab, hd["kv"])
            dk_ref[:, ks] += _dot_tn(dab, hd["qv"])
            e1 = hd["dattn"] * hd["attn_f"]
            dgc = (jnp.concatenate(hd["dgc"], axis=0) + jnp.sum(e1, axis=1, keepdims=True)
                   - jnp.sum(e1.T, axis=1, keepdims=True))
            dgc_ref[hi] = jnp.sum(jnp.where(eye, jnp.broadcast_to(dgc, (CT, CT)), 0.0), axis=0, keepdims=True)

    rev = lambda t: nt - 1 - t
    hk = pl.BlockSpec((CT, KB * GDN_D), lambda kg, t: (rev(t), kg))
    hv = pl.BlockSpec((CT, NH * GDN_D), lambda kg, t: (rev(t), kg))
    col = pl.BlockSpec((CT, LANES), lambda kg, t: (rev(t), 0))
    return _pcall(
        body, (q, k, w, vn, gc, states, dy), name="gdn_scan_bwd", grid=(GDN_HV // NH, nt),
        in_specs=[hk, hk, hv, hv, col, pl.BlockSpec((NH, ncs, GDN_D, GDN_D), lambda kg, t: (kg, rev(t), 0, 0)), hv],
        out_specs=(hv, hv, hk, hk, pl.BlockSpec((NH, 1, CT), lambda kg, t: (kg, 0, rev(t)))),
        out_shape=(SDS((S, 4096), BF16), SDS((S, 4096), BF16), SDS((S, 2048), F32), SDS((S, 2048), F32),
                   SDS((GDN_HV, 1, S), F32)),
        scratch_shapes=[pltpu.VMEM((NH, GDN_D, GDN_D), F32)], sem=("parallel", "arbitrary"), sides=sides)


def _gdn_chunk_bwd(k, v, beta, gc, tmat, du, dw, dk_p, dgc_p, sides=()):
    S = k.shape[0]
    nt = S // CT

    def body(k_ref, v_ref, beta_ref, gc_ref, t_ref, du_ref, dw_ref, dkp_ref, dgcp_ref,
             dk_ref, dv_ref, dbeta_ref, dgc_ref):
        kg_id = pl.program_id(1)
        same, causal, strict, _, eye = _chunk_masks()
        lane = _iota((CT, LANES), 1)

        @pl.when(kg_id == 0)
        def _():
            dbeta_ref[...] = jnp.zeros_like(dbeta_ref)
            dgc_ref[...] = jnp.zeros_like(dgc_ref)

        for kb_i in range(KB):
            ks = slice(kb_i * GDN_D, (kb_i + 1) * GDN_D)
            kv = k_ref[:, ks]
            kf = kv.astype(F32)
            kk = _dot_nt(kv, kv)
            dk = dkp_ref[:, ks]
            for hp in range(HP):
                hi = kb_i * HP + hp
                h = NH * kg_id + hi
                cs = slice(hi * GDN_D, (hi + 1) * GDN_D)
                bcol, gcol = _sel_col(beta_ref[...], h), _sel_col(gc_ref[...], h)
                dm = _decay(gcol, causal)
                vf = v_ref[:, cs].astype(F32)
                kb = kf * bcol
                a = jnp.where(strict, (kk * bcol) * dm, 0.0)
                egc = jnp.exp(gcol)
                kg_f = kb * egc
                tb = _unfold_bd(t_ref[hi], same).astype(BF16)
                duw = jnp.concatenate([du_ref[:, cs], dw_ref[:, cs]], axis=1)
                dt = _dot_nt(duw, jnp.concatenate([(vf * bcol).astype(BF16), kg_f.astype(BF16)], axis=1))
                dvb_dkg = _dot_tn(tb, duw)
                dvb, dkg = dvb_dkg[:, :GDN_D], dvb_dkg[:, GDN_D:]
                da = -_dot_nt(_dot_tn(tb, dt.astype(BF16)).astype(BF16), tb)
                rm = jnp.where(strict, da, 0.0)
                rdb = (rm * dm).astype(BF16)
                dkb = _dot(rdb, kv) + dkg * egc
                dk = dk + _dot_tn(rdb, kb.astype(BF16)) + dkb * bcol
                e2 = rm * a
                dgc_in = jnp.sum(jnp.where(eye, jnp.broadcast_to(dgcp_ref[hi], (CT, CT)), 0.0), axis=1, keepdims=True)
                dgc = (jnp.sum(e2, axis=1, keepdims=True) - jnp.sum(e2.T, axis=1, keepdims=True)
                       + jnp.sum(dkg * kg_f, axis=1, keepdims=True) + dgc_in)
                dbeta = jnp.sum(dkb * kf, axis=1, keepdims=True) + jnp.sum(dvb * vf, axis=1, keepdims=True)
                dv_ref[:, cs] = dvb * bcol
                dbeta_ref[...] += jnp.where(lane == h, dbeta, 0.0)
                dgc_ref[...] += jnp.where(lane == h, dgc, 0.0)
            dk_ref[:, ks] = dk

    hk = pl.BlockSpec((CT, KB * GDN_D), lambda t, kg: (t, kg))
    hv = pl.BlockSpec((CT, NH * GDN_D), lambda t, kg: (t, kg))
    col = pl.BlockSpec((CT, LANES), lambda t, kg: (t, 0))
    return _pcall(
        body, (k, v, beta, gc, tmat, du, dw, dk_p, dgc_p), name="gdn_chunk_bwd", grid=(nt, GDN_HV // NH),
        in_specs=[hk, hv, col, col, pl.BlockSpec((NH, None, CT, LANES), lambda t, kg: (kg, t, 0, 0)), hv, hv, hk,
                  pl.BlockSpec((NH, 1, CT), lambda t, kg: (kg, 0, t))],
        out_specs=(hk, hv, col, col),
        out_shape=(SDS((S, 2048), F32), SDS((S, 4096), F32), SDS((S, LANES), F32), SDS((S, LANES), F32)),
        sem=("parallel", "arbitrary"), sides=sides)


def _gdn_post_fwd(y, h1, norm_g):
    S = y.shape[0]

    def body(y_ref, z_ref, g_ref, o_ref):
        g = g_ref[...]
        for hh in range(GDN_HV):
            sl = slice(hh * GDN_D, (hh + 1) * GDN_D)
            yh, zh = y_ref[:, sl], z_ref[:, sl]
            yn = yh * lax.rsqrt(jnp.mean(yh * yh, -1, keepdims=True) + RMS_EPS)
            o_ref[:, sl] = (yn * g * (zh * _sigmoid(zh))).astype(BF16)

    row = lambda off: pl.BlockSpec((ROWS, 4096), lambda t: (t, off))
    return pl.pallas_call(
        body, name="gdn_post_fwd", grid=(S // ROWS,),
        in_specs=[row(0), row(2), pl.BlockSpec((1, GDN_D), lambda t: (0, 0))], out_specs=row(0),
        out_shape=SDS((S, 4096), BF16), compiler_params=_cp("parallel"),
    )(y, h1, norm_g)


def _gdn_post_bwd(do, y, h1, norm_g, sides=()):
    S = y.shape[0]

    def body(do_ref, y_ref, z_ref, g_ref, dy_ref, dz_ref, dg_ref):
        t = pl.program_id(0)
        g = g_ref[...]
        pg = jnp.zeros((1, GDN_D), F32)
        for hh in range(GDN_HV):
            sl = slice(hh * GDN_D, (hh + 1) * GDN_D)
            yh, zh, doh = y_ref[:, sl], z_ref[:, sl], do_ref[:, sl]
            rstd = lax.rsqrt(jnp.mean(yh * yh, -1, keepdims=True) + RMS_EPS)
            yn = yh * rstd
            sg = _sigmoid(zh)
            dz_ref[:, sl] = (doh * (yn * g) * (sg * (1.0 + zh * (1.0 - sg)))).astype(BF16)
            dyg = doh * (zh * sg)
            dyn = dyg * g
            dy_ref[:, sl] = (rstd * (dyn - yn * jnp.mean(dyn * yn, -1, keepdims=True))).astype(BF16)
            pg = pg + jnp.sum(dyg * yn, axis=0, keepdims=True)

        @pl.when(t == 0)
        def _():
            dg_ref[...] = pg

        @pl.when(t > 0)
        def _():
            dg_ref[...] += pg

    row = lambda off: pl.BlockSpec((ROWS, 4096), lambda t: (t, off))
    vec = pl.BlockSpec((1, GDN_D), lambda t: (0, 0))
    return _pcall(
        body, (do, y, h1, norm_g), name="gdn_post_bwd", grid=(S // ROWS,), in_specs=[row(0), row(0), row(2), vec],
        out_specs=(row(0), row(2), vec),
        out_shape=(SDS((S, 4096), BF16), SDS((S, 3 * 4096), BF16), SDS((1, GDN_D), F32)),
        sem=("arbitrary",), sides=sides)


def _cols(g):
    return jnp.transpose(g, (1, 0, 2)).reshape(g.shape[1], -1)


def _rows(g):
    return g.reshape(-1, g.shape[-1])


def _local_step(x, tgt, sh, small):
    S = x.shape[0]
    xb = x.astype(BF16)
    rc = _ret_consts()
    cos, sin = _rope_tables(S)

    natural = ("ret_w_in", "mlp_w1_0", "mlp_w1_1", "conv_w")

    def gather(names):
        return _ag_side([sh[k] for k in names], [k != "conv_w" for k in names], [k in natural for k in names])

    (wri,) = _comm_call("ag_ret_in", gather(["ret_w_in"]))
    h0, ((g_ro, w1_0, g_go),) = _mm(xb, wri, "nn", "mm_ret_in", out_dtype=BF16, epi="rope", extra=(cos, sin),
                                    sides=[gather(["ret_w_out", "mlp_w1_0", "gdn_w_out"])])
    wro, wgo, w1 = _rows(g_ro), _rows(g_go), [w1_0, None]
    (yr, o0, ret_st), ((g_w20, conv_w),) = _ret_fwd(h0, small["ret_gn_g"], rc, sides=[gather(["mlp_w2_0", "conv_w"])])
    w2 = [_rows(g_w20), None]
    mix0 = _mm(o0, wro, "nn", "mm_ret_out")
    x1, x1b, z1 = _ln_fwd(x, mix0, small["ln_mix_g"][0:1], small["ln_mix_b"][0:1], "ln_mix0_fwd")
    (hh0, a0), ((g_gi,),) = _mm(x1b, w1[0], "nn", "mm_mlp0_up", epi="relu2", sides=[gather(["gdn_w_in"])])
    wgi = _cols(g_gi)
    wgi_main = wgi[:, :GDN_QKV + 4096]
    wba = jnp.pad(wgi[:, GDN_QKV + 4096:], ((0, 0), (0, LANES - 2 * GDN_HV)))
    m0, ((w1[1],),) = _mm(a0, w2[0], "nn", "mm_mlp0_down", sides=[gather(["mlp_w1_1"])])
    x2, x2b, z2 = _ln_fwd(x1, m0, small["ln_ffn_g"][0:1], small["ln_ffn_b"][0:1], "ln_ffn0_fwd")

    h1, ((g_w21,),) = _mm(x2b, wgi_main, "nn", "mm_gdn_in", sides=[gather(["mlp_w2_1"])])
    w2[1] = _rows(g_w21)
    ba = _mm(x2b, wba, "nn", "mm_gdn_ba")
    qn = _gdn_conv_fwd(h1, conv_w, "q")
    kn = _gdn_conv_fwd(h1, conv_w, "k")
    vg = _gdn_conv_fwd(h1, conv_w, "v")
    beta, g, gc = _gdn_scal_fwd(ba, small["a_log"], small["dt_bias"])
    tmat, u, w = _gdn_chunk_fwd(kn, vg, beta, gc)
    yg, vn, gdn_st = _gdn_scan_fwd(qn, kn, u, w, gc)
    o1 = _gdn_post_fwd(yg, h1, small["norm_g"])
    mix1 = _mm(o1, wgo, "nn", "mm_gdn_out")
    x3, x3b, z3 = _ln_fwd(x2, mix1, small["ln_mix_g"][1:2], small["ln_mix_b"][1:2], "ln_mix1_fwd")
    hh1, a1 = _mm(x3b, w1[1], "nn", "mm_mlp1_up", epi="relu2")
    z4 = _mm(a1, w2[1], "nn", "mm_mlp1_down", epi="add", extra=x3, scale=ALPHA)

    loss, dz4, dz4b, d_lnf_g1, d_lnf_b1 = _loss_ln_bwd(z4, small["ln_ffn_g"][1:2], small["ln_ffn_b"][1:2], tgt,
                                                       "loss_ln_ffn1_bwd")
    dhh1 = _mm(dz4b, w2[1], "nt", "mm_mlp1_down_dx", epi="drelu2", extra=hh1, out_dtype=BF16)
    dw2_1 = _mm(a1, dz4b, "tn", "mm_mlp1_down_dw")
    dx3 = _mm(dhh1, w1[1], "nt", "mm_mlp1_up_dx", epi="add", extra=dz4, scale=ALPHA)
    dw1_1 = _mm(x3b, dhh1, "tn", "mm_mlp1_up_dw", shard_major=True)
    dz3, dz3b, d_lnm_g1, d_lnm_b1 = _ln_bwd(dx3, z3, small["ln_mix_g"][1:2], "ln_mix1_bwd")
    shards_of = lambda g: g.reshape(N_CHIPS, -1, g.shape[-1])
    g_a = [dw1_1, shards_of(dw2_1)]
    do1, (th_a,) = _mm(dz3b, wgo, "nt", "mm_gdn_out_dx", sides=[_rs_swap_side(g_a)])
    sums_a = _rs_add(g_a, th_a, "a")
    dwgo = _mm(o1, dz3b, "tn", "mm_gdn_out_dw")
    dyg, dh1_z, d_norm_g = _gdn_post_bwd(do1, yg, h1, small["norm_g"])
    du, dw, dqn, dk_p, dgc_p = _gdn_scan_bwd(qn, kn, w, vn, gc, gdn_st, dyg)
    dkn, dvg, dbeta, dgc = _gdn_chunk_bwd(kn, vg, beta, gc, tmat, du, dw, dk_p, dgc_p)
    dba, d_a_log, d_dt_bias = _gdn_scal_bwd(ba, small["a_log"], small["dt_bias"], g, dbeta, dgc)
    dacc = _gdn_conv_bwd_act(h1, conv_w, dqn, "q")
    dacc = _gdn_conv_bwd_act(h1, conv_w, dkn, "k", dacc)
    dacc = _gdn_conv_bwd_act(h1, conv_w, dvg, "v", dacc)
    dh1, d_conv_w = _gdn_conv_bwd_in(h1, conv_w, dacc, dh1_z)
    g_go = [shards_of(dwgo)]
    dx2_ba, (th_go,) = _mm(dba, wba, "nt", "mm_gdn_ba_dx", epi="add", extra=dz3, scale=ALPHA,
                           sides=[_rs_swap_side(g_go)])
    sums_a = sums_a + _rs_add(g_go, th_go, "go")
    dx2, (parts_a,) = _mm(dh1, wgi_main, "nt", "mm_gdn_in_dx", epi="add", extra=dx2_ba,
                          sides=[_rs_owner_side(sums_a)])
    mine_a = _rs_sum(parts_a, "a")
    dwgi_main, (back_a,) = _mm(x2b, dh1, "tn", "mm_gdn_in_dw", sides=[_rs_back_side(mine_a)])
    red_w1_1, red_w2_1, red_go = zip(mine_a, back_a)
    dwba = _mm(x2b, dba, "tn", "mm_gdn_ba_dw")
    dwgi = jnp.concatenate([dwgi_main, dwba[:, :2 * GDN_HV]], axis=1)
    g_b = [jnp.transpose(dwgi.reshape(dwgi.shape[0], N_CHIPS, -1), (1, 0, 2))]

    dz2, dz2b, d_lnf_g0, d_lnf_b0 = _ln_bwd(dx2, z2, small["ln_ffn_g"][0:1], "ln_ffn0_bwd")
    dhh0, (th_b,) = _mm(dz2b, w2[0], "nt", "mm_mlp0_down_dx", epi="drelu2", extra=hh0, out_dtype=BF16,
                        sides=[_rs_swap_side(g_b)])
    sums_b = _rs_add(g_b, th_b, "b")
    dw2_0 = _mm(a0, dz2b, "tn", "mm_mlp0_down_dw")
    dx1, (parts_b,) = _mm(dhh0, w1[0], "nt", "mm_mlp0_up_dx", epi="add", extra=dz2, scale=ALPHA,
                          sides=[_rs_owner_side(sums_b)])
    mine_b = _rs_sum(parts_b, "b")
    dw1_0 = _mm(x1b, dhh0, "tn", "mm_mlp0_up_dw", shard_major=True)
    dz1, dz1b, d_lnm_g0, d_lnm_b0 = _ln_bwd(dx1, z1, small["ln_mix_g"][0:1], "ln_mix0_bwd")
    g_c = [dw1_0, shards_of(dw2_0)]
    do0, (th_c, back_b) = _mm(dz1b, wro, "nt", "mm_ret_out_dx", sides=[_rs_swap_side(g_c), _rs_back_side(mine_b)])
    (red_gi,) = zip(mine_b, back_b)
    sums_c = _rs_add(g_c, th_c, "c")
    dwro = _mm(o0, dz1b, "tn", "mm_ret_out_dw")
    g_ro = [shards_of(dwro)]
    (dq0, dk0, dv0, dgate, d_gn_g), (th_ro, parts_c) = _ret_bwd(
        h0, do0, yr, small["ret_gn_g"], ret_st, rc, cos, sin, sides=[_rs_swap_side(g_ro), _rs_owner_side(sums_c)])
    sums_ro = _rs_add(g_ro, th_ro, "ro")
    mine_c = _rs_sum(parts_c, "c")
    dh0 = jnp.concatenate([dq0, dk0, dv0, dgate], axis=1)
    dwri, (parts_ro, back_c) = _mm(xb, dh0, "tn", "mm_ret_in_dw", shard_major=True,
                                   sides=[_rs_owner_side(sums_ro), _rs_back_side(mine_c)])
    red_w1_0, red_w2_0 = zip(mine_c, back_c)
    mine_ro = _rs_sum(parts_ro, "ro")
    g_d = [dwri]
    sums_d = _rs_add(g_d, _comm_call("rs_swap_halves_d", _rs_swap_side(g_d)), "d")
    grad_x, (parts_d, back_ro) = _mm(dh0, wri, "nt", "mm_ret_in_dx", epi="add", extra=dz1, scale=ALPHA,
                                     sides=[_rs_owner_side(sums_d), _rs_back_side(mine_ro)])
    (red_ro,) = zip(mine_ro, back_ro)
    mine_d = _rs_sum(parts_d, "d")
    (red_ri,) = zip(mine_d, _comm_call("rs_swap_reduced_d", _rs_back_side(mine_d)))

    big = dict(ret_w_in=red_ri, ret_w_out=red_ro, gdn_w_in=red_gi, gdn_w_out=red_go,
               mlp_w1=(red_w1_0, red_w1_1), mlp_w2=(red_w2_0, red_w2_1))
    sm = dict(ret_gn_g=d_gn_g, a_log=d_a_log, dt_bias=d_dt_bias, norm_g=d_norm_g,
              ln_mix_g=jnp.concatenate([d_lnm_g0, d_lnm_g1], 0), ln_mix_b=jnp.concatenate([d_lnm_b0, d_lnm_b1], 0),
              ln_ffn_g=jnp.concatenate([d_lnf_g0, d_lnf_g1], 0), ln_ffn_b=jnp.concatenate([d_lnf_b0, d_lnf_b1], 0),
              conv_w=d_conv_w)
    return loss, grad_x, big, sm


def _coords():
    return lax.axis_index("x"), lax.axis_index("y"), lax.axis_index("c")


HBM_SPEC = pl.BlockSpec(memory_space=pl.ANY)


def _other_chips(x, y):
    return [(1 - x, y), (x, 1 - y), (1 - x, 1 - y)]


def _ag_side(shards, split, cols):
    n = len(shards)

    def rows_of(p, core):
        half = shards[p].shape[0] // 2
        return pl.ds(core * half, half) if split[p] else slice(None)

    def slot(outs, p, s, core=None):
        r = slice(None) if core is None else rows_of(p, core)
        if cols[p]:
            w = shards[p].shape[1]
            return outs[p].at[r, pl.ds(pl.multiple_of(s * w, LANES), w)]
        return outs[p].at[s, r]

    def over_ici(xyc, ins, outs):
        x, y, c = xyc
        return [(ins[p].at[rows_of(p, c)], slot(outs, p, 2 * x + y, c), (cx, cy, c))
                for p in range(n) for cx, cy in _other_chips(x, y)]

    def to_sibling(xyc, ins, outs):
        x, y, c = xyc
        zones = [slot(outs, p, 2 * cx + cy, c) for p in range(n) if split[p] for cx, cy in _other_chips(x, y)]
        return [(z, z, (x, y, 1 - c)) for z in zones]

    def own(xyc, ins, outs):
        x, y, _ = xyc
        return [(ins[p], slot(outs, p, 2 * x + y)) for p in range(n)]

    n_split = sum(bool(s) for s in split)
    phases, counts = [over_ici], [3 * n]
    if n_split:
        phases, counts = phases + [to_sibling], counts + [3 * n_split]
    shapes = [SDS((s.shape[0], N_CHIPS * s.shape[1]) if cols[p] else (N_CHIPS,) + s.shape, s.dtype)
              for p, s in enumerate(shards)]
    return _Side(shards, shapes, phases, counts, local=own, n_local=n)


def _rs_swap_side(grads):
    n = len(grads)
    halves = [g.shape[1] // 2 for g in grads]

    def swap(xyc, ins, outs):
        x, y, c = xyc
        return [(ins[p].at[:, pl.ds((1 - c) * halves[p], halves[p])], outs[p], (x, y, 1 - c)) for p in range(n)]

    return _Side(grads, [SDS((N_CHIPS, halves[p]) + g.shape[2:], F32) for p, g in enumerate(grads)], [swap], [n])


def _rs_add(grads, theirs, tag):
    c = lax.axis_index("c")
    return [_add_half(g, t, c, "rs_add_%s%d" % (tag, p)) for p, (g, t) in enumerate(zip(grads, theirs))]


def _rs_owner_side(chip_sums):
    n = len(chip_sums)

    def to_owner(xyc, ins, outs):
        x, y, c = xyc
        return [(ins[p].at[2 * cx + cy], outs[p].at[2 * x + y], (cx, cy, c))
                for p in range(n) for cx, cy in _other_chips(x, y)]

    def own(xyc, ins, outs):
        x, y, _ = xyc
        return [(ins[p].at[2 * x + y], outs[p].at[2 * x + y]) for p in range(n)]

    return _Side(chip_sums, [SDS(s.shape, s.dtype) for s in chip_sums], [to_owner], [3 * n], local=own, n_local=n)


def _rs_sum(parts, tag):
    return [_sum_chips(pt, "rs_sum_%s%d" % (tag, p)) for p, pt in enumerate(parts)]


def _rs_back_side(mine):
    n = len(mine)

    def swap(xyc, ins, outs):
        x, y, c = xyc
        return [(ins[p], outs[p], (x, y, 1 - c)) for p in range(n)]

    return _Side(mine, [SDS(m.shape, F32) for m in mine], [swap], [n])


def _add_half(g, theirs, c, name):
    _, R, C = g.shape
    half = R // 2
    tr = min(256, half)
    nb = half // tr

    def body(c_ref, g_ref, t_ref, o_ref):
        o_ref[...] = (g_ref[...] + t_ref[...]).astype(BF16)

    blk = pl.BlockSpec((None, tr, C), lambda s, i, c_ref: (s, i, 0))
    return pl.pallas_call(
        body, name=name,
        grid_spec=pltpu.PrefetchScalarGridSpec(
            num_scalar_prefetch=1, grid=(N_CHIPS, nb),
            in_specs=[pl.BlockSpec((None, tr, C), lambda s, i, c_ref: (s, c_ref[0] * nb + i, 0)), blk],
            out_specs=blk),
        out_shape=SDS((N_CHIPS, half, C), BF16), compiler_params=_cp("parallel", "parallel"),
    )(jnp.reshape(c, (1,)).astype(jnp.int32), g, theirs)


def _sum_chips(parts, name):
    _, r, C = parts.shape
    tr = min(256, r)

    def body(p_ref, o_ref):
        f = lambda s: p_ref[s].astype(F32)
        o_ref[...] = ((f(0) + f(1)) + f(2)) + f(3)

    return pl.pallas_call(
        body, name=name, grid=(r // tr,), in_specs=[pl.BlockSpec((N_CHIPS, tr, C), lambda i: (0, i, 0))],
        out_specs=pl.BlockSpec((tr, C), lambda i: (i, 0)), out_shape=SDS((r, C), F32), compiler_params=_cp("parallel"),
    )(parts)


def _all_reduce_small(buf):
    rows = buf.shape[0]

    def body(x_ref, o_ref, all_ref, send_sems, recv_sems):
        x, y, c = _coords()
        me = 4 * x + 2 * y + c
        all_ref[me] = x_ref[...]
        flips = [(fx, fy, fc) for fx in (0, 1) for fy in (0, 1) for fc in (0, 1)][1:]
        copies = []
        for k, (fx, fy, fc) in enumerate(flips):
            to = (x ^ fx, y ^ fy, c ^ fc)
            copies.append(pltpu.make_async_remote_copy(src_ref=x_ref, dst_ref=all_ref.at[me], send_sem=send_sems.at[k],
                                                       recv_sem=recv_sems.at[k], device_id=to, device_id_type=MESH))
        for cp in copies:
            cp.start()
        for cp in copies:
            cp.wait_recv()
        for cp in copies:
            cp.wait_send()
        acc = all_ref[0]
        for d in range(1, N_DEV):
            acc = acc + all_ref[d]
        o_ref[...] = acc

    vm = pl.BlockSpec(memory_space=pltpu.VMEM)
    return pl.pallas_call(
        body, name="all_reduce_small", in_specs=[vm], out_specs=vm, out_shape=SDS((rows, LANES), F32),
        scratch_shapes=[pltpu.VMEM((N_DEV, rows, LANES), F32), pltpu.SemaphoreType.DMA((N_DEV - 1,)),
                        pltpu.SemaphoreType.DMA((N_DEV - 1,))],
    )(buf)


def _adam_update(w, gv, m, v):
    mn = ADAM_B1 * m + (1.0 - ADAM_B1) * gv
    vn = ADAM_B2 * v + (1.0 - ADAM_B2) * (gv * gv)
    m_hat = mn / (1.0 - ADAM_B1 ** ADAM_STEP)
    v_hat = vn / (1.0 - ADAM_B2 ** ADAM_STEP)
    return -ADAM_LR * (m_hat / (jnp.sqrt(v_hat) + ADAM_EPS) + ADAM_WD * w), mn, vn


def _adamw_halves(w, mine, theirs, m, v, c, name, row0=0, bufs=None):
    R, C = w.shape
    half = mine.shape[0]
    tr = min(128, half)
    nbh = half // tr
    b0 = row0 // tr
    assert row0 % tr == 0 and half % tr == 0

    def body(c_ref, w_ref, a_ref, b_ref, m_ref, v_ref, *rest):
        g_ref, d_ref, mo_ref, vo_ref = rest[-4:]
        is_mine = (pl.program_id(0) // nbh) == c_ref[0]
        gv = jnp.where(is_mine, a_ref[...], b_ref[...])
        g_ref[...] = gv
        d_ref[...], mo_ref[...], vo_ref[...] = _adam_update(w_ref[...], gv, m_ref[...], v_ref[...])

    blk = pl.BlockSpec((tr, C), lambda i, c_ref: (b0 + i, 0))
    ablk = pl.BlockSpec((tr, C), lambda i, c_ref: (jnp.where(i // nbh == c_ref[0], i % nbh, 0), 0))
    bblk = pl.BlockSpec((tr, C), lambda i, c_ref: (jnp.where(i // nbh == c_ref[0], 0, i % nbh), 0))
    extra = [] if bufs is None else list(bufs)
    return pl.pallas_call(
        body, name=name,
        grid_spec=pltpu.PrefetchScalarGridSpec(num_scalar_prefetch=1, grid=(2 * nbh,),
                                               in_specs=[blk, ablk, bblk, blk, blk] + [HBM_SPEC] * len(extra),
                                               out_specs=(blk,) * 4),
        out_shape=(SDS((R, C), F32),) * 4, compiler_params=_cp("parallel"),
        input_output_aliases={6 + i: i for i in range(len(extra))},
    )(jnp.reshape(c, (1,)).astype(jnp.int32), w, mine, theirs, m, v, *extra)


def _adamw(w, g, m, v, name):
    R, C = w.shape
    tr = min(256, R)
    assert R % tr == 0

    def body(w_ref, g_ref, m_ref, v_ref, d_ref, mo_ref, vo_ref):
        d_ref[...], mo_ref[...], vo_ref[...] = _adam_update(w_ref[...], g_ref[...], m_ref[...], v_ref[...])

    blk = pl.BlockSpec((tr, C), lambda i: (i, 0))
    return pl.pallas_call(
        body, name=name, grid=(R // tr,), in_specs=[blk] * 4, out_specs=(blk,) * 3,
        out_shape=(SDS((R, C), F32),) * 3, compiler_params=_cp("parallel"),
    )(w, g, m, v)


def _pack(arrs):
    rows = []
    for a in arrs:
        flat = a.reshape(-1).astype(F32)
        pad = (-flat.shape[0]) % LANES
        rows.append(jnp.pad(flat, (0, pad)).reshape(-1, LANES))
    buf = jnp.concatenate(rows, axis=0)
    pad_rows = (-buf.shape[0]) % 8
    return jnp.pad(buf, ((0, pad_rows), (0, 0)))


def _unpack(buf, shapes):
    out, r = [], 0
    for shp in shapes:
        size = int(np.prod(shp))
        nr = -(-size // LANES)
        out.append(buf[r:r + nr].reshape(-1)[:size].reshape(shp))
        r += nr
    return out


def _pad_lanes(a):
    return jnp.pad(a, ((0, 0), (0, LANES - a.shape[1])))


def kernel(x, ret_w_in, ret_gn_g, ret_w_out, gdn_w_in, gdn_conv_w, gdn_a_log, gdn_dt_bias, gdn_norm_g, gdn_w_out, ln_mix_g, ln_mix_b, mlp_w1, mlp_w2, ln_ffn_g, ln_ffn_b, loss_target, m_ret_w_in, m_ret_gn_g, m_ret_w_out, m_gdn_w_in, m_gdn_conv_w, m_gdn_a_log, m_gdn_dt_bias, m_gdn_norm_g, m_gdn_w_out, m_ln_mix_g, m_ln_mix_b, m_mlp_w1, m_mlp_w2, m_ln_ffn_g, m_ln_ffn_b, v_ret_w_in, v_ret_gn_g, v_ret_w_out, v_gdn_w_in, v_gdn_conv_w, v_gdn_a_log, v_gdn_dt_bias, v_gdn_norm_g, v_gdn_w_out, v_ln_mix_g, v_ln_mix_b, v_mlp_w1, v_mlp_w2, v_ln_ffn_g, v_ln_ffn_b):
    cx, cy = lax.axis_index("x"), lax.axis_index("y")
    chip = 2 * cx + cy

    sh = dict(ret_w_in=ret_w_in[0].astype(BF16), ret_w_out=ret_w_out[0].astype(BF16),
              gdn_w_in=gdn_w_in[0].astype(BF16), gdn_w_out=gdn_w_out[0].astype(BF16),
              mlp_w1_0=mlp_w1[0].astype(BF16), mlp_w1_1=mlp_w1[1].astype(BF16),
              mlp_w2_0=mlp_w2[0].astype(BF16), mlp_w2_1=mlp_w2[1].astype(BF16), conv_w=gdn_conv_w[0])
    small = dict(ret_gn_g=ret_gn_g, a_log=_pad_lanes(gdn_a_log), dt_bias=_pad_lanes(gdn_dt_bias), norm_g=gdn_norm_g,
                 ln_mix_g=ln_mix_g, ln_mix_b=ln_mix_b, ln_ffn_g=ln_ffn_g, ln_ffn_b=ln_ffn_b)

    loss, grad_x, big, sm = _local_step(x[0], loss_target[0], sh, small)

    small_names = ["ret_gn_g", "a_log", "dt_bias", "norm_g", "ln_mix_g", "ln_mix_b", "ln_ffn_g", "ln_ffn_b", "conv_w"]
    small_shapes = [(1, 4096), (1, LANES), (1, LANES), (1, GDN_D), (2, D_MODEL), (2, D_MODEL), (2, D_MODEL),
                    (2, D_MODEL), (4, GDN_QKV)]
    red = _all_reduce_small(_pack([loss] + [sm[k] for k in small_names]))
    red_loss, *red_small = _unpack(red, [(1, 1)] + small_shapes)
    gs = dict(zip(small_names, red_small))
    g_conv = lax.dynamic_slice_in_dim(gs["conv_w"], chip * 2048, 2048, axis=1)
    g_a_log, g_dt_bias = gs["a_log"][:, :GDN_HV], gs["dt_bias"][:, :GDN_HV]

    big_w = [(ret_w_in, m_ret_w_in, v_ret_w_in, [big["ret_w_in"]]), (ret_w_out, m_ret_w_out, v_ret_w_out, [big["ret_w_out"]]),
             (gdn_w_in, m_gdn_w_in, v_gdn_w_in, [big["gdn_w_in"]]), (gdn_w_out, m_gdn_w_out, v_gdn_w_out, [big["gdn_w_out"]]),
             (mlp_w1, m_mlp_w1, v_mlp_w1, big["mlp_w1"]), (mlp_w2, m_mlp_w2, v_mlp_w2, big["mlp_w2"])]
    core = lax.axis_index("c")
    big_out = []
    for i, (w_, m_, v_, layers) in enumerate(big_w):
        two_d = lambda a: a.reshape(-1, a.shape[-1])
        res = None
        for j, (mine, theirs) in enumerate(layers):
            res = _adamw_halves(two_d(w_), mine, theirs, two_d(m_), two_d(v_), core, "adamw_%d_%d" % (i, j),
                                row0=j * 2 * mine.shape[0], bufs=res)
        big_out.append(tuple(a.reshape(w_.shape) for a in res))
    sm_w = [(ret_gn_g, m_ret_gn_g, v_ret_gn_g, gs["ret_gn_g"]), (gdn_conv_w, m_gdn_conv_w, v_gdn_conv_w, g_conv),
            (gdn_a_log, m_gdn_a_log, v_gdn_a_log, g_a_log), (gdn_dt_bias, m_gdn_dt_bias, v_gdn_dt_bias, g_dt_bias),
            (gdn_norm_g, m_gdn_norm_g, v_gdn_norm_g, gs["norm_g"]), (ln_mix_g, m_ln_mix_g, v_ln_mix_g, gs["ln_mix_g"]),
            (ln_mix_b, m_ln_mix_b, v_ln_mix_b, gs["ln_mix_b"]), (ln_ffn_g, m_ln_ffn_g, v_ln_ffn_g, gs["ln_ffn_g"]),
            (ln_ffn_b, m_ln_ffn_b, v_ln_ffn_b, gs["ln_ffn_b"])]
    sm_shapes = [w_.shape for w_, _, _, _ in sm_w]
    d_s, nm_s, nv_s = _adamw(_pack([w_ for w_, _, _, _ in sm_w]), _pack([g_ for _, _, _, g_ in sm_w]),
                             _pack([m_ for _, m_, _, _ in sm_w]), _pack([v_ for _, _, v_, _ in sm_w]), "adamw_small")
    d_s, nm_s, nv_s = (_unpack(a, sm_shapes) for a in (d_s, nm_s, nv_s))
    sm_out = [(g_.reshape(w_.shape), d_s[i], nm_s[i], nv_s[i]) for i, (w_, _, _, g_) in enumerate(sm_w)]

    per_w = [big_out[0], sm_out[0], big_out[1], big_out[2], sm_out[1], sm_out[2], sm_out[3], sm_out[4], big_out[3],
             sm_out[5], sm_out[6], big_out[4], big_out[5], sm_out[7], sm_out[8]]
    outs = [red_loss.reshape(()), grad_x[None]]
    for kind in range(4):
        outs.extend(t[kind] for t in per_w)
    return tuple(outs)
```

```python
import functools

import numpy as np
import jax
import jax.numpy as jnp
from jax import lax
from jax.experimental import pallas as pl
from jax.experimental.pallas import tpu as pltpu

F32 = jnp.float32
BF16 = jnp.bfloat16
MESH = pl.DeviceIdType.MESH
SDS = jax.ShapeDtypeStruct

D_MODEL = 2048
CHUNK = 64
RET_HEADS, RET_DK, RET_DV = 8, 256, 512
GDN_HV, GDN_D = 32, 128
HP = 2
KB = 2
NH = KB * HP
GDN_QKV = 8192
ALPHA = 4.0 ** 0.25
LN_EPS, GN_EPS, RMS_EPS, L2_EPS = 1e-5, 1e-6, 1e-6, 1e-6
ADAM_LR, ADAM_B1, ADAM_B2, ADAM_EPS, ADAM_WD, ADAM_STEP = 0.001, 0.9, 0.999, 1e-8, 0.01, 10

VMEM_LIMIT_BYTES = 56 * 1024 * 1024
RT = 256
CT = 256
ROWS = 256
LANES = 128
N_CHIPS = 4
N_DEV = 8


def _cp(*sem):
    return pltpu.CompilerParams(dimension_semantics=sem, vmem_limit_bytes=VMEM_LIMIT_BYTES)


def _dot(a, b):
    return jnp.dot(a, b, preferred_element_type=F32)


def _dot_nt(a, b):
    return lax.dot_general(a, b, (((1,), (1,)), ((), ())), preferred_element_type=F32)


def _dot_tn(a, b):
    return lax.dot_general(a, b, (((0,), (0,)), ((), ())), preferred_element_type=F32)


def _split2(x):
    hi = x.astype(BF16)
    lo = (x - hi.astype(F32)).astype(BF16)
    return hi, lo


def _dotx3(a, b):
    ah, al = _split2(a)
    bh, bl = _split2(b)
    return _dot(ah, bh) + (_dot(ah, bl) + _dot(al, bh))


def _dot_exact_l(l_bf16, x):
    hi = x.astype(BF16)
    r = x - hi.astype(F32)
    mid = r.astype(BF16)
    lo = (r - mid.astype(F32)).astype(BF16)
    return _dot(l_bf16, hi) + (_dot(l_bf16, mid) + _dot(l_bf16, lo))


def _sigmoid(x):
    return 1.0 / (1.0 + jnp.exp(-x))


def _iota(shape, dim):
    return lax.broadcasted_iota(jnp.int32, shape, dim)


class _Side:
    def __init__(self, arrays, out_shapes, phases, counts, local=None, n_local=0):
        self.arrays, self.out_shapes, self.phases, self.counts = list(arrays), list(out_shapes), phases, counts
        self.local, self.n_local = local, n_local

    def sem_shapes(self):
        sems = [pltpu.SemaphoreType.DMA((sum(self.counts),))] * 2
        return sems + ([pltpu.SemaphoreType.DMA((self.n_local,))] if self.n_local else [])

    def ops(self, ins, outs, sems):
        send_sems, recv_sems = sems[0], sems[1]

        def copies(ph):
            if ph < 0:
                return [pltpu.make_async_copy(src, dst, sems[2].at[i])
                        for i, (src, dst) in enumerate(self.local(_coords(), ins, outs))] if self.n_local else []
            off = sum(self.counts[:ph])
            return [pltpu.make_async_remote_copy(src_ref=src, dst_ref=dst, send_sem=send_sems.at[off + i],
                                                 recv_sem=recv_sems.at[off + i], device_id=to, device_id_type=MESH)
                    for i, (src, dst, to) in enumerate(self.phases[ph](_coords(), ins, outs))]

        def start(ph):
            for cp in copies(ph):
                cp.start()

        def wait(ph):
            cps = copies(ph)
            if ph < 0:
                for cp in cps:
                    cp.wait()
                return
            for cp in cps:
                cp.wait_recv()
            for cp in cps:
                cp.wait_send()

        return start, wait


def _comm_call(name, side):
    n = len(side.arrays)

    def body(*refs):
        start, wait = side.ops(refs[:n], refs[n:2 * n], refs[2 * n:])
        start(-1)
        for ph in range(len(side.phases)):
            start(ph)
            wait(ph)
        wait(-1)

    return pl.pallas_call(body, name=name, in_specs=[HBM_SPEC] * n, out_specs=[HBM_SPEC] * n,
                          out_shape=side.out_shapes, scratch_shapes=side.sem_shapes())(*side.arrays)


SIDE_SWITCH = 0.85


def _pcall(body, args, *, name, grid, in_specs, out_specs, out_shape, sem, scratch_shapes=(), sides=()):
    single = not isinstance(out_shape, (tuple, list))
    if not sides:
        return pl.pallas_call(body, name=name, grid=grid, in_specs=list(in_specs), out_specs=out_specs,
                              out_shape=out_shape, scratch_shapes=list(scratch_shapes), compiler_params=_cp(*sem))(*args)
    o_shapes = (out_shape,) if single else tuple(out_shape)
    o_specs = (out_specs,) if single else tuple(out_specs)
    n_in, n_out, n_scr = len(args), len(o_shapes), len(scratch_shapes)
    ns = [len(s.arrays) for s in sides]
    steps = int(np.prod(grid))

    def carrier(*refs):
        pos = n_in
        s_ins = []
        for k in ns:
            s_ins.append(refs[pos:pos + k])
            pos += k
        outs = refs[pos:pos + n_out]
        pos += n_out
        s_outs = []
        for k in ns:
            s_outs.append(refs[pos:pos + k])
            pos += k
        scr = refs[pos:pos + n_scr]
        pos += n_scr
        step = pl.program_id(0)
        for d in range(1, len(grid)):
            step = step * grid[d] + pl.program_id(d)
        hooks = []
        for i, s in enumerate(sides):
            k = len(s.sem_shapes())
            hooks.append(s.ops(s_ins[i], s_outs[i], refs[pos:pos + k]))
            pos += k
        for (start, wait), s in zip(hooks, sides):
            def first(start=start):
                start(-1)
                start(0)
            pl.when(step == 0)(first)
            if len(s.phases) == 2:
                def switch(start=start, wait=wait):
                    wait(0)
                    start(1)
                pl.when(step == int(steps * SIDE_SWITCH))(switch)
            else:
                assert len(s.phases) == 1
        body(*refs[:n_in], *outs, *scr)
        for (start, wait), s in zip(hooks, sides):
            def last(wait=wait, n_ph=len(s.phases)):
                wait(n_ph - 1)
                wait(-1)
            pl.when(step == steps - 1)(last)

    res = pl.pallas_call(
        carrier, name=name, grid=grid, in_specs=list(in_specs) + [HBM_SPEC] * sum(ns),
        out_specs=o_specs + (HBM_SPEC,) * sum(ns),
        out_shape=o_shapes + tuple(sh for s in sides for sh in s.out_shapes),
        scratch_shapes=list(scratch_shapes) + [sm for s in sides for sm in s.sem_shapes()],
        compiler_params=_cp(*(("arbitrary",) * len(grid))),
    )(*args, *[a for s in sides for a in s.arrays])
    main, rest, side_res = res[:n_out], list(res[n_out:]), []
    for k in ns:
        side_res.append(rest[:k])
        rest = rest[k:]
    return (main[0] if single else tuple(main)), side_res


def _mm(a, b, mode, name, *, out_dtype=F32, tm=1024, tn=1024, tk=2048, epi=None, extra=None, scale=1.0,
        shard_major=False, sides=()):
    if mode == "nn":
        (M, K), (K2, N) = a.shape, b.shape
    elif mode == "nt":
        (M, K), (N, K2) = a.shape, b.shape
    else:
        (K, M), (K2, N) = a.shape, b.shape
    assert K == K2, (a.shape, b.shape, mode)
    tm, tn, tk = min(tm, M), min(tn, N), min(tk, K)
    assert M % tm == 0 and N % tn == 0 and K % tk == 0, (M, N, K, tm, tn, tk)
    nk = K // tk
    dims = {"nn": (((1,), (0,)), ((), ())), "nt": (((1,), (1,)), ((), ())), "tn": (((0,), (0,)), ((), ()))}[mode]
    if mode == "tn":
        a_spec = pl.BlockSpec((tk, tm), lambda i, j, k: (k, i))
    else:
        a_spec = pl.BlockSpec((tm, tk), lambda i, j, k: (i, k))
    if mode == "nt":
        b_spec = pl.BlockSpec((tn, tk), lambda i, j, k: (j, k))
    else:
        b_spec = pl.BlockSpec((tk, tn), lambda i, j, k: (k, j))
    tile = pl.BlockSpec((tm, tn), lambda i, j, k: (i, j))
    in_specs, ins = [a_spec, b_spec], [a, b]
    if epi == "rope":
        half = RET_DK // 2
        assert tn % RET_DK == 0 and (2 * RET_HEADS * RET_DK) % tn == 0
        in_specs += [pl.BlockSpec((tm, half), lambda i, j, k: (i, 0))] * 2
        ins += list(extra)
    elif epi == "ln":
        assert tn == N
        vec = pl.BlockSpec((1, tn), lambda i, j, k: (0, j))
        in_specs += [tile, vec, vec]
        ins += list(extra)
    elif extra is not None:
        in_specs.append(tile)
        ins.append(extra)
    n_extra = len(ins) - 2
    if epi == "relu2":
        out_shape = (SDS((M, N), F32), SDS((M, N), BF16))
        out_specs = (tile, tile)
    elif epi == "ln":
        out_shape = (SDS((M, N), F32), SDS((M, N), BF16), SDS((M, N), F32))
        out_specs = (tile, tile, tile)
    elif shard_major:
        per = (N // N_CHIPS) // tn
        assert per * tn * N_CHIPS == N
        out_shape = (SDS((N_CHIPS, M, N // N_CHIPS), out_dtype),)
        out_specs = (pl.BlockSpec((None, tm, tn), lambda i, j, k: (j // per, i, j % per)),)
    else:
        out_shape = (SDS((M, N), out_dtype),)
        out_specs = (tile,)
    n_out = len(out_shape)

    def body(*refs):
        a_ref, b_ref = refs[0], refs[1]
        x_ref = refs[2] if n_extra else None
        pos = 2 + n_extra
        o_refs = refs[pos:pos + n_out]
        acc_ref = refs[pos + n_out] if nk > 1 else None

        def prod():
            av, bv = a_ref[...], b_ref[...]
            if av.dtype != BF16:
                av = av.astype(BF16)
            if bv.dtype != BF16:
                bv = bv.astype(BF16)
            return lax.dot_general(av, bv, dims, preferred_element_type=F32)

        def finish(acc):
            if epi == "relu2":
                o_refs[0][...] = acc
                r = jnp.maximum(acc, 0.0)
                o_refs[1][...] = (r * r).astype(BF16)
            elif epi == "drelu2":
                o_refs[0][...] = (acc * (2.0 * jnp.maximum(x_ref[...], 0.0))).astype(out_dtype)
            elif epi == "add":
                o_refs[0][...] = (acc + scale * x_ref[...]).astype(out_dtype)
            elif epi == "ln":
                z = ALPHA * x_ref[...] + acc
                xh, _ = _ln_stats(z)
                o = xh * refs[3][...] + refs[4][...]
                o_refs[0][...] = o
                o_refs[1][...] = o.astype(BF16)
                o_refs[2][...] = z
            elif epi == "rope":
                j = pl.program_id(1)
                qk_tiles = 2 * RET_HEADS * RET_DK // tn

                @pl.when(j < qk_tiles)
                def _():
                    c, s = refs[2][...], refs[3][...]
                    sc = jnp.where(j >= qk_tiles // 2, RET_DK ** -0.5, 1.0)
                    for hh in range(tn // RET_DK):
                        lo = slice(hh * RET_DK, hh * RET_DK + half)
                        hi = slice(hh * RET_DK + half, (hh + 1) * RET_DK)
                        t1, t2 = acc[:, lo], acc[:, hi]
                        o_refs[0][:, lo] = ((t1 * c - t2 * s) * sc).astype(out_dtype)
                        o_refs[0][:, hi] = ((t1 * s + t2 * c) * sc).astype(out_dtype)

                @pl.when(j >= qk_tiles)
                def _():
                    o_refs[0][...] = acc.astype(out_dtype)
            else:
                o_refs[0][...] = acc.astype(out_dtype)

        if nk == 1:
            finish(prod())
        else:
            k = pl.program_id(2)

            @pl.when(k == 0)
            def _():
                acc_ref[...] = prod()

            @pl.when(k > 0)
            def _():
                acc_ref[...] += prod()

            @pl.when(k == nk - 1)
            def _():
                finish(acc_ref[...])

    res = _pcall(body, ins, name=name, grid=(M // tm, N // tn, nk), in_specs=in_specs, out_specs=out_specs,
                 out_shape=out_shape, scratch_shapes=[pltpu.VMEM((tm, tn), F32)] if nk > 1 else [],
                 sem=("parallel", "parallel", "arbitrary"), sides=sides)
    main, side_res = res if sides else (res, None)
    main = main if n_out > 1 else main[0]
    return (main, side_res) if sides else main


def _ln_stats(z):
    mu = jnp.mean(z, -1, keepdims=True)
    zc = z - mu
    var = jnp.mean(zc * zc, -1, keepdims=True)
    rstd = lax.rsqrt(var + LN_EPS)
    return zc * rstd, rstd


def _ln_bwd(dout, z, g, name):
    S, Dm = z.shape
    row = pl.BlockSpec((ROWS, Dm), lambda t: (t, 0))
    vec = pl.BlockSpec((1, Dm), lambda t: (0, 0))

    def body(d_ref, z_ref, g_ref, dz_ref, dzb_ref, dg_ref, db_ref):
        t = pl.program_id(0)
        xh, rstd = _ln_stats(z_ref[...])
        d = d_ref[...]
        dxh = d * g_ref[...]
        m1 = jnp.mean(dxh, -1, keepdims=True)
        m2 = jnp.mean(dxh * xh, -1, keepdims=True)
        dz = rstd * (dxh - m1 - xh * m2)
        dz_ref[...] = dz
        dzb_ref[...] = dz.astype(BF16)
        pg = jnp.sum(d * xh, axis=0, keepdims=True)
        pb = jnp.sum(d, axis=0, keepdims=True)

        @pl.when(t == 0)
        def _():
            dg_ref[...] = pg
            db_ref[...] = pb

        @pl.when(t > 0)
        def _():
            dg_ref[...] += pg
            db_ref[...] += pb

    return pl.pallas_call(
        body, name=name, grid=(S // ROWS,), in_specs=[row, row, vec], out_specs=(row, row, vec, vec),
        out_shape=(SDS((S, Dm), F32), SDS((S, Dm), BF16), SDS((1, Dm), F32), SDS((1, Dm), F32)),
        compiler_params=_cp("arbitrary"),
    )(dout, z, g)


def _loss_ln_bwd(z, g, b, tgt, name):
    S, Dm = z.shape
    row = pl.BlockSpec((ROWS, Dm), lambda t: (t, 0))
    vec = pl.BlockSpec((1, Dm), lambda t: (0, 0))
    one = pl.BlockSpec((1, 1), lambda t: (0, 0))

    def body(z_ref, g_ref, b_ref, t_ref, l_ref, dz_ref, dzb_ref, dg_ref, db_ref):
        t = pl.program_id(0)
        xh, rstd = _ln_stats(z_ref[...])
        diff = xh * g_ref[...] + b_ref[...] - t_ref[...]
        part = jnp.sum(jnp.sum(diff * diff, axis=1, keepdims=True), axis=0, keepdims=True) * (0.5 / Dm)
        d = diff * (1.0 / Dm)
        dxh = d * g_ref[...]
        m1 = jnp.mean(dxh, -1, keepdims=True)
        m2 = jnp.mean(dxh * xh, -1, keepdims=True)
        dz = rstd * (dxh - m1 - xh * m2)
        dz_ref[...] = dz
        dzb_ref[...] = dz.astype(BF16)
        pg = jnp.sum(d * xh, axis=0, keepdims=True)
        pb = jnp.sum(d, axis=0, keepdims=True)

        @pl.when(t == 0)
        def _():
            l_ref[...] = part
            dg_ref[...] = pg
            db_ref[...] = pb

        @pl.when(t > 0)
        def _():
            l_ref[...] += part
            dg_ref[...] += pg
            db_ref[...] += pb

    return pl.pallas_call(
        body, name=name, grid=(S // ROWS,), in_specs=[row, vec, vec, row], out_specs=(one, row, row, vec, vec),
        out_shape=(SDS((1, 1), F32), SDS((S, Dm), F32), SDS((S, Dm), BF16), SDS((1, Dm), F32), SDS((1, Dm), F32)),
        compiler_params=_cp("arbitrary"),
    )(z, g, b, tgt)


def _ret_consts():
    h = np.arange(RET_HEADS, dtype=np.float64)
    lg = np.log1p(-np.exp2(-5.0 - h))
    return jnp.asarray(np.concatenate([lg, np.exp(lg * RT)]).astype(np.float32))


def _rope_tables(S):
    half = RET_DK // 2
    inv = 10000.0 ** (-jnp.arange(half, dtype=F32) / half)
    ang = jnp.arange(S).astype(F32)[:, None] * inv[None, :]
    return jnp.cos(ang), jnp.sin(ang)


def _ret_masks(lgh):
    ri, ci = _iota((RT, RT), 0), _iota((RT, RT), 1)
    visible = (ci >> 6) <= (ri >> 6)
    m = jnp.where(visible, jnp.exp(lgh * jnp.abs(ri - ci).astype(F32)), 0.0)
    pos = _iota((RT, 1), 0).astype(F32)
    return m, jnp.exp(lgh * (pos + 1.0)), jnp.exp(lgh * (RT - 1.0 - pos))


def _ret_fwd(h0, gn_g, consts, sides=()):
    S = h0.shape[0]
    nt = S // RT

    def body(c_ref, q_ref, k_ref, v_ref, gate_ref, g_ref, y_ref, o_ref, st_ref, s_scr):
        h, t = pl.program_id(0), pl.program_id(1)

        @pl.when(t == 0)
        def _():
            s_scr[...] = jnp.zeros_like(s_scr)

        lgh, cdec = c_ref[h], c_ref[RET_HEADS + h]
        m, dq, dk = _ret_masks(lgh)
        qv, kv, vv = q_ref[...], k_ref[...], v_ref[...]
        p = (_dot_nt(qv, kv) * m).astype(BF16)
        sp = s_scr[...]
        spb = sp.astype(BF16)
        st_ref[...] = spb
        qd = (qv.astype(F32) * dq).astype(BF16)
        kd = (kv.astype(F32) * dk).astype(BF16)
        y = _dot(p, vv) + _dot(qd, spb)
        y_ref[...] = y
        s_scr[...] = sp * cdec + _dot_tn(kd, vv)
        yn, _ = _gn_stats(y)
        gate = gate_ref[...].astype(F32)
        o_ref[...] = (gate * _sigmoid(gate) * (yn * g_ref[...])).astype(BF16)

    qk = lambda off: pl.BlockSpec((RT, RET_DK), lambda h, t: (t, off + h))
    vs = lambda off: pl.BlockSpec((RT, RET_DV), lambda h, t: (t, off + h))
    return _pcall(
        body, (consts, h0, h0, h0, h0, gn_g), name="ret_fwd", grid=(RET_HEADS, nt),
        in_specs=[pl.BlockSpec(memory_space=pltpu.SMEM), qk(0), qk(RET_HEADS), vs(RET_HEADS), vs(2 * RET_HEADS),
                  pl.BlockSpec((1, RET_DV), lambda h, t: (0, h))],
        out_specs=(vs(0), vs(0), pl.BlockSpec((None, None, RET_DK, RET_DV), lambda h, t: (h, t, 0, 0))),
        out_shape=(SDS((S, 4096), F32), SDS((S, 4096), BF16), SDS((RET_HEADS, nt, RET_DK, RET_DV), BF16)),
        scratch_shapes=[pltpu.VMEM((RET_DK, RET_DV), F32)], sem=("parallel", "arbitrary"), sides=sides)


def _ret_bwd(h0, do, y, gn_g, states, consts, cos, sin, sides=()):
    S = h0.shape[0]
    nt = S // RT
    half = RET_DK // 2

    def body(c_ref, q_ref, k_ref, v_ref, gate_ref, do_ref, y_ref, g_ref, st_ref, cos_ref, sin_ref,
             dq_ref, dk_ref, dv_ref, dgate_ref, dg_ref, ds_scr):
        h, t = pl.program_id(0), pl.program_id(1)

        @pl.when(t == 0)
        def _():
            ds_scr[...] = jnp.zeros_like(ds_scr)
            dg_ref[...] = jnp.zeros_like(dg_ref)

        yn, rstd = _gn_stats(y_ref[...])
        gate, g, dov = gate_ref[...].astype(F32), g_ref[...], do_ref[...]
        sg = _sigmoid(gate)
        dgate_ref[...] = (dov * (yn * g) * (sg * (1.0 + gate * (1.0 - sg)))).astype(BF16)
        dyg = dov * (gate * sg)
        dyn = dyg * g
        m1 = jnp.mean(dyn, -1, keepdims=True)
        m2 = jnp.mean(dyn * yn, -1, keepdims=True)
        dyv = (rstd * (dyn - m1 - yn * m2)).astype(BF16)
        dg_ref[...] += jnp.sum(dyg * yn, axis=0, keepdims=True)

        lgh, cdec = c_ref[h], c_ref[RET_HEADS + h]
        m, dqc, dkc = _ret_masks(lgh)
        qv, kv, vv, spb = q_ref[...], k_ref[...], v_ref[...], st_ref[...]
        p = (_dot_nt(qv, kv) * m).astype(BF16)
        qd = (qv.astype(F32) * dqc).astype(BF16)
        kd = (kv.astype(F32) * dkc).astype(BF16)
        dsn = ds_scr[...]
        dsb = dsn.astype(BF16)
        dsc = (_dot_nt(dyv, vv) * m).astype(BF16)
        dq = _dot(dsc, kv) + _dot_nt(dyv, spb) * dqc
        dk = _dot_tn(dsc, qv) + _dot_nt(vv, dsb) * dkc
        dv_ref[...] = (_dot_tn(p, dyv) + _dot(kd, dsb)).astype(BF16)
        ds_scr[...] = dsn * cdec + _dot_tn(qd, dyv)
        c, s = cos_ref[...], sin_ref[...]

        def unrot(d):
            d1, d2 = d[:, :half], d[:, half:]
            return jnp.concatenate([d1 * c + d2 * s, d2 * c - d1 * s], axis=-1)

        dq_ref[...] = unrot(dq).astype(BF16)
        dk_ref[...] = (unrot(dk) * (RET_DK ** -0.5)).astype(BF16)

    rev = lambda t: nt - 1 - t
    qkb = lambda off: pl.BlockSpec((RT, RET_DK), lambda h, t: (rev(t), off + h))
    vsb = lambda off: pl.BlockSpec((RT, RET_DV), lambda h, t: (rev(t), off + h))
    qk, vs = qkb(0), vsb(0)
    tab = pl.BlockSpec((RT, half), lambda h, t: (rev(t), 0))
    vec = pl.BlockSpec((1, RET_DV), lambda h, t: (0, h))
    return _pcall(
        body, (consts, h0, h0, h0, h0, do, y, gn_g, states, cos, sin), name="ret_bwd", grid=(RET_HEADS, nt),
        in_specs=[pl.BlockSpec(memory_space=pltpu.SMEM), qk, qkb(RET_HEADS), vsb(RET_HEADS), vsb(2 * RET_HEADS), vs, vs,
                  vec, pl.BlockSpec((None, None, RET_DK, RET_DV), lambda h, t: (h, rev(t), 0, 0)), tab, tab],
        out_specs=(qk, qk, vs, vs, vec),
        out_shape=(SDS((S, 2048), BF16), SDS((S, 2048), BF16), SDS((S, 4096), BF16), SDS((S, 4096), BF16),
                   SDS((1, 4096), F32)),
        scratch_shapes=[pltpu.VMEM((RET_DK, RET_DV), F32)], sem=("parallel", "arbitrary"), sides=sides)


def _gn_stats(y):
    mu = jnp.mean(y, -1, keepdims=True)
    yc = y - mu
    var = jnp.mean(yc * yc, -1, keepdims=True)
    rstd = lax.rsqrt(var + GN_EPS)
    return yc * rstd, rstd


CONV_RB, CONV_CB = 32, 512


def _conv_tiles(rows, cols):
    return [(r0, slice(c0, c0 + CONV_CB)) for r0 in range(0, rows, CONV_RB) for c0 in range(0, cols, CONV_CB)]


def _conv_fill(x_ref, halo_ref, ext_scr, t):
    ext_scr[0:8, :] = jnp.where(t == 0, 0.0, halo_ref[...])
    ext_scr[8:, :] = x_ref[...]


def _conv_tile(ext_scr, w, r0, cs):
    acc = w[3:4, cs] * ext_scr[pl.ds(8 + r0, CONV_RB), cs]
    for j in range(3):
        acc = acc + w[j:j + 1, cs] * ext_scr[pl.ds(5 + j + r0, CONV_RB), cs]
    return acc


def _gdn_conv_fwd(h1, conv_w, kind):
    S = h1.shape[0]
    base = {"q": 0, "k": 1, "v": 2}[kind]
    ncb = 2 if kind == "v" else 1
    C = 2048

    def body(x_ref, halo_ref, w_ref, o_ref, ext_scr):
        _conv_fill(x_ref, halo_ref, ext_scr, pl.program_id(0))
        w = w_ref[...]
        scale = GDN_D ** -0.5 if kind == "q" else 1.0
        for r0, cs in _conv_tiles(ROWS, C):
            rs = pl.ds(r0, CONV_RB)
            acc = _conv_tile(ext_scr, w, r0, cs)
            c = acc * _sigmoid(acc)
            if kind == "v":
                o_ref[rs, cs] = c.astype(BF16)
            else:
                for hh in range(CONV_CB // GDN_D):
                    ch = c[:, hh * GDN_D:(hh + 1) * GDN_D]
                    r = lax.rsqrt(jnp.sum(ch * ch, -1, keepdims=True) + L2_EPS)
                    o_ref[rs, pl.ds(cs.start + hh * GDN_D, GDN_D)] = (ch * (r * scale)).astype(BF16)

    hb = ROWS // 8
    return pl.pallas_call(
        body, name="gdn_conv_fwd_" + kind, grid=(S // ROWS, ncb),
        in_specs=[pl.BlockSpec((ROWS, C), lambda t, j: (t, base + j)),
                  pl.BlockSpec((8, C), lambda t, j: (jnp.maximum(t * hb - 1, 0), base + j)),
                  pl.BlockSpec((4, C), lambda t, j: (0, base + j))],
        out_specs=pl.BlockSpec((ROWS, C), lambda t, j: (t, j)), out_shape=SDS((S, C * ncb), BF16),
        scratch_shapes=[pltpu.VMEM((ROWS + 8, C), F32)], compiler_params=_cp("parallel", "parallel"),
    )(h1, h1, conv_w)


def _gdn_conv_bwd_act(h1, conv_w, dn, kind, buf=None):
    S = h1.shape[0]
    base = {"q": 0, "k": 1, "v": 2}[kind]
    ncb = 2 if kind == "v" else 1
    C = 2048

    def body(x_ref, halo_ref, w_ref, dn_ref, *rest):
        o_ref, ext_scr = rest[-2], rest[-1]
        _conv_fill(x_ref, halo_ref, ext_scr, pl.program_id(0))
        w = w_ref[...]
        scale = GDN_D ** -0.5 if kind == "q" else 1.0
        for r0, cs in _conv_tiles(ROWS, C):
            rs = pl.ds(r0, CONV_RB)
            acc = _conv_tile(ext_scr, w, r0, cs)
            sg = _sigmoid(acc)
            dsilu = sg * (1.0 + acc * (1.0 - sg))
            if kind == "v":
                o_ref[rs, cs] = dn_ref[rs, cs] * dsilu
            else:
                c = acc * sg
                for hh in range(CONV_CB // GDN_D):
                    sl = slice(hh * GDN_D, (hh + 1) * GDN_D)
                    gl = pl.ds(cs.start + hh * GDN_D, GDN_D)
                    ch, dnh = c[:, sl], dn_ref[rs, gl]
                    r = lax.rsqrt(jnp.sum(ch * ch, -1, keepdims=True) + L2_EPS)
                    proj = jnp.sum(dnh * ch, -1, keepdims=True)
                    o_ref[rs, gl] = (scale * r) * (dnh - ch * (proj * r * r)) * dsilu[:, sl]

    hb = ROWS // 8
    return pl.pallas_call(
        body, name="gdn_conv_bwd_act_" + kind, grid=(S // ROWS, ncb),
        in_specs=[pl.BlockSpec((ROWS, C), lambda t, j: (t, base + j)),
                  pl.BlockSpec((8, C), lambda t, j: (jnp.maximum(t * hb - 1, 0), base + j)),
                  pl.BlockSpec((4, C), lambda t, j: (0, base + j)),
                  pl.BlockSpec((ROWS, C), lambda t, j: (t, j))] + ([] if buf is None else [HBM_SPEC]),
        out_specs=pl.BlockSpec((ROWS, C), lambda t, j: (t, base + j)), out_shape=SDS((S, GDN_QKV), F32),
        input_output_aliases={} if buf is None else {4: 0},
        scratch_shapes=[pltpu.VMEM((ROWS + 8, C), F32)], compiler_params=_cp("parallel", "parallel"),
    )(*((h1, h1, conv_w, dn) + (() if buf is None else (buf,))))


def _gdn_conv_bwd_in(h1, conv_w, dacc, dh1_buf):
    S = h1.shape[0]
    C = 2048
    nt = S // ROWS
    hb = ROWS // 8

    def body(x_ref, halo_ref, w_ref, d_ref, dhalo_ref, buf_ref, di_ref, dw_ref, ext_scr, dext_scr):
        t = pl.program_id(1)
        _conv_fill(x_ref, halo_ref, ext_scr, t)
        dext_scr[0:ROWS, :] = d_ref[...]
        dext_scr[ROWS:, :] = jnp.where(t == nt - 1, 0.0, dhalo_ref[...])
        w = w_ref[...]

        @pl.when(t == 0)
        def _():
            dw_ref[...] = jnp.zeros_like(dw_ref)

        for c0 in range(0, C, CONV_CB):
            cs = slice(c0, c0 + CONV_CB)
            pw = [jnp.zeros((1, CONV_CB), F32) for _ in range(4)]
            for r0 in range(0, ROWS, CONV_RB):
                d = dext_scr[pl.ds(r0, CONV_RB), cs]
                di = w[3:4, cs] * d
                for j in range(3):
                    di = di + w[j:j + 1, cs] * dext_scr[pl.ds(3 - j + r0, CONV_RB), cs]
                di_ref[pl.ds(r0, CONV_RB), cs] = di.astype(BF16)
                for j in range(4):
                    pw[j] = pw[j] + jnp.sum(d * ext_scr[pl.ds(5 + j + r0, CONV_RB), cs], axis=0, keepdims=True)
            dw_ref[:, cs] += jnp.concatenate(pw, axis=0)

    return pl.pallas_call(
        body, name="gdn_conv_bwd_in", grid=(GDN_QKV // C, nt),
        in_specs=[pl.BlockSpec((ROWS, C), lambda j, t: (t, j)),
                  pl.BlockSpec((8, C), lambda j, t: (jnp.maximum(t * hb - 1, 0), j)),
                  pl.BlockSpec((4, C), lambda j, t: (0, j)),
                  pl.BlockSpec((ROWS, C), lambda j, t: (t, j)),
                  pl.BlockSpec((8, C), lambda j, t: (jnp.minimum((t + 1) * hb, nt * hb - 1), j)), HBM_SPEC],
        out_specs=(pl.BlockSpec((ROWS, C), lambda j, t: (t, j)), pl.BlockSpec((4, C), lambda j, t: (0, j))),
        out_shape=(SDS(dh1_buf.shape, BF16), SDS((4, GDN_QKV), F32)), input_output_aliases={5: 0},
        scratch_shapes=[pltpu.VMEM((ROWS + 8, C), F32), pltpu.VMEM((ROWS + 8, C), F32)],
        compiler_params=_cp("parallel", "arbitrary"),
    )(h1, h1, conv_w, dacc, dacc, dh1_buf)


def _chunk_masks():
    ri, ci = _iota((CT, CT), 0), _iota((CT, CT), 1)
    same = (ri >> 6) == (ci >> 6)
    return same, same & (ri >= ci), same & (ri > ci), same & (ri <= ci), ri == ci


def _fold_dup(m):
    h = m[:, :LANES] + m[:, LANES:]
    return h + pltpu.roll(h, CHUNK, axis=1)


def _unfold_bd(d, same):
    return jnp.where(same, jnp.concatenate([d, d], axis=1), 0.0)


def _softplus(x):
    return jnp.maximum(x, 0.0) + jnp.log(1.0 + jnp.exp(-jnp.abs(x)))


def _gdn_scal_fwd(ba, a_log, dt_bias):
    S = ba.shape[0]

    def body(ba_ref, al_ref, dt_ref, beta_ref, g_ref, gc_ref):
        bav = ba_ref[...]
        beta_ref[...] = _sigmoid(bav)
        a = pltpu.roll(bav, LANES - GDN_HV, axis=1)
        g = -jnp.exp(al_ref[...]) * _softplus(a + dt_ref[...])
        g_ref[...] = g
        causal = _chunk_masks()[1]
        gc_ref[...] = _dot_exact_l(causal.astype(BF16), g)

    row = pl.BlockSpec((CT, LANES), lambda t: (t, 0))
    vec = pl.BlockSpec((1, LANES), lambda t: (0, 0))
    return pl.pallas_call(
        body, name="gdn_scal_fwd", grid=(S // CT,), in_specs=[row, vec, vec], out_specs=(row, row, row),
        out_shape=(SDS((S, LANES), F32),) * 3, compiler_params=_cp("parallel"),
    )(ba, a_log, dt_bias)


def _gdn_scal_bwd(ba, a_log, dt_bias, g, dbeta, dgc):
    S = ba.shape[0]

    def body(ba_ref, al_ref, dt_ref, g_ref, dbeta_ref, dgc_ref, dba_ref, dal_ref, ddt_ref):
        t = pl.program_id(0)
        bav = ba_ref[...]
        beta = _sigmoid(bav)
        db = dbeta_ref[...] * beta * (1.0 - beta)
        a = pltpu.roll(bav, LANES - GDN_HV, axis=1)
        dgv = _dot_exact_l(_chunk_masks()[3].astype(BF16), dgc_ref[...])
        da = dgv * (-jnp.exp(al_ref[...])) * _sigmoid(a + dt_ref[...])
        lane = _iota(bav.shape, 1)
        da_sh = pltpu.roll(da, GDN_HV, axis=1)
        dba = jnp.where(lane < GDN_HV, db, jnp.where(lane < 2 * GDN_HV, da_sh, 0.0))
        dba_ref[...] = dba.astype(BF16)
        keep = lane < GDN_HV
        pal = jnp.sum(jnp.where(keep, dgv * g_ref[...], 0.0), axis=0, keepdims=True)
        pdt = jnp.sum(jnp.where(keep, da, 0.0), axis=0, keepdims=True)

        @pl.when(t == 0)
        def _():
            dal_ref[...] = pal
            ddt_ref[...] = pdt

        @pl.when(t > 0)
        def _():
            dal_ref[...] += pal
            ddt_ref[...] += pdt

    row = pl.BlockSpec((CT, LANES), lambda t: (t, 0))
    vec = pl.BlockSpec((1, LANES), lambda t: (0, 0))
    return pl.pallas_call(
        body, name="gdn_scal_bwd", grid=(S // CT,), in_specs=[row, vec, vec, row, row, row],
        out_specs=(row, vec, vec), out_shape=(SDS((S, LANES), BF16), SDS((1, LANES), F32), SDS((1, LANES), F32)),
        compiler_params=_cp("arbitrary"),
    )(ba, a_log, dt_bias, g, dbeta, dgc)


def _sel_col(x, h):
    return jnp.sum(jnp.where(_iota(x.shape, 1) == h, x, 0.0), axis=1, keepdims=True)


def _decay(gcol, causal):
    gm = jnp.broadcast_to(gcol, (CT, CT))
    diff = gm - gm.T
    return jnp.where(causal, jnp.exp(jnp.where(causal, diff, 0.0)), 0.0)


def _gdn_chunk_fwd(k, v, beta, gc):
    S = k.shape[0]
    nt = S // CT

    def body(k_ref, v_ref, beta_ref, gc_ref, t_ref, u_ref, w_ref):
        kg_id = pl.program_id(1)
        same, causal, strict, _, _ = _chunk_masks()
        eye_dup = jnp.where((_iota((CT, LANES), 0) & (CHUNK - 1)) == (_iota((CT, LANES), 1) & (CHUNK - 1)), 1.0, 0.0)
        kfs, xds, pds, cols = [], [], [], []
        for kb in range(KB):
            kv = k_ref[:, kb * GDN_D:(kb + 1) * GDN_D]
            kfs.append(kv.astype(F32))
            kk = _dot_nt(kv, kv)
            for hp in range(HP):
                h = NH * kg_id + kb * HP + hp
                bcol, gcol = _sel_col(beta_ref[...], h), _sel_col(gc_ref[...], h)
                x = jnp.where(strict, -(kk * bcol) * _decay(gcol, causal), 0.0)
                xds.append(_fold_dup(x))
                pds.append(eye_dup)
                cols.append((bcol, gcol))
        for m in range(6):
            for hi in range(NH):
                xh, xl = _split2(xds[hi])
                ph, pl_ = _split2(pds[hi])
                lh, ll = _unfold_bd(xh, same), _unfold_bd(xl, same)
                rh, rl = jnp.concatenate([xh, ph], axis=1), jnp.concatenate([xl, pl_], axis=1)
                out = _dot(lh, rh) + (_dot(lh, rl) + _dot(ll, rh))
                pds[hi] = pds[hi] + out[:, LANES:]
                if m < 5:
                    xds[hi] = out[:, :LANES]
        for hi in range(NH):
            bcol, gcol = cols[hi]
            cs = slice(hi * GDN_D, (hi + 1) * GDN_D)
            t_ref[hi] = pds[hi]
            tb = _unfold_bd(pds[hi], same).astype(BF16)
            vb = (v_ref[:, cs].astype(F32) * bcol).astype(BF16)
            kg = (kfs[hi // HP] * (bcol * jnp.exp(gcol))).astype(BF16)
            uw = _dot(tb, jnp.concatenate([vb, kg], axis=1))
            u_ref[:, cs] = uw[:, :GDN_D]
            w_ref[:, cs] = uw[:, GDN_D:].astype(BF16)

    col = pl.BlockSpec((CT, LANES), lambda t, kg: (t, 0))
    hv = pl.BlockSpec((CT, NH * GDN_D), lambda t, kg: (t, kg))
    return pl.pallas_call(
        body, name="gdn_chunk_fwd", grid=(nt, GDN_HV // NH),
        in_specs=[pl.BlockSpec((CT, KB * GDN_D), lambda t, kg: (t, kg)), hv, col, col],
        out_specs=(pl.BlockSpec((NH, None, CT, LANES), lambda t, kg: (kg, t, 0, 0)), hv, hv),
        out_shape=(SDS((GDN_HV, nt, CT, LANES), F32), SDS((S, 4096), F32), SDS((S, 4096), BF16)),
        compiler_params=_cp("parallel", "parallel"),
    )(k, v, beta, gc)


def _last_of_chunk(gcol, rows, c):
    return jnp.sum(jnp.where(rows == c * CHUNK + CHUNK - 1, gcol, 0.0), axis=0, keepdims=True)


def _gdn_scan_fwd(q, k, u, w, gc):
    S = q.shape[0]
    nt = S // CT
    ncs = CT // CHUNK

    def body(q_ref, k_ref, u_ref, w_ref, gc_ref, y_ref, vn_ref, st_ref, s_scr, vn_scr):
        kg_id, t = pl.program_id(0), pl.program_id(1)

        @pl.when(t == 0)
        def _():
            s_scr[...] = jnp.zeros_like(s_scr)

        causal = _chunk_masks()[1]
        rows = _iota((CT, 1), 0)
        heads = []
        for kb in range(KB):
            ks = slice(kb * GDN_D, (kb + 1) * GDN_D)
            qv, kv = q_ref[:, ks], k_ref[:, ks]
            qf, kf = qv.astype(F32), kv.astype(F32)
            qk = _dot_nt(qv, kv)
            for hp in range(HP):
                gcol = _sel_col(gc_ref[...], NH * kg_id + kb * HP + hp)
                heads.append((gcol, (qk * _decay(gcol, causal)).astype(BF16), (qf * jnp.exp(gcol)).astype(BF16), kf))
        vn_scr[...] = jnp.zeros_like(vn_scr)
        for c in range(ncs):
            r = slice(c * CHUNK, (c + 1) * CHUNK)
            for hi in range(NH):
                gcol, attn, qd, kf = heads[hi]
                cs = slice(hi * GDN_D, (hi + 1) * GDN_D)
                s = s_scr[hi]
                st_ref[hi, c] = s
                sb = s.astype(BF16)
                gl = _last_of_chunk(gcol, rows, c)
                kd = (kf[r] * jnp.exp(gl - gcol[r])).astype(BF16)
                vn = (u_ref[r, cs] - _dot(w_ref[r, cs], sb)).astype(BF16)
                vn_scr[r, cs] = vn
                y_ref[r, cs] = _dot(qd[r], sb) + _dot(attn[r], vn_scr[:, cs])
                s_scr[hi] = s * jnp.exp(gl) + _dot_tn(kd, vn)
        vn_ref[...] = vn_scr[...]

    hk = pl.BlockSpec((CT, KB * GDN_D), lambda kg, t: (t, kg))
    hv = pl.BlockSpec((CT, NH * GDN_D), lambda kg, t: (t, kg))
    col = pl.BlockSpec((CT, LANES), lambda kg, t: (t, 0))
    return pl.pallas_call(
        body, name="gdn_scan_fwd", grid=(GDN_HV // NH, nt), in_specs=[hk, hk, hv, hv, col],
        out_specs=(hv, hv, pl.BlockSpec((NH, ncs, GDN_D, GDN_D), lambda kg, t: (kg, t, 0, 0))),
        out_shape=(SDS((S, 4096), F32), SDS((S, 4096), BF16), SDS((GDN_HV, S // CHUNK, GDN_D, GDN_D), F32)),
        scratch_shapes=[pltpu.VMEM((NH, GDN_D, GDN_D), F32), pltpu.VMEM((CT, NH * GDN_D), BF16)],
        compiler_params=_cp("parallel", "arbitrary"),
    )(q, k, u, w, gc)


def _gdn_scan_bwd(q, k, w, vn, gc, states, dy, sides=()):
    S = q.shape[0]
    nt = S // CT
    ncs = CT // CHUNK

    def body(q_ref, k_ref, w_ref, vn_ref, gc_ref, st_ref, dy_ref, du_ref, dw_ref, dq_ref, dk_ref, dgc_ref, ds_scr):
        kg_id, t = pl.program_id(0), pl.program_id(1)

        @pl.when(t == 0)
        def _():
            ds_scr[...] = jnp.zeros_like(ds_scr)

        _, causal, _, _, eye = _chunk_masks()
        rows = _iota((CT, 1), 0)
        heads = []
        for kb in range(KB):
            ks = slice(kb * GDN_D, (kb + 1) * GDN_D)
            qv, kv = q_ref[:, ks], k_ref[:, ks]
            qf, kf = qv.astype(F32), kv.astype(F32)
            qk = _dot_nt(qv, kv)
            for hp in range(HP):
                hi = kb * HP + hp
                cs = slice(hi * GDN_D, (hi + 1) * GDN_D)
                gcol = _sel_col(gc_ref[...], NH * kg_id + hi)
                dm = _decay(gcol, causal)
                attn_f = qk * dm
                egc = jnp.exp(gcol)
                qd_f = qf * egc
                dyv, vnv = dy_ref[:, cs], vn_ref[:, cs]
                heads.append(dict(cs=cs, ks=ks, gcol=gcol, dm=dm, attn_f=attn_f, egc=egc, qd_f=qd_f, kf=kf, qv=qv, kv=kv,
                                  qd=qd_f.astype(BF16), dy=dyv, vn=vnv, dattn=_dot_nt(dyv, vnv),
                                  at_dy=_dot_tn(attn_f.astype(BF16), dyv), dgc=[None] * ncs))
        dq_ref[...] = jnp.zeros_like(dq_ref)
        dk_ref[...] = jnp.zeros_like(dk_ref)
        for c in reversed(range(ncs)):
            r = slice(c * CHUNK, (c + 1) * CHUNK)
            for hi in range(NH):
                hd = heads[hi]
                cs, ks, gcol = hd["cs"], hd["ks"], hd["gcol"]
                s = st_ref[hi, c]
                sb = s.astype(BF16)
                dsn = ds_scr[hi]
                dsb = dsn.astype(BF16)
                gl = _last_of_chunk(gcol, rows, c)
                cd = jnp.exp(gl)
                ekd = jnp.exp(gl - gcol[r])
                kd_f = hd["kf"][r] * ekd
                dvn = (hd["at_dy"][r] + _dot(kd_f.astype(BF16), dsb)).astype(BF16)
                dqd = _dot_nt(hd["dy"][r], sb)
                dkd = _dot_nt(hd["vn"][r], dsb)
                dcd = jnp.sum(jnp.sum(s * dsn, axis=1, keepdims=True), axis=0, keepdims=True)
                ds_scr[hi] = dsn * cd + _dot_tn(hd["qd"][r], hd["dy"][r]) - _dot_tn(w_ref[r, cs], dvn)
                du_ref[r, cs] = dvn
                dw_ref[r, cs] = (-_dot_nt(dvn, sb)).astype(BF16)
                dq_ref[r, ks] += dqd * hd["egc"][r]
                dk_ref[r, ks] += dkd * ekd
                rs_q = jnp.sum(dqd * hd["qd_f"][r], axis=1, keepdims=True)
                rs_k = jnp.sum(dkd * kd_f, axis=1, keepdims=True)
                tot = jnp.sum(rs_k, axis=0, keepdims=True) + dcd * cd
                hd["dgc"][c] = rs_q - rs_k + jnp.where(rows[r] == c * CHUNK + CHUNK - 1, tot, 0.0)
        for hi in range(NH):
            hd = heads[hi]
            ks = hd["ks"]
            dab = (hd["dattn"] * hd["dm"]).astype(BF16)
            dq_ref[:, ks] += _dot(dab, hd["kv"])
            dk_ref[:, ks] += _dot_tn(dab, hd["qv"])
            e1 = hd["dattn"] * hd["attn_f"]
            dgc = (jnp.concatenate(hd["dgc"], axis=0) + jnp.sum(e1, axis=1, keepdims=True)
                   - jnp.sum(e1.T, axis=1, keepdims=True))
            dgc_ref[hi] = jnp.sum(jnp.where(eye, jnp.broadcast_to(dgc, (CT, CT)), 0.0), axis=0, keepdims=True)

    rev = lambda t: nt - 1 - t
    hk = pl.BlockSpec((CT, KB * GDN_D), lambda kg, t: (rev(t), kg))
    hv = pl.BlockSpec((CT, NH * GDN_D), lambda kg, t: (rev(t), kg))
    col = pl.BlockSpec((CT, LANES), lambda kg, t: (rev(t), 0))
    return _pcall(
        body, (q, k, w, vn, gc, states, dy), name="gdn_scan_bwd", grid=(GDN_HV // NH, nt),
        in_specs=[hk, hk, hv, hv, col, pl.BlockSpec((NH, ncs, GDN_D, GDN_D), lambda kg, t: (kg, rev(t), 0, 0)), hv],
        out_specs=(hv, hv, hk, hk, pl.BlockSpec((NH, 1, CT), lambda kg, t: (kg, 0, rev(t)))),
        out_shape=(SDS((S, 4096), BF16), SDS((S, 4096), BF16), SDS((S, 2048), F32), SDS((S, 2048), F32),
                   SDS((GDN_HV, 1, S), F32)),
        scratch_shapes=[pltpu.VMEM((NH, GDN_D, GDN_D), F32)], sem=("parallel", "arbitrary"), sides=sides)


def _gdn_chunk_bwd(k, v, beta, gc, tmat, du, dw, dk_p, dgc_p, sides=()):
    S = k.shape[0]
    nt = S // CT

    def body(k_ref, v_ref, beta_ref, gc_ref, t_ref, du_ref, dw_ref, dkp_ref, dgcp_ref,
             dk_ref, dv_ref, dbeta_ref, dgc_ref):
        kg_id = pl.program_id(1)
        same, causal, strict, _, eye = _chunk_masks()
        lane = _iota((CT, LANES), 1)

        @pl.when(kg_id == 0)
        def _():
            dbeta_ref[...] = jnp.zeros_like(dbeta_ref)
            dgc_ref[...] = jnp.zeros_like(dgc_ref)

        for kb_i in range(KB):
            ks = slice(kb_i * GDN_D, (kb_i + 1) * GDN_D)
            kv = k_ref[:, ks]
            kf = kv.astype(F32)
            kk = _dot_nt(kv, kv)
            dk = dkp_ref[:, ks]
            for hp in range(HP):
                hi = kb_i * HP + hp
                h = NH * kg_id + hi
                cs = slice(hi * GDN_D, (hi + 1) * GDN_D)
                bcol, gcol = _sel_col(beta_ref[...], h), _sel_col(gc_ref[...], h)
                dm = _decay(gcol, causal)
                vf = v_ref[:, cs].astype(F32)
                kb = kf * bcol
                a = jnp.where(strict, (kk * bcol) * dm, 0.0)
                egc = jnp.exp(gcol)
                kg_f = kb * egc
                tb = _unfold_bd(t_ref[hi], same).astype(BF16)
                duw = jnp.concatenate([du_ref[:, cs], dw_ref[:, cs]], axis=1)
                dt = _dot_nt(duw, jnp.concatenate([(vf * bcol).astype(BF16), kg_f.astype(BF16)], axis=1))
                dvb_dkg = _dot_tn(tb, duw)
                dvb, dkg = dvb_dkg[:, :GDN_D], dvb_dkg[:, GDN_D:]
                da = -_dot_nt(_dot_tn(tb, dt.astype(BF16)).astype(BF16), tb)
                rm = jnp.where(strict, da, 0.0)
                rdb = (rm * dm).astype(BF16)
                dkb = _dot(rdb, kv) + dkg * egc
                dk = dk + _dot_tn(rdb, kb.astype(BF16)) + dkb * bcol
                e2 = rm * a
                dgc_in = jnp.sum(jnp.where(eye, jnp.broadcast_to(dgcp_ref[hi], (CT, CT)), 0.0), axis=1, keepdims=True)
                dgc = (jnp.sum(e2, axis=1, keepdims=True) - jnp.sum(e2.T, axis=1, keepdims=True)
                       + jnp.sum(dkg * kg_f, axis=1, keepdims=True) + dgc_in)
                dbeta = jnp.sum(dkb * kf, axis=1, keepdims=True) + jnp.sum(dvb * vf, axis=1, keepdims=True)
                dv_ref[:, cs] = dvb * bcol
                dbeta_ref[...] += jnp.where(lane == h, dbeta, 0.0)
                dgc_ref[...] += jnp.where(lane == h, dgc, 0.0)
            dk_ref[:, ks] = dk

    hk = pl.BlockSpec((CT, KB * GDN_D), lambda t, kg: (t, kg))
    hv = pl.BlockSpec((CT, NH * GDN_D), lambda t, kg: (t, kg))
    col = pl.BlockSpec((CT, LANES), lambda t, kg: (t, 0))
    return _pcall(
        body, (k, v, beta, gc, tmat, du, dw, dk_p, dgc_p), name="gdn_chunk_bwd", grid=(nt, GDN_HV // NH),
        in_specs=[hk, hv, col, col, pl.BlockSpec((NH, None, CT, LANES), lambda t, kg: (kg, t, 0, 0)), hv, hv, hk,
                  pl.BlockSpec((NH, 1, CT), lambda t, kg: (kg, 0, t))],
        out_specs=(hk, hv, col, col),
        out_shape=(SDS((S, 2048), F32), SDS((S, 4096), F32), SDS((S, LANES), F32), SDS((S, LANES), F32)),
        sem=("parallel", "arbitrary"), sides=sides)


def _gdn_post_fwd(y, h1, norm_g):
    S = y.shape[0]

    def body(y_ref, z_ref, g_ref, o_ref):
        g = g_ref[...]
        for hh in range(GDN_HV):
            sl = slice(hh * GDN_D, (hh + 1) * GDN_D)
            yh, zh = y_ref[:, sl], z_ref[:, sl]
            yn = yh * lax.rsqrt(jnp.mean(yh * yh, -1, keepdims=True) + RMS_EPS)
            o_ref[:, sl] = (yn * g * (zh * _sigmoid(zh))).astype(BF16)

    row = lambda off: pl.BlockSpec((ROWS, 4096), lambda t: (t, off))
    return pl.pallas_call(
        body, name="gdn_post_fwd", grid=(S // ROWS,),
        in_specs=[row(0), row(2), pl.BlockSpec((1, GDN_D), lambda t: (0, 0))], out_specs=row(0),
        out_shape=SDS((S, 4096), BF16), compiler_params=_cp("parallel"),
    )(y, h1, norm_g)


def _gdn_post_bwd(do, y, h1, norm_g, sides=()):
    S = y.shape[0]

    def body(do_ref, y_ref, z_ref, g_ref, dy_ref, dz_ref, dg_ref):
        t = pl.program_id(0)
        g = g_ref[...]
        pg = jnp.zeros((1, GDN_D), F32)
        for hh in range(GDN_HV):
            sl = slice(hh * GDN_D, (hh + 1) * GDN_D)
            yh, zh, doh = y_ref[:, sl], z_ref[:, sl], do_ref[:, sl]
            rstd = lax.rsqrt(jnp.mean(yh * yh, -1, keepdims=True) + RMS_EPS)
            yn = yh * rstd
            sg = _sigmoid(zh)
            dz_ref[:, sl] = (doh * (yn * g) * (sg * (1.0 + zh * (1.0 - sg)))).astype(BF16)
            dyg = doh * (zh * sg)
            dyn = dyg * g
            dy_ref[:, sl] = (rstd * (dyn - yn * jnp.mean(dyn * yn, -1, keepdims=True))).astype(BF16)
            pg = pg + jnp.sum(dyg * yn, axis=0, keepdims=True)

        @pl.when(t == 0)
        def _():
            dg_ref[...] = pg

        @pl.when(t > 0)
        def _():
            dg_ref[...] += pg

    row = lambda off: pl.BlockSpec((ROWS, 4096), lambda t: (t, off))
    vec = pl.BlockSpec((1, GDN_D), lambda t: (0, 0))
    return _pcall(
        body, (do, y, h1, norm_g), name="gdn_post_bwd", grid=(S // ROWS,), in_specs=[row(0), row(0), row(2), vec],
        out_specs=(row(0), row(2), vec),
        out_shape=(SDS((S, 4096), BF16), SDS((S, 3 * 4096), BF16), SDS((1, GDN_D), F32)),
        sem=("arbitrary",), sides=sides)


def _cols(g):
    return jnp.transpose(g, (1, 0, 2)).reshape(g.shape[1], -1)


def _rows(g):
    return g.reshape(-1, g.shape[-1])


def _local_step(x, tgt, sh, small):
    S = x.shape[0]
    xb = x.astype(BF16)
    rc = _ret_consts()
    cos, sin = _rope_tables(S)

    natural = ("ret_w_in", "mlp_w1_0", "mlp_w1_1", "conv_w")

    def gather(names):
        return _ag_side([sh[k] for k in names], [k != "conv_w" for k in names], [k in natural for k in names])

    ln_tiles = dict(tm=512, tn=D_MODEL, tk=1024)
    (wri,) = _comm_call("ag_ret_in", _ag_side([sh["ret_w_in"]], [True], [True], place_own=False))
    n_own = sh["ret_w_in"].shape[1]
    wri = lax.dynamic_update_slice_in_dim(wri, sh["ret_w_in"], (2 * lax.axis_index("x") + lax.axis_index("y")) * n_own, 1)
    h0, ((g_ro, w1_0),) = _mm(xb, wri, "nn", "mm_ret_in", out_dtype=BF16, epi="rope", extra=(cos, sin),
                              sides=[gather(["ret_w_out", "mlp_w1_0"])])
    wro, w1 = _rows(g_ro), [w1_0, None]
    (yr, o0, ret_st), ((g_gi, conv_w),) = _ret_fwd(h0, small["ret_gn_g"], rc, sides=[gather(["gdn_w_in", "conv_w"])])
    wgi = _cols(g_gi)
    wgi_main = wgi[:, :GDN_QKV + 4096]
    wba = jnp.pad(wgi[:, GDN_QKV + 4096:], ((0, 0), (0, LANES - 2 * GDN_HV)))
    x1, x1b, z1 = _mm(o0, wro, "nn", "mm_ret_out", epi="ln",
                      extra=(x, small["ln_mix_g"][0:1], small["ln_mix_b"][0:1]), **ln_tiles)
    (hh0, a0), ((g_w20,),) = _mm(x1b, w1[0], "nn", "mm_mlp0_up", epi="relu2", sides=[gather(["mlp_w2_0"])])
    w2 = [_rows(g_w20), None]
    (x2, x2b, z2), ((w1[1],),) = _mm(a0, w2[0], "nn", "mm_mlp0_down", epi="ln",
                                     extra=(x1, small["ln_ffn_g"][0:1], small["ln_ffn_b"][0:1]),
                                     sides=[gather(["mlp_w1_1"])], **ln_tiles)

    h1, ((g_w21, g_go),) = _mm(x2b, wgi_main, "nn", "mm_gdn_in", sides=[gather(["mlp_w2_1", "gdn_w_out"])])
    w2[1], wgo = _rows(g_w21), _rows(g_go)
    ba = _mm(x2b, wba, "nn", "mm_gdn_ba")
    qn = _gdn_conv_fwd(h1, conv_w, "q")
    kn = _gdn_conv_fwd(h1, conv_w, "k")
    vg = _gdn_conv_fwd(h1, conv_w, "v")
    beta, g, gc = _gdn_scal_fwd(ba, small["a_log"], small["dt_bias"])
    tmat, u, w = _gdn_chunk_fwd(kn, vg, beta, gc)
    yg, vn, gdn_st = _gdn_scan_fwd(qn, kn, u, w, gc)
    o1 = _gdn_post_fwd(yg, h1, small["norm_g"])
    x3, x3b, z3 = _mm(o1, wgo, "nn", "mm_gdn_out", epi="ln",
                      extra=(x2, small["ln_mix_g"][1:2], small["ln_mix_b"][1:2]), **ln_tiles)
    hh1, a1 = _mm(x3b, w1[1], "nn", "mm_mlp1_up", epi="relu2")
    z4 = _mm(a1, w2[1], "nn", "mm_mlp1_down", epi="add", extra=x3, scale=ALPHA)

    loss, dz4, dz4b, d_lnf_g1, d_lnf_b1 = _loss_ln_bwd(z4, small["ln_ffn_g"][1:2], small["ln_ffn_b"][1:2], tgt,
                                                       "loss_ln_ffn1_bwd")
    dhh1 = _mm(dz4b, w2[1], "nt", "mm_mlp1_down_dx", epi="drelu2", extra=hh1, out_dtype=BF16)
    dw2_1 = _mm(a1, dz4b, "tn", "mm_mlp1_down_dw")
    dx3 = _mm(dhh1, w1[1], "nt", "mm_mlp1_up_dx", epi="add", extra=dz4, scale=ALPHA)
    dw1_1 = _mm(x3b, dhh1, "tn", "mm_mlp1_up_dw", shard_major=True)
    dz3, dz3b, d_lnm_g1, d_lnm_b1 = _ln_bwd(dx3, z3, small["ln_mix_g"][1:2], "ln_mix1_bwd")
    shards_of = lambda g: g.reshape(N_CHIPS, -1, g.shape[-1])
    g_a = [dw1_1, shards_of(dw2_1)]
    do1, (th_a,) = _mm(dz3b, wgo, "nt", "mm_gdn_out_dx", sides=[_rs_swap_side(g_a)])
    sums_a = _rs_add(g_a, th_a, "a")
    dwgo = _mm(o1, dz3b, "tn", "mm_gdn_out_dw")
    dyg, dh1_z, d_norm_g = _gdn_post_bwd(do1, yg, h1, small["norm_g"])
    du, dw, dqn, dk_p, dgc_p = _gdn_scan_bwd(qn, kn, w, vn, gc, gdn_st, dyg)
    dkn, dvg, dbeta, dgc = _gdn_chunk_bwd(kn, vg, beta, gc, tmat, du, dw, dk_p, dgc_p)
    dba, d_a_log, d_dt_bias = _gdn_scal_bwd(ba, small["a_log"], small["dt_bias"], g, dbeta, dgc)
    dacc = _gdn_conv_bwd_act(h1, conv_w, dqn, "q")
    dacc = _gdn_conv_bwd_act(h1, conv_w, dkn, "k", dacc)
    dacc = _gdn_conv_bwd_act(h1, conv_w, dvg, "v", dacc)
    dh1, d_conv_w = _gdn_conv_bwd_in(h1, conv_w, dacc, dh1_z)
    g_go = [shards_of(dwgo)]
    dx2_ba, (th_go,) = _mm(dba, wba, "nt", "mm_gdn_ba_dx", epi="add", extra=dz3, scale=ALPHA,
                           sides=[_rs_swap_side(g_go)])
    sums_a = sums_a + _rs_add(g_go, th_go, "go")
    dx2, (parts_a,) = _mm(dh1, wgi_main, "nt", "mm_gdn_in_dx", epi="add", extra=dx2_ba,
                          sides=[_rs_owner_side(sums_a)])
    mine_a = _rs_sum(parts_a, "a")
    dwgi_main, (back_a,) = _mm(x2b, dh1, "tn", "mm_gdn_in_dw", sides=[_rs_back_side(mine_a)])
    red_w1_1, red_w2_1, red_go = zip(mine_a, back_a)
    dwba = _mm(x2b, dba, "tn", "mm_gdn_ba_dw")
    dwgi = jnp.concatenate([dwgi_main, dwba[:, :2 * GDN_HV]], axis=1)
    g_b = [jnp.transpose(dwgi.reshape(dwgi.shape[0], N_CHIPS, -1), (1, 0, 2))]

    dz2, dz2b, d_lnf_g0, d_lnf_b0 = _ln_bwd(dx2, z2, small["ln_ffn_g"][0:1], "ln_ffn0_bwd")
    dhh0, (th_b,) = _mm(dz2b, w2[0], "nt", "mm_mlp0_down_dx", epi="drelu2", extra=hh0, out_dtype=BF16,
                        sides=[_rs_swap_side(g_b)])
    sums_b = _rs_add(g_b, th_b, "b")
    dw2_0 = _mm(a0, dz2b, "tn", "mm_mlp0_down_dw")
    dx1, (parts_b,) = _mm(dhh0, w1[0], "nt", "mm_mlp0_up_dx", epi="add", extra=dz2, scale=ALPHA,
                          sides=[_rs_owner_side(sums_b)])
    mine_b = _rs_sum(parts_b, "b")
    dw1_0 = _mm(x1b, dhh0, "tn", "mm_mlp0_up_dw", shard_major=True)
    dz1, dz1b, d_lnm_g0, d_lnm_b0 = _ln_bwd(dx1, z1, small["ln_mix_g"][0:1], "ln_mix0_bwd")
    g_c = [dw1_0, shards_of(dw2_0)]
    do0, (th_c, back_b) = _mm(dz1b, wro, "nt", "mm_ret_out_dx", sides=[_rs_swap_side(g_c), _rs_back_side(mine_b)])
    (red_gi,) = zip(mine_b, back_b)
    sums_c = _rs_add(g_c, th_c, "c")
    dwro = _mm(o0, dz1b, "tn", "mm_ret_out_dw")
    g_ro = [shards_of(dwro)]
    (dq0, dk0, dv0, dgate, d_gn_g), (th_ro, parts_c) = _ret_bwd(
        h0, do0, yr, small["ret_gn_g"], ret_st, rc, cos, sin, sides=[_rs_swap_side(g_ro), _rs_owner_side(sums_c)])
    sums_ro = _rs_add(g_ro, th_ro, "ro")
    mine_c = _rs_sum(parts_c, "c")
    dh0 = jnp.concatenate([dq0, dk0, dv0, dgate], axis=1)
    dwri, (parts_ro, back_c) = _mm(xb, dh0, "tn", "mm_ret_in_dw", shard_major=True,
                                   sides=[_rs_owner_side(sums_ro), _rs_back_side(mine_c)])
    red_w1_0, red_w2_0 = zip(mine_c, back_c)
    mine_ro = _rs_sum(parts_ro, "ro")
    g_d = [dwri]
    sums_d = _rs_add(g_d, _comm_call("rs_swap_halves_d", _rs_swap_side(g_d)), "d")
    grad_x, (parts_d, back_ro) = _mm(dh0, wri, "nt", "mm_ret_in_dx", epi="add", extra=dz1, scale=ALPHA,
                                     sides=[_rs_owner_side(sums_d), _rs_back_side(mine_ro)])
    (red_ro,) = zip(mine_ro, back_ro)
    mine_d = _rs_sum(parts_d, "d")
    (red_ri,) = zip(mine_d, _comm_call("rs_swap_reduced_d", _rs_back_side(mine_d)))

    big = dict(ret_w_in=red_ri, ret_w_out=red_ro, gdn_w_in=red_gi, gdn_w_out=red_go,
               mlp_w1=(red_w1_0, red_w1_1), mlp_w2=(red_w2_0, red_w2_1))
    sm = dict(ret_gn_g=d_gn_g, a_log=d_a_log, dt_bias=d_dt_bias, norm_g=d_norm_g,
              ln_mix_g=jnp.concatenate([d_lnm_g0, d_lnm_g1], 0), ln_mix_b=jnp.concatenate([d_lnm_b0, d_lnm_b1], 0),
              ln_ffn_g=jnp.concatenate([d_lnf_g0, d_lnf_g1], 0), ln_ffn_b=jnp.concatenate([d_lnf_b0, d_lnf_b1], 0),
              conv_w=d_conv_w)
    return loss, grad_x, big, sm


def _coords():
    return lax.axis_index("x"), lax.axis_index("y"), lax.axis_index("c")


HBM_SPEC = pl.BlockSpec(memory_space=pl.ANY)


def _other_chips(x, y):
    return [(1 - x, y), (x, 1 - y), (1 - x, 1 - y)]


def _ag_side(shards, split, cols, place_own=True):
    n = len(shards)

    def rows_of(p, core):
        half = shards[p].shape[0] // 2
        return pl.ds(core * half, half) if split[p] else slice(None)

    def slot(outs, p, s, core=None):
        r = slice(None) if core is None else rows_of(p, core)
        if cols[p]:
            w = shards[p].shape[1]
            return outs[p].at[r, pl.ds(pl.multiple_of(s * w, LANES), w)]
        return outs[p].at[s, r]

    def over_ici(xyc, ins, outs):
        x, y, c = xyc
        return [(ins[p].at[rows_of(p, c)], slot(outs, p, 2 * x + y, c), (cx, cy, c))
                for p in range(n) for cx, cy in _other_chips(x, y)]

    def to_sibling(xyc, ins, outs):
        x, y, c = xyc
        zones = [slot(outs, p, 2 * cx + cy, c) for p in range(n) if split[p] for cx, cy in _other_chips(x, y)]
        return [(z, z, (x, y, 1 - c)) for z in zones]

    def own(xyc, ins, outs):
        x, y, _ = xyc
        return [(ins[p], slot(outs, p, 2 * x + y)) for p in range(n)]

    n_split = sum(bool(s) for s in split)
    phases, counts = [over_ici], [3 * n]
    if n_split:
        phases, counts = phases + [to_sibling], counts + [3 * n_split]
    shapes = [SDS((s.shape[0], N_CHIPS * s.shape[1]) if cols[p] else (N_CHIPS,) + s.shape, s.dtype)
              for p, s in enumerate(shards)]
    if not place_own:
        return _Side(shards, shapes, phases, counts)
    return _Side(shards, shapes, phases, counts, local=own, n_local=n)


def _rs_swap_side(grads):
    n = len(grads)
    halves = [g.shape[1] // 2 for g in grads]

    def swap(xyc, ins, outs):
        x, y, c = xyc
        return [(ins[p].at[:, pl.ds((1 - c) * halves[p], halves[p])], outs[p], (x, y, 1 - c)) for p in range(n)]

    return _Side(grads, [SDS((N_CHIPS, halves[p]) + g.shape[2:], F32) for p, g in enumerate(grads)], [swap], [n])


def _rs_add(grads, theirs, tag):
    c = lax.axis_index("c")
    return [_add_half(g, t, c, "rs_add_%s%d" % (tag, p)) for p, (g, t) in enumerate(zip(grads, theirs))]


def _rs_owner_side(chip_sums):
    n = len(chip_sums)

    def to_owner(xyc, ins, outs):
        x, y, c = xyc
        return [(ins[p].at[2 * cx + cy], outs[p].at[2 * x + y], (cx, cy, c))
                for p in range(n) for cx, cy in _other_chips(x, y)]

    def own(xyc, ins, outs):
        x, y, _ = xyc
        return [(ins[p].at[2 * x + y], outs[p].at[2 * x + y]) for p in range(n)]

    return _Side(chip_sums, [SDS(s.shape, s.dtype) for s in chip_sums], [to_owner], [3 * n], local=own, n_local=n)


def _rs_sum(parts, tag):
    return [_sum_chips(pt, "rs_sum_%s%d" % (tag, p)) for p, pt in enumerate(parts)]


def _rs_back_side(mine):
    n = len(mine)

    def swap(xyc, ins, outs):
        x, y, c = xyc
        return [(ins[p], outs[p], (x, y, 1 - c)) for p in range(n)]

    return _Side(mine, [SDS(m.shape, F32) for m in mine], [swap], [n])


def _add_half(g, theirs, c, name):
    _, R, C = g.shape
    half = R // 2
    tr = min(256, half)
    nb = half // tr

    def body(c_ref, g_ref, t_ref, o_ref):
        o_ref[...] = (g_ref[...] + t_ref[...]).astype(BF16)

    blk = pl.BlockSpec((None, tr, C), lambda s, i, c_ref: (s, i, 0))
    return pl.pallas_call(
        body, name=name,
        grid_spec=pltpu.PrefetchScalarGridSpec(
            num_scalar_prefetch=1, grid=(N_CHIPS, nb),
            in_specs=[pl.BlockSpec((None, tr, C), lambda s, i, c_ref: (s, c_ref[0] * nb + i, 0)), blk],
            out_specs=blk),
        out_shape=SDS((N_CHIPS, half, C), BF16), compiler_params=_cp("parallel", "parallel"),
    )(jnp.reshape(c, (1,)).astype(jnp.int32), g, theirs)


def _sum_chips(parts, name):
    _, r, C = parts.shape
    tr = min(256, r)

    def body(p_ref, o_ref):
        f = lambda s: p_ref[s].astype(F32)
        o_ref[...] = ((f(0) + f(1)) + f(2)) + f(3)

    return pl.pallas_call(
        body, name=name, grid=(r // tr,), in_specs=[pl.BlockSpec((N_CHIPS, tr, C), lambda i: (0, i, 0))],
        out_specs=pl.BlockSpec((tr, C), lambda i: (i, 0)), out_shape=SDS((r, C), F32), compiler_params=_cp("parallel"),
    )(parts)


def _all_reduce_small(buf):
    rows = buf.shape[0]

    def body(x_ref, o_ref, all_ref, send_sems, recv_sems):
        x, y, c = _coords()
        me = 4 * x + 2 * y + c
        all_ref[me] = x_ref[...]
        flips = [(fx, fy, fc) for fx in (0, 1) for fy in (0, 1) for fc in (0, 1)][1:]
        copies = []
        for k, (fx, fy, fc) in enumerate(flips):
            to = (x ^ fx, y ^ fy, c ^ fc)
            copies.append(pltpu.make_async_remote_copy(src_ref=x_ref, dst_ref=all_ref.at[me], send_sem=send_sems.at[k],
                                                       recv_sem=recv_sems.at[k], device_id=to, device_id_type=MESH))
        for cp in copies:
            cp.start()
        for cp in copies:
            cp.wait_recv()
        for cp in copies:
            cp.wait_send()
        acc = all_ref[0]
        for d in range(1, N_DEV):
            acc = acc + all_ref[d]
        o_ref[...] = acc

    vm = pl.BlockSpec(memory_space=pltpu.VMEM)
    return pl.pallas_call(
        body, name="all_reduce_small", in_specs=[vm], out_specs=vm, out_shape=SDS((rows, LANES), F32),
        scratch_shapes=[pltpu.VMEM((N_DEV, rows, LANES), F32), pltpu.SemaphoreType.DMA((N_DEV - 1,)),
                        pltpu.SemaphoreType.DMA((N_DEV - 1,))],
    )(buf)


def _adam_update(w, gv, m, v):
    mn = ADAM_B1 * m + (1.0 - ADAM_B1) * gv
    vn = ADAM_B2 * v + (1.0 - ADAM_B2) * (gv * gv)
    m_hat = mn / (1.0 - ADAM_B1 ** ADAM_STEP)
    v_hat = vn / (1.0 - ADAM_B2 ** ADAM_STEP)
    return -ADAM_LR * (m_hat / (jnp.sqrt(v_hat) + ADAM_EPS) + ADAM_WD * w), mn, vn


def _adamw_halves(w, mine, theirs, m, v, c, name, row0=0, bufs=None):
    R, C = w.shape
    half = mine.shape[0]
    tr = min(128, half)
    nbh = half // tr
    b0 = row0 // tr
    assert row0 % tr == 0 and half % tr == 0

    def body(c_ref, w_ref, a_ref, b_ref, m_ref, v_ref, *rest):
        g_ref, d_ref, mo_ref, vo_ref = rest[-4:]
        is_mine = (pl.program_id(0) // nbh) == c_ref[0]
        gv = jnp.where(is_mine, a_ref[...], b_ref[...])
        g_ref[...] = gv
        d_ref[...], mo_ref[...], vo_ref[...] = _adam_update(w_ref[...], gv, m_ref[...], v_ref[...])

    blk = pl.BlockSpec((tr, C), lambda i, c_ref: (b0 + i, 0))
    ablk = pl.BlockSpec((tr, C), lambda i, c_ref: (jnp.where(i // nbh == c_ref[0], i % nbh, 0), 0))
    bblk = pl.BlockSpec((tr, C), lambda i, c_ref: (jnp.where(i // nbh == c_ref[0], 0, i % nbh), 0))
    extra = [] if bufs is None else list(bufs)
    return pl.pallas_call(
        body, name=name,
        grid_spec=pltpu.PrefetchScalarGridSpec(num_scalar_prefetch=1, grid=(2 * nbh,),
                                               in_specs=[blk, ablk, bblk, blk, blk] + [HBM_SPEC] * len(extra),
                                               out_specs=(blk,) * 4),
        out_shape=(SDS((R, C), F32),) * 4, compiler_params=_cp("parallel"),
        input_output_aliases={6 + i: i for i in range(len(extra))},
    )(jnp.reshape(c, (1,)).astype(jnp.int32), w, mine, theirs, m, v, *extra)


def _adamw(w, g, m, v, name):
    R, C = w.shape
    tr = min(256, R)
    assert R % tr == 0

    def body(w_ref, g_ref, m_ref, v_ref, d_ref, mo_ref, vo_ref):
        d_ref[...], mo_ref[...], vo_ref[...] = _adam_update(w_ref[...], g_ref[...], m_ref[...], v_ref[...])

    blk = pl.BlockSpec((tr, C), lambda i: (i, 0))
    return pl.pallas_call(
        body, name=name, grid=(R // tr,), in_specs=[blk] * 4, out_specs=(blk,) * 3,
        out_shape=(SDS((R, C), F32),) * 3, compiler_params=_cp("parallel"),
    )(w, g, m, v)


def _pack(arrs):
    rows = []
    for a in arrs:
        flat = a.reshape(-1).astype(F32)
        pad = (-flat.shape[0]) % LANES
        rows.append(jnp.pad(flat, (0, pad)).reshape(-1, LANES))
    buf = jnp.concatenate(rows, axis=0)
    pad_rows = (-buf.shape[0]) % 8
    return jnp.pad(buf, ((0, pad_rows), (0, 0)))


def _unpack(buf, shapes):
    out, r = [], 0
    for shp in shapes:
        size = int(np.prod(shp))
        nr = -(-size // LANES)
        out.append(buf[r:r + nr].reshape(-1)[:size].reshape(shp))
        r += nr
    return out


def _pad_lanes(a):
    return jnp.pad(a, ((0, 0), (0, LANES - a.shape[1])))


def kernel(x, ret_w_in, ret_gn_g, ret_w_out, gdn_w_in, gdn_conv_w, gdn_a_log, gdn_dt_bias, gdn_norm_g, gdn_w_out, ln_mix_g, ln_mix_b, mlp_w1, mlp_w2, ln_ffn_g, ln_ffn_b, loss_target, m_ret_w_in, m_ret_gn_g, m_ret_w_out, m_gdn_w_in, m_gdn_conv_w, m_gdn_a_log, m_gdn_dt_bias, m_gdn_norm_g, m_gdn_w_out, m_ln_mix_g, m_ln_mix_b, m_mlp_w1, m_mlp_w2, m_ln_ffn_g, m_ln_ffn_b, v_ret_w_in, v_ret_gn_g, v_ret_w_out, v_gdn_w_in, v_gdn_conv_w, v_gdn_a_log, v_gdn_dt_bias, v_gdn_norm_g, v_gdn_w_out, v_ln_mix_g, v_ln_mix_b, v_mlp_w1, v_mlp_w2, v_ln_ffn_g, v_ln_ffn_b):
    cx, cy = lax.axis_index("x"), lax.axis_index("y")
    chip = 2 * cx + cy

    sh = dict(ret_w_in=ret_w_in[0].astype(BF16), ret_w_out=ret_w_out[0].astype(BF16),
              gdn_w_in=gdn_w_in[0].astype(BF16), gdn_w_out=gdn_w_out[0].astype(BF16),
              mlp_w1_0=mlp_w1[0].astype(BF16), mlp_w1_1=mlp_w1[1].astype(BF16),
              mlp_w2_0=mlp_w2[0].astype(BF16), mlp_w2_1=mlp_w2[1].astype(BF16), conv_w=gdn_conv_w[0])
    small = dict(ret_gn_g=ret_gn_g, a_log=_pad_lanes(gdn_a_log), dt_bias=_pad_lanes(gdn_dt_bias), norm_g=gdn_norm_g,
                 ln_mix_g=ln_mix_g, ln_mix_b=ln_mix_b, ln_ffn_g=ln_ffn_g, ln_ffn_b=ln_ffn_b)

    loss, grad_x, big, sm = _local_step(x[0], loss_target[0], sh, small)

    small_names = ["ret_gn_g", "a_log", "dt_bias", "norm_g", "ln_mix_g", "ln_mix_b", "ln_ffn_g", "ln_ffn_b", "conv_w"]
    small_shapes = [(1, 4096), (1, LANES), (1, LANES), (1, GDN_D), (2, D_MODEL), (2, D_MODEL), (2, D_MODEL),
                    (2, D_MODEL), (4, GDN_QKV)]
    red = _all_reduce_small(_pack([loss] + [sm[k] for k in small_names]))
    red_loss, *red_small = _unpack(red, [(1, 1)] + small_shapes)
    gs = dict(zip(small_names, red_small))
    g_conv = lax.dynamic_slice_in_dim(gs["conv_w"], chip * 2048, 2048, axis=1)
    g_a_log, g_dt_bias = gs["a_log"][:, :GDN_HV], gs["dt_bias"][:, :GDN_HV]

    big_w = [(ret_w_in, m_ret_w_in, v_ret_w_in, [big["ret_w_in"]]), (ret_w_out, m_ret_w_out, v_ret_w_out, [big["ret_w_out"]]),
             (gdn_w_in, m_gdn_w_in, v_gdn_w_in, [big["gdn_w_in"]]), (gdn_w_out, m_gdn_w_out, v_gdn_w_out, [big["gdn_w_out"]]),
             (mlp_w1, m_mlp_w1, v_mlp_w1, big["mlp_w1"]), (mlp_w2, m_mlp_w2, v_mlp_w2, big["mlp_w2"])]
    core = lax.axis_index("c")
    big_out = []
    for i, (w_, m_, v_, layers) in enumerate(big_w):
        two_d = lambda a: a.reshape(-1, a.shape[-1])
        res = None
        for j, (mine, theirs) in enumerate(layers):
            res = _adamw_halves(two_d(w_), mine, theirs, two_d(m_), two_d(v_), core, "adamw_%d_%d" % (i, j),
                                row0=j * 2 * mine.shape[0], bufs=res)
        big_out.append(tuple(a.reshape(w_.shape) for a in res))
    sm_w = [(ret_gn_g, m_ret_gn_g, v_ret_gn_g, gs["ret_gn_g"]), (gdn_conv_w, m_gdn_conv_w, v_gdn_conv_w, g_conv),
            (gdn_a_log, m_gdn_a_log, v_gdn_a_log, g_a_log), (gdn_dt_bias, m_gdn_dt_bias, v_gdn_dt_bias, g_dt_bias),
            (gdn_norm_g, m_gdn_norm_g, v_gdn_norm_g, gs["norm_g"]), (ln_mix_g, m_ln_mix_g, v_ln_mix_g, gs["ln_mix_g"]),
            (ln_mix_b, m_ln_mix_b, v_ln_mix_b, gs["ln_mix_b"]), (ln_ffn_g, m_ln_ffn_g, v_ln_ffn_g, gs["ln_ffn_g"]),
            (ln_ffn_b, m_ln_ffn_b, v_ln_ffn_b, gs["ln_ffn_b"])]
    sm_shapes = [w_.shape for w_, _, _, _ in sm_w]
    d_s, nm_s, nv_s = _adamw(_pack([w_ for w_, _, _, _ in sm_w]), _pack([g_ for _, _, _, g_ in sm_w]),
                             _pack([m_ for _, m_, _, _ in sm_w]), _pack([v_ for _, _, v_, _ in sm_w]), "adamw_small")
    d_s, nm_s, nv_s = (_unpack(a, sm_shapes) for a in (d_s, nm_s, nv_s))
    sm_out = [(g_.reshape(w_.shape), d_s[i], nm_s[i], nv_s[i]) for i, (w_, _, _, g_) in enumerate(sm_w)]

    per_w = [big_out[0], sm_out[0], big_out[1], big_out[2], sm_out[1], sm_out[2], sm_out[3], sm_out[4], big_out[3],
             sm_out[5], sm_out[6], big_out[4], big_out[5], sm_out[7], sm_out[8]]
    outs = [red_loss.reshape(()), grad_x[None]]
    for kind in range(4):
        outs.extend(t[kind] for t in per_w)
    return tuple(outs)
```

```python
import functools

import numpy as np
import jax
import jax.numpy as jnp
from jax import lax
from jax.experimental import pallas as pl
from jax.experimental.pallas import tpu as pltpu

F32 = jnp.float32
BF16 = jnp.bfloat16
MESH = pl.DeviceIdType.MESH
SDS = jax.ShapeDtypeStruct

D_MODEL = 2048
CHUNK = 64
RET_HEADS, RET_DK, RET_DV = 8, 256, 512
GDN_HV, GDN_D = 32, 128
HP = 2
KB = 2
NH = KB * HP
GDN_QKV = 8192
ALPHA = 4.0 ** 0.25
LN_EPS, GN_EPS, RMS_EPS, L2_EPS = 1e-5, 1e-6, 1e-6, 1e-6
ADAM_LR, ADAM_B1, ADAM_B2, ADAM_EPS, ADAM_WD, ADAM_STEP = 0.001, 0.9, 0.999, 1e-8, 0.01, 10

VMEM_LIMIT_BYTES = 56 * 1024 * 1024
RT = 256
CT = 256
ROWS = 256
LANES = 128
N_CHIPS = 4
N_DEV = 8


def _cp(*sem):
    return pltpu.CompilerParams(dimension_semantics=sem, vmem_limit_bytes=VMEM_LIMIT_BYTES)


def _dot(a, b):
    return jnp.dot(a, b, preferred_element_type=F32)


def _dot_nt(a, b):
    return lax.dot_general(a, b, (((1,), (1,)), ((), ())), preferred_element_type=F32)


def _dot_tn(a, b):
    return lax.dot_general(a, b, (((0,), (0,)), ((), ())), preferred_element_type=F32)


def _split2(x):
    hi = x.astype(BF16)
    lo = (x - hi.astype(F32)).astype(BF16)
    return hi, lo


def _dotx3(a, b):
    ah, al = _split2(a)
    bh, bl = _split2(b)
    return _dot(ah, bh) + (_dot(ah, bl) + _dot(al, bh))


def _dot_exact_l(l_bf16, x):
    hi = x.astype(BF16)
    r = x - hi.astype(F32)
    mid = r.astype(BF16)
    lo = (r - mid.astype(F32)).astype(BF16)
    return _dot(l_bf16, hi) + (_dot(l_bf16, mid) + _dot(l_bf16, lo))


def _sigmoid(x):
    return 1.0 / (1.0 + jnp.exp(-x))


def _iota(shape, dim):
    return lax.broadcasted_iota(jnp.int32, shape, dim)


class _Side:
    def __init__(self, arrays, out_shapes, phases, counts, local=None, n_local=0):
        self.arrays, self.out_shapes, self.phases, self.counts = list(arrays), list(out_shapes), phases, counts
        self.local, self.n_local = local, n_local

    def sem_shapes(self):
        sems = [pltpu.SemaphoreType.DMA((sum(self.counts),))] * 2
        return sems + ([pltpu.SemaphoreType.DMA((self.n_local,))] if self.n_local else [])

    def ops(self, ins, outs, sems):
        send_sems, recv_sems = sems[0], sems[1]

        def copies(ph):
            if ph < 0:
                return [pltpu.make_async_copy(src, dst, sems[2].at[i])
                        for i, (src, dst) in enumerate(self.local(_coords(), ins, outs))] if self.n_local else []
            off = sum(self.counts[:ph])
            return [pltpu.make_async_remote_copy(src_ref=src, dst_ref=dst, send_sem=send_sems.at[off + i],
                                                 recv_sem=recv_sems.at[off + i], device_id=to, device_id_type=MESH)
                    for i, (src, dst, to) in enumerate(self.phases[ph](_coords(), ins, outs))]

        def start(ph):
            for cp in copies(ph):
                cp.start()

        def wait(ph):
            cps = copies(ph)
            if ph < 0:
                for cp in cps:
                    cp.wait()
                return
            for cp in cps:
                cp.wait_recv()
            for cp in cps:
                cp.wait_send()

        return start, wait


def _comm_call(name, side):
    n = len(side.arrays)

    def body(*refs):
        start, wait = side.ops(refs[:n], refs[n:2 * n], refs[2 * n:])
        start(-1)
        for ph in range(len(side.phases)):
            start(ph)
            wait(ph)
        wait(-1)

    return pl.pallas_call(body, name=name, in_specs=[HBM_SPEC] * n, out_specs=[HBM_SPEC] * n,
                          out_shape=side.out_shapes, scratch_shapes=side.sem_shapes())(*side.arrays)


SIDE_SWITCH = 0.85


def _pcall(body, args, *, name, grid, in_specs, out_specs, out_shape, sem, scratch_shapes=(), sides=()):
    single = not isinstance(out_shape, (tuple, list))
    if not sides:
        return pl.pallas_call(body, name=name, grid=grid, in_specs=list(in_specs), out_specs=out_specs,
                              out_shape=out_shape, scratch_shapes=list(scratch_shapes), compiler_params=_cp(*sem))(*args)
    o_shapes = (out_shape,) if single else tuple(out_shape)
    o_specs = (out_specs,) if single else tuple(out_specs)
    n_in, n_out, n_scr = len(args), len(o_shapes), len(scratch_shapes)
    ns = [len(s.arrays) for s in sides]
    steps = int(np.prod(grid))

    def carrier(*refs):
        pos = n_in
        s_ins = []
        for k in ns:
            s_ins.append(refs[pos:pos + k])
            pos += k
        outs = refs[pos:pos + n_out]
        pos += n_out
        s_outs = []
        for k in ns:
            s_outs.append(refs[pos:pos + k])
            pos += k
        scr = refs[pos:pos + n_scr]
        pos += n_scr
        step = pl.program_id(0)
        for d in range(1, len(grid)):
            step = step * grid[d] + pl.program_id(d)
        hooks = []
        for i, s in enumerate(sides):
            k = len(s.sem_shapes())
            hooks.append(s.ops(s_ins[i], s_outs[i], refs[pos:pos + k]))
            pos += k
        for (start, wait), s in zip(hooks, sides):
            def first(start=start):
                start(-1)
                start(0)
            pl.when(step == 0)(first)
            if len(s.phases) == 2:
                def switch(start=start, wait=wait):
                    wait(0)
                    start(1)
                pl.when(step == int(steps * SIDE_SWITCH))(switch)
            else:
                assert len(s.phases) == 1
        body(*refs[:n_in], *outs, *scr)
        for (start, wait), s in zip(hooks, sides):
            def last(wait=wait, n_ph=len(s.phases)):
                wait(n_ph - 1)
                wait(-1)
            pl.when(step == steps - 1)(last)

    res = pl.pallas_call(
        carrier, name=name, grid=grid, in_specs=list(in_specs) + [HBM_SPEC] * sum(ns),
        out_specs=o_specs + (HBM_SPEC,) * sum(ns),
        out_shape=o_shapes + tuple(sh for s in sides for sh in s.out_shapes),
        scratch_shapes=list(scratch_shapes) + [sm for s in sides for sm in s.sem_shapes()],
        compiler_params=_cp(*(("arbitrary",) * len(grid))),
    )(*args, *[a for s in sides for a in s.arrays])
    main, rest, side_res = res[:n_out], list(res[n_out:]), []
    for k in ns:
        side_res.append(rest[:k])
        rest = rest[k:]
    return (main[0] if single else tuple(main)), side_res


def _mm(a, b, mode, name, *, out_dtype=F32, tm=1024, tn=1024, tk=2048, epi=None, extra=None, scale=1.0,
        shard_major=False, sides=()):
    if mode == "nn":
        (M, K), (K2, N) = a.shape, b.shape
    elif mode == "nt":
        (M, K), (N, K2) = a.shape, b.shape
    else:
        (K, M), (K2, N) = a.shape, b.shape
    assert K == K2, (a.shape, b.shape, mode)
    tm, tn, tk = min(tm, M), min(tn, N), min(tk, K)
    assert M % tm == 0 and N % tn == 0 and K % tk == 0, (M, N, K, tm, tn, tk)
    nk = K // tk
    dims = {"nn": (((1,), (0,)), ((), ())), "nt": (((1,), (1,)), ((), ())), "tn": (((0,), (0,)), ((), ()))}[mode]
    if mode == "tn":
        a_spec = pl.BlockSpec((tk, tm), lambda i, j, k: (k, i))
    else:
        a_spec = pl.BlockSpec((tm, tk), lambda i, j, k: (i, k))
    if mode == "nt":
        b_spec = pl.BlockSpec((tn, tk), lambda i, j, k: (j, k))
    else:
        b_spec = pl.BlockSpec((tk, tn), lambda i, j, k: (k, j))
    tile = pl.BlockSpec((tm, tn), lambda i, j, k: (i, j))
    in_specs, ins = [a_spec, b_spec], [a, b]
    if epi == "rope":
        half = RET_DK // 2
        assert tn % RET_DK == 0 and (2 * RET_HEADS * RET_DK) % tn == 0
        in_specs += [pl.BlockSpec((tm, half), lambda i, j, k: (i, 0))] * 2
        ins += list(extra)
    elif epi == "ln":
        assert tn == N
        vec = pl.BlockSpec((1, tn), lambda i, j, k: (0, j))
        in_specs += [tile, vec, vec]
        ins += list(extra)
    elif extra is not None:
        in_specs.append(tile)
        ins.append(extra)
    n_extra = len(ins) - 2
    if epi == "relu2":
        out_shape = (SDS((M, N), F32), SDS((M, N), BF16))
        out_specs = (tile, tile)
    elif epi == "ln":
        out_shape = (SDS((M, N), F32), SDS((M, N), BF16), SDS((M, N), F32))
        out_specs = (tile, tile, tile)
    elif shard_major:
        per = (N // N_CHIPS) // tn
        assert per * tn * N_CHIPS == N
        out_shape = (SDS((N_CHIPS, M, N // N_CHIPS), out_dtype),)
        out_specs = (pl.BlockSpec((None, tm, tn), lambda i, j, k: (j // per, i, j % per)),)
    else:
        out_shape = (SDS((M, N), out_dtype),)
        out_specs = (tile,)
    n_out = len(out_shape)

    def body(*refs):
        a_ref, b_ref = refs[0], refs[1]
        x_ref = refs[2] if n_extra else None
        pos = 2 + n_extra
        o_refs = refs[pos:pos + n_out]
        acc_ref = refs[pos + n_out] if nk > 1 else None

        def prod():
            av, bv = a_ref[...], b_ref[...]
            if av.dtype != BF16:
                av = av.astype(BF16)
            if bv.dtype != BF16:
                bv = bv.astype(BF16)
            return lax.dot_general(av, bv, dims, preferred_element_type=F32)

        def finish(acc):
            if epi == "relu2":
                o_refs[0][...] = acc
                r = jnp.maximum(acc, 0.0)
                o_refs[1][...] = (r * r).astype(BF16)
            elif epi == "drelu2":
                o_refs[0][...] = (acc * (2.0 * jnp.maximum(x_ref[...], 0.0))).astype(out_dtype)
            elif epi == "add":
                o_refs[0][...] = (acc + scale * x_ref[...]).astype(out_dtype)
            elif epi == "ln":
                z = ALPHA * x_ref[...] + acc
                xh, _ = _ln_stats(z)
                o = xh * refs[3][...] + refs[4][...]
                o_refs[0][...] = o
                o_refs[1][...] = o.astype(BF16)
                o_refs[2][...] = z
            elif epi == "rope":
                j = pl.program_id(1)
                qk_tiles = 2 * RET_HEADS * RET_DK // tn

                @pl.when(j < qk_tiles)
                def _():
                    c, s = refs[2][...], refs[3][...]
                    sc = jnp.where(j >= qk_tiles // 2, RET_DK ** -0.5, 1.0)
                    for hh in range(tn // RET_DK):
                        lo = slice(hh * RET_DK, hh * RET_DK + half)
                        hi = slice(hh * RET_DK + half, (hh + 1) * RET_DK)
                        t1, t2 = acc[:, lo], acc[:, hi]
                        o_refs[0][:, lo] = ((t1 * c - t2 * s) * sc).astype(out_dtype)
                        o_refs[0][:, hi] = ((t1 * s + t2 * c) * sc).astype(out_dtype)

                @pl.when(j >= qk_tiles)
                def _():
                    o_refs[0][...] = acc.astype(out_dtype)
            else:
                o_refs[0][...] = acc.astype(out_dtype)

        if nk == 1:
            finish(prod())
        else:
            k = pl.program_id(2)

            @pl.when(k == 0)
            def _():
                acc_ref[...] = prod()

            @pl.when(k > 0)
            def _():
                acc_ref[...] += prod()

            @pl.when(k == nk - 1)
            def _():
                finish(acc_ref[...])

    res = _pcall(body, ins, name=name, grid=(M // tm, N // tn, nk), in_specs=in_specs, out_specs=out_specs,
                 out_shape=out_shape, scratch_shapes=[pltpu.VMEM((tm, tn), F32)] if nk > 1 else [],
                 sem=("parallel", "parallel", "arbitrary"), sides=sides)
    main, side_res = res if sides else (res, None)
    main = main if n_out > 1 else main[0]
    return (main, side_res) if sides else main


def _ln_stats(z):
    mu = jnp.mean(z, -1, keepdims=True)
    zc = z - mu
    var = jnp.mean(zc * zc, -1, keepdims=True)
    rstd = lax.rsqrt(var + LN_EPS)
    return zc * rstd, rstd


def _ln_bwd(dout, z, g, name):
    S, Dm = z.shape
    row = pl.BlockSpec((ROWS, Dm), lambda t: (t, 0))
    vec = pl.BlockSpec((1, Dm), lambda t: (0, 0))

    def body(d_ref, z_ref, g_ref, dz_ref, dzb_ref, dg_ref, db_ref):
        t = pl.program_id(0)
        xh, rstd = _ln_stats(z_ref[...])
        d = d_ref[...]
        dxh = d * g_ref[...]
        m1 = jnp.mean(dxh, -1, keepdims=True)
        m2 = jnp.mean(dxh * xh, -1, keepdims=True)
        dz = rstd * (dxh - m1 - xh * m2)
        dz_ref[...] = dz
        dzb_ref[...] = dz.astype(BF16)
        pg = jnp.sum(d * xh, axis=0, keepdims=True)
        pb = jnp.sum(d, axis=0, keepdims=True)

        @pl.when(t == 0)
        def _():
            dg_ref[...] = pg
            db_ref[...] = pb

        @pl.when(t > 0)
        def _():
            dg_ref[...] += pg
            db_ref[...] += pb

    return pl.pallas_call(
        body, name=name, grid=(S // ROWS,), in_specs=[row, row, vec], out_specs=(row, row, vec, vec),
        out_shape=(SDS((S, Dm), F32), SDS((S, Dm), BF16), SDS((1, Dm), F32), SDS((1, Dm), F32)),
        compiler_params=_cp("arbitrary"),
    )(dout, z, g)


def _loss_ln_bwd(z, g, b, tgt, name):
    S, Dm = z.shape
    row = pl.BlockSpec((ROWS, Dm), lambda t: (t, 0))
    vec = pl.BlockSpec((1, Dm), lambda t: (0, 0))
    one = pl.BlockSpec((1, 1), lambda t: (0, 0))

    def body(z_ref, g_ref, b_ref, t_ref, l_ref, dz_ref, dzb_ref, dg_ref, db_ref):
        t = pl.program_id(0)
        xh, rstd = _ln_stats(z_ref[...])
        diff = xh * g_ref[...] + b_ref[...] - t_ref[...]
        part = jnp.sum(jnp.sum(diff * diff, axis=1, keepdims=True), axis=0, keepdims=True) * (0.5 / Dm)
        d = diff * (1.0 / Dm)
        dxh = d * g_ref[...]
        m1 = jnp.mean(dxh, -1, keepdims=True)
        m2 = jnp.mean(dxh * xh, -1, keepdims=True)
        dz = rstd * (dxh - m1 - xh * m2)
        dz_ref[...] = dz
        dzb_ref[...] = dz.astype(BF16)
        pg = jnp.sum(d * xh, axis=0, keepdims=True)
        pb = jnp.sum(d, axis=0, keepdims=True)

        @pl.when(t == 0)
        def _():
            l_ref[...] = part
            dg_ref[...] = pg
            db_ref[...] = pb

        @pl.when(t > 0)
        def _():
            l_ref[...] += part
            dg_ref[...] += pg
            db_ref[...] += pb

    return pl.pallas_call(
        body, name=name, grid=(S // ROWS,), in_specs=[row, vec, vec, row], out_specs=(one, row, row, vec, vec),
        out_shape=(SDS((1, 1), F32), SDS((S, Dm), F32), SDS((S, Dm), BF16), SDS((1, Dm), F32), SDS((1, Dm), F32)),
        compiler_params=_cp("arbitrary"),
    )(z, g, b, tgt)


def _ret_consts():
    h = np.arange(RET_HEADS, dtype=np.float64)
    lg = np.log1p(-np.exp2(-5.0 - h))
    return jnp.asarray(np.concatenate([lg, np.exp(lg * RT)]).astype(np.float32))


def _rope_tables(S):
    half = RET_DK // 2
    inv = 10000.0 ** (-jnp.arange(half, dtype=F32) / half)
    ang = jnp.arange(S).astype(F32)[:, None] * inv[None, :]
    return jnp.cos(ang), jnp.sin(ang)


def _ret_masks(lgh):
    ri, ci = _iota((RT, RT), 0), _iota((RT, RT), 1)
    visible = (ci >> 6) <= (ri >> 6)
    m = jnp.where(visible, jnp.exp(lgh * jnp.abs(ri - ci).astype(F32)), 0.0)
    pos = _iota((RT, 1), 0).astype(F32)
    return m, jnp.exp(lgh * (pos + 1.0)), jnp.exp(lgh * (RT - 1.0 - pos))


def _ret_fwd(h0, gn_g, consts, sides=()):
    S = h0.shape[0]
    nt = S // RT

    def body(c_ref, q_ref, k_ref, v_ref, gate_ref, g_ref, y_ref, o_ref, st_ref, s_scr):
        h, t = pl.program_id(0), pl.program_id(1)

        @pl.when(t == 0)
        def _():
            s_scr[...] = jnp.zeros_like(s_scr)

        lgh, cdec = c_ref[h], c_ref[RET_HEADS + h]
        m, dq, dk = _ret_masks(lgh)
        qv, kv, vv = q_ref[...], k_ref[...], v_ref[...]
        p = (_dot_nt(qv, kv) * m).astype(BF16)
        sp = s_scr[...]
        spb = sp.astype(BF16)
        st_ref[...] = spb
        qd = (qv.astype(F32) * dq).astype(BF16)
        kd = (kv.astype(F32) * dk).astype(BF16)
        y = _dot(p, vv) + _dot(qd, spb)
        y_ref[...] = y
        s_scr[...] = sp * cdec + _dot_tn(kd, vv)
        yn, _ = _gn_stats(y)
        gate = gate_ref[...].astype(F32)
        o_ref[...] = (gate * _sigmoid(gate) * (yn * g_ref[...])).astype(BF16)

    qk = lambda off: pl.BlockSpec((RT, RET_DK), lambda h, t: (t, off + h))
    vs = lambda off: pl.BlockSpec((RT, RET_DV), lambda h, t: (t, off + h))
    return _pcall(
        body, (consts, h0, h0, h0, h0, gn_g), name="ret_fwd", grid=(RET_HEADS, nt),
        in_specs=[pl.BlockSpec(memory_space=pltpu.SMEM), qk(0), qk(RET_HEADS), vs(RET_HEADS), vs(2 * RET_HEADS),
                  pl.BlockSpec((1, RET_DV), lambda h, t: (0, h))],
        out_specs=(vs(0), vs(0), pl.BlockSpec((None, None, RET_DK, RET_DV), lambda h, t: (h, t, 0, 0))),
        out_shape=(SDS((S, 4096), F32), SDS((S, 4096), BF16), SDS((RET_HEADS, nt, RET_DK, RET_DV), BF16)),
        scratch_shapes=[pltpu.VMEM((RET_DK, RET_DV), F32)], sem=("parallel", "arbitrary"), sides=sides)


def _ret_bwd(h0, do, y, gn_g, states, consts, cos, sin, sides=()):
    S = h0.shape[0]
    nt = S // RT
    half = RET_DK // 2

    def body(c_ref, q_ref, k_ref, v_ref, gate_ref, do_ref, y_ref, g_ref, st_ref, cos_ref, sin_ref,
             dq_ref, dk_ref, dv_ref, dgate_ref, dg_ref, ds_scr):
        h, t = pl.program_id(0), pl.program_id(1)

        @pl.when(t == 0)
        def _():
            ds_scr[...] = jnp.zeros_like(ds_scr)
            dg_ref[...] = jnp.zeros_like(dg_ref)

        yn, rstd = _gn_stats(y_ref[...])
        gate, g, dov = gate_ref[...].astype(F32), g_ref[...], do_ref[...]
        sg = _sigmoid(gate)
        dgate_ref[...] = (dov * (yn * g) * (sg * (1.0 + gate * (1.0 - sg)))).astype(BF16)
        dyg = dov * (gate * sg)
        dyn = dyg * g
        m1 = jnp.mean(dyn, -1, keepdims=True)
        m2 = jnp.mean(dyn * yn, -1, keepdims=True)
        dyv = (rstd * (dyn - m1 - yn * m2)).astype(BF16)
        dg_ref[...] += jnp.sum(dyg * yn, axis=0, keepdims=True)

        lgh, cdec = c_ref[h], c_ref[RET_HEADS + h]
        m, dqc, dkc = _ret_masks(lgh)
        qv, kv, vv, spb = q_ref[...], k_ref[...], v_ref[...], st_ref[...]
        p = (_dot_nt(qv, kv) * m).astype(BF16)
        qd = (qv.astype(F32) * dqc).astype(BF16)
        kd = (kv.astype(F32) * dkc).astype(BF16)
        dsn = ds_scr[...]
        dsb = dsn.astype(BF16)
        dsc = (_dot_nt(dyv, vv) * m).astype(BF16)
        dq = _dot(dsc, kv) + _dot_nt(dyv, spb) * dqc
        dk = _dot_tn(dsc, qv) + _dot_nt(vv, dsb) * dkc
        dv_ref[...] = (_dot_tn(p, dyv) + _dot(kd, dsb)).astype(BF16)
        ds_scr[...] = dsn * cdec + _dot_tn(qd, dyv)
        c, s = cos_ref[...], sin_ref[...]

        def unrot(d):
            d1, d2 = d[:, :half], d[:, half:]
            return jnp.concatenate([d1 * c + d2 * s, d2 * c - d1 * s], axis=-1)

        dq_ref[...] = unrot(dq).astype(BF16)
        dk_ref[...] = (unrot(dk) * (RET_DK ** -0.5)).astype(BF16)

    rev = lambda t: nt - 1 - t
    qkb = lambda off: pl.BlockSpec((RT, RET_DK), lambda h, t: (rev(t), off + h))
    vsb = lambda off: pl.BlockSpec((RT, RET_DV), lambda h, t: (rev(t), off + h))
    qk, vs = qkb(0), vsb(0)
    tab = pl.BlockSpec((RT, half), lambda h, t: (rev(t), 0))
    vec = pl.BlockSpec((1, RET_DV), lambda h, t: (0, h))
    return _pcall(
        body, (consts, h0, h0, h0, h0, do, y, gn_g, states, cos, sin), name="ret_bwd", grid=(RET_HEADS, nt),
        in_specs=[pl.BlockSpec(memory_space=pltpu.SMEM), qk, qkb(RET_HEADS), vsb(RET_HEADS), vsb(2 * RET_HEADS), vs, vs,
                  vec, pl.BlockSpec((None, None, RET_DK, RET_DV), lambda h, t: (h, rev(t), 0, 0)), tab, tab],
        out_specs=(qk, qk, vs, vs, vec),
        out_shape=(SDS((S, 2048), BF16), SDS((S, 2048), BF16), SDS((S, 4096), BF16), SDS((S, 4096), BF16),
                   SDS((1, 4096), F32)),
        scratch_shapes=[pltpu.VMEM((RET_DK, RET_DV), F32)], sem=("parallel", "arbitrary"), sides=sides)


def _gn_stats(y):
    mu = jnp.mean(y, -1, keepdims=True)
    yc = y - mu
    var = jnp.mean(yc * yc, -1, keepdims=True)
    rstd = lax.rsqrt(var + GN_EPS)
    return yc * rstd, rstd


CONV_RB, CONV_CB = 32, 512


def _conv_tiles(rows, cols):
    return [(r0, slice(c0, c0 + CONV_CB)) for r0 in range(0, rows, CONV_RB) for c0 in range(0, cols, CONV_CB)]


def _conv_fill(x_ref, halo_ref, ext_scr, t):
    ext_scr[0:8, :] = jnp.where(t == 0, 0.0, halo_ref[...])
    ext_scr[8:, :] = x_ref[...]


def _conv_tile(ext_scr, w, r0, cs):
    acc = w[3:4, cs] * ext_scr[pl.ds(8 + r0, CONV_RB), cs]
    for j in range(3):
        acc = acc + w[j:j + 1, cs] * ext_scr[pl.ds(5 + j + r0, CONV_RB), cs]
    return acc


def _gdn_conv_fwd(h1, conv_w, kind):
    S = h1.shape[0]
    base = {"q": 0, "k": 1, "v": 2}[kind]
    ncb = 2 if kind == "v" else 1
    C = 2048

    def body(x_ref, halo_ref, w_ref, o_ref, ext_scr):
        _conv_fill(x_ref, halo_ref, ext_scr, pl.program_id(0))
        w = w_ref[...]
        scale = GDN_D ** -0.5 if kind == "q" else 1.0
        for r0, cs in _conv_tiles(ROWS, C):
            rs = pl.ds(r0, CONV_RB)
            acc = _conv_tile(ext_scr, w, r0, cs)
            c = acc * _sigmoid(acc)
            if kind == "v":
                o_ref[rs, cs] = c.astype(BF16)
            else:
                for hh in range(CONV_CB // GDN_D):
                    ch = c[:, hh * GDN_D:(hh + 1) * GDN_D]
                    r = lax.rsqrt(jnp.sum(ch * ch, -1, keepdims=True) + L2_EPS)
                    o_ref[rs, pl.ds(cs.start + hh * GDN_D, GDN_D)] = (ch * (r * scale)).astype(BF16)

    hb = ROWS // 8
    return pl.pallas_call(
        body, name="gdn_conv_fwd_" + kind, grid=(S // ROWS, ncb),
        in_specs=[pl.BlockSpec((ROWS, C), lambda t, j: (t, base + j)),
                  pl.BlockSpec((8, C), lambda t, j: (jnp.maximum(t * hb - 1, 0), base + j)),
                  pl.BlockSpec((4, C), lambda t, j: (0, base + j))],
        out_specs=pl.BlockSpec((ROWS, C), lambda t, j: (t, j)), out_shape=SDS((S, C * ncb), BF16),
        scratch_shapes=[pltpu.VMEM((ROWS + 8, C), F32)], compiler_params=_cp("parallel", "parallel"),
    )(h1, h1, conv_w)


def _gdn_conv_bwd_act(h1, conv_w, dn, kind, buf=None):
    S = h1.shape[0]
    base = {"q": 0, "k": 1, "v": 2}[kind]
    ncb = 2 if kind == "v" else 1
    C = 2048

    def body(x_ref, halo_ref, w_ref, dn_ref, *rest):
        o_ref, ext_scr = rest[-2], rest[-1]
        _conv_fill(x_ref, halo_ref, ext_scr, pl.program_id(0))
        w = w_ref[...]
        scale = GDN_D ** -0.5 if kind == "q" else 1.0
        for r0, cs in _conv_tiles(ROWS, C):
            rs = pl.ds(r0, CONV_RB)
            acc = _conv_tile(ext_scr, w, r0, cs)
            sg = _sigmoid(acc)
            dsilu = sg * (1.0 + acc * (1.0 - sg))
            if kind == "v":
                o_ref[rs, cs] = dn_ref[rs, cs] * dsilu
            else:
                c = acc * sg
                for hh in range(CONV_CB // GDN_D):
                    sl = slice(hh * GDN_D, (hh + 1) * GDN_D)
                    gl = pl.ds(cs.start + hh * GDN_D, GDN_D)
                    ch, dnh = c[:, sl], dn_ref[rs, gl]
                    r = lax.rsqrt(jnp.sum(ch * ch, -1, keepdims=True) + L2_EPS)
                    proj = jnp.sum(dnh * ch, -1, keepdims=True)
                    o_ref[rs, gl] = (scale * r) * (dnh - ch * (proj * r * r)) * dsilu[:, sl]

    hb = ROWS // 8
    return pl.pallas_call(
        body, name="gdn_conv_bwd_act_" + kind, grid=(S // ROWS, ncb),
        in_specs=[pl.BlockSpec((ROWS, C), lambda t, j: (t, base + j)),
                  pl.BlockSpec((8, C), lambda t, j: (jnp.maximum(t * hb - 1, 0), base + j)),
                  pl.BlockSpec((4, C), lambda t, j: (0, base + j)),
                  pl.BlockSpec((ROWS, C), lambda t, j: (t, j))] + ([] if buf is None else [HBM_SPEC]),
        out_specs=pl.BlockSpec((ROWS, C), lambda t, j: (t, base + j)), out_shape=SDS((S, GDN_QKV), F32),
        input_output_aliases={} if buf is None else {4: 0},
        scratch_shapes=[pltpu.VMEM((ROWS + 8, C), F32)], compiler_params=_cp("parallel", "parallel"),
    )(*((h1, h1, conv_w, dn) + (() if buf is None else (buf,))))


def _gdn_conv_bwd_in(h1, conv_w, dacc, dh1_buf):
    S = h1.shape[0]
    C = 2048
    nt = S // ROWS
    hb = ROWS // 8

    def body(x_ref, halo_ref, w_ref, d_ref, dhalo_ref, buf_ref, di_ref, dw_ref, ext_scr, dext_scr):
        t = pl.program_id(1)
        _conv_fill(x_ref, halo_ref, ext_scr, t)
        dext_scr[0:ROWS, :] = d_ref[...]
        dext_scr[ROWS:, :] = jnp.where(t == nt - 1, 0.0, dhalo_ref[...])
        w = w_ref[...]

        @pl.when(t == 0)
        def _():
            dw_ref[...] = jnp.zeros_like(dw_ref)

        for c0 in range(0, C, CONV_CB):
            cs = slice(c0, c0 + CONV_CB)
            pw = [jnp.zeros((1, CONV_CB), F32) for _ in range(4)]
            for r0 in range(0, ROWS, CONV_RB):
                d = dext_scr[pl.ds(r0, CONV_RB), cs]
                di = w[3:4, cs] * d
                for j in range(3):
                    di = di + w[j:j + 1, cs] * dext_scr[pl.ds(3 - j + r0, CONV_RB), cs]
                di_ref[pl.ds(r0, CONV_RB), cs] = di.astype(BF16)
                for j in range(4):
                    pw[j] = pw[j] + jnp.sum(d * ext_scr[pl.ds(5 + j + r0, CONV_RB), cs], axis=0, keepdims=True)
            dw_ref[:, cs] += jnp.concatenate(pw, axis=0)

    return pl.pallas_call(
        body, name="gdn_conv_bwd_in", grid=(GDN_QKV // C, nt),
        in_specs=[pl.BlockSpec((ROWS, C), lambda j, t: (t, j)),
                  pl.BlockSpec((8, C), lambda j, t: (jnp.maximum(t * hb - 1, 0), j)),
                  pl.BlockSpec((4, C), lambda j, t: (0, j)),
                  pl.BlockSpec((ROWS, C), lambda j, t: (t, j)),
                  pl.BlockSpec((8, C), lambda j, t: (jnp.minimum((t + 1) * hb, nt * hb - 1), j)), HBM_SPEC],
        out_specs=(pl.BlockSpec((ROWS, C), lambda j, t: (t, j)), pl.BlockSpec((4, C), lambda j, t: (0, j))),
        out_shape=(SDS(dh1_buf.shape, BF16), SDS((4, GDN_QKV), F32)), input_output_aliases={5: 0},
        scratch_shapes=[pltpu.VMEM((ROWS + 8, C), F32), pltpu.VMEM((ROWS + 8, C), F32)],
        compiler_params=_cp("parallel", "arbitrary"),
    )(h1, h1, conv_w, dacc, dacc, dh1_buf)


def _chunk_masks():
    ri, ci = _iota((CT, CT), 0), _iota((CT, CT), 1)
    same = (ri >> 6) == (ci >> 6)
    return same, same & (ri >= ci), same & (ri > ci), same & (ri <= ci), ri == ci


def _fold_dup(m):
    h = m[:, :LANES] + m[:, LANES:]
    return h + pltpu.roll(h, CHUNK, axis=1)


def _unfold_bd(d, same):
    return jnp.where(same, jnp.concatenate([d, d], axis=1), 0.0)


def _softplus(x):
    return jnp.maximum(x, 0.0) + jnp.log(1.0 + jnp.exp(-jnp.abs(x)))


def _gdn_scal_fwd(ba, a_log, dt_bias):
    S = ba.shape[0]

    def body(ba_ref, al_ref, dt_ref, beta_ref, g_ref, gc_ref):
        bav = ba_ref[...]
        beta_ref[...] = _sigmoid(bav)
        a = pltpu.roll(bav, LANES - GDN_HV, axis=1)
        g = -jnp.exp(al_ref[...]) * _softplus(a + dt_ref[...])
        g_ref[...] = g
        causal = _chunk_masks()[1]
        gc_ref[...] = _dot_exact_l(causal.astype(BF16), g)

    row = pl.BlockSpec((CT, LANES), lambda t: (t, 0))
    vec = pl.BlockSpec((1, LANES), lambda t: (0, 0))
    return pl.pallas_call(
        body, name="gdn_scal_fwd", grid=(S // CT,), in_specs=[row, vec, vec], out_specs=(row, row, row),
        out_shape=(SDS((S, LANES), F32),) * 3, compiler_params=_cp("parallel"),
    )(ba, a_log, dt_bias)


def _gdn_scal_bwd(ba, a_log, dt_bias, g, dbeta, dgc):
    S = ba.shape[0]

    def body(ba_ref, al_ref, dt_ref, g_ref, dbeta_ref, dgc_ref, dba_ref, dal_ref, ddt_ref):
        t = pl.program_id(0)
        bav = ba_ref[...]
        beta = _sigmoid(bav)
        db = dbeta_ref[...] * beta * (1.0 - beta)
        a = pltpu.roll(bav, LANES - GDN_HV, axis=1)
        dgv = _dot_exact_l(_chunk_masks()[3].astype(BF16), dgc_ref[...])
        da = dgv * (-jnp.exp(al_ref[...])) * _sigmoid(a + dt_ref[...])
        lane = _iota(bav.shape, 1)
        da_sh = pltpu.roll(da, GDN_HV, axis=1)
        dba = jnp.where(lane < GDN_HV, db, jnp.where(lane < 2 * GDN_HV, da_sh, 0.0))
        dba_ref[...] = dba.astype(BF16)
        keep = lane < GDN_HV
        pal = jnp.sum(jnp.where(keep, dgv * g_ref[...], 0.0), axis=0, keepdims=True)
        pdt = jnp.sum(jnp.where(keep, da, 0.0), axis=0, keepdims=True)

        @pl.when(t == 0)
        def _():
            dal_ref[...] = pal
            ddt_ref[...] = pdt

        @pl.when(t > 0)
        def _():
            dal_ref[...] += pal
            ddt_ref[...] += pdt

    row = pl.BlockSpec((CT, LANES), lambda t: (t, 0))
    vec = pl.BlockSpec((1, LANES), lambda t: (0, 0))
    return pl.pallas_call(
        body, name="gdn_scal_bwd", grid=(S // CT,), in_specs=[row, vec, vec, row, row, row],
        out_specs=(row, vec, vec), out_shape=(SDS((S, LANES), BF16), SDS((1, LANES), F32), SDS((1, LANES), F32)),
        compiler_params=_cp("arbitrary"),
    )(ba, a_log, dt_bias, g, dbeta, dgc)


def _sel_col(x, h):
    return jnp.sum(jnp.where(_iota(x.shape, 1) == h, x, 0.0), axis=1, keepdims=True)


def _decay(gcol, causal):
    gm = jnp.broadcast_to(gcol, (CT, CT))
    diff = gm - gm.T
    return jnp.where(causal, jnp.exp(jnp.where(causal, diff, 0.0)), 0.0)


def _gdn_chunk_fwd(k, v, beta, gc):
    S = k.shape[0]
    nt = S // CT

    def body(k_ref, v_ref, beta_ref, gc_ref, t_ref, u_ref, w_ref):
        kg_id = pl.program_id(1)
        same, causal, strict, _, _ = _chunk_masks()
        eye_dup = jnp.where((_iota((CT, LANES), 0) & (CHUNK - 1)) == (_iota((CT, LANES), 1) & (CHUNK - 1)), 1.0, 0.0)
        kfs, xds, pds, cols = [], [], [], []
        for kb in range(KB):
            kv = k_ref[:, kb * GDN_D:(kb + 1) * GDN_D]
            kfs.append(kv.astype(F32))
            kk = _dot_nt(kv, kv)
            for hp in range(HP):
                h = NH * kg_id + kb * HP + hp
                bcol, gcol = _sel_col(beta_ref[...], h), _sel_col(gc_ref[...], h)
                x = jnp.where(strict, -(kk * bcol) * _decay(gcol, causal), 0.0)
                xds.append(_fold_dup(x))
                pds.append(eye_dup)
                cols.append((bcol, gcol))
        for m in range(6):
            for hi in range(NH):
                xh, xl = _split2(xds[hi])
                ph, pl_ = _split2(pds[hi])
                lh, ll = _unfold_bd(xh, same), _unfold_bd(xl, same)
                rh, rl = jnp.concatenate([xh, ph], axis=1), jnp.concatenate([xl, pl_], axis=1)
                out = _dot(lh, rh) + (_dot(lh, rl) + _dot(ll, rh))
                pds[hi] = pds[hi] + out[:, LANES:]
                if m < 5:
                    xds[hi] = out[:, :LANES]
        for hi in range(NH):
            bcol, gcol = cols[hi]
            cs = slice(hi * GDN_D, (hi + 1) * GDN_D)
            t_ref[hi] = pds[hi]
            tb = _unfold_bd(pds[hi], same).astype(BF16)
            vb = (v_ref[:, cs].astype(F32) * bcol).astype(BF16)
            kg = (kfs[hi // HP] * (bcol * jnp.exp(gcol))).astype(BF16)
            uw = _dot(tb, jnp.concatenate([vb, kg], axis=1))
            u_ref[:, cs] = uw[:, :GDN_D]
            w_ref[:, cs] = uw[:, GDN_D:].astype(BF16)

    col = pl.BlockSpec((CT, LANES), lambda t, kg: (t, 0))
    hv = pl.BlockSpec((CT, NH * GDN_D), lambda t, kg: (t, kg))
    return pl.pallas_call(
        body, name="gdn_chunk_fwd", grid=(nt, GDN_HV // NH),
        in_specs=[pl.BlockSpec((CT, KB * GDN_D), lambda t, kg: (t, kg)), hv, col, col],
        out_specs=(pl.BlockSpec((NH, None, CT, LANES), lambda t, kg: (kg, t, 0, 0)), hv, hv),
        out_shape=(SDS((GDN_HV, nt, CT, LANES), F32), SDS((S, 4096), F32), SDS((S, 4096), BF16)),
        compiler_params=_cp("parallel", "parallel"),
    )(k, v, beta, gc)


def _last_of_chunk(gcol, rows, c):
    return jnp.sum(jnp.where(rows == c * CHUNK + CHUNK - 1, gcol, 0.0), axis=0, keepdims=True)


def _gdn_scan_fwd(q, k, u, w, gc):
    S = q.shape[0]
    nt = S // CT
    ncs = CT // CHUNK

    def body(q_ref, k_ref, u_ref, w_ref, gc_ref, y_ref, vn_ref, st_ref, s_scr, vn_scr):
        kg_id, t = pl.program_id(0), pl.program_id(1)

        @pl.when(t == 0)
        def _():
            s_scr[...] = jnp.zeros_like(s_scr)

        causal = _chunk_masks()[1]
        rows = _iota((CT, 1), 0)
        heads = []
        for kb in range(KB):
            ks = slice(kb * GDN_D, (kb + 1) * GDN_D)
            qv, kv = q_ref[:, ks], k_ref[:, ks]
            qf, kf = qv.astype(F32), kv.astype(F32)
            qk = _dot_nt(qv, kv)
            for hp in range(HP):
                gcol = _sel_col(gc_ref[...], NH * kg_id + kb * HP + hp)
                heads.append((gcol, (qk * _decay(gcol, causal)).astype(BF16), (qf * jnp.exp(gcol)).astype(BF16), kf))
        vn_scr[...] = jnp.zeros_like(vn_scr)
        for c in range(ncs):
            r = slice(c * CHUNK, (c + 1) * CHUNK)
            for hi in range(NH):
                gcol, attn, qd, kf = heads[hi]
                cs = slice(hi * GDN_D, (hi + 1) * GDN_D)
                s = s_scr[hi]
                st_ref[hi, c] = s
                sb = s.astype(BF16)
                gl = _last_of_chunk(gcol, rows, c)
                kd = (kf[r] * jnp.exp(gl - gcol[r])).astype(BF16)
                vn = (u_ref[r, cs] - _dot(w_ref[r, cs], sb)).astype(BF16)
                vn_scr[r, cs] = vn
                y_ref[r, cs] = _dot(qd[r], sb) + _dot(attn[r], vn_scr[:, cs])
                s_scr[hi] = s * jnp.exp(gl) + _dot_tn(kd, vn)
        vn_ref[...] = vn_scr[...]

    hk = pl.BlockSpec((CT, KB * GDN_D), lambda kg, t: (t, kg))
    hv = pl.BlockSpec((CT, NH * GDN_D), lambda kg, t: (t, kg))
    col = pl.BlockSpec((CT, LANES), lambda kg, t: (t, 0))
    return pl.pallas_call(
        body, name="gdn_scan_fwd", grid=(GDN_HV // NH, nt), in_specs=[hk, hk, hv, hv, col],
        out_specs=(hv, hv, pl.BlockSpec((NH, ncs, GDN_D, GDN_D), lambda kg, t: (kg, t, 0, 0))),
        out_shape=(SDS((S, 4096), F32), SDS((S, 4096), BF16), SDS((GDN_HV, S // CHUNK, GDN_D, GDN_D), F32)),
        scratch_shapes=[pltpu.VMEM((NH, GDN_D, GDN_D), F32), pltpu.VMEM((CT, NH * GDN_D), BF16)],
        compiler_params=_cp("parallel", "arbitrary"),
    )(q, k, u, w, gc)


def _gdn_scan_bwd(q, k, w, vn, gc, states, dy, sides=()):
    S = q.shape[0]
    nt = S // CT
    ncs = CT // CHUNK

    def body(q_ref, k_ref, w_ref, vn_ref, gc_ref, st_ref, dy_ref, du_ref, dw_ref, dq_ref, dk_ref, dgc_ref, ds_scr):
        kg_id, t = pl.program_id(0), pl.program_id(1)

        @pl.when(t == 0)
        def _():
            ds_scr[...] = jnp.zeros_like(ds_scr)

        _, causal, _, _, eye = _chunk_masks()
        rows = _iota((CT, 1), 0)
        heads = []
        for kb in range(KB):
            ks = slice(kb * GDN_D, (kb + 1) * GDN_D)
            qv, kv = q_ref[:, ks], k_ref[:, ks]
            qf, kf = qv.astype(F32), kv.astype(F32)
            qk = _dot_nt(qv, kv)
            for hp in range(HP):
                hi = kb * HP + hp
                cs = slice(hi * GDN_D, (hi + 1) * GDN_D)
                gcol = _sel_col(gc_ref[...], NH * kg_id + hi)
                dm = _decay(gcol, causal)
                attn_f = qk * dm
                egc = jnp.exp(gcol)
                qd_f = qf * egc
                dyv, vnv = dy_ref[:, cs], vn_ref[:, cs]
                heads.append(dict(cs=cs, ks=ks, gcol=gcol, dm=dm, attn_f=attn_f, egc=egc, qd_f=qd_f, kf=kf, qv=qv, kv=kv,
                                  qd=qd_f.astype(BF16), dy=dyv, vn=vnv, dattn=_dot_nt(dyv, vnv),
                                  at_dy=_dot_tn(attn_f.astype(BF16), dyv), dgc=[None] * ncs))
        dq_ref[...] = jnp.zeros_like(dq_ref)
        dk_ref[...] = jnp.zeros_like(dk_ref)
        for c in reversed(range(ncs)):
            r = slice(c * CHUNK, (c + 1) * CHUNK)
            for hi in range(NH):
                hd = heads[hi]
                cs, ks, gcol = hd["cs"], hd["ks"], hd["gcol"]
                s = st_ref[hi, c]
                sb = s.astype(BF16)
                dsn = ds_scr[hi]
                dsb = dsn.astype(BF16)
                gl = _last_of_chunk(gcol, rows, c)
                cd = jnp.exp(gl)
                ekd = jnp.exp(gl - gcol[r])
                kd_f = hd["kf"][r] * ekd
                dvn = (hd["at_dy"][r] + _dot(kd_f.astype(BF16), dsb)).astype(BF16)
                dqd = _dot_nt(hd["dy"][r], sb)
                dkd = _dot_nt(hd["vn"][r], dsb)
                dcd = jnp.sum(jnp.sum(s * dsn, axis=1, keepdims=True), axis=0, keepdims=True)
                ds_scr[hi] = dsn * cd + _dot_tn(hd["qd"][r], hd["dy"][r]) - _dot_tn(w_ref[r, cs], dvn)
                du_ref[r, cs] = dvn
                dw_ref[r, cs] = (-_dot_nt(dvn, sb)).astype(BF16)
                dq_ref[r, ks] += dqd * hd["egc"][r]
                dk_ref[r, ks] += dkd * ekd
                rs_q = jnp.sum(dqd * hd["qd_f"][r], axis=1, keepdims=True)
                rs_k = jnp.sum(dkd * kd_f, axis=1, keepdims=True)
                tot = jnp.sum(rs_k, axis=0, keepdims=True) + dcd * cd
                hd["dgc"][c] = rs_q - rs_k + jnp.where(rows[r] == c * CHUNK + CHUNK - 1, tot, 0.0)
        for hi in range(NH):
            hd = heads[hi]
            ks = hd["ks"]
            dab = (hd["dattn"] * hd["dm"]).astype(BF16)
            dq_ref[:, ks] += _dot(dab, hd["kv"])
            dk_ref[:, ks] += _dot_tn(dab, hd["qv"])
            e1 = hd["dattn"] * hd["attn_f"]
            dgc = (jnp.concatenate(hd["dgc"], axis=0) + jnp.sum(e1, axis=1, keepdims=True)
                   - jnp.sum(e1.T, axis=1, keepdims=True))
            dgc_ref[hi] = jnp.sum(jnp.where(eye, jnp.broadcast_to(dgc, (CT, CT)), 0.0), axis=0, keepdims=True)

    rev = lambda t: nt - 1 - t
    hk = pl.BlockSpec((CT, KB * GDN_D), lambda kg, t: (rev(t), kg))
    hv = pl.BlockSpec((CT, NH * GDN_D), lambda kg, t: (rev(t), kg))
    col = pl.BlockSpec((CT, LANES), lambda kg, t: (rev(t), 0))
    return _pcall(
        body, (q, k, w, vn, gc, states, dy), name="gdn_scan_bwd", grid=(GDN_HV // NH, nt),
        in_specs=[hk, hk, hv, hv, col, pl.BlockSpec((NH, ncs, GDN_D, GDN_D), lambda kg, t: (kg, rev(t), 0, 0)), hv],
        out_specs=(hv, hv, hk, hk, pl.BlockSpec((NH, 1, CT), lambda kg, t: (kg, 0, rev(t)))),
        out_shape=(SDS((S, 4096), BF16), SDS((S, 4096), BF16), SDS((S, 2048), F32), SDS((S, 2048), F32),
                   SDS((GDN_HV, 1, S), F32)),
        scratch_shapes=[pltpu.VMEM((NH, GDN_D, GDN_D), F32)], sem=("parallel", "arbitrary"), sides=sides)


def _gdn_chunk_bwd(k, v, beta, gc, tmat, du, dw, dk_p, dgc_p, sides=()):
    S = k.shape[0]
    nt = S // CT

    def body(k_ref, v_ref, beta_ref, gc_ref, t_ref, du_ref, dw_ref, dkp_ref, dgcp_ref,
             dk_ref, dv_ref, dbeta_ref, dgc_ref):
        kg_id = pl.program_id(1)
        same, causal, strict, _, eye = _chunk_masks()
        lane = _iota((CT, LANES), 1)

        @pl.when(kg_id == 0)
        def _():
            dbeta_ref[...] = jnp.zeros_like(dbeta_ref)
            dgc_ref[...] = jnp.zeros_like(dgc_ref)

        for kb_i in range(KB):
            ks = slice(kb_i * GDN_D, (kb_i + 1) * GDN_D)
            kv = k_ref[:, ks]
            kf = kv.astype(F32)
            kk = _dot_nt(kv, kv)
            dk = dkp_ref[:, ks]
            for hp in range(HP):
                hi = kb_i * HP + hp
                h = NH * kg_id + hi
                cs = slice(hi * GDN_D, (hi + 1) * GDN_D)
                bcol, gcol = _sel_col(beta_ref[...], h), _sel_col(gc_ref[...], h)
                dm = _decay(gcol, causal)
                vf = v_ref[:, cs].astype(F32)
                kb = kf * bcol
                a = jnp.where(strict, (kk * bcol) * dm, 0.0)
                egc = jnp.exp(gcol)
                kg_f = kb * egc
                tb = _unfold_bd(t_ref[hi], same).astype(BF16)
                duw = jnp.concatenate([du_ref[:, cs], dw_ref[:, cs]], axis=1)
                dt = _dot_nt(duw, jnp.concatenate([(vf * bcol).astype(BF16), kg_f.astype(BF16)], axis=1))
                dvb_dkg = _dot_tn(tb, duw)
                dvb, dkg = dvb_dkg[:, :GDN_D], dvb_dkg[:, GDN_D:]
                da = -_dot_nt(_dot_tn(tb, dt.astype(BF16)).astype(BF16), tb)
                rm = jnp.where(strict, da, 0.0)
                rdb = (rm * dm).astype(BF16)
                dkb = _dot(rdb, kv) + dkg * egc
                dk = dk + _dot_tn(rdb, kb.astype(BF16)) + dkb * bcol
                e2 = rm * a
                dgc_in = jnp.sum(jnp.where(eye, jnp.broadcast_to(dgcp_ref[hi], (CT, CT)), 0.0), axis=1, keepdims=True)
                dgc = (jnp.sum(e2, axis=1, keepdims=True) - jnp.sum(e2.T, axis=1, keepdims=True)
                       + jnp.sum(dkg * kg_f, axis=1, keepdims=True) + dgc_in)
                dbeta = jnp.sum(dkb * kf, axis=1, keepdims=True) + jnp.sum(dvb * vf, axis=1, keepdims=True)
                dv_ref[:, cs] = dvb * bcol
                dbeta_ref[...] += jnp.where(lane == h, dbeta, 0.0)
                dgc_ref[...] += jnp.where(lane == h, dgc, 0.0)
            dk_ref[:, ks] = dk

    hk = pl.BlockSpec((CT, KB * GDN_D), lambda t, kg: (t, kg))
    hv = pl.BlockSpec((CT, NH * GDN_D), lambda t, kg: (t, kg))
    col = pl.BlockSpec((CT, LANES), lambda t, kg: (t, 0))
    return _pcall(
        body, (k, v, beta, gc, tmat, du, dw, dk_p, dgc_p), name="gdn_chunk_bwd", grid=(nt, GDN_HV // NH),
        in_specs=[hk, hv, col, col, pl.BlockSpec((NH, None, CT, LANES), lambda t, kg: (kg, t, 0, 0)), hv, hv, hk,
                  pl.BlockSpec((NH, 1, CT), lambda t, kg: (kg, 0, t))],
        out_specs=(hk, hv, col, col),
        out_shape=(SDS((S, 2048), F32), SDS((S, 4096), F32), SDS((S, LANES), F32), SDS((S, LANES), F32)),
        sem=("parallel", "arbitrary"), sides=sides)


def _gdn_post_fwd(y, h1, norm_g):
    S = y.shape[0]

    def body(y_ref, z_ref, g_ref, o_ref):
        g = g_ref[...]
        for hh in range(GDN_HV):
            sl = slice(hh * GDN_D, (hh + 1) * GDN_D)
            yh, zh = y_ref[:, sl], z_ref[:, sl]
            yn = yh * lax.rsqrt(jnp.mean(yh * yh, -1, keepdims=True) + RMS_EPS)
            o_ref[:, sl] = (yn * g * (zh * _sigmoid(zh))).astype(BF16)

    row = lambda off: pl.BlockSpec((ROWS, 4096), lambda t: (t, off))
    return pl.pallas_call(
        body, name="gdn_post_fwd", grid=(S // ROWS,),
        in_specs=[row(0), row(2), pl.BlockSpec((1, GDN_D), lambda t: (0, 0))], out_specs=row(0),
        out_shape=SDS((S, 4096), BF16), compiler_params=_cp("parallel"),
    )(y, h1, norm_g)


def _gdn_post_bwd(do, y, h1, norm_g, sides=()):
    S = y.shape[0]

    def body(do_ref, y_ref, z_ref, g_ref, dy_ref, dz_ref, dg_ref):
        t = pl.program_id(0)
        g = g_ref[...]
        pg = jnp.zeros((1, GDN_D), F32)
        for hh in range(GDN_HV):
            sl = slice(hh * GDN_D, (hh + 1) * GDN_D)
            yh, zh, doh = y_ref[:, sl], z_ref[:, sl], do_ref[:, sl]
            rstd = lax.rsqrt(jnp.mean(yh * yh, -1, keepdims=True) + RMS_EPS)
            yn = yh * rstd
            sg = _sigmoid(zh)
            dz_ref[:, sl] = (doh * (yn * g) * (sg * (1.0 + zh * (1.0 - sg)))).astype(BF16)
            dyg = doh * (zh * sg)
            dyn = dyg * g
            dy_ref[:, sl] = (rstd * (dyn - yn * jnp.mean(dyn * yn, -1, keepdims=True))).astype(BF16)
            pg = pg + jnp.sum(dyg * yn, axis=0, keepdims=True)

        @pl.when(t == 0)
        def _():
            dg_ref[...] = pg

        @pl.when(t > 0)
        def _():
            dg_ref[...] += pg

    row = lambda off: pl.BlockSpec((ROWS, 4096), lambda t: (t, off))
    vec = pl.BlockSpec((1, GDN_D), lambda t: (0, 0))
    return _pcall(
        body, (do, y, h1, norm_g), name="gdn_post_bwd", grid=(S // ROWS,), in_specs=[row(0), row(0), row(2), vec],
        out_specs=(row(0), row(2), vec),
        out_shape=(SDS((S, 4096), BF16), SDS((S, 3 * 4096), BF16), SDS((1, GDN_D), F32)),
        sem=("arbitrary",), sides=sides)


def _cols(g):
    return jnp.transpose(g, (1, 0, 2)).reshape(g.shape[1], -1)


def _rows(g):
    return g.reshape(-1, g.shape[-1])


def _local_step(x, tgt, sh, small):
    S = x.shape[0]
    xb = x.astype(BF16)
    rc = _ret_consts()
    cos, sin = _rope_tables(S)

    natural = ("ret_w_in", "mlp_w1_0", "mlp_w1_1", "conv_w")

    def gather(names):
        return _ag_side([sh[k] for k in names], [k != "conv_w" for k in names], [k in natural for k in names])

    ln_tiles = dict(tm=512, tn=D_MODEL, tk=1024)
    (wri,) = _comm_call("ag_ret_in", gather(["ret_w_in"]))
    h0, ((g_ro, g_gi),) = _mm(xb, wri, "nn", "mm_ret_in", out_dtype=BF16, epi="rope", extra=(cos, sin),
                              sides=[gather(["ret_w_out", "gdn_w_in"])])
    wro = _rows(g_ro)
    (yr, o0, ret_st), ((w1_0, conv_w),) = _ret_fwd(h0, small["ret_gn_g"], rc, sides=[gather(["mlp_w1_0", "conv_w"])])
    w1 = [w1_0, None]
    wgi = _cols(g_gi)
    wgi_main = wgi[:, :GDN_QKV + 4096]
    wba = jnp.pad(wgi[:, GDN_QKV + 4096:], ((0, 0), (0, LANES - 2 * GDN_HV)))
    x1, x1b, z1 = _mm(o0, wro, "nn", "mm_ret_out", epi="ln",
                      extra=(x, small["ln_mix_g"][0:1], small["ln_mix_b"][0:1]), **ln_tiles)
    (hh0, a0), ((g_w20,),) = _mm(x1b, w1[0], "nn", "mm_mlp0_up", epi="relu2", sides=[gather(["mlp_w2_0"])])
    w2 = [_rows(g_w20), None]
    (x2, x2b, z2), ((w1[1],),) = _mm(a0, w2[0], "nn", "mm_mlp0_down", epi="ln",
                                     extra=(x1, small["ln_ffn_g"][0:1], small["ln_ffn_b"][0:1]),
                                     sides=[gather(["mlp_w1_1"])], **ln_tiles)

    h1, ((g_w21, g_go),) = _mm(x2b, wgi_main, "nn", "mm_gdn_in", sides=[gather(["mlp_w2_1", "gdn_w_out"])])
    w2[1], wgo = _rows(g_w21), _rows(g_go)
    ba = _mm(x2b, wba, "nn", "mm_gdn_ba")
    qn = _gdn_conv_fwd(h1, conv_w, "q")
    kn = _gdn_conv_fwd(h1, conv_w, "k")
    vg = _gdn_conv_fwd(h1, conv_w, "v")
    beta, g, gc = _gdn_scal_fwd(ba, small["a_log"], small["dt_bias"])
    tmat, u, w = _gdn_chunk_fwd(kn, vg, beta, gc)
    yg, vn, gdn_st = _gdn_scan_fwd(qn, kn, u, w, gc)
    o1 = _gdn_post_fwd(yg, h1, small["norm_g"])
    x3, x3b, z3 = _mm(o1, wgo, "nn", "mm_gdn_out", epi="ln",
                      extra=(x2, small["ln_mix_g"][1:2], small["ln_mix_b"][1:2]), **ln_tiles)
    hh1, a1 = _mm(x3b, w1[1], "nn", "mm_mlp1_up", epi="relu2")
    z4 = _mm(a1, w2[1], "nn", "mm_mlp1_down", epi="add", extra=x3, scale=ALPHA)

    loss, dz4, dz4b, d_lnf_g1, d_lnf_b1 = _loss_ln_bwd(z4, small["ln_ffn_g"][1:2], small["ln_ffn_b"][1:2], tgt,
                                                       "loss_ln_ffn1_bwd")
    dhh1 = _mm(dz4b, w2[1], "nt", "mm_mlp1_down_dx", epi="drelu2", extra=hh1, out_dtype=BF16)
    dw2_1 = _mm(a1, dz4b, "tn", "mm_mlp1_down_dw")
    dx3 = _mm(dhh1, w1[1], "nt", "mm_mlp1_up_dx", epi="add", extra=dz4, scale=ALPHA)
    dw1_1 = _mm(x3b, dhh1, "tn", "mm_mlp1_up_dw", shard_major=True)
    dz3, dz3b, d_lnm_g1, d_lnm_b1 = _ln_bwd(dx3, z3, small["ln_mix_g"][1:2], "ln_mix1_bwd")
    shards_of = lambda g: g.reshape(N_CHIPS, -1, g.shape[-1])
    g_a = [dw1_1, shards_of(dw2_1)]
    do1, (th_a,) = _mm(dz3b, wgo, "nt", "mm_gdn_out_dx", sides=[_rs_swap_side(g_a)])
    sums_a = _rs_add(g_a, th_a, "a")
    dwgo = _mm(o1, dz3b, "tn", "mm_gdn_out_dw")
    dyg, dh1_z, d_norm_g = _gdn_post_bwd(do1, yg, h1, small["norm_g"])
    du, dw, dqn, dk_p, dgc_p = _gdn_scan_bwd(qn, kn, w, vn, gc, gdn_st, dyg)
    dkn, dvg, dbeta, dgc = _gdn_chunk_bwd(kn, vg, beta, gc, tmat, du, dw, dk_p, dgc_p)
    dba, d_a_log, d_dt_bias = _gdn_scal_bwd(ba, small["a_log"], small["dt_bias"], g, dbeta, dgc)
    dacc = _gdn_conv_bwd_act(h1, conv_w, dqn, "q")
    dacc = _gdn_conv_bwd_act(h1, conv_w, dkn, "k", dacc)
    dacc = _gdn_conv_bwd_act(h1, conv_w, dvg, "v", dacc)
    dh1, d_conv_w = _gdn_conv_bwd_in(h1, conv_w, dacc, dh1_z)
    g_go = [shards_of(dwgo)]
    dx2_ba, (th_go,) = _mm(dba, wba, "nt", "mm_gdn_ba_dx", epi="add", extra=dz3, scale=ALPHA,
                           sides=[_rs_swap_side(g_go)])
    sums_a = sums_a + _rs_add(g_go, th_go, "go")
    dx2, (parts_a,) = _mm(dh1, wgi_main, "nt", "mm_gdn_in_dx", epi="add", extra=dx2_ba,
                          sides=[_rs_owner_side(sums_a)])
    mine_a = _rs_sum(parts_a, "a")
    dwgi_main, (back_a,) = _mm(x2b, dh1, "tn", "mm_gdn_in_dw", sides=[_rs_back_side(mine_a)])
    red_w1_1, red_w2_1, red_go = zip(mine_a, back_a)
    dwba = _mm(x2b, dba, "tn", "mm_gdn_ba_dw")
    dwgi = jnp.concatenate([dwgi_main, dwba[:, :2 * GDN_HV]], axis=1)
    g_b = [jnp.transpose(dwgi.reshape(dwgi.shape[0], N_CHIPS, -1), (1, 0, 2))]

    dz2, dz2b, d_lnf_g0, d_lnf_b0 = _ln_bwd(dx2, z2, small["ln_ffn_g"][0:1], "ln_ffn0_bwd")
    dhh0, (th_b,) = _mm(dz2b, w2[0], "nt", "mm_mlp0_down_dx", epi="drelu2", extra=hh0, out_dtype=BF16,
                        sides=[_rs_swap_side(g_b)])
    sums_b = _rs_add(g_b, th_b, "b")
    dw2_0 = _mm(a0, dz2b, "tn", "mm_mlp0_down_dw")
    dx1, (parts_b,) = _mm(dhh0, w1[0], "nt", "mm_mlp0_up_dx", epi="add", extra=dz2, scale=ALPHA,
                          sides=[_rs_owner_side(sums_b)])
    mine_b = _rs_sum(parts_b, "b")
    dw1_0 = _mm(x1b, dhh0, "tn", "mm_mlp0_up_dw", shard_major=True)
    dz1, dz1b, d_lnm_g0, d_lnm_b0 = _ln_bwd(dx1, z1, small["ln_mix_g"][0:1], "ln_mix0_bwd")
    g_c = [dw1_0, shards_of(dw2_0)]
    do0, (th_c, back_b) = _mm(dz1b, wro, "nt", "mm_ret_out_dx", sides=[_rs_swap_side(g_c), _rs_back_side(mine_b)])
    (red_gi,) = zip(mine_b, back_b)
    sums_c = _rs_add(g_c, th_c, "c")
    dwro = _mm(o0, dz1b, "tn", "mm_ret_out_dw")
    g_ro = [shards_of(dwro)]
    (dq0, dk0, dv0, dgate, d_gn_g), (th_ro, parts_c) = _ret_bwd(
        h0, do0, yr, small["ret_gn_g"], ret_st, rc, cos, sin, sides=[_rs_swap_side(g_ro), _rs_owner_side(sums_c)])
    sums_ro = _rs_add(g_ro, th_ro, "ro")
    mine_c = _rs_sum(parts_c, "c")
    dh0 = jnp.concatenate([dq0, dk0, dv0, dgate], axis=1)
    dwri, (parts_ro, back_c) = _mm(xb, dh0, "tn", "mm_ret_in_dw", shard_major=True,
                                   sides=[_rs_owner_side(sums_ro), _rs_back_side(mine_c)])
    red_w1_0, red_w2_0 = zip(mine_c, back_c)
    mine_ro = _rs_sum(parts_ro, "ro")
    g_d = [dwri]
    sums_d = _rs_add(g_d, _comm_call("rs_swap_halves_d", _rs_swap_side(g_d)), "d")
    grad_x, (parts_d, back_ro) = _mm(dh0, wri, "nt", "mm_ret_in_dx", epi="add", extra=dz1, scale=ALPHA,
                                     sides=[_rs_owner_side(sums_d), _rs_back_side(mine_ro)])
    (red_ro,) = zip(mine_ro, back_ro)
    mine_d = _rs_sum(parts_d, "d")
    (red_ri,) = zip(mine_d, _comm_call("rs_swap_reduced_d", _rs_back_side(mine_d)))

    big = dict(ret_w_in=red_ri, ret_w_out=red_ro, gdn_w_in=red_gi, gdn_w_out=red_go,
               mlp_w1=(red_w1_0, red_w1_1), mlp_w2=(red_w2_0, red_w2_1))
    sm = dict(ret_gn_g=d_gn_g, a_log=d_a_log, dt_bias=d_dt_bias, norm_g=d_norm_g,
              ln_mix_g=jnp.concatenate([d_lnm_g0, d_lnm_g1], 0), ln_mix_b=jnp.concatenate([d_lnm_b0, d_lnm_b1], 0),
              ln_ffn_g=jnp.concatenate([d_lnf_g0, d_lnf_g1], 0), ln_ffn_b=jnp.concatenate([d_lnf_b0, d_lnf_b1], 0),
              conv_w=d_conv_w)
    return loss, grad_x, big, sm


def _coords():
    return lax.axis_index("x"), lax.axis_index("y"), lax.axis_index("c")


HBM_SPEC = pl.BlockSpec(memory_space=pl.ANY)


def _other_chips(x, y):
    return [(1 - x, y), (x, 1 - y), (1 - x, 1 - y)]


def _ag_side(shards, split, cols):
    n = len(shards)

    def rows_of(p, core):
        half = shards[p].shape[0] // 2
        return pl.ds(core * half, half) if split[p] else slice(None)

    def slot(outs, p, s, core=None):
        r = slice(None) if core is None else rows_of(p, core)
        if cols[p]:
            w = shards[p].shape[1]
            return outs[p].at[r, pl.ds(pl.multiple_of(s * w, LANES), w)]
        return outs[p].at[s, r]

    def over_ici(xyc, ins, outs):
        x, y, c = xyc
        return [(ins[p].at[rows_of(p, c)], slot(outs, p, 2 * x + y, c), (cx, cy, c))
                for p in range(n) for cx, cy in _other_chips(x, y)]

    def to_sibling(xyc, ins, outs):
        x, y, c = xyc
        zones = [slot(outs, p, 2 * cx + cy, c) for p in range(n) if split[p] for cx, cy in _other_chips(x, y)]
        return [(z, z, (x, y, 1 - c)) for z in zones]

    def own(xyc, ins, outs):
        x, y, _ = xyc
        return [(ins[p], slot(outs, p, 2 * x + y)) for p in range(n)]

    n_split = sum(bool(s) for s in split)
    phases, counts = [over_ici], [3 * n]
    if n_split:
        phases, counts = phases + [to_sibling], counts + [3 * n_split]
    shapes = [SDS((s.shape[0], N_CHIPS * s.shape[1]) if cols[p] else (N_CHIPS,) + s.shape, s.dtype)
              for p, s in enumerate(shards)]
    return _Side(shards, shapes, phases, counts, local=own, n_local=n)


def _rs_swap_side(grads):
    n = len(grads)
    halves = [g.shape[1] // 2 for g in grads]

    def swap(xyc, ins, outs):
        x, y, c = xyc
        return [(ins[p].at[:, pl.ds((1 - c) * halves[p], halves[p])], outs[p], (x, y, 1 - c)) for p in range(n)]

    return _Side(grads, [SDS((N_CHIPS, halves[p]) + g.shape[2:], F32) for p, g in enumerate(grads)], [swap], [n])


def _rs_add(grads, theirs, tag):
    c = lax.axis_index("c")
    return [_add_half(g, t, c, "rs_add_%s%d" % (tag, p)) for p, (g, t) in enumerate(zip(grads, theirs))]


def _rs_owner_side(chip_sums):
    n = len(chip_sums)

    def to_owner(xyc, ins, outs):
        x, y, c = xyc
        return [(ins[p].at[2 * cx + cy], outs[p].at[2 * x + y], (cx, cy, c))
                for p in range(n) for cx, cy in _other_chips(x, y)]

    def own(xyc, ins, outs):
        x, y, _ = xyc
        return [(ins[p].at[2 * x + y], outs[p].at[2 * x + y]) for p in range(n)]

    return _Side(chip_sums, [SDS(s.shape, s.dtype) for s in chip_sums], [to_owner], [3 * n], local=own, n_local=n)


def _rs_sum(parts, tag):
    return [_sum_chips(pt, "rs_sum_%s%d" % (tag, p)) for p, pt in enumerate(parts)]


def _rs_back_side(mine):
    n = len(mine)

    def swap(xyc, ins, outs):
        x, y, c = xyc
        return [(ins[p], outs[p], (x, y, 1 - c)) for p in range(n)]

    return _Side(mine, [SDS(m.shape, F32) for m in mine], [swap], [n])


def _add_half(g, theirs, c, name):
    _, R, C = g.shape
    half = R // 2
    tr = min(256, half)
    nb = half // tr

    def body(c_ref, g_ref, t_ref, o_ref):
        o_ref[...] = (g_ref[...] + t_ref[...]).astype(BF16)

    blk = pl.BlockSpec((None, tr, C), lambda s, i, c_ref: (s, i, 0))
    return pl.pallas_call(
        body, name=name,
        grid_spec=pltpu.PrefetchScalarGridSpec(
            num_scalar_prefetch=1, grid=(N_CHIPS, nb),
            in_specs=[pl.BlockSpec((None, tr, C), lambda s, i, c_ref: (s, c_ref[0] * nb + i, 0)), blk],
            out_specs=blk),
        out_shape=SDS((N_CHIPS, half, C), BF16), compiler_params=_cp("parallel", "parallel"),
    )(jnp.reshape(c, (1,)).astype(jnp.int32), g, theirs)


def _sum_chips(parts, name):
    _, r, C = parts.shape
    tr = min(256, r)

    def body(p_ref, o_ref):
        f = lambda s: p_ref[s].astype(F32)
        o_ref[...] = ((f(0) + f(1)) + f(2)) + f(3)

    return pl.pallas_call(
        body, name=name, grid=(r // tr,), in_specs=[pl.BlockSpec((N_CHIPS, tr, C), lambda i: (0, i, 0))],
        out_specs=pl.BlockSpec((tr, C), lambda i: (i, 0)), out_shape=SDS((r, C), F32), compiler_params=_cp("parallel"),
    )(parts)


def _all_reduce_small(buf):
    rows = buf.shape[0]

    def body(x_ref, o_ref, all_ref, send_sems, recv_sems):
        x, y, c = _coords()
        me = 4 * x + 2 * y + c
        all_ref[me] = x_ref[...]
        flips = [(fx, fy, fc) for fx in (0, 1) for fy in (0, 1) for fc in (0, 1)][1:]
        copies = []
        for k, (fx, fy, fc) in enumerate(flips):
            to = (x ^ fx, y ^ fy, c ^ fc)
            copies.append(pltpu.make_async_remote_copy(src_ref=x_ref, dst_ref=all_ref.at[me], send_sem=send_sems.at[k],
                                                       recv_sem=recv_sems.at[k], device_id=to, device_id_type=MESH))
        for cp in copies:
            cp.start()
        for cp in copies:
            cp.wait_recv()
        for cp in copies:
            cp.wait_send()
        acc = all_ref[0]
        for d in range(1, N_DEV):
            acc = acc + all_ref[d]
        o_ref[...] = acc

    vm = pl.BlockSpec(memory_space=pltpu.VMEM)
    return pl.pallas_call(
        body, name="all_reduce_small", in_specs=[vm], out_specs=vm, out_shape=SDS((rows, LANES), F32),
        scratch_shapes=[pltpu.VMEM((N_DEV, rows, LANES), F32), pltpu.SemaphoreType.DMA((N_DEV - 1,)),
                        pltpu.SemaphoreType.DMA((N_DEV - 1,))],
    )(buf)


def _adam_update(w, gv, m, v):
    mn = ADAM_B1 * m + (1.0 - ADAM_B1) * gv
    vn = ADAM_B2 * v + (1.0 - ADAM_B2) * (gv * gv)
    m_hat = mn / (1.0 - ADAM_B1 ** ADAM_STEP)
    v_hat = vn / (1.0 - ADAM_B2 ** ADAM_STEP)
    return -ADAM_LR * (m_hat / (jnp.sqrt(v_hat) + ADAM_EPS) + ADAM_WD * w), mn, vn


def _adamw_halves(w, mine, theirs, m, v, c, name, row0=0, bufs=None):
    R, C = w.shape
    half = mine.shape[0]
    tr = min(128, half)
    nbh = half // tr
    b0 = row0 // tr
    assert row0 % tr == 0 and half % tr == 0

    def body(c_ref, w_ref, a_ref, b_ref, m_ref, v_ref, *rest):
        g_ref, d_ref, mo_ref, vo_ref = rest[-4:]
        is_mine = (pl.program_id(0) // nbh) == c_ref[0]
        gv = jnp.where(is_mine, a_ref[...], b_ref[...])
        g_ref[...] = gv
        d_ref[...], mo_ref[...], vo_ref[...] = _adam_update(w_ref[...], gv, m_ref[...], v_ref[...])

    blk = pl.BlockSpec((tr, C), lambda i, c_ref: (b0 + i, 0))
    ablk = pl.BlockSpec((tr, C), lambda i, c_ref: (jnp.where(i // nbh == c_ref[0], i % nbh, 0), 0))
    bblk = pl.BlockSpec((tr, C), lambda i, c_ref: (jnp.where(i // nbh == c_ref[0], 0, i % nbh), 0))
    extra = [] if bufs is None else list(bufs)
    return pl.pallas_call(
        body, name=name,
        grid_spec=pltpu.PrefetchScalarGridSpec(num_scalar_prefetch=1, grid=(2 * nbh,),
                                               in_specs=[blk, ablk, bblk, blk, blk] + [HBM_SPEC] * len(extra),
                                               out_specs=(blk,) * 4),
        out_shape=(SDS((R, C), F32),) * 4, compiler_params=_cp("parallel"),
        input_output_aliases={6 + i: i for i in range(len(extra))},
    )(jnp.reshape(c, (1,)).astype(jnp.int32), w, mine, theirs, m, v, *extra)


def _adamw(w, g, m, v, name):
    R, C = w.shape
    tr = min(256, R)
    assert R % tr == 0

    def body(w_ref, g_ref, m_ref, v_ref, d_ref, mo_ref, vo_ref):
        d_ref[...], mo_ref[...], vo_ref[...] = _adam_update(w_ref[...], g_ref[...], m_ref[...], v_ref[...])

    blk = pl.BlockSpec((tr, C), lambda i: (i, 0))
    return pl.pallas_call(
        body, name=name, grid=(R // tr,), in_specs=[blk] * 4, out_specs=(blk,) * 3,
        out_shape=(SDS((R, C), F32),) * 3, compiler_params=_cp("parallel"),
    )(w, g, m, v)


def _pack(arrs):
    rows = []
    for a in arrs:
        flat = a.reshape(-1).astype(F32)
        pad = (-flat.shape[0]) % LANES
        rows.append(jnp.pad(flat, (0, pad)).reshape(-1, LANES))
    buf = jnp.concatenate(rows, axis=0)
    pad_rows = (-buf.shape[0]) % 8
    return jnp.pad(buf, ((0, pad_rows), (0, 0)))


def _unpack(buf, shapes):
    out, r = [], 0
    for shp in shapes:
        size = int(np.prod(shp))
        nr = -(-size // LANES)
        out.append(buf[r:r + nr].reshape(-1)[:size].reshape(shp))
        r += nr
    return out


def _pad_lanes(a):
    return jnp.pad(a, ((0, 0), (0, LANES - a.shape[1])))


def kernel(x, ret_w_in, ret_gn_g, ret_w_out, gdn_w_in, gdn_conv_w, gdn_a_log, gdn_dt_bias, gdn_norm_g, gdn_w_out, ln_mix_g, ln_mix_b, mlp_w1, mlp_w2, ln_ffn_g, ln_ffn_b, loss_target, m_ret_w_in, m_ret_gn_g, m_ret_w_out, m_gdn_w_in, m_gdn_conv_w, m_gdn_a_log, m_gdn_dt_bias, m_gdn_norm_g, m_gdn_w_out, m_ln_mix_g, m_ln_mix_b, m_mlp_w1, m_mlp_w2, m_ln_ffn_g, m_ln_ffn_b, v_ret_w_in, v_ret_gn_g, v_ret_w_out, v_gdn_w_in, v_gdn_conv_w, v_gdn_a_log, v_gdn_dt_bias, v_gdn_norm_g, v_gdn_w_out, v_ln_mix_g, v_ln_mix_b, v_mlp_w1, v_mlp_w2, v_ln_ffn_g, v_ln_ffn_b):
    cx, cy = lax.axis_index("x"), lax.axis_index("y")
    chip = 2 * cx + cy

    sh = dict(ret_w_in=ret_w_in[0].astype(BF16), ret_w_out=ret_w_out[0].astype(BF16),
              gdn_w_in=gdn_w_in[0].astype(BF16), gdn_w_out=gdn_w_out[0].astype(BF16),
              mlp_w1_0=mlp_w1[0].astype(BF16), mlp_w1_1=mlp_w1[1].astype(BF16),
              mlp_w2_0=mlp_w2[0].astype(BF16), mlp_w2_1=mlp_w2[1].astype(BF16), conv_w=gdn_conv_w[0])
    small = dict(ret_gn_g=ret_gn_g, a_log=_pad_lanes(gdn_a_log), dt_bias=_pad_lanes(gdn_dt_bias), norm_g=gdn_norm_g,
                 ln_mix_g=ln_mix_g, ln_mix_b=ln_mix_b, ln_ffn_g=ln_ffn_g, ln_ffn_b=ln_ffn_b)

    loss, grad_x, big, sm = _local_step(x[0], loss_target[0], sh, small)

    small_names = ["ret_gn_g", "a_log", "dt_bias", "norm_g", "ln_mix_g", "ln_mix_b", "ln_ffn_g", "ln_ffn_b", "conv_w"]
    small_shapes = [(1, 4096), (1, LANES), (1, LANES), (1, GDN_D), (2, D_MODEL), (2, D_MODEL), (2, D_MODEL),
                    (2, D_MODEL), (4, GDN_QKV)]
    red = _all_reduce_small(_pack([loss] + [sm[k] for k in small_names]))
    red_loss, *red_small = _unpack(red, [(1, 1)] + small_shapes)
    gs = dict(zip(small_names, red_small))
    g_conv = lax.dynamic_slice_in_dim(gs["conv_w"], chip * 2048, 2048, axis=1)
    g_a_log, g_dt_bias = gs["a_log"][:, :GDN_HV], gs["dt_bias"][:, :GDN_HV]

    big_w = [(ret_w_in, m_ret_w_in, v_ret_w_in, [big["ret_w_in"]]), (ret_w_out, m_ret_w_out, v_ret_w_out, [big["ret_w_out"]]),
             (gdn_w_in, m_gdn_w_in, v_gdn_w_in, [big["gdn_w_in"]]), (gdn_w_out, m_gdn_w_out, v_gdn_w_out, [big["gdn_w_out"]]),
             (mlp_w1, m_mlp_w1, v_mlp_w1, big["mlp_w1"]), (mlp_w2, m_mlp_w2, v_mlp_w2, big["mlp_w2"])]
    core = lax.axis_index("c")
    big_out = []
    for i, (w_, m_, v_, layers) in enumerate(big_w):
        two_d = lambda a: a.reshape(-1, a.shape[-1])
        res = None
        for j, (mine, theirs) in enumerate(layers):
            res = _adamw_halves(two_d(w_), mine, theirs, two_d(m_), two_d(v_), core, "adamw_%d_%d" % (i, j),
                                row0=j * 2 * mine.shape[0], bufs=res)
        big_out.append(tuple(a.reshape(w_.shape) for a in res))
    sm_w = [(ret_gn_g, m_ret_gn_g, v_ret_gn_g, gs["ret_gn_g"]), (gdn_conv_w, m_gdn_conv_w, v_gdn_conv_w, g_conv),
            (gdn_a_log, m_gdn_a_log, v_gdn_a_log, g_a_log), (gdn_dt_bias, m_gdn_dt_bias, v_gdn_dt_bias, g_dt_bias),
            (gdn_norm_g, m_gdn_norm_g, v_gdn_norm_g, gs["norm_g"]), (ln_mix_g, m_ln_mix_g, v_ln_mix_g, gs["ln_mix_g"]),
            (ln_mix_b, m_ln_mix_b, v_ln_mix_b, gs["ln_mix_b"]), (ln_ffn_g, m_ln_ffn_g, v_ln_ffn_g, gs["ln_ffn_g"]),
            (ln_ffn_b, m_ln_ffn_b, v_ln_ffn_b, gs["ln_ffn_b"])]
    sm_shapes = [w_.shape for w_, _, _, _ in sm_w]
    d_s, nm_s, nv_s = _adamw(_pack([w_ for w_, _, _, _ in sm_w]), _pack([g_ for _, _, _, g_ in sm_w]),
                             _pack([m_ for _, m_, _, _ in sm_w]), _pack([v_ for _, _, v_, _ in sm_w]), "adamw_small")
    d_s, nm_s, nv_s = (_unpack(a, sm_shapes) for a in (d_s, nm_s, nv_s))
    sm_out = [(g_.reshape(w_.shape), d_s[i], nm_s[i], nv_s[i]) for i, (w_, _, _, g_) in enumerate(sm_w)]

    per_w = [big_out[0], sm_out[0], big_out[1], big_out[2], sm_out[1], sm_out[2], sm_out[3], sm_out[4], big_out[3],
             sm_out[5], sm_out[6], big_out[4], big_out[5], sm_out[7], sm_out[8]]
    outs = [red_loss.reshape(()), grad_x[None]]
    for kind in range(4):
        outs.extend(t[kind] for t in per_w)
    return tuple(outs)
```

```python
import functools

import numpy as np
import jax
import jax.numpy as jnp
from jax import lax
from jax.experimental import pallas as pl
from jax.experimental.pallas import tpu as pltpu

F32 = jnp.float32
BF16 = jnp.bfloat16
MESH = pl.DeviceIdType.MESH
SDS = jax.ShapeDtypeStruct

D_MODEL = 2048
CHUNK = 64
RET_HEADS, RET_DK, RET_DV = 8, 256, 512
GDN_HV, GDN_D = 32, 128
HP = 2
KB = 2
NH = KB * HP
GDN_QKV = 8192
ALPHA = 4.0 ** 0.25
LN_EPS, GN_EPS, RMS_EPS, L2_EPS = 1e-5, 1e-6, 1e-6, 1e-6
ADAM_LR, ADAM_B1, ADAM_B2, ADAM_EPS, ADAM_WD, ADAM_STEP = 0.001, 0.9, 0.999, 1e-8, 0.01, 10

VMEM_LIMIT_BYTES = 56 * 1024 * 1024
RT = 256
CT = 256
ROWS = 256
LANES = 128
N_CHIPS = 4
N_DEV = 8


def _cp(*sem):
    return pltpu.CompilerParams(dimension_semantics=sem, vmem_limit_bytes=VMEM_LIMIT_BYTES)


def _dot(a, b):
    return jnp.dot(a, b, preferred_element_type=F32)


def _dot_nt(a, b):
    return lax.dot_general(a, b, (((1,), (1,)), ((), ())), preferred_element_type=F32)


def _dot_tn(a, b):
    return lax.dot_general(a, b, (((0,), (0,)), ((), ())), preferred_element_type=F32)


def _split2(x):
    hi = x.astype(BF16)
    lo = (x - hi.astype(F32)).astype(BF16)
    return hi, lo


def _dotx3(a, b):
    ah, al = _split2(a)
    bh, bl = _split2(b)
    return _dot(ah, bh) + (_dot(ah, bl) + _dot(al, bh))


def _dot_exact_l(l_bf16, x):
    hi = x.astype(BF16)
    r = x - hi.astype(F32)
    mid = r.astype(BF16)
    lo = (r - mid.astype(F32)).astype(BF16)
    return _dot(l_bf16, hi) + (_dot(l_bf16, mid) + _dot(l_bf16, lo))


def _sigmoid(x):
    return 1.0 / (1.0 + jnp.exp(-x))


def _iota(shape, dim):
    return lax.broadcasted_iota(jnp.int32, shape, dim)


class _Side:
    def __init__(self, arrays, out_shapes, phases, counts, local=None, n_local=0):
        self.arrays, self.out_shapes, self.phases, self.counts = list(arrays), list(out_shapes), phases, counts
        self.local, self.n_local = local, n_local

    def sem_shapes(self):
        sems = [pltpu.SemaphoreType.DMA((sum(self.counts),))] * 2
        return sems + ([pltpu.SemaphoreType.DMA((self.n_local,))] if self.n_local else [])

    def ops(self, ins, outs, sems):
        send_sems, recv_sems = sems[0], sems[1]

        def copies(ph):
            if ph < 0:
                return [pltpu.make_async_copy(src, dst, sems[2].at[i])
                        for i, (src, dst) in enumerate(self.local(_coords(), ins, outs))] if self.n_local else []
            off = sum(self.counts[:ph])
            return [pltpu.make_async_remote_copy(src_ref=src, dst_ref=dst, send_sem=send_sems.at[off + i],
                                                 recv_sem=recv_sems.at[off + i], device_id=to, device_id_type=MESH)
                    for i, (src, dst, to) in enumerate(self.phases[ph](_coords(), ins, outs))]

        def start(ph):
            for cp in copies(ph):
                cp.start()

        def wait(ph):
            cps = copies(ph)
            if ph < 0:
                for cp in cps:
                    cp.wait()
                return
            for cp in cps:
                cp.wait_recv()
            for cp in cps:
                cp.wait_send()

        return start, wait


def _comm_call(name, side):
    n = len(side.arrays)

    def body(*refs):
        start, wait = side.ops(refs[:n], refs[n:2 * n], refs[2 * n:])
        start(-1)
        for ph in range(len(side.phases)):
            start(ph)
            wait(ph)
        wait(-1)

    return pl.pallas_call(body, name=name, in_specs=[HBM_SPEC] * n, out_specs=[HBM_SPEC] * n,
                          out_shape=side.out_shapes, scratch_shapes=side.sem_shapes())(*side.arrays)


SIDE_SWITCH = 0.85


def _pcall(body, args, *, name, grid, in_specs, out_specs, out_shape, sem, scratch_shapes=(), sides=()):
    single = not isinstance(out_shape, (tuple, list))
    if not sides:
        return pl.pallas_call(body, name=name, grid=grid, in_specs=list(in_specs), out_specs=out_specs,
                              out_shape=out_shape, scratch_shapes=list(scratch_shapes), compiler_params=_cp(*sem))(*args)
    o_shapes = (out_shape,) if single else tuple(out_shape)
    o_specs = (out_specs,) if single else tuple(out_specs)
    n_in, n_out, n_scr = len(args), len(o_shapes), len(scratch_shapes)
    ns = [len(s.arrays) for s in sides]
    steps = int(np.prod(grid))

    def carrier(*refs):
        pos = n_in
        s_ins = []
        for k in ns:
            s_ins.append(refs[pos:pos + k])
            pos += k
        outs = refs[pos:pos + n_out]
        pos += n_out
        s_outs = []
        for k in ns:
            s_outs.append(refs[pos:pos + k])
            pos += k
        scr = refs[pos:pos + n_scr]
        pos += n_scr
        step = pl.program_id(0)
        for d in range(1, len(grid)):
            step = step * grid[d] + pl.program_id(d)
        hooks = []
        for i, s in enumerate(sides):
            k = len(s.sem_shapes())
            hooks.append(s.ops(s_ins[i], s_outs[i], refs[pos:pos + k]))
            pos += k
        for (start, wait), s in zip(hooks, sides):
            def first(start=start):
                start(-1)
                start(0)
            pl.when(step == 0)(first)
            if len(s.phases) == 2:
                def switch(start=start, wait=wait):
                    wait(0)
                    start(1)
                pl.when(step == int(steps * SIDE_SWITCH))(switch)
            else:
                assert len(s.phases) == 1
        body(*refs[:n_in], *outs, *scr)
        for (start, wait), s in zip(hooks, sides):
            def last(wait=wait, n_ph=len(s.phases)):
                wait(n_ph - 1)
                wait(-1)
            pl.when(step == steps - 1)(last)

    res = pl.pallas_call(
        carrier, name=name, grid=grid, in_specs=list(in_specs) + [HBM_SPEC] * sum(ns),
        out_specs=o_specs + (HBM_SPEC,) * sum(ns),
        out_shape=o_shapes + tuple(sh for s in sides for sh in s.out_shapes),
        scratch_shapes=list(scratch_shapes) + [sm for s in sides for sm in s.sem_shapes()],
        compiler_params=_cp(*(("arbitrary",) * len(grid))),
    )(*args, *[a for s in sides for a in s.arrays])
    main, rest, side_res = res[:n_out], list(res[n_out:]), []
    for k in ns:
        side_res.append(rest[:k])
        rest = rest[k:]
    return (main[0] if single else tuple(main)), side_res


def _mm(a, b, mode, name, *, out_dtype=F32, tm=1024, tn=1024, tk=2048, epi=None, extra=None, scale=1.0,
        shard_major=False, sides=()):
    if mode == "nn":
        (M, K), (K2, N) = a.shape, b.shape
    elif mode == "nt":
        (M, K), (N, K2) = a.shape, b.shape
    else:
        (K, M), (K2, N) = a.shape, b.shape
    assert K == K2, (a.shape, b.shape, mode)
    tm, tn, tk = min(tm, M), min(tn, N), min(tk, K)
    assert M % tm == 0 and N % tn == 0 and K % tk == 0, (M, N, K, tm, tn, tk)
    nk = K // tk
    dims = {"nn": (((1,), (0,)), ((), ())), "nt": (((1,), (1,)), ((), ())), "tn": (((0,), (0,)), ((), ()))}[mode]
    if mode == "tn":
        a_spec = pl.BlockSpec((tk, tm), lambda i, j, k: (k, i))
    else:
        a_spec = pl.BlockSpec((tm, tk), lambda i, j, k: (i, k))
    if mode == "nt":
        b_spec = pl.BlockSpec((tn, tk), lambda i, j, k: (j, k))
    else:
        b_spec = pl.BlockSpec((tk, tn), lambda i, j, k: (k, j))
    tile = pl.BlockSpec((tm, tn), lambda i, j, k: (i, j))
    in_specs, ins = [a_spec, b_spec], [a, b]
    if epi == "rope":
        half = RET_DK // 2
        assert tn % RET_DK == 0 and (2 * RET_HEADS * RET_DK) % tn == 0
        in_specs += [pl.BlockSpec((tm, half), lambda i, j, k: (i, 0))] * 2
        ins += list(extra)
    elif epi == "ln":
        assert tn == N
        vec = pl.BlockSpec((1, tn), lambda i, j, k: (0, j))
        in_specs += [tile, vec, vec]
        ins += list(extra)
    elif extra is not None:
        in_specs.append(tile)
        ins.append(extra)
    n_extra = len(ins) - 2
    if epi == "relu2":
        out_shape = (SDS((M, N), F32), SDS((M, N), BF16))
        out_specs = (tile, tile)
    elif epi == "ln":
        out_shape = (SDS((M, N), F32), SDS((M, N), BF16), SDS((M, N), F32))
        out_specs = (tile, tile, tile)
    elif shard_major:
        per = (N // N_CHIPS) // tn
        assert per * tn * N_CHIPS == N
        out_shape = (SDS((N_CHIPS, M, N // N_CHIPS), out_dtype),)
        out_specs = (pl.BlockSpec((None, tm, tn), lambda i, j, k: (j // per, i, j % per)),)
    else:
        out_shape = (SDS((M, N), out_dtype),)
        out_specs = (tile,)
    n_out = len(out_shape)

    def body(*refs):
        a_ref, b_ref = refs[0], refs[1]
        x_ref = refs[2] if n_extra else None
        pos = 2 + n_extra
        o_refs = refs[pos:pos + n_out]
        acc_ref = refs[pos + n_out] if nk > 1 else None

        def prod():
            av, bv = a_ref[...], b_ref[...]
            if av.dtype != BF16:
                av = av.astype(BF16)
            if bv.dtype != BF16:
                bv = bv.astype(BF16)
            return lax.dot_general(av, bv, dims, preferred_element_type=F32)

        def finish(acc):
            if epi == "relu2":
                o_refs[0][...] = acc
                r = jnp.maximum(acc, 0.0)
                o_refs[1][...] = (r * r).astype(BF16)
            elif epi == "drelu2":
                o_refs[0][...] = (acc * (2.0 * jnp.maximum(x_ref[...], 0.0))).astype(out_dtype)
            elif epi == "add":
                o_refs[0][...] = (acc + scale * x_ref[...]).astype(out_dtype)
            elif epi == "ln":
                z = ALPHA * x_ref[...] + acc
                xh, _ = _ln_stats(z)
                o = xh * refs[3][...] + refs[4][...]
                o_refs[0][...] = o
                o_refs[1][...] = o.astype(BF16)
                o_refs[2][...] = z
            elif epi == "rope":
                j = pl.program_id(1)
                qk_tiles = 2 * RET_HEADS * RET_DK // tn

                @pl.when(j < qk_tiles)
                def _():
                    c, s = refs[2][...], refs[3][...]
                    sc = jnp.where(j >= qk_tiles // 2, RET_DK ** -0.5, 1.0)
                    for hh in range(tn // RET_DK):
                        lo = slice(hh * RET_DK, hh * RET_DK + half)
                        hi = slice(hh * RET_DK + half, (hh + 1) * RET_DK)
                        t1, t2 = acc[:, lo], acc[:, hi]
                        o_refs[0][:, lo] = ((t1 * c - t2 * s) * sc).astype(out_dtype)
                        o_refs[0][:, hi] = ((t1 * s + t2 * c) * sc).astype(out_dtype)

                @pl.when(j >= qk_tiles)
                def _():
                    o_refs[0][...] = acc.astype(out_dtype)
            else:
                o_refs[0][...] = acc.astype(out_dtype)

        if nk == 1:
            finish(prod())
        else:
            k = pl.program_id(2)

            @pl.when(k == 0)
            def _():
                acc_ref[...] = prod()

            @pl.when(k > 0)
            def _():
                acc_ref[...] += prod()

            @pl.when(k == nk - 1)
            def _():
                finish(acc_ref[...])

    res = _pcall(body, ins, name=name, grid=(M // tm, N // tn, nk), in_specs=in_specs, out_specs=out_specs,
                 out_shape=out_shape, scratch_shapes=[pltpu.VMEM((tm, tn), F32)] if nk > 1 else [],
                 sem=("parallel", "parallel", "arbitrary"), sides=sides)
    main, side_res = res if sides else (res, None)
    main = main if n_out > 1 else main[0]
    return (main, side_res) if sides else main


def _ln_stats(z):
    mu = jnp.mean(z, -1, keepdims=True)
    zc = z - mu
    var = jnp.mean(zc * zc, -1, keepdims=True)
    rstd = lax.rsqrt(var + LN_EPS)
    return zc * rstd, rstd


def _ln_bwd(dout, z, g, name):
    S, Dm = z.shape
    row = pl.BlockSpec((ROWS, Dm), lambda t: (t, 0))
    vec = pl.BlockSpec((1, Dm), lambda t: (0, 0))

    def body(d_ref, z_ref, g_ref, dz_ref, dzb_ref, dg_ref, db_ref):
        t = pl.program_id(0)
        xh, rstd = _ln_stats(z_ref[...])
        d = d_ref[...]
        dxh = d * g_ref[...]
        m1 = jnp.mean(dxh, -1, keepdims=True)
        m2 = jnp.mean(dxh * xh, -1, keepdims=True)
        dz = rstd * (dxh - m1 - xh * m2)
        dz_ref[...] = dz
        dzb_ref[...] = dz.astype(BF16)
        pg = jnp.sum(d * xh, axis=0, keepdims=True)
        pb = jnp.sum(d, axis=0, keepdims=True)

        @pl.when(t == 0)
        def _():
            dg_ref[...] = pg
            db_ref[...] = pb

        @pl.when(t > 0)
        def _():
            dg_ref[...] += pg
            db_ref[...] += pb

    return pl.pallas_call(
        body, name=name, grid=(S // ROWS,), in_specs=[row, row, vec], out_specs=(row, row, vec, vec),
        out_shape=(SDS((S, Dm), F32), SDS((S, Dm), BF16), SDS((1, Dm), F32), SDS((1, Dm), F32)),
        compiler_params=_cp("arbitrary"),
    )(dout, z, g)


def _loss_ln_bwd(z, g, b, tgt, name):
    S, Dm = z.shape
    row = pl.BlockSpec((ROWS, Dm), lambda t: (t, 0))
    vec = pl.BlockSpec((1, Dm), lambda t: (0, 0))
    one = pl.BlockSpec((1, 1), lambda t: (0, 0))

    def body(z_ref, g_ref, b_ref, t_ref, l_ref, dz_ref, dzb_ref, dg_ref, db_ref):
        t = pl.program_id(0)
        xh, rstd = _ln_stats(z_ref[...])
        diff = xh * g_ref[...] + b_ref[...] - t_ref[...]
        part = jnp.sum(jnp.sum(diff * diff, axis=1, keepdims=True), axis=0, keepdims=True) * (0.5 / Dm)
        d = diff * (1.0 / Dm)
        dxh = d * g_ref[...]
        m1 = jnp.mean(dxh, -1, keepdims=True)
        m2 = jnp.mean(dxh * xh, -1, keepdims=True)
        dz = rstd * (dxh - m1 - xh * m2)
        dz_ref[...] = dz
        dzb_ref[...] = dz.astype(BF16)
        pg = jnp.sum(d * xh, axis=0, keepdims=True)
        pb = jnp.sum(d, axis=0, keepdims=True)

        @pl.when(t == 0)
        def _():
            l_ref[...] = part
            dg_ref[...] = pg
            db_ref[...] = pb

        @pl.when(t > 0)
        def _():
            l_ref[...] += part
            dg_ref[...] += pg
            db_ref[...] += pb

    return pl.pallas_call(
        body, name=name, grid=(S // ROWS,), in_specs=[row, vec, vec, row], out_specs=(one, row, row, vec, vec),
        out_shape=(SDS((1, 1), F32), SDS((S, Dm), F32), SDS((S, Dm), BF16), SDS((1, Dm), F32), SDS((1, Dm), F32)),
        compiler_params=_cp("arbitrary"),
    )(z, g, b, tgt)


def _ret_consts():
    h = np.arange(RET_HEADS, dtype=np.float64)
    lg = np.log1p(-np.exp2(-5.0 - h))
    return jnp.asarray(np.concatenate([lg, np.exp(lg * RT)]).astype(np.float32))


def _rope_tables(S):
    half = RET_DK // 2
    inv = 10000.0 ** (-jnp.arange(half, dtype=F32) / half)
    ang = jnp.arange(S).astype(F32)[:, None] * inv[None, :]
    return jnp.cos(ang), jnp.sin(ang)


def _ret_masks(lgh):
    ri, ci = _iota((RT, RT), 0), _iota((RT, RT), 1)
    visible = (ci >> 6) <= (ri >> 6)
    m = jnp.where(visible, jnp.exp(lgh * jnp.abs(ri - ci).astype(F32)), 0.0)
    pos = _iota((RT, 1), 0).astype(F32)
    return m, jnp.exp(lgh * (pos + 1.0)), jnp.exp(lgh * (RT - 1.0 - pos))


def _ret_fwd(h0, gn_g, consts, sides=()):
    S = h0.shape[0]
    nt = S // RT

    def body(c_ref, q_ref, k_ref, v_ref, gate_ref, g_ref, y_ref, o_ref, st_ref, s_scr):
        h, t = pl.program_id(0), pl.program_id(1)

        @pl.when(t == 0)
        def _():
            s_scr[...] = jnp.zeros_like(s_scr)

        lgh, cdec = c_ref[h], c_ref[RET_HEADS + h]
        m, dq, dk = _ret_masks(lgh)
        qv, kv, vv = q_ref[...], k_ref[...], v_ref[...]
        p = (_dot_nt(qv, kv) * m).astype(BF16)
        sp = s_scr[...]
        spb = sp.astype(BF16)
        st_ref[...] = spb
        qd = (qv.astype(F32) * dq).astype(BF16)
        kd = (kv.astype(F32) * dk).astype(BF16)
        y = _dot(p, vv) + _dot(qd, spb)
        y_ref[...] = y
        s_scr[...] = sp * cdec + _dot_tn(kd, vv)
        yn, _ = _gn_stats(y)
        gate = gate_ref[...].astype(F32)
        o_ref[...] = (gate * _sigmoid(gate) * (yn * g_ref[...])).astype(BF16)

    qk = lambda off: pl.BlockSpec((RT, RET_DK), lambda h, t: (t, off + h))
    vs = lambda off: pl.BlockSpec((RT, RET_DV), lambda h, t: (t, off + h))
    return _pcall(
        body, (consts, h0, h0, h0, h0, gn_g), name="ret_fwd", grid=(RET_HEADS, nt),
        in_specs=[pl.BlockSpec(memory_space=pltpu.SMEM), qk(0), qk(RET_HEADS), vs(RET_HEADS), vs(2 * RET_HEADS),
                  pl.BlockSpec((1, RET_DV), lambda h, t: (0, h))],
        out_specs=(vs(0), vs(0), pl.BlockSpec((None, None, RET_DK, RET_DV), lambda h, t: (h, t, 0, 0))),
        out_shape=(SDS((S, 4096), F32), SDS((S, 4096), BF16), SDS((RET_HEADS, nt, RET_DK, RET_DV), BF16)),
        scratch_shapes=[pltpu.VMEM((RET_DK, RET_DV), F32)], sem=("parallel", "arbitrary"), sides=sides)


def _ret_bwd(h0, do, y, gn_g, states, consts, cos, sin, sides=()):
    S = h0.shape[0]
    nt = S // RT
    half = RET_DK // 2

    def body(c_ref, q_ref, k_ref, v_ref, gate_ref, do_ref, y_ref, g_ref, st_ref, cos_ref, sin_ref,
             dq_ref, dk_ref, dv_ref, dgate_ref, dg_ref, ds_scr):
        h, t = pl.program_id(0), pl.program_id(1)

        @pl.when(t == 0)
        def _():
            ds_scr[...] = jnp.zeros_like(ds_scr)
            dg_ref[...] = jnp.zeros_like(dg_ref)

        yn, rstd = _gn_stats(y_ref[...])
        gate, g, dov = gate_ref[...].astype(F32), g_ref[...], do_ref[...]
        sg = _sigmoid(gate)
        dgate_ref[...] = (dov * (yn * g) * (sg * (1.0 + gate * (1.0 - sg)))).astype(BF16)
        dyg = dov * (gate * sg)
        dyn = dyg * g
        m1 = jnp.mean(dyn, -1, keepdims=True)
        m2 = jnp.mean(dyn * yn, -1, keepdims=True)
        dyv = (rstd * (dyn - m1 - yn * m2)).astype(BF16)
        dg_ref[...] += jnp.sum(dyg * yn, axis=0, keepdims=True)

        lgh, cdec = c_ref[h], c_ref[RET_HEADS + h]
        m, dqc, dkc = _ret_masks(lgh)
        qv, kv, vv, spb = q_ref[...], k_ref[...], v_ref[...], st_ref[...]
        p = (_dot_nt(qv, kv) * m).astype(BF16)
        qd = (qv.astype(F32) * dqc).astype(BF16)
        kd = (kv.astype(F32) * dkc).astype(BF16)
        dsn = ds_scr[...]
        dsb = dsn.astype(BF16)
        dsc = (_dot_nt(dyv, vv) * m).astype(BF16)
        dq = _dot(dsc, kv) + _dot_nt(dyv, spb) * dqc
        dk = _dot_tn(dsc, qv) + _dot_nt(vv, dsb) * dkc
        dv_ref[...] = (_dot_tn(p, dyv) + _dot(kd, dsb)).astype(BF16)
        ds_scr[...] = dsn * cdec + _dot_tn(qd, dyv)
        c, s = cos_ref[...], sin_ref[...]

        def unrot(d):
            d1, d2 = d[:, :half], d[:, half:]
            return jnp.concatenate([d1 * c + d2 * s, d2 * c - d1 * s], axis=-1)

        dq_ref[...] = unrot(dq).astype(BF16)
        dk_ref[...] = (unrot(dk) * (RET_DK ** -0.5)).astype(BF16)

    rev = lambda t: nt - 1 - t
    qkb = lambda off: pl.BlockSpec((RT, RET_DK), lambda h, t: (rev(t), off + h))
    vsb = lambda off: pl.BlockSpec((RT, RET_DV), lambda h, t: (rev(t), off + h))
    qk, vs = qkb(0), vsb(0)
    tab = pl.BlockSpec((RT, half), lambda h, t: (rev(t), 0))
    vec = pl.BlockSpec((1, RET_DV), lambda h, t: (0, h))
    return _pcall(
        body, (consts, h0, h0, h0, h0, do, y, gn_g, states, cos, sin), name="ret_bwd", grid=(RET_HEADS, nt),
        in_specs=[pl.BlockSpec(memory_space=pltpu.SMEM), qk, qkb(RET_HEADS), vsb(RET_HEADS), vsb(2 * RET_HEADS), vs, vs,
                  vec, pl.BlockSpec((None, None, RET_DK, RET_DV), lambda h, t: (h, rev(t), 0, 0)), tab, tab],
        out_specs=(qk, qk, vs, vs, vec),
        out_shape=(SDS((S, 2048), BF16), SDS((S, 2048), BF16), SDS((S, 4096), BF16), SDS((S, 4096), BF16),
                   SDS((1, 4096), F32)),
        scratch_shapes=[pltpu.VMEM((RET_DK, RET_DV), F32)], sem=("parallel", "arbitrary"), sides=sides)


def _gn_stats(y):
    mu = jnp.mean(y, -1, keepdims=True)
    yc = y - mu
    var = jnp.mean(yc * yc, -1, keepdims=True)
    rstd = lax.rsqrt(var + GN_EPS)
    return yc * rstd, rstd


CONV_RB, CONV_CB = 32, 512


def _conv_tiles(rows, cols):
    return [(r0, slice(c0, c0 + CONV_CB)) for r0 in range(0, rows, CONV_RB) for c0 in range(0, cols, CONV_CB)]


def _conv_fill(x_ref, halo_ref, ext_scr, t):
    ext_scr[0:8, :] = jnp.where(t == 0, 0.0, halo_ref[...])
    ext_scr[8:, :] = x_ref[...]


def _conv_tile(ext_scr, w, r0, cs):
    acc = w[3:4, cs] * ext_scr[pl.ds(8 + r0, CONV_RB), cs]
    for j in range(3):
        acc = acc + w[j:j + 1, cs] * ext_scr[pl.ds(5 + j + r0, CONV_RB), cs]
    return acc


def _gdn_conv_fwd(h1, conv_w, kind):
    S = h1.shape[0]
    base = {"q": 0, "k": 1, "v": 2}[kind]
    ncb = 2 if kind == "v" else 1
    C = 2048

    def body(x_ref, halo_ref, w_ref, o_ref, ext_scr):
        _conv_fill(x_ref, halo_ref, ext_scr, pl.program_id(0))
        w = w_ref[...]
        scale = GDN_D ** -0.5 if kind == "q" else 1.0
        for r0, cs in _conv_tiles(ROWS, C):
            rs = pl.ds(r0, CONV_RB)
            acc = _conv_tile(ext_scr, w, r0, cs)
            c = acc * _sigmoid(acc)
            if kind == "v":
                o_ref[rs, cs] = c.astype(BF16)
            else:
                for hh in range(CONV_CB // GDN_D):
                    ch = c[:, hh * GDN_D:(hh + 1) * GDN_D]
                    r = lax.rsqrt(jnp.sum(ch * ch, -1, keepdims=True) + L2_EPS)
                    o_ref[rs, pl.ds(cs.start + hh * GDN_D, GDN_D)] = (ch * (r * scale)).astype(BF16)

    hb = ROWS // 8
    return pl.pallas_call(
        body, name="gdn_conv_fwd_" + kind, grid=(S // ROWS, ncb),
        in_specs=[pl.BlockSpec((ROWS, C), lambda t, j: (t, base + j)),
                  pl.BlockSpec((8, C), lambda t, j: (jnp.maximum(t * hb - 1, 0), base + j)),
                  pl.BlockSpec((4, C), lambda t, j: (0, base + j))],
        out_specs=pl.BlockSpec((ROWS, C), lambda t, j: (t, j)), out_shape=SDS((S, C * ncb), BF16),
        scratch_shapes=[pltpu.VMEM((ROWS + 8, C), F32)], compiler_params=_cp("parallel", "parallel"),
    )(h1, h1, conv_w)


def _gdn_conv_bwd_act(h1, conv_w, dn, kind, buf=None):
    S = h1.shape[0]
    base = {"q": 0, "k": 1, "v": 2}[kind]
    ncb = 2 if kind == "v" else 1
    C = 2048

    def body(x_ref, halo_ref, w_ref, dn_ref, *rest):
        o_ref, ext_scr = rest[-2], rest[-1]
        _conv_fill(x_ref, halo_ref, ext_scr, pl.program_id(0))
        w = w_ref[...]
        scale = GDN_D ** -0.5 if kind == "q" else 1.0
        for r0, cs in _conv_tiles(ROWS, C):
            rs = pl.ds(r0, CONV_RB)
            acc = _conv_tile(ext_scr, w, r0, cs)
            sg = _sigmoid(acc)
            dsilu = sg * (1.0 + acc * (1.0 - sg))
            if kind == "v":
                o_ref[rs, cs] = dn_ref[rs, cs] * dsilu
            else:
                c = acc * sg
                for hh in range(CONV_CB // GDN_D):
                    sl = slice(hh * GDN_D, (hh + 1) * GDN_D)
                    gl = pl.ds(cs.start + hh * GDN_D, GDN_D)
                    ch, dnh = c[:, sl], dn_ref[rs, gl]
                    r = lax.rsqrt(jnp.sum(ch * ch, -1, keepdims=True) + L2_EPS)
                    proj = jnp.sum(dnh * ch, -1, keepdims=True)
                    o_ref[rs, gl] = (scale * r) * (dnh - ch * (proj * r * r)) * dsilu[:, sl]

    hb = ROWS // 8
    return pl.pallas_call(
        body, name="gdn_conv_bwd_act_" + kind, grid=(S // ROWS, ncb),
        in_specs=[pl.BlockSpec((ROWS, C), lambda t, j: (t, base + j)),
                  pl.BlockSpec((8, C), lambda t, j: (jnp.maximum(t * hb - 1, 0), base + j)),
                  pl.BlockSpec((4, C), lambda t, j: (0, base + j)),
                  pl.BlockSpec((ROWS, C), lambda t, j: (t, j))] + ([] if buf is None else [HBM_SPEC]),
        out_specs=pl.BlockSpec((ROWS, C), lambda t, j: (t, base + j)), out_shape=SDS((S, GDN_QKV), F32),
        input_output_aliases={} if buf is None else {4: 0},
        scratch_shapes=[pltpu.VMEM((ROWS + 8, C), F32)], compiler_params=_cp("parallel", "parallel"),
    )(*((h1, h1, conv_w, dn) + (() if buf is None else (buf,))))


def _gdn_conv_bwd_in(h1, conv_w, dacc, dh1_buf):
    S = h1.shape[0]
    C = 2048
    nt = S // ROWS
    hb = ROWS // 8

    def body(x_ref, halo_ref, w_ref, d_ref, dhalo_ref, buf_ref, di_ref, dw_ref, ext_scr, dext_scr):
        t = pl.program_id(1)
        _conv_fill(x_ref, halo_ref, ext_scr, t)
        dext_scr[0:ROWS, :] = d_ref[...]
        dext_scr[ROWS:, :] = jnp.where(t == nt - 1, 0.0, dhalo_ref[...])
        w = w_ref[...]

        @pl.when(t == 0)
        def _():
            dw_ref[...] = jnp.zeros_like(dw_ref)

        for c0 in range(0, C, CONV_CB):
            cs = slice(c0, c0 + CONV_CB)
            pw = [jnp.zeros((1, CONV_CB), F32) for _ in range(4)]
            for r0 in range(0, ROWS, CONV_RB):
                d = dext_scr[pl.ds(r0, CONV_RB), cs]
                di = w[3:4, cs] * d
                for j in range(3):
                    di = di + w[j:j + 1, cs] * dext_scr[pl.ds(3 - j + r0, CONV_RB), cs]
                di_ref[pl.ds(r0, CONV_RB), cs] = di.astype(BF16)
                for j in range(4):
                    pw[j] = pw[j] + jnp.sum(d * ext_scr[pl.ds(5 + j + r0, CONV_RB), cs], axis=0, keepdims=True)
            dw_ref[:, cs] += jnp.concatenate(pw, axis=0)

    return pl.pallas_call(
        body, name="gdn_conv_bwd_in", grid=(GDN_QKV // C, nt),
        in_specs=[pl.BlockSpec((ROWS, C), lambda j, t: (t, j)),
                  pl.BlockSpec((8, C), lambda j, t: (jnp.maximum(t * hb - 1, 0), j)),
                  pl.BlockSpec((4, C), lambda j, t: (0, j)),
                  pl.BlockSpec((ROWS, C), lambda j, t: (t, j)),
                  pl.BlockSpec((8, C), lambda j, t: (jnp.minimum((t + 1) * hb, nt * hb - 1), j)), HBM_SPEC],
        out_specs=(pl.BlockSpec((ROWS, C), lambda j, t: (t, j)), pl.BlockSpec((4, C), lambda j, t: (0, j))),
        out_shape=(SDS(dh1_buf.shape, BF16), SDS((4, GDN_QKV), F32)), input_output_aliases={5: 0},
        scratch_shapes=[pltpu.VMEM((ROWS + 8, C), F32), pltpu.VMEM((ROWS + 8, C), F32)],
        compiler_params=_cp("parallel", "arbitrary"),
    )(h1, h1, conv_w, dacc, dacc, dh1_buf)


def _chunk_masks():
    ri, ci = _iota((CT, CT), 0), _iota((CT, CT), 1)
    same = (ri >> 6) == (ci >> 6)
    return same, same & (ri >= ci), same & (ri > ci), same & (ri <= ci), ri == ci


def _fold_dup(m):
    h = m[:, :LANES] + m[:, LANES:]
    return h + pltpu.roll(h, CHUNK, axis=1)


def _unfold_bd(d, same):
    return jnp.where(same, jnp.concatenate([d, d], axis=1), 0.0)


def _softplus(x):
    return jnp.maximum(x, 0.0) + jnp.log(1.0 + jnp.exp(-jnp.abs(x)))


def _gdn_scal_fwd(ba, a_log, dt_bias):
    S = ba.shape[0]

    def body(ba_ref, al_ref, dt_ref, beta_ref, g_ref, gc_ref):
        bav = ba_ref[...]
        beta_ref[...] = _sigmoid(bav)
        a = pltpu.roll(bav, LANES - GDN_HV, axis=1)
        g = -jnp.exp(al_ref[...]) * _softplus(a + dt_ref[...])
        g_ref[...] = g
        causal = _chunk_masks()[1]
        gc_ref[...] = _dot_exact_l(causal.astype(BF16), g)

    row = pl.BlockSpec((CT, LANES), lambda t: (t, 0))
    vec = pl.BlockSpec((1, LANES), lambda t: (0, 0))
    return pl.pallas_call(
        body, name="gdn_scal_fwd", grid=(S // CT,), in_specs=[row, vec, vec], out_specs=(row, row, row),
        out_shape=(SDS((S, LANES), F32),) * 3, compiler_params=_cp("parallel"),
    )(ba, a_log, dt_bias)


def _gdn_scal_bwd(ba, a_log, dt_bias, g, dbeta, dgc):
    S = ba.shape[0]

    def body(ba_ref, al_ref, dt_ref, g_ref, dbeta_ref, dgc_ref, dba_ref, dal_ref, ddt_ref):
        t = pl.program_id(0)
        bav = ba_ref[...]
        beta = _sigmoid(bav)
        db = dbeta_ref[...] * beta * (1.0 - beta)
        a = pltpu.roll(bav, LANES - GDN_HV, axis=1)
        dgv = _dot_exact_l(_chunk_masks()[3].astype(BF16), dgc_ref[...])
        da = dgv * (-jnp.exp(al_ref[...])) * _sigmoid(a + dt_ref[...])
        lane = _iota(bav.shape, 1)
        da_sh = pltpu.roll(da, GDN_HV, axis=1)
        dba = jnp.where(lane < GDN_HV, db, jnp.where(lane < 2 * GDN_HV, da_sh, 0.0))
        dba_ref[...] = dba.astype(BF16)
        keep = lane < GDN_HV
        pal = jnp.sum(jnp.where(keep, dgv * g_ref[...], 0.0), axis=0, keepdims=True)
        pdt = jnp.sum(jnp.where(keep, da, 0.0), axis=0, keepdims=True)

        @pl.when(t == 0)
        def _():
            dal_ref[...] = pal
            ddt_ref[...] = pdt

        @pl.when(t > 0)
        def _():
            dal_ref[...] += pal
            ddt_ref[...] += pdt

    row = pl.BlockSpec((CT, LANES), lambda t: (t, 0))
    vec = pl.BlockSpec((1, LANES), lambda t: (0, 0))
    return pl.pallas_call(
        body, name="gdn_scal_bwd", grid=(S // CT,), in_specs=[row, vec, vec, row, row, row],
        out_specs=(row, vec, vec), out_shape=(SDS((S, LANES), BF16), SDS((1, LANES), F32), SDS((1, LANES), F32)),
        compiler_params=_cp("arbitrary"),
    )(ba, a_log, dt_bias, g, dbeta, dgc)


def _sel_col(x, h):
    return jnp.sum(jnp.where(_iota(x.shape, 1) == h, x, 0.0), axis=1, keepdims=True)


def _decay(gcol, causal):
    gm = jnp.broadcast_to(gcol, (CT, CT))
    diff = gm - gm.T
    return jnp.where(causal, jnp.exp(jnp.where(causal, diff, 0.0)), 0.0)


def _gdn_chunk_fwd(k, v, beta, gc):
    S = k.shape[0]
    nt = S // CT

    def body(k_ref, v_ref, beta_ref, gc_ref, t_ref, u_ref, w_ref):
        kg_id = pl.program_id(1)
        same, causal, strict, _, _ = _chunk_masks()
        eye_dup = jnp.where((_iota((CT, LANES), 0) & (CHUNK - 1)) == (_iota((CT, LANES), 1) & (CHUNK - 1)), 1.0, 0.0)
        kfs, xds, pds, cols = [], [], [], []
        for kb in range(KB):
            kv = k_ref[:, kb * GDN_D:(kb + 1) * GDN_D]
            kfs.append(kv.astype(F32))
            kk = _dot_nt(kv, kv)
            for hp in range(HP):
                h = NH * kg_id + kb * HP + hp
                bcol, gcol = _sel_col(beta_ref[...], h), _sel_col(gc_ref[...], h)
                x = jnp.where(strict, -(kk * bcol) * _decay(gcol, causal), 0.0)
                xds.append(_fold_dup(x))
                pds.append(eye_dup)
                cols.append((bcol, gcol))
        for m in range(6):
            for hi in range(NH):
                xh, xl = _split2(xds[hi])
                ph, pl_ = _split2(pds[hi])
                lh, ll = _unfold_bd(xh, same), _unfold_bd(xl, same)
                rh, rl = jnp.concatenate([xh, ph], axis=1), jnp.concatenate([xl, pl_], axis=1)
                out = _dot(lh, rh) + (_dot(lh, rl) + _dot(ll, rh))
                pds[hi] = pds[hi] + out[:, LANES:]
                if m < 5:
                    xds[hi] = out[:, :LANES]
        for hi in range(NH):
            bcol, gcol = cols[hi]
            cs = slice(hi * GDN_D, (hi + 1) * GDN_D)
            t_ref[hi] = pds[hi]
            tb = _unfold_bd(pds[hi], same).astype(BF16)
            vb = (v_ref[:, cs].astype(F32) * bcol).astype(BF16)
            kg = (kfs[hi // HP] * (bcol * jnp.exp(gcol))).astype(BF16)
            uw = _dot(tb, jnp.concatenate([vb, kg], axis=1))
            u_ref[:, cs] = uw[:, :GDN_D]
            w_ref[:, cs] = uw[:, GDN_D:].astype(BF16)

    col = pl.BlockSpec((CT, LANES), lambda t, kg: (t, 0))
    hv = pl.BlockSpec((CT, NH * GDN_D), lambda t, kg: (t, kg))
    return pl.pallas_call(
        body, name="gdn_chunk_fwd", grid=(nt, GDN_HV // NH),
        in_specs=[pl.BlockSpec((CT, KB * GDN_D), lambda t, kg: (t, kg)), hv, col, col],
        out_specs=(pl.BlockSpec((NH, None, CT, LANES), lambda t, kg: (kg, t, 0, 0)), hv, hv),
        out_shape=(SDS((GDN_HV, nt, CT, LANES), F32), SDS((S, 4096), F32), SDS((S, 4096), BF16)),
        compiler_params=_cp("parallel", "parallel"),
    )(k, v, beta, gc)


def _last_of_chunk(gcol, rows, c):
    return jnp.sum(jnp.where(rows == c * CHUNK + CHUNK - 1, gcol, 0.0), axis=0, keepdims=True)


def _gdn_scan_fwd(q, k, u, w, gc):
    S = q.shape[0]
    nt = S // CT
    ncs = CT // CHUNK

    def body(q_ref, k_ref, u_ref, w_ref, gc_ref, y_ref, vn_ref, st_ref, s_scr, vn_scr):
        kg_id, t = pl.program_id(0), pl.program_id(1)

        @pl.when(t == 0)
        def _():
            s_scr[...] = jnp.zeros_like(s_scr)

        causal = _chunk_masks()[1]
        rows = _iota((CT, 1), 0)
        heads = []
        for kb in range(KB):
            ks = slice(kb * GDN_D, (kb + 1) * GDN_D)
            qv, kv = q_ref[:, ks], k_ref[:, ks]
            qf, kf = qv.astype(F32), kv.astype(F32)
            qk = _dot_nt(qv, kv)
            for hp in range(HP):
                gcol = _sel_col(gc_ref[...], NH * kg_id + kb * HP + hp)
                heads.append((gcol, (qk * _decay(gcol, causal)).astype(BF16), (qf * jnp.exp(gcol)).astype(BF16), kf))
        vn_scr[...] = jnp.zeros_like(vn_scr)
        for c in range(ncs):
            r = slice(c * CHUNK, (c + 1) * CHUNK)
            for hi in range(NH):
                gcol, attn, qd, kf = heads[hi]
                cs = slice(hi * GDN_D, (hi + 1) * GDN_D)
                s = s_scr[hi]
                st_ref[hi, c] = s
                sb = s.astype(BF16)
                gl = _last_of_chunk(gcol, rows, c)
                kd = (kf[r] * jnp.exp(gl - gcol[r])).astype(BF16)
                vn = (u_ref[r, cs] - _dot(w_ref[r, cs], sb)).astype(BF16)
                vn_scr[r, cs] = vn
                y_ref[r, cs] = _dot(qd[r], sb) + _dot(attn[r], vn_scr[:, cs])
                s_scr[hi] = s * jnp.exp(gl) + _dot_tn(kd, vn)
        vn_ref[...] = vn_scr[...]

    hk = pl.BlockSpec((CT, KB * GDN_D), lambda kg, t: (t, kg))
    hv = pl.BlockSpec((CT, NH * GDN_D), lambda kg, t: (t, kg))
    col = pl.BlockSpec((CT, LANES), lambda kg, t: (t, 0))
    return pl.pallas_call(
        body, name="gdn_scan_fwd", grid=(GDN_HV // NH, nt), in_specs=[hk, hk, hv, hv, col],
        out_specs=(hv, hv, pl.BlockSpec((NH, ncs, GDN_D, GDN_D), lambda kg, t: (kg, t, 0, 0))),
        out_shape=(SDS((S, 4096), F32), SDS((S, 4096), BF16), SDS((GDN_HV, S // CHUNK, GDN_D, GDN_D), F32)),
        scratch_shapes=[pltpu.VMEM((NH, GDN_D, GDN_D), F32), pltpu.VMEM((CT, NH * GDN_D), BF16)],
        compiler_params=_cp("parallel", "arbitrary"),
    )(q, k, u, w, gc)


def _gdn_scan_bwd(q, k, w, vn, gc, states, dy, sides=()):
    S = q.shape[0]
    nt = S // CT
    ncs = CT // CHUNK

    def body(q_ref, k_ref, w_ref, vn_ref, gc_ref, st_ref, dy_ref, du_ref, dw_ref, dq_ref, dk_ref, dgc_ref, ds_scr):
        kg_id, t = pl.program_id(0), pl.program_id(1)

        @pl.when(t == 0)
        def _():
            ds_scr[...] = jnp.zeros_like(ds_scr)

        _, causal, _, _, eye = _chunk_masks()
        rows = _iota((CT, 1), 0)
        heads = []
        for kb in range(KB):
            ks = slice(kb * GDN_D, (kb + 1) * GDN_D)
            qv, kv = q_ref[:, ks], k_ref[:, ks]
            qf, kf = qv.astype(F32), kv.astype(F32)
            qk = _dot_nt(qv, kv)
            for hp in range(HP):
                hi = kb * HP + hp
                cs = slice(hi * GDN_D, (hi + 1) * GDN_D)
                gcol = _sel_col(gc_ref[...], NH * kg_id + hi)
                dm = _decay(gcol, causal)
                attn_f = qk * dm
                egc = jnp.exp(gcol)
                qd_f = qf * egc
                dyv, vnv = dy_ref[:, cs], vn_ref[:, cs]
                heads.append(dict(cs=cs, ks=ks, gcol=gcol, dm=dm, attn_f=attn_f, egc=egc, qd_f=qd_f, kf=kf, qv=qv, kv=kv,
                                  qd=qd_f.astype(BF16), dy=dyv, vn=vnv, dattn=_dot_nt(dyv, vnv),
                                  at_dy=_dot_tn(attn_f.astype(BF16), dyv), dgc=[None] * ncs))
        dq_ref[...] = jnp.zeros_like(dq_ref)
        dk_ref[...] = jnp.zeros_like(dk_ref)
        for c in reversed(range(ncs)):
            r = slice(c * CHUNK, (c + 1) * CHUNK)
            for hi in range(NH):
                hd = heads[hi]
                cs, ks, gcol = hd["cs"], hd["ks"], hd["gcol"]
                s = st_ref[hi, c]
                sb = s.astype(BF16)
                dsn = ds_scr[hi]
                dsb = dsn.astype(BF16)
                gl = _last_of_chunk(gcol, rows, c)
                cd = jnp.exp(gl)
                ekd = jnp.exp(gl - gcol[r])
                kd_f = hd["kf"][r] * ekd
                dvn = (hd["at_dy"][r] + _dot(kd_f.astype(BF16), dsb)).astype(BF16)
                dqd = _dot_nt(hd["dy"][r], sb)
                dkd = _dot_nt(hd["vn"][r], dsb)
                dcd = jnp.sum(jnp.sum(s * dsn, axis=1, keepdims=True), axis=0, keepdims=True)
                ds_scr[hi] = dsn * cd + _dot_tn(hd["qd"][r], hd["dy"][r]) - _dot_tn(w_ref[r, cs], dvn)
                du_ref[r, cs] = dvn
                dw_ref[r, cs] = (-_dot_nt(dvn, sb)).astype(BF16)
                dq_ref[r, ks] += dqd * hd["egc"][r]
                dk_ref[r, ks] += dkd * ekd
                rs_q = jnp.sum(dqd * hd["qd_f"][r], axis=1, keepdims=True)
                rs_k = jnp.sum(dkd * kd_f, axis=1, keepdims=True)
                tot = jnp.sum(rs_k, axis=0, keepdims=True) + dcd * cd
                hd["dgc"][c] = rs_q - rs_k + jnp.where(rows[r] == c * CHUNK + CHUNK - 1, tot, 0.0)
        for hi in range(NH):
            hd = heads[hi]
            ks = hd["ks"]
            dab = (hd["dattn"] * hd["dm"]).astype(BF16)
            dq_ref[:, ks] += _dot(dab, hd["kv"])
            dk_ref[:, ks] += _dot_tn(dab, hd["qv"])
            e1 = hd["dattn"] * hd["attn_f"]
            dgc = (jnp.concatenate(hd["dgc"], axis=0) + jnp.sum(e1, axis=1, keepdims=True)
                   - jnp.sum(e1.T, axis=1, keepdims=True))
            dgc_ref[hi] = jnp.sum(jnp.where(eye, jnp.broadcast_to(dgc, (CT, CT)), 0.0), axis=0, keepdims=True)

    rev = lambda t: nt - 1 - t
    hk = pl.BlockSpec((CT, KB * GDN_D), lambda kg, t: (rev(t), kg))
    hv = pl.BlockSpec((CT, NH * GDN_D), lambda kg, t: (rev(t), kg))
    col = pl.BlockSpec((CT, LANES), lambda kg, t: (rev(t), 0))
    return _pcall(
        body, (q, k, w, vn, gc, states, dy), name="gdn_scan_bwd", grid=(GDN_HV // NH, nt),
        in_specs=[hk, hk, hv, hv, col, pl.BlockSpec((NH, ncs, GDN_D, GDN_D), lambda kg, t: (kg, rev(t), 0, 0)), hv],
        out_specs=(hv, hv, hk, hk, pl.BlockSpec((NH, 1, CT), lambda kg, t: (kg, 0, rev(t)))),
        out_shape=(SDS((S, 4096), BF16), SDS((S, 4096), BF16), SDS((S, 2048), F32), SDS((S, 2048), F32),
                   SDS((GDN_HV, 1, S), F32)),
        scratch_shapes=[pltpu.VMEM((NH, GDN_D, GDN_D), F32)], sem=("parallel", "arbitrary"), sides=sides)


def _gdn_chunk_bwd(k, v, beta, gc, tmat, du, dw, dk_p, dgc_p, sides=()):
    S = k.shape[0]
    nt = S // CT

    def body(k_ref, v_ref, beta_ref, gc_ref, t_ref, du_ref, dw_ref, dkp_ref, dgcp_ref,
             dk_ref, dv_ref, dbeta_ref, dgc_ref):
        kg_id = pl.program_id(1)
        same, causal, strict, _, eye = _chunk_masks()
        lane = _iota((CT, LANES), 1)

        @pl.when(kg_id == 0)
        def _():
            dbeta_ref[...] = jnp.zeros_like(dbeta_ref)
            dgc_ref[...] = jnp.zeros_like(dgc_ref)

        for kb_i in range(KB):
            ks = slice(kb_i * GDN_D, (kb_i + 1) * GDN_D)
            kv = k_ref[:, ks]
            kf = kv.astype(F32)
            kk = _dot_nt(kv, kv)
            dk = dkp_ref[:, ks]
            for hp in range(HP):
                hi = kb_i * HP + hp
                h = NH * kg_id + hi
                cs = slice(hi * GDN_D, (hi + 1) * GDN_D)
                bcol, gcol = _sel_col(beta_ref[...], h), _sel_col(gc_ref[...], h)
                dm = _decay(gcol, causal)
                vf = v_ref[:, cs].astype(F32)
                kb = kf * bcol
                a = jnp.where(strict, (kk * bcol) * dm, 0.0)
                egc = jnp.exp(gcol)
                kg_f = kb * egc
                tb = _unfold_bd(t_ref[hi], same).astype(BF16)
                duw = jnp.concatenate([du_ref[:, cs], dw_ref[:, cs]], axis=1)
                dt = _dot_nt(duw, jnp.concatenate([(vf * bcol).astype(BF16), kg_f.astype(BF16)], axis=1))
                dvb_dkg = _dot_tn(tb, duw)
                dvb, dkg = dvb_dkg[:, :GDN_D], dvb_dkg[:, GDN_D:]
                da = -_dot_nt(_dot_tn(tb, dt.astype(BF16)).astype(BF16), tb)
                rm = jnp.where(strict, da, 0.0)
                rdb = (rm * dm).astype(BF16)
                dkb = _dot(rdb, kv) + dkg * egc
                dk = dk + _dot_tn(rdb, kb.astype(BF16)) + dkb * bcol
                e2 = rm * a
                dgc_in = jnp.sum(jnp.where(eye, jnp.broadcast_to(dgcp_ref[hi], (CT, CT)), 0.0), axis=1, keepdims=True)
                dgc = (jnp.sum(e2, axis=1, keepdims=True) - jnp.sum(e2.T, axis=1, keepdims=True)
                       + jnp.sum(dkg * kg_f, axis=1, keepdims=True) + dgc_in)
                dbeta = jnp.sum(dkb * kf, axis=1, keepdims=True) + jnp.sum(dvb * vf, axis=1, keepdims=True)
                dv_ref[:, cs] = dvb * bcol
                dbeta_ref[...] += jnp.where(lane == h, dbeta, 0.0)
                dgc_ref[...] += jnp.where(lane == h, dgc, 0.0)
            dk_ref[:, ks] = dk

    hk = pl.BlockSpec((CT, KB * GDN_D), lambda t, kg: (t, kg))
    hv = pl.BlockSpec((CT, NH * GDN_D), lambda t, kg: (t, kg))
    col = pl.BlockSpec((CT, LANES), lambda t, kg: (t, 0))
    return _pcall(
        body, (k, v, beta, gc, tmat, du, dw, dk_p, dgc_p), name="gdn_chunk_bwd", grid=(nt, GDN_HV // NH),
        in_specs=[hk, hv, col, col, pl.BlockSpec((NH, None, CT, LANES), lambda t, kg: (kg, t, 0, 0)), hv, hv, hk,
                  pl.BlockSpec((NH, 1, CT), lambda t, kg: (kg, 0, t))],
        out_specs=(hk, hv, col, col),
        out_shape=(SDS((S, 2048), F32), SDS((S, 4096), F32), SDS((S, LANES), F32), SDS((S, LANES), F32)),
        sem=("parallel", "arbitrary"), sides=sides)


POST_RB, POST_CB = 64, 512
GDN_D_LOG2 = 7


def _head_ones():
    r, c = _iota((POST_CB, POST_CB), 0), _iota((POST_CB, POST_CB), 1)
    return ((r >> GDN_D_LOG2) == (c >> GDN_D_LOG2)).astype(BF16)


def _head_sums(x, ones):
    hi, lo = _split2(x)
    return _dot(hi, ones) + _dot(lo, ones)


def _post_tiles():
    return [(pl.ds(r0, POST_RB), pl.ds(c0, POST_CB)) for r0 in range(0, ROWS, POST_RB) for c0 in range(0, 4096, POST_CB)]


def _gdn_post_fwd(y, h1, norm_g):
    S = y.shape[0]

    def body(y_ref, z_ref, g_ref, o_ref):
        g4 = jnp.concatenate([g_ref[...]] * (POST_CB // GDN_D), axis=1)
        ones = _head_ones()
        for rs, cs in _post_tiles():
            yv, zv = y_ref[rs, cs], z_ref[rs, cs]
            yn = yv * lax.rsqrt(_head_sums(yv * yv, ones) * (1.0 / GDN_D) + RMS_EPS)
            o_ref[rs, cs] = (yn * g4 * (zv * _sigmoid(zv))).astype(BF16)

    row = lambda off: pl.BlockSpec((ROWS, 4096), lambda t: (t, off))
    return pl.pallas_call(
        body, name="gdn_post_fwd", grid=(S // ROWS,),
        in_specs=[row(0), row(2), pl.BlockSpec((1, GDN_D), lambda t: (0, 0))], out_specs=row(0),
        out_shape=SDS((S, 4096), BF16), compiler_params=_cp("parallel"),
    )(y, h1, norm_g)


def _gdn_post_bwd(do, y, h1, norm_g, sides=()):
    S = y.shape[0]

    def body(do_ref, y_ref, z_ref, g_ref, dy_ref, dz_ref, dg_ref):
        t = pl.program_id(0)
        g4 = jnp.concatenate([g_ref[...]] * (POST_CB // GDN_D), axis=1)
        ones = _head_ones()
        pg = jnp.zeros((1, GDN_D), F32)
        for rs, cs in _post_tiles():
            yv, zv, dov = y_ref[rs, cs], z_ref[rs, cs], do_ref[rs, cs]
            rstd = lax.rsqrt(_head_sums(yv * yv, ones) * (1.0 / GDN_D) + RMS_EPS)
            yn = yv * rstd
            sg = _sigmoid(zv)
            dz_ref[rs, cs] = (dov * (yn * g4) * (sg * (1.0 + zv * (1.0 - sg)))).astype(BF16)
            dyg = dov * (zv * sg)
            dyn = dyg * g4
            dy_ref[rs, cs] = (rstd * (dyn - yn * (_head_sums(dyn * yn, ones) * (1.0 / GDN_D)))).astype(BF16)
            s4 = jnp.sum(dyg * yn, axis=0, keepdims=True)
            for hh in range(POST_CB // GDN_D):
                pg = pg + s4[:, hh * GDN_D:(hh + 1) * GDN_D]

        @pl.when(t == 0)
        def _():
            dg_ref[...] = pg

        @pl.when(t > 0)
        def _():
            dg_ref[...] += pg

    row = lambda off: pl.BlockSpec((ROWS, 4096), lambda t: (t, off))
    vec = pl.BlockSpec((1, GDN_D), lambda t: (0, 0))
    return _pcall(
        body, (do, y, h1, norm_g), name="gdn_post_bwd", grid=(S // ROWS,), in_specs=[row(0), row(0), row(2), vec],
        out_specs=(row(0), row(2), vec),
        out_shape=(SDS((S, 4096), BF16), SDS((S, 3 * 4096), BF16), SDS((1, GDN_D), F32)),
        sem=("arbitrary",), sides=sides)


def _cols(g):
    return jnp.transpose(g, (1, 0, 2)).reshape(g.shape[1], -1)


def _rows(g):
    return g.reshape(-1, g.shape[-1])


def _local_step(x, tgt, sh, small):
    S = x.shape[0]
    xb = x.astype(BF16)
    rc = _ret_consts()
    cos, sin = _rope_tables(S)

    natural = ("ret_w_in", "mlp_w1_0", "mlp_w1_1", "conv_w")

    def gather(names):
        return _ag_side([sh[k] for k in names], [k != "conv_w" for k in names], [k in natural for k in names])

    ln_tiles = dict(tm=512, tn=D_MODEL, tk=1024)
    (wri,) = _comm_call("ag_ret_in", _ag_side([sh["ret_w_in"]], [True], [True], place_own=False))
    n_own = sh["ret_w_in"].shape[1]
    wri = lax.dynamic_update_slice_in_dim(wri, sh["ret_w_in"], (2 * lax.axis_index("x") + lax.axis_index("y")) * n_own, 1)
    h0, ((g_ro, w1_0),) = _mm(xb, wri, "nn", "mm_ret_in", out_dtype=BF16, epi="rope", extra=(cos, sin),
                              sides=[gather(["ret_w_out", "mlp_w1_0"])])
    wro, w1 = _rows(g_ro), [w1_0, None]
    (yr, o0, ret_st), ((g_gi, conv_w),) = _ret_fwd(h0, small["ret_gn_g"], rc, sides=[gather(["gdn_w_in", "conv_w"])])
    wgi = _cols(g_gi)
    wgi_main = wgi[:, :GDN_QKV + 4096]
    wba = jnp.pad(wgi[:, GDN_QKV + 4096:], ((0, 0), (0, LANES - 2 * GDN_HV)))
    x1, x1b, z1 = _mm(o0, wro, "nn", "mm_ret_out", epi="ln",
                      extra=(x, small["ln_mix_g"][0:1], small["ln_mix_b"][0:1]), **ln_tiles)
    (hh0, a0), ((g_w20,),) = _mm(x1b, w1[0], "nn", "mm_mlp0_up", epi="relu2", sides=[gather(["mlp_w2_0"])])
    w2 = [_rows(g_w20), None]
    (x2, x2b, z2), ((w1[1],),) = _mm(a0, w2[0], "nn", "mm_mlp0_down", epi="ln",
                                     extra=(x1, small["ln_ffn_g"][0:1], small["ln_ffn_b"][0:1]),
                                     sides=[gather(["mlp_w1_1"])], **ln_tiles)

    h1, ((g_w21, g_go),) = _mm(x2b, wgi_main, "nn", "mm_gdn_in", sides=[gather(["mlp_w2_1", "gdn_w_out"])])
    w2[1], wgo = _rows(g_w21), _rows(g_go)
    ba = _mm(x2b, wba, "nn", "mm_gdn_ba")
    qn = _gdn_conv_fwd(h1, conv_w, "q")
    kn = _gdn_conv_fwd(h1, conv_w, "k")
    vg = _gdn_conv_fwd(h1, conv_w, "v")
    beta, g, gc = _gdn_scal_fwd(ba, small["a_log"], small["dt_bias"])
    tmat, u, w = _gdn_chunk_fwd(kn, vg, beta, gc)
    yg, vn, gdn_st = _gdn_scan_fwd(qn, kn, u, w, gc)
    o1 = _gdn_post_fwd(yg, h1, small["norm_g"])
    x3, x3b, z3 = _mm(o1, wgo, "nn", "mm_gdn_out", epi="ln",
                      extra=(x2, small["ln_mix_g"][1:2], small["ln_mix_b"][1:2]), **ln_tiles)
    hh1, a1 = _mm(x3b, w1[1], "nn", "mm_mlp1_up", epi="relu2")
    z4 = _mm(a1, w2[1], "nn", "mm_mlp1_down", epi="add", extra=x3, scale=ALPHA)

    loss, dz4, dz4b, d_lnf_g1, d_lnf_b1 = _loss_ln_bwd(z4, small["ln_ffn_g"][1:2], small["ln_ffn_b"][1:2], tgt,
                                                       "loss_ln_ffn1_bwd")
    dhh1 = _mm(dz4b, w2[1], "nt", "mm_mlp1_down_dx", epi="drelu2", extra=hh1, out_dtype=BF16)
    dw2_1 = _mm(a1, dz4b, "tn", "mm_mlp1_down_dw")
    dx3 = _mm(dhh1, w1[1], "nt", "mm_mlp1_up_dx", epi="add", extra=dz4, scale=ALPHA)
    dw1_1 = _mm(x3b, dhh1, "tn", "mm_mlp1_up_dw", shard_major=True)
    dz3, dz3b, d_lnm_g1, d_lnm_b1 = _ln_bwd(dx3, z3, small["ln_mix_g"][1:2], "ln_mix1_bwd")
    shards_of = lambda g: g.reshape(N_CHIPS, -1, g.shape[-1])
    g_a = [dw1_1, shards_of(dw2_1)]
    do1, (th_a,) = _mm(dz3b, wgo, "nt", "mm_gdn_out_dx", sides=[_rs_swap_side(g_a)])
    sums_a = _rs_add(g_a, th_a, "a")
    dwgo = _mm(o1, dz3b, "tn", "mm_gdn_out_dw")
    dyg, dh1_z, d_norm_g = _gdn_post_bwd(do1, yg, h1, small["norm_g"])
    du, dw, dqn, dk_p, dgc_p = _gdn_scan_bwd(qn, kn, w, vn, gc, gdn_st, dyg)
    dkn, dvg, dbeta, dgc = _gdn_chunk_bwd(kn, vg, beta, gc, tmat, du, dw, dk_p, dgc_p)
    dba, d_a_log, d_dt_bias = _gdn_scal_bwd(ba, small["a_log"], small["dt_bias"], g, dbeta, dgc)
    dacc = _gdn_conv_bwd_act(h1, conv_w, dqn, "q")
    dacc = _gdn_conv_bwd_act(h1, conv_w, dkn, "k", dacc)
    dacc = _gdn_conv_bwd_act(h1, conv_w, dvg, "v", dacc)
    dh1, d_conv_w = _gdn_conv_bwd_in(h1, conv_w, dacc, dh1_z)
    g_go = [shards_of(dwgo)]
    dx2_ba, (th_go,) = _mm(dba, wba, "nt", "mm_gdn_ba_dx", epi="add", extra=dz3, scale=ALPHA,
                           sides=[_rs_swap_side(g_go)])
    sums_a = sums_a + _rs_add(g_go, th_go, "go")
    dx2, (parts_a,) = _mm(dh1, wgi_main, "nt", "mm_gdn_in_dx", epi="add", extra=dx2_ba,
                          sides=[_rs_owner_side(sums_a)])
    mine_a = _rs_sum(parts_a, "a")
    dwgi_main, (back_a,) = _mm(x2b, dh1, "tn", "mm_gdn_in_dw", sides=[_rs_back_side(mine_a)])
    red_w1_1, red_w2_1, red_go = zip(mine_a, back_a)
    dwba = _mm(x2b, dba, "tn", "mm_gdn_ba_dw")
    dwgi = jnp.concatenate([dwgi_main, dwba[:, :2 * GDN_HV]], axis=1)
    g_b = [jnp.transpose(dwgi.reshape(dwgi.shape[0], N_CHIPS, -1), (1, 0, 2))]

    dz2, dz2b, d_lnf_g0, d_lnf_b0 = _ln_bwd(dx2, z2, small["ln_ffn_g"][0:1], "ln_ffn0_bwd")
    dhh0, (th_b,) = _mm(dz2b, w2[0], "nt", "mm_mlp0_down_dx", epi="drelu2", extra=hh0, out_dtype=BF16,
                        sides=[_rs_swap_side(g_b)])
    sums_b = _rs_add(g_b, th_b, "b")
    dw2_0 = _mm(a0, dz2b, "tn", "mm_mlp0_down_dw")
    dx1, (parts_b,) = _mm(dhh0, w1[0], "nt", "mm_mlp0_up_dx", epi="add", extra=dz2, scale=ALPHA,
                          sides=[_rs_owner_side(sums_b)])
    mine_b = _rs_sum(parts_b, "b")
    dw1_0 = _mm(x1b, dhh0, "tn", "mm_mlp0_up_dw", shard_major=True)
    dz1, dz1b, d_lnm_g0, d_lnm_b0 = _ln_bwd(dx1, z1, small["ln_mix_g"][0:1], "ln_mix0_bwd")
    g_c = [dw1_0, shards_of(dw2_0)]
    do0, (th_c, back_b) = _mm(dz1b, wro, "nt", "mm_ret_out_dx", sides=[_rs_swap_side(g_c), _rs_back_side(mine_b)])
    (red_gi,) = zip(mine_b, back_b)
    sums_c = _rs_add(g_c, th_c, "c")
    dwro = _mm(o0, dz1b, "tn", "mm_ret_out_dw")
    g_ro = [shards_of(dwro)]
    (dq0, dk0, dv0, dgate, d_gn_g), (th_ro, parts_c) = _ret_bwd(
        h0, do0, yr, small["ret_gn_g"], ret_st, rc, cos, sin, sides=[_rs_swap_side(g_ro), _rs_owner_side(sums_c)])
    sums_ro = _rs_add(g_ro, th_ro, "ro")
    mine_c = _rs_sum(parts_c, "c")
    dh0 = jnp.concatenate([dq0, dk0, dv0, dgate], axis=1)
    dwri, (parts_ro, back_c) = _mm(xb, dh0, "tn", "mm_ret_in_dw", shard_major=True,
                                   sides=[_rs_owner_side(sums_ro), _rs_back_side(mine_c)])
    red_w1_0, red_w2_0 = zip(mine_c, back_c)
    mine_ro = _rs_sum(parts_ro, "ro")
    g_d = [dwri]
    sums_d = _rs_add(g_d, _comm_call("rs_swap_halves_d", _rs_swap_side(g_d)), "d")
    grad_x, (parts_d, back_ro) = _mm(dh0, wri, "nt", "mm_ret_in_dx", epi="add", extra=dz1, scale=ALPHA,
                                     sides=[_rs_owner_side(sums_d), _rs_back_side(mine_ro)])
    (red_ro,) = zip(mine_ro, back_ro)
    mine_d = _rs_sum(parts_d, "d")
    (red_ri,) = zip(mine_d, _comm_call("rs_swap_reduced_d", _rs_back_side(mine_d)))

    big = dict(ret_w_in=red_ri, ret_w_out=red_ro, gdn_w_in=red_gi, gdn_w_out=red_go,
               mlp_w1=(red_w1_0, red_w1_1), mlp_w2=(red_w2_0, red_w2_1))
    sm = dict(ret_gn_g=d_gn_g, a_log=d_a_log, dt_bias=d_dt_bias, norm_g=d_norm_g,
              ln_mix_g=jnp.concatenate([d_lnm_g0, d_lnm_g1], 0), ln_mix_b=jnp.concatenate([d_lnm_b0, d_lnm_b1], 0),
              ln_ffn_g=jnp.concatenate([d_lnf_g0, d_lnf_g1], 0), ln_ffn_b=jnp.concatenate([d_lnf_b0, d_lnf_b1], 0),
              conv_w=d_conv_w)
    return loss, grad_x, big, sm


def _coords():
    return lax.axis_index("x"), lax.axis_index("y"), lax.axis_index("c")


HBM_SPEC = pl.BlockSpec(memory_space=pl.ANY)


def _other_chips(x, y):
    return [(1 - x, y), (x, 1 - y), (1 - x, 1 - y)]


def _ag_side(shards, split, cols, place_own=True):
    n = len(shards)

    def rows_of(p, core):
        half = shards[p].shape[0] // 2
        return pl.ds(core * half, half) if split[p] else slice(None)

    def slot(outs, p, s, core=None):
        r = slice(None) if core is None else rows_of(p, core)
        if cols[p]:
            w = shards[p].shape[1]
            return outs[p].at[r, pl.ds(pl.multiple_of(s * w, LANES), w)]
        return outs[p].at[s, r]

    def over_ici(xyc, ins, outs):
        x, y, c = xyc
        return [(ins[p].at[rows_of(p, c)], slot(outs, p, 2 * x + y, c), (cx, cy, c))
                for p in range(n) for cx, cy in _other_chips(x, y)]

    def to_sibling(xyc, ins, outs):
        x, y, c = xyc
        zones = [slot(outs, p, 2 * cx + cy, c) for p in range(n) if split[p] for cx, cy in _other_chips(x, y)]
        return [(z, z, (x, y, 1 - c)) for z in zones]

    def own(xyc, ins, outs):
        x, y, _ = xyc
        return [(ins[p], slot(outs, p, 2 * x + y)) for p in range(n)]

    n_split = sum(bool(s) for s in split)
    phases, counts = [over_ici], [3 * n]
    if n_split:
        phases, counts = phases + [to_sibling], counts + [3 * n_split]
    shapes = [SDS((s.shape[0], N_CHIPS * s.shape[1]) if cols[p] else (N_CHIPS,) + s.shape, s.dtype)
              for p, s in enumerate(shards)]
    if not place_own:
        return _Side(shards, shapes, phases, counts)
    return _Side(shards, shapes, phases, counts, local=own, n_local=n)


def _rs_swap_side(grads):
    n = len(grads)
    halves = [g.shape[1] // 2 for g in grads]

    def swap(xyc, ins, outs):
        x, y, c = xyc
        return [(ins[p].at[:, pl.ds((1 - c) * halves[p], halves[p])], outs[p], (x, y, 1 - c)) for p in range(n)]

    return _Side(grads, [SDS((N_CHIPS, halves[p]) + g.shape[2:], F32) for p, g in enumerate(grads)], [swap], [n])


def _rs_add(grads, theirs, tag):
    c = lax.axis_index("c")
    return [_add_half(g, t, c, "rs_add_%s%d" % (tag, p)) for p, (g, t) in enumerate(zip(grads, theirs))]


def _rs_owner_side(chip_sums):
    n = len(chip_sums)

    def to_owner(xyc, ins, outs):
        x, y, c = xyc
        return [(ins[p].at[2 * cx + cy], outs[p].at[2 * x + y], (cx, cy, c))
                for p in range(n) for cx, cy in _other_chips(x, y)]

    def own(xyc, ins, outs):
        x, y, _ = xyc
        return [(ins[p].at[2 * x + y], outs[p].at[2 * x + y]) for p in range(n)]

    return _Side(chip_sums, [SDS(s.shape, s.dtype) for s in chip_sums], [to_owner], [3 * n], local=own, n_local=n)


def _rs_sum(parts, tag):
    return [_sum_chips(pt, "rs_sum_%s%d" % (tag, p)) for p, pt in enumerate(parts)]


def _rs_back_side(mine):
    n = len(mine)

    def swap(xyc, ins, outs):
        x, y, c = xyc
        return [(ins[p], outs[p], (x, y, 1 - c)) for p in range(n)]

    return _Side(mine, [SDS(m.shape, F32) for m in mine], [swap], [n])


def _add_half(g, theirs, c, name):
    _, R, C = g.shape
    half = R // 2
    tr = min(256, half)
    nb = half // tr

    def body(c_ref, g_ref, t_ref, o_ref):
        o_ref[...] = (g_ref[...] + t_ref[...]).astype(BF16)

    blk = pl.BlockSpec((None, tr, C), lambda s, i, c_ref: (s, i, 0))
    return pl.pallas_call(
        body, name=name,
        grid_spec=pltpu.PrefetchScalarGridSpec(
            num_scalar_prefetch=1, grid=(N_CHIPS, nb),
            in_specs=[pl.BlockSpec((None, tr, C), lambda s, i, c_ref: (s, c_ref[0] * nb + i, 0)), blk],
            out_specs=blk),
        out_shape=SDS((N_CHIPS, half, C), BF16), compiler_params=_cp("parallel", "parallel"),
    )(jnp.reshape(c, (1,)).astype(jnp.int32), g, theirs)


def _sum_chips(parts, name):
    _, r, C = parts.shape
    tr = min(256, r)

    def body(p_ref, o_ref):
        f = lambda s: p_ref[s].astype(F32)
        o_ref[...] = ((f(0) + f(1)) + f(2)) + f(3)

    return pl.pallas_call(
        body, name=name, grid=(r // tr,), in_specs=[pl.BlockSpec((N_CHIPS, tr, C), lambda i: (0, i, 0))],
        out_specs=pl.BlockSpec((tr, C), lambda i: (i, 0)), out_shape=SDS((r, C), F32), compiler_params=_cp("parallel"),
    )(parts)


def _all_reduce_small(buf):
    rows = buf.shape[0]

    def body(x_ref, o_ref, all_ref, send_sems, recv_sems):
        x, y, c = _coords()
        me = 4 * x + 2 * y + c
        all_ref[me] = x_ref[...]
        flips = [(fx, fy, fc) for fx in (0, 1) for fy in (0, 1) for fc in (0, 1)][1:]
        copies = []
        for k, (fx, fy, fc) in enumerate(flips):
            to = (x ^ fx, y ^ fy, c ^ fc)
            copies.append(pltpu.make_async_remote_copy(src_ref=x_ref, dst_ref=all_ref.at[me], send_sem=send_sems.at[k],
                                                       recv_sem=recv_sems.at[k], device_id=to, device_id_type=MESH))
        for cp in copies:
            cp.start()
        for cp in copies:
            cp.wait_recv()
        for cp in copies:
            cp.wait_send()
        acc = all_ref[0]
        for d in range(1, N_DEV):
            acc = acc + all_ref[d]
        o_ref[...] = acc

    vm = pl.BlockSpec(memory_space=pltpu.VMEM)
    return pl.pallas_call(
        body, name="all_reduce_small", in_specs=[vm], out_specs=vm, out_shape=SDS((rows, LANES), F32),
        scratch_shapes=[pltpu.VMEM((N_DEV, rows, LANES), F32), pltpu.SemaphoreType.DMA((N_DEV - 1,)),
                        pltpu.SemaphoreType.DMA((N_DEV - 1,))],
    )(buf)


def _adam_update(w, gv, m, v):
    mn = ADAM_B1 * m + (1.0 - ADAM_B1) * gv
    vn = ADAM_B2 * v + (1.0 - ADAM_B2) * (gv * gv)
    m_hat = mn / (1.0 - ADAM_B1 ** ADAM_STEP)
    v_hat = vn / (1.0 - ADAM_B2 ** ADAM_STEP)
    return -ADAM_LR * (m_hat / (jnp.sqrt(v_hat) + ADAM_EPS) + ADAM_WD * w), mn, vn


def _adamw_halves(w, mine, theirs, m, v, c, name, row0=0, bufs=None):
    R, C = w.shape
    half = mine.shape[0]
    tr = min(128, half)
    nbh = half // tr
    b0 = row0 // tr
    assert row0 % tr == 0 and half % tr == 0

    def body(c_ref, w_ref, a_ref, b_ref, m_ref, v_ref, *rest):
        g_ref, d_ref, mo_ref, vo_ref = rest[-4:]
        is_mine = (pl.program_id(0) // nbh) == c_ref[0]
        gv = jnp.where(is_mine, a_ref[...], b_ref[...])
        g_ref[...] = gv
        d_ref[...], mo_ref[...], vo_ref[...] = _adam_update(w_ref[...], gv, m_ref[...], v_ref[...])

    blk = pl.BlockSpec((tr, C), lambda i, c_ref: (b0 + i, 0))
    ablk = pl.BlockSpec((tr, C), lambda i, c_ref: (jnp.where(i // nbh == c_ref[0], i % nbh, 0), 0))
    bblk = pl.BlockSpec((tr, C), lambda i, c_ref: (jnp.where(i // nbh == c_ref[0], 0, i % nbh), 0))
    extra = [] if bufs is None else list(bufs)
    return pl.pallas_call(
        body, name=name,
        grid_spec=pltpu.PrefetchScalarGridSpec(num_scalar_prefetch=1, grid=(2 * nbh,),
                                               in_specs=[blk, ablk, bblk, blk, blk] + [HBM_SPEC] * len(extra),
                                               out_specs=(blk,) * 4),
        out_shape=(SDS((R, C), F32),) * 4, compiler_params=_cp("parallel"),
        input_output_aliases={6 + i: i for i in range(len(extra))},
    )(jnp.reshape(c, (1,)).astype(jnp.int32), w, mine, theirs, m, v, *extra)


def _adamw(w, g, m, v, name):
    R, C = w.shape
    tr = min(256, R)
    assert R % tr == 0

    def body(w_ref, g_ref, m_ref, v_ref, d_ref, mo_ref, vo_ref):
        d_ref[...], mo_ref[...], vo_ref[...] = _adam_update(w_ref[...], g_ref[...], m_ref[...], v_ref[...])

    blk = pl.BlockSpec((tr, C), lambda i: (i, 0))
    return pl.pallas_call(
        body, name=name, grid=(R // tr,), in_specs=[blk] * 4, out_specs=(blk,) * 3,
        out_shape=(SDS((R, C), F32),) * 3, compiler_params=_cp("parallel"),
    )(w, g, m, v)


def _pack(arrs):
    rows = []
    for a in arrs:
        flat = a.reshape(-1).astype(F32)
        pad = (-flat.shape[0]) % LANES
        rows.append(jnp.pad(flat, (0, pad)).reshape(-1, LANES))
    buf = jnp.concatenate(rows, axis=0)
    pad_rows = (-buf.shape[0]) % 8
    return jnp.pad(buf, ((0, pad_rows), (0, 0)))


def _unpack(buf, shapes):
    out, r = [], 0
    for shp in shapes:
        size = int(np.prod(shp))
        nr = -(-size // LANES)
        out.append(buf[r:r + nr].reshape(-1)[:size].reshape(shp))
        r += nr
    return out


def _pad_lanes(a):
    return jnp.pad(a, ((0, 0), (0, LANES - a.shape[1])))


def kernel(x, ret_w_in, ret_gn_g, ret_w_out, gdn_w_in, gdn_conv_w, gdn_a_log, gdn_dt_bias, gdn_norm_g, gdn_w_out, ln_mix_g, ln_mix_b, mlp_w1, mlp_w2, ln_ffn_g, ln_ffn_b, loss_target, m_ret_w_in, m_ret_gn_g, m_ret_w_out, m_gdn_w_in, m_gdn_conv_w, m_gdn_a_log, m_gdn_dt_bias, m_gdn_norm_g, m_gdn_w_out, m_ln_mix_g, m_ln_mix_b, m_mlp_w1, m_mlp_w2, m_ln_ffn_g, m_ln_ffn_b, v_ret_w_in, v_ret_gn_g, v_ret_w_out, v_gdn_w_in, v_gdn_conv_w, v_gdn_a_log, v_gdn_dt_bias, v_gdn_norm_g, v_gdn_w_out, v_ln_mix_g, v_ln_mix_b, v_mlp_w1, v_mlp_w2, v_ln_ffn_g, v_ln_ffn_b):
    cx, cy = lax.axis_index("x"), lax.axis_index("y")
    chip = 2 * cx + cy

    sh = dict(ret_w_in=ret_w_in[0].astype(BF16), ret_w_out=ret_w_out[0].astype(BF16),
              gdn_w_in=gdn_w_in[0].astype(BF16), gdn_w_out=gdn_w_out[0].astype(BF16),
              mlp_w1_0=mlp_w1[0].astype(BF16), mlp_w1_1=mlp_w1[1].astype(BF16),
              mlp_w2_0=mlp_w2[0].astype(BF16), mlp_w2_1=mlp_w2[1].astype(BF16), conv_w=gdn_conv_w[0])
    small = dict(ret_gn_g=ret_gn_g, a_log=_pad_lanes(gdn_a_log), dt_bias=_pad_lanes(gdn_dt_bias), norm_g=gdn_norm_g,
                 ln_mix_g=ln_mix_g, ln_mix_b=ln_mix_b, ln_ffn_g=ln_ffn_g, ln_ffn_b=ln_ffn_b)

    loss, grad_x, big, sm = _local_step(x[0], loss_target[0], sh, small)

    small_names = ["ret_gn_g", "a_log", "dt_bias", "norm_g", "ln_mix_g", "ln_mix_b", "ln_ffn_g", "ln_ffn_b", "conv_w"]
    small_shapes = [(1, 4096), (1, LANES), (1, LANES), (1, GDN_D), (2, D_MODEL), (2, D_MODEL), (2, D_MODEL),
                    (2, D_MODEL), (4, GDN_QKV)]
    red = _all_reduce_small(_pack([loss] + [sm[k] for k in small_names]))
    red_loss, *red_small = _unpack(red, [(1, 1)] + small_shapes)
    gs = dict(zip(small_names, red_small))
    g_conv = lax.dynamic_slice_in_dim(gs["conv_w"], chip * 2048, 2048, axis=1)
    g_a_log, g_dt_bias = gs["a_log"][:, :GDN_HV], gs["dt_bias"][:, :GDN_HV]

    big_w = [(ret_w_in, m_ret_w_in, v_ret_w_in, [big["ret_w_in"]]), (ret_w_out, m_ret_w_out, v_ret_w_out, [big["ret_w_out"]]),
             (gdn_w_in, m_gdn_w_in, v_gdn_w_in, [big["gdn_w_in"]]), (gdn_w_out, m_gdn_w_out, v_gdn_w_out, [big["gdn_w_out"]]),
             (mlp_w1, m_mlp_w1, v_mlp_w1, big["mlp_w1"]), (mlp_w2, m_mlp_w2, v_mlp_w2, big["mlp_w2"])]
    core = lax.axis_index("c")
    big_out = []
    for i, (w_, m_, v_, layers) in enumerate(big_w):
        two_d = lambda a: a.reshape(-1, a.shape[-1])
        res = None
        for j, (mine, theirs) in enumerate(layers):
            res = _adamw_halves(two_d(w_), mine, theirs, two_d(m_), two_d(v_), core, "adamw_%d_%d" % (i, j),
                                row0=j * 2 * mine.shape[0], bufs=res)
        big_out.append(tuple(a.reshape(w_.shape) for a in res))
    sm_w = [(ret_gn_g, m_ret_gn_g, v_ret_gn_g, gs["ret_gn_g"]), (gdn_conv_w, m_gdn_conv_w, v_gdn_conv_w, g_conv),
            (gdn_a_log, m_gdn_a_log, v_gdn_a_log, g_a_log), (gdn_dt_bias, m_gdn_dt_bias, v_gdn_dt_bias, g_dt_bias),
            (gdn_norm_g, m_gdn_norm_g, v_gdn_norm_g, gs["norm_g"]), (ln_mix_g, m_ln_mix_g, v_ln_mix_g, gs["ln_mix_g"]),
            (ln_mix_b, m_ln_mix_b, v_ln_mix_b, gs["ln_mix_b"]), (ln_ffn_g, m_ln_ffn_g, v_ln_ffn_g, gs["ln_ffn_g"]),
            (ln_ffn_b, m_ln_ffn_b, v_ln_ffn_b, gs["ln_ffn_b"])]
    sm_shapes = [w_.shape for w_, _, _, _ in sm_w]
    d_s, nm_s, nv_s = _adamw(_pack([w_ for w_, _, _, _ in sm_w]), _pack([g_ for _, _, _, g_ in sm_w]),
                             _pack([m_ for _, m_, _, _ in sm_w]), _pack([v_ for _, _, v_, _ in sm_w]), "adamw_small")
    d_s, nm_s, nv_s = (_unpack(a, sm_shapes) for a in (d_s, nm_s, nv_s))
    sm_out = [(g_.reshape(w_.shape), d_s[i], nm_s[i], nv_s[i]) for i, (w_, _, _, g_) in enumerate(sm_w)]

    per_w = [big_out[0], sm_out[0], big_out[1], big_out[2], sm_out[1], sm_out[2], sm_out[3], sm_out[4], big_out[3],
             sm_out[5], sm_out[6], big_out[4], big_out[5], sm_out[7], sm_out[8]]
    outs = [red_loss.reshape(()), grad_x[None]]
    for kind in range(4):
        outs.extend(t[kind] for t in per_w)
    return tuple(outs)
```
